```python
import jax
import jax.numpy as jnp
from jax import lax
import numpy as np

D_MODEL = 1024
BATCH = 32
SEQ = 2048
DEPTH = 2
DEC_BATCH = 8
DEC_SEQ = 32
PAST_LEN = 1024

CHUNK = 64
CONV_W = 4
A_HEADS = 8
A_DK = 128
A_DV = 128
A_QK = A_HEADS * A_DK
A_VW = A_HEADS * A_DV
A_CONV_CH = 2 * A_QK + A_VW
B_HEADS = 16
B_HD = 64
B_W = B_HEADS * B_HD
BAND_CHUNKS = 8
BAND_ROWS = BAND_CHUNKS * CHUNK
MAX_REL = 128
C_HEADS = 8
C_DK = 128
C_DV = 256
C_QK = C_HEADS * C_DK
C_VW = C_HEADS * C_DV
GATE_RANK = 16
GATE_TAU = 16.0
GLA_BLOCK = 16
MEM_LEN = 256
M_HEADS = 4
M_HD = 128
M_W = M_HEADS * M_HD
EPS = 1e-6

L0_SPLITS = (A_CONV_CH, A_HEADS, A_HEADS, A_VW, B_W, B_W, B_W, B_W)
L1_SPLITS = (C_QK, C_QK, C_VW, GATE_RANK, C_VW)
L0_IN = sum(L0_SPLITS)
L1_IN = sum(L1_SPLITS)

kernel_name = 'hybrid_streaming_encoder_step'


def _split(t, sizes):
    return jnp.split(t, [int(s) for s in np.cumsum(sizes)[:-1]], axis=-1)


def rms_norm(x, g):
    xf = x.astype(jnp.float32)
    y = xf * lax.rsqrt(jnp.mean(xf * xf, axis=-1, keepdims=True) + EPS)
    return (y * g.astype(jnp.float32)).astype(x.dtype)


def l2_norm(x):
    xf = x.astype(jnp.float32)
    return xf * lax.rsqrt(jnp.sum(xf * xf, axis=-1, keepdims=True) + EPS)


def causal_conv(u, buf, w):
    L = u.shape[1]
    ext = jnp.concatenate([buf.astype(u.dtype), u], axis=1)
    y = ext[:, 0:L] * w[0]
    for i in range(1, CONV_W):
        y = y + ext[:, i:i + L] * w[i]
    return y, ext[:, ext.shape[1] - (CONV_W - 1):]


def gdn_block(S, q, k, v, g, beta):
    L = q.shape[1]
    S = S.astype(jnp.float32)
    G = jnp.moveaxis(jnp.cumsum(g, axis=1), 1, -1)
    bt = jnp.moveaxis(beta, 1, -1)
    incl = jnp.tril(jnp.ones((L, L), dtype=bool))
    strict = jnp.tril(jnp.ones((L, L), dtype=bool), -1)
    diff = G[..., :, None] - G[..., None, :]
    dec = jnp.where(incl, jnp.exp(jnp.where(incl, diff, 0.0)), 0.0)
    kk = jnp.einsum('bihd,bjhd->bhij', k, k)
    a_mat = jnp.where(strict, bt[..., :, None] * kk * dec, 0.0) + jnp.eye(L, dtype=jnp.float32)
    rhs = jnp.concatenate([jnp.einsum('bjhv,bhj->bhjv', v, bt),
                           jnp.einsum('bjhd,bhj->bhjd', k, bt * jnp.exp(G))], axis=-1)
    sol = lax.linalg.triangular_solve(a_mat, rhs, left_side=True, lower=True)
    u = sol[..., :A_DV] - jnp.einsum('bhjd,bhdv->bhjv', sol[..., A_DV:], S)
    qk = jnp.einsum('bihd,bjhd->bhij', q, k) * dec
    o = (jnp.einsum('bihd,bhdv->bhiv', q, S) * jnp.exp(G)[..., None]
         + jnp.einsum('bhij,bhjv->bhiv', qk, u))
    g_last = G[..., -1]
    S_new = (S * jnp.exp(g_last)[..., None, None]
             + jnp.einsum('bjhd,bhj,bhjv->bhdv', k, jnp.exp(g_last[..., None] - G), u))
    return S_new, jnp.moveaxis(o, 1, 2)


def gla_block(S, q, k, v, lg):
    L = q.shape[1]
    S = S.astype(jnp.float32)
    cb = jnp.cumsum(lg, axis=1)
    incl = jnp.tril(jnp.ones((L, L), dtype=bool))[None, :, :, None, None]
    diff = cb[:, :, None] - cb[:, None, :]
    dec = jnp.where(incl, jnp.exp(jnp.where(incl, diff, 0.0)), 0.0)
    att = jnp.einsum('bihd,bjhd,bijhd->bhij', q, k, dec)
    o = (jnp.einsum('bihd,bhdv->bihv', q * jnp.exp(cb), S)
         + jnp.einsum('bhij,bjhv->bihv', att, v))
    c_last = cb[:, -1]
    S_new = (S * jnp.exp(c_last)[..., None]
             + jnp.einsum('bjhd,bjhv->bhdv', k * jnp.exp(c_last[:, None] - cb), v))
    return S_new, o


def blocked_scan(block_fn, state0, xs, block):
    n_b, n_t = xs[0].shape[:2]
    nb = n_t // block
    xs_b = tuple(jnp.moveaxis(t.reshape((n_b, nb, block) + t.shape[2:]), 1, 0) for t in xs)
    state, out = lax.scan(lambda s, xb: block_fn(s, *xb), state0, xs_b)
    out = jnp.moveaxis(out, 0, 1)
    return state, out.reshape((n_b, n_t) + out.shape[3:])


def _band_bias(rel_table, n_q, n_k, offset):
    d = jnp.arange(n_q)[:, None] + offset - jnp.arange(n_k)[None, :]
    return rel_table[:, jnp.clip(d, -MAX_REL, MAX_REL) + MAX_REL].astype(jnp.float32)


def _softmax_attend(q, k, v, bias, valid):
    s = jnp.einsum('bqhd,bkhd->bhqk', q, k).astype(jnp.float32) * (q.shape[-1] ** -0.5)
    if bias is not None:
        s = s + bias
    if valid is not None:
        s = jnp.where(valid, s, -jnp.inf)
    p = jax.nn.softmax(s, axis=-1).astype(v.dtype)
    return jnp.einsum('bhqk,bkhd->bqhd', p, v)


def band_attn_prompt(q, k, v, rel_table):
    n_b, n_t, n_h, n_d = q.shape
    n_c = n_t // CHUNK
    n_k = BAND_ROWS + CHUNK
    pad = jnp.zeros((n_b, BAND_ROWS, n_h, n_d), k.dtype)
    kp = jnp.concatenate([pad, k], axis=1)
    vp = jnp.concatenate([pad, v], axis=1)
    bias = _band_bias(rel_table, CHUNK, n_k, BAND_ROWS)

    def one_chunk(c):
        start = c * CHUNK
        qc = lax.dynamic_slice_in_dim(q, start, CHUNK, axis=1)
        kc = lax.dynamic_slice_in_dim(kp, start, n_k, axis=1)
        vc = lax.dynamic_slice_in_dim(vp, start, n_k, axis=1)
        valid = (start - BAND_ROWS + jnp.arange(n_k)) >= 0
        return _softmax_attend(qc, kc, vc, bias, valid)

    out = lax.map(one_chunk, jnp.arange(n_c))
    return jnp.moveaxis(out, 0, 1).reshape(n_b, n_t, n_h, n_d)


def band_attn_sample(q, k, v, k_cache, v_cache, rel_table):
    w = k_cache.shape[1]
    kk = jnp.concatenate([k_cache.astype(k.dtype), k], axis=1)
    vv = jnp.concatenate([v_cache.astype(v.dtype), v], axis=1)
    bias = _band_bias(rel_table, q.shape[1], kk.shape[1], w)
    return _softmax_attend(q, kk, vv, bias, None)


def layer0_mixer(h, conv_buf, S0, band_k_cache, band_v_cache, w_in, conv_w, a_log, dt_bias,
                 a_onorm_g, b_q_g, b_k_g, rel_table, w_out, prompt):
    n_b, L, _ = h.shape
    f32 = jnp.float32
    u, a_raw, b_raw, z_a, q_b, k_b, v_b, z_b = _split(h @ w_in, L0_SPLITS)
    u, new_buf = causal_conv(u, conv_buf, conv_w)
    q_a, k_a, v_a = _split(jax.nn.silu(u), (A_QK, A_QK, A_VW))
    q_a = l2_norm(q_a.reshape(n_b, L, A_HEADS, A_DK)) * (A_DK ** -0.5)
    k_a = l2_norm(k_a.reshape(n_b, L, A_HEADS, A_DK))
    v_a = v_a.reshape(n_b, L, A_HEADS, A_DV).astype(f32)
    beta = jax.nn.sigmoid(b_raw.astype(f32))
    g = -jnp.exp(a_log.astype(f32)) * jax.nn.softplus(a_raw.astype(f32) + dt_bias.astype(f32))
    if prompt:
        S_new, o_a = blocked_scan(gdn_block, S0, (q_a, k_a, v_a, g, beta), CHUNK)
    else:
        S_new, o_a = gdn_block(S0, q_a, k_a, v_a, g, beta)
    o_a = rms_norm(o_a, a_onorm_g).reshape(n_b, L, A_VW).astype(h.dtype) * jax.nn.silu(z_a)
    q_b = rms_norm(q_b.reshape(n_b, L, B_HEADS, B_HD), b_q_g)
    k_b = rms_norm(k_b.reshape(n_b, L, B_HEADS, B_HD), b_k_g)
    v_b = v_b.reshape(n_b, L, B_HEADS, B_HD)
    if prompt:
        o_b = band_attn_prompt(q_b, k_b, v_b, rel_table)
        k_rows, v_rows = k_b[:, -BAND_ROWS:], v_b[:, -BAND_ROWS:]
    else:
        o_b = band_attn_sample(q_b, k_b, v_b, band_k_cache, band_v_cache, rel_table)
        k_rows, v_rows = k_b, v_b
    o_b = o_b.reshape(n_b, L, B_W) * jax.nn.silu(z_b)
    y = jnp.concatenate([o_a, o_b], axis=-1) @ w_out
    return y, new_buf, S_new, k_rows, v_rows


def layer1_mixer(h, S0, w_in, w_gate_up, gate_bias, c_onorm_g, w_out, prompt):
    n_b, L, _ = h.shape
    f32 = jnp.float32
    q, k, v, lr, z = _split(h @ w_in, L1_SPLITS)
    q = q.reshape(n_b, L, C_HEADS, C_DK).astype(f32) * (C_DK ** -0.5)
    k = k.reshape(n_b, L, C_HEADS, C_DK).astype(f32)
    v = v.reshape(n_b, L, C_HEADS, C_DV).astype(f32)
    lg = jax.nn.log_sigmoid((lr @ w_gate_up + gate_bias).astype(f32)) / GATE_TAU
    lg = lg.reshape(n_b, L, C_HEADS, C_DK)
    if prompt:
        S_new, o = blocked_scan(gla_block, S0, (q, k, v, lg), GLA_BLOCK)
    else:
        S_new, o = gla_block(S0, q, k, v, lg)
    o = rms_norm(o, c_onorm_g).reshape(n_b, L, C_VW).astype(h.dtype) * jax.nn.silu(z)
    return o @ w_out, S_new


def memory_kv(mem, mem_norm_g, w_mkv, mk_g):
    n_b, n_m, _ = mem.shape
    k, v = _split(rms_norm(mem, mem_norm_g) @ w_mkv, (M_W, M_W))
    k = rms_norm(k.reshape(n_b, n_m, M_HEADS, M_HD), mk_g)
    return k, v.reshape(n_b, n_m, M_HEADS, M_HD)


def memory_attend(h, mk, mv, w_mq, mq_g, w_mo):
    n_b, L, _ = h.shape
    q, z = _split(h @ w_mq, (M_W, M_W))
    q = rms_norm(q.reshape(n_b, L, M_HEADS, M_HD), mq_g)
    o = _softmax_attend(q, mk.astype(q.dtype), mv.astype(q.dtype), None, None)
    return (o.reshape(n_b, L, M_W) * jax.nn.silu(z)) @ w_mo


def setup_inputs(seed: int = 0) -> dict:
    key = jax.random.key(seed)
    keys = iter(jax.random.split(key, 64))
    f32 = jnp.float32

    def nrm(shape, scale):
        return jax.random.normal(next(keys), shape, f32) * scale

    def gain(n):
        return 1.0 + nrm((n,), 0.02)

    def mixer_dt_bias():
        u = jax.random.uniform(next(keys), (A_HEADS,), f32)
        dt = jnp.exp(u * (np.log(0.1) - np.log(0.001)) + np.log(0.001)).astype(f32)
        return dt + jnp.log(-jnp.expm1(-dt))

    band_cache = min(BAND_ROWS, PAST_LEN)
    D = D_MODEL
    return {
        'x_prompt': nrm((BATCH, SEQ, D), 1.0),
        'x_sample': nrm((DEC_BATCH, DEC_SEQ, D), 1.0),
        'mem_prompt': nrm((BATCH, MEM_LEN, D), 1.0),
        'state_l0_gdn_conv': nrm((DEC_BATCH, CONV_W - 1, A_CONV_CH), 1.0),
        'state_l0_gdn': nrm((DEC_BATCH, A_HEADS, A_DK, A_DV), 0.1),
        'cache_l0_band_k': nrm((DEC_BATCH, band_cache, B_HEADS, B_HD), 1.0),
        'cache_l0_band_v': nrm((DEC_BATCH, band_cache, B_HEADS, B_HD), 1.0),
        'cache_l0_mem_k': nrm((DEC_BATCH, MEM_LEN, M_HEADS, M_HD), 1.0),
        'cache_l0_mem_v': nrm((DEC_BATCH, MEM_LEN, M_HEADS, M_HD), 1.0),
        'state_l1_gla': nrm((DEC_BATCH, C_HEADS, C_DK, C_DV), 0.1),
        'cache_l1_mem_k': nrm((DEC_BATCH, MEM_LEN, M_HEADS, M_HD), 1.0),
        'cache_l1_mem_v': nrm((DEC_BATCH, MEM_LEN, M_HEADS, M_HD), 1.0),
        'l0_norm_g': gain(D),
        'l0_w_in': nrm((D, L0_IN), D ** -0.5),
        'l0_conv_w': nrm((CONV_W, A_CONV_CH), CONV_W ** -0.5),
        'l0_a_log': jnp.log(jax.random.uniform(next(keys), (A_HEADS,), f32, 1.0, 16.0)),
        'l0_dt_bias': mixer_dt_bias(),
        'l0_a_onorm_g': gain(A_DV),
        'l0_b_q_g': gain(B_HD),
        'l0_b_k_g': gain(B_HD),
        'l0_b_rel_bias': nrm((B_HEADS, 2 * MAX_REL + 1), 0.2),
        'l0_w_out': nrm((A_VW + B_W, D), (A_VW + B_W) ** -0.5),
        'l0_mnorm_g': gain(D),
        'l0_mem_norm_g': gain(D),
        'l0_w_mkv': nrm((D, 2 * M_W), D ** -0.5),
        'l0_mk_g': gain(M_HD),
        'l0_w_mq': nrm((D, 2 * M_W), D ** -0.5),
        'l0_mq_g': gain(M_HD),
        'l0_w_mo': nrm((M_W, D), M_W ** -0.5),
        'l1_norm_g': gain(D),
        'l1_w_in': nrm((D, L1_IN), D ** -0.5),
        'l1_w_gate_up': nrm((GATE_RANK, C_QK), GATE_RANK ** -0.5),
        'l1_gate_bias': nrm((C_QK,), 0.1),
        'l1_c_onorm_g': gain(C_DV),
        'l1_w_out': nrm((C_VW, D), C_VW ** -0.5),
        'l1_mnorm_g': gain(D),
        'l1_mem_norm_g': gain(D),
        'l1_w_mkv': nrm((D, 2 * M_W), D ** -0.5),
        'l1_mk_g': gain(M_HD),
        'l1_w_mq': nrm((D, 2 * M_W), D ** -0.5),
        'l1_mq_g': gain(M_HD),
        'l1_w_mo': nrm((M_W, D), M_W ** -0.5),
    }


def reference(x_prompt, x_sample, mem_prompt, state_l0_gdn_conv, state_l0_gdn, cache_l0_band_k,
              cache_l0_band_v, cache_l0_mem_k, cache_l0_mem_v, state_l1_gla, cache_l1_mem_k,
              cache_l1_mem_v, l0_norm_g, l0_w_in, l0_conv_w, l0_a_log, l0_dt_bias, l0_a_onorm_g,
              l0_b_q_g, l0_b_k_g, l0_b_rel_bias, l0_w_out, l0_mnorm_g, l0_mem_norm_g, l0_w_mkv,
              l0_mk_g, l0_w_mq, l0_mq_g, l0_w_mo, l1_norm_g, l1_w_in, l1_w_gate_up, l1_gate_bias,
              l1_c_onorm_g, l1_w_out, l1_mnorm_g, l1_mem_norm_g, l1_w_mkv, l1_mk_g, l1_w_mq,
              l1_mq_g, l1_w_mo):
    f32 = jnp.float32
    mix_params = (
        (l0_w_in, l0_conv_w, l0_a_log, l0_dt_bias, l0_a_onorm_g, l0_b_q_g, l0_b_k_g, l0_b_rel_bias, l0_w_out),
        (l1_w_in, l1_w_gate_up, l1_gate_bias, l1_c_onorm_g, l1_w_out),
    )
    norm_g = (l0_norm_g, l1_norm_g)
    mnorm_g = (l0_mnorm_g, l1_mnorm_g)
    mem_in_params = ((l0_mem_norm_g, l0_w_mkv, l0_mk_g), (l1_mem_norm_g, l1_w_mkv, l1_mk_g))
    mem_q_params = ((l0_w_mq, l0_mq_g, l0_w_mo), (l1_w_mq, l1_mq_g, l1_w_mo))
    sample_mem = ((cache_l0_mem_k, cache_l0_mem_v), (cache_l1_mem_k, cache_l1_mem_v))
    sample_rec = ((state_l0_gdn_conv, state_l0_gdn, cache_l0_band_k, cache_l0_band_v), (state_l1_gla,))

    n_p = x_prompt.shape[0]
    yp, ys = x_prompt, x_sample
    prompt_new, sample_new = [], []
    for l in range(DEPTH):
        hp = rms_norm(yp, norm_g[l])
        hs = rms_norm(ys, norm_g[l])
        if l % 2 == 0:
            buf0 = jnp.zeros((n_p, CONV_W - 1, A_CONV_CH), hp.dtype)
            s0 = jnp.zeros((n_p, A_HEADS, A_DK, A_DV), f32)
            mp, *stp = layer0_mixer(hp, buf0, s0, None, None, *mix_params[l], prompt=True)
            ms, *sts = layer0_mixer(hs, *sample_rec[l], *mix_params[l], prompt=False)
        else:
            s0 = jnp.zeros((n_p, C_HEADS, C_DK, C_DV), f32)
            mp, *stp = layer1_mixer(hp, s0, *mix_params[l], prompt=True)
            ms, *sts = layer1_mixer(hs, *sample_rec[l], *mix_params[l], prompt=False)
        yp = yp + mp
        ys = ys + ms
        mk_p, mv_p = memory_kv(mem_prompt, *mem_in_params[l])
        yp = yp + memory_attend(rms_norm(yp, mnorm_g[l]), mk_p, mv_p, *mem_q_params[l])
        ys = ys + memory_attend(rms_norm(ys, mnorm_g[l]), *sample_mem[l], *mem_q_params[l])
        prompt_new += [*stp, mk_p, mv_p]
        sample_new += sts
    p_gdn_conv, p_gdn, p_band_k, p_band_v, p_mem0_k, p_mem0_v, p_gla, p_mem1_k, p_mem1_v = prompt_new
    s_gdn_conv, s_gdn, s_band_k, s_band_v, s_gla = sample_new
    return (yp, ys, p_gdn_conv, p_gdn, p_band_k, p_band_v, p_mem0_k, p_mem0_v, p_gla, p_mem1_k,
            p_mem1_v, s_gdn_conv, s_gdn, s_band_k, s_band_v, s_gla)
```

```python
import functools

import jax
import jax.numpy as jnp
import numpy as np
from jax import lax
from jax.experimental import pallas as pl
from jax.experimental.pallas import tpu as pltpu

F32 = jnp.float32
BF16 = jnp.bfloat16
NORM_EPS = 1e-6

D_MODEL_ = 1024
CHUNK_ = 64
CONV_TAPS = 4
GDN_HEADS = 8
GDN_DK = 128
BAND_HEADS = 16
BAND_HD = 64
BAND_PAST = 512
BAND_MAX_REL = 128
GLA_HEADS = 8
GLA_DK = 128
GLA_DV = 256
GLA_RANK = 16
GLA_TAU = 16.0
GLA_SUB = 16
MEM_HEADS = 4
MEM_HD = 128
INV_SUB = 16
LANES = 128
VMEM_LIMIT = 56 * 1024 * 1024


def _dot(a, b):
    return jnp.dot(a, b, preferred_element_type=F32)


def _dot_nt(a, b):
    return lax.dot_general(a, b, (((1,), (1,)), ((), ())), preferred_element_type=F32)


def _dot_tn(a, b):
    return lax.dot_general(a, b, (((0,), (0,)), ((), ())), preferred_element_type=F32)


def _split2(x):
    hi = x.astype(BF16)
    lo = (x - hi.astype(F32)).astype(BF16)
    return hi, lo


def _split3(x):
    hi = x.astype(BF16)
    r = x - hi.astype(F32)
    mid = r.astype(BF16)
    lo = (r - mid.astype(F32)).astype(BF16)
    return hi, mid, lo


def _dot_x3(a, b):
    ah, al = _split2(a)
    bh, bl = _split2(b)
    return _dot(ah, bh) + (_dot(ah, bl) + _dot(al, bh))


def _dot_exact_lhs(a_bf, b):
    h, m, l = _split3(b)
    return _dot(a_bf, h) + (_dot(a_bf, m) + _dot(a_bf, l))


def _rms(x, g):
    ms = jnp.mean(x * x, axis=-1, keepdims=True)
    return x * lax.rsqrt(ms + NORM_EPS) * g


def _silu(x):
    return x * jax.nn.sigmoid(x)


def _softplus(x):
    return jnp.maximum(x, 0.0) + jnp.log1p(jnp.exp(-jnp.abs(x)))


def _softmax_rows(s):
    m = jnp.max(s, axis=-1, keepdims=True)
    e = jnp.exp(s - m)
    return e * (1.0 / jnp.sum(e, axis=-1, keepdims=True))


def _const_spec(shape):
    nd = len(shape)
    return pl.BlockSpec(shape, lambda *_: (0,) * nd, pipeline_mode=pl.Buffered(1))


def _params(sem):
    return pltpu.CompilerParams(dimension_semantics=sem, vmem_limit_bytes=VMEM_LIMIT)


def _row_tile(n, want):
    t = min(n, want)
    assert n % t == 0
    return t


def _proj0_body(x_ref, g_ref, wu_ref, wab_ref, wza_ref, wq_ref, wk_ref, wv_ref, wzb_ref,
                alog_ref, dtb_ref, u_ref, gb_ref, za_ref, q_ref, k_ref, v_ref, zb_ref):
    h = _rms(x_ref[...], g_ref[...]).astype(BF16)
    u_ref[...] = _dot(h, wu_ref[...])
    ab = _dot(h, wab_ref[...])
    lane = lax.broadcasted_iota(jnp.int32, ab.shape, 1)
    gval = -jnp.exp(alog_ref[...]) * _softplus(ab + dtb_ref[...])
    gb_ref[...] = jnp.where(lane < GDN_HEADS, gval, jax.nn.sigmoid(ab))
    za_ref[...] = _silu(_dot(h, wza_ref[...]))
    q_ref[...] = _dot(h, wq_ref[...])
    k_ref[...] = _dot(h, wk_ref[...])
    v_ref[...] = _dot(h, wv_ref[...])
    zb_ref[...] = _silu(_dot(h, wzb_ref[...]))


def _proj0(x2, g, w):
    n, d = x2.shape
    tm = _row_tile(n, 256)
    widths = (3 * 1024, LANES, 1024, 1024, 1024, 1024, 1024)
    row = lambda wd: pl.BlockSpec((tm, wd), lambda i: (i, 0))
    return pl.pallas_call(
        _proj0_body,
        grid=(n // tm,),
        in_specs=[row(d), _const_spec((1, d))]
        + [_const_spec((d, wd)) for wd in widths]
        + [_const_spec((1, LANES)), _const_spec((1, LANES))],
        out_specs=[row(wd) for wd in widths],
        out_shape=[jax.ShapeDtypeStruct((n, wd), F32) for wd in widths],
        compiler_params=_params(("arbitrary",)),
        name="proj0",
    )(x2, g, w["wu"], w["wab"], w["wza"], w["wq"], w["wk"], w["wv"], w["wzb"], w["alog"], w["dtb"])


def _proj1_body(x_ref, g_ref, wq_ref, wk_ref, wv_ref, wlr_ref, wz_ref, wg_ref, gbias_ref,
                q_ref, k_ref, v_ref, lg_ref, z_ref):
    h = _rms(x_ref[...], g_ref[...]).astype(BF16)
    q_ref[...] = _dot(h, wq_ref[...])
    k_ref[...] = _dot(h, wk_ref[...])
    v_ref[...] = _dot(h, wv_ref[...])
    lr = _dot(h, wlr_ref[...])
    pre = _dot(lr.astype(BF16), wg_ref[...]) + gbias_ref[...]
    lg_ref[...] = -_softplus(-pre) * (1.0 / GLA_TAU)
    z_ref[...] = _silu(_dot(h, wz_ref[...]))


def _proj1(x2, g, w):
    n, d = x2.shape
    tm = _row_tile(n, 256)
    row = lambda wd: pl.BlockSpec((tm, wd), lambda i: (i, 0))
    outw = (1024, 1024, 2048, 1024, 2048)
    return pl.pallas_call(
        _proj1_body,
        grid=(n // tm,),
        in_specs=[row(d), _const_spec((1, d)), _const_spec((d, 1024)), _const_spec((d, 1024)),
                  _const_spec((d, 2048)), _const_spec((d, LANES)), _const_spec((d, 2048)),
                  _const_spec((LANES, 1024)), _const_spec((1, 1024))],
        out_specs=[row(wd) for wd in outw],
        out_shape=[jax.ShapeDtypeStruct((n, wd), F32) for wd in outw],
        compiler_params=_params(("arbitrary",)),
        name="proj1",
    )(x2, g, w["wq"], w["wk"], w["wv"], w["wlr"], w["wz"], w["wg"], w["gbias"])


def _memkv_body(m_ref, g_ref, w_ref, kg_ref, k_ref, v_ref):
    h = _rms(m_ref[...], g_ref[...]).astype(BF16)
    kv = _dot(h, w_ref[...])
    hw = MEM_HEADS * MEM_HD
    for hh in range(MEM_HEADS):
        sl = slice(hh * MEM_HD, (hh + 1) * MEM_HD)
        k_ref[:, sl] = _rms(kv[:, sl], kg_ref[...])
    v_ref[...] = kv[:, hw:]


def _memkv(m2, g, w_bf, kg):
    n, d = m2.shape
    tm = _row_tile(n, 256)
    hw = MEM_HEADS * MEM_HD
    row = lambda wd: pl.BlockSpec((tm, wd), lambda i: (i, 0))
    return pl.pallas_call(
        _memkv_body,
        grid=(n // tm,),
        in_specs=[row(d), _const_spec((1, d)), _const_spec((d, 2 * hw)), _const_spec((1, MEM_HD))],
        out_specs=[row(hw), row(hw)],
        out_shape=[jax.ShapeDtypeStruct((n, hw), F32)] * 2,
        compiler_params=_params(("arbitrary",)),
        name="memkv",
    )(m2, g, w_bf, kg)


def _unit_lower_inverse(a, eye, bd):
    mm = lambda x, y: _dot(x.astype(BF16), y.astype(BF16))
    d = jnp.where(bd, a, 0.0)
    nl = a - d
    d2 = mm(d, d)
    d4 = mm(d2, d2)
    d8 = mm(d4, d4)
    td = mm(mm(mm(eye - d, eye + d2), eye + d4), eye + d8)
    m = mm(td, nl)
    m2 = mm(m, m)
    t0 = mm(mm(eye - m, eye + m2), td)
    r = eye - t0 - _dot_x3(a, t0)
    return t0 + mm(t0, r)


def _gdn_body(u_ref, gb_ref, za_ref, cinit_ref, sinit_ref, cw_ref, og_ref,
              o_ref, s_ref, ubuf, *, L):
    c = pl.program_id(1)
    ns = 3 * GDN_HEADS
    hist = 8

    @pl.when(c == 0)
    def _():
        for j in range(ns):
            ubuf[j, 0:hist, :] = cinit_ref[0, :, j * LANES:(j + 1) * LANES]
        s_ref[...] = sinit_ref[...]

    for j in range(ns):
        ubuf[j, hist:hist + L, :] = u_ref[0, :, j * LANES:(j + 1) * LANES]
    base = hist - (CONV_TAPS - 1)
    y = ubuf[:, base:base + L, :] * cw_ref[:, 0:1, :]
    for i in range(1, CONV_TAPS):
        y = y + ubuf[:, base + i:base + i + L, :] * cw_ref[:, i:i + 1, :]
    ubuf[:, base:hist, :] = ubuf[:, base + L:hist + L, :]
    y = _silu(y)

    gbv = gb_ref[0]
    row = lax.broadcasted_iota(jnp.int32, (L, L), 0)
    col = lax.broadcasted_iota(jnp.int32, (L, L), 1)
    incl = row >= col
    strict = row > col
    sub_shift = INV_SUB.bit_length() - 1
    bd = lax.shift_right_logical(row, sub_shift) == lax.shift_right_logical(col, sub_shift)
    eye = jnp.where(row == col, 1.0, 0.0).astype(F32)
    tri = jnp.where(incl, 1.0, 0.0).astype(BF16)
    gcum = _dot_exact_lhs(tri, gbv)
    gpad = jnp.concatenate([gcum, jnp.zeros((LANES - L, LANES), F32)], axis=0)
    gt = gpad.T
    og = og_ref[...]

    for h in range(GDN_HEADS):
        sl = slice(h * LANES, (h + 1) * LANES)
        q = y[h]
        k = y[GDN_HEADS + h]
        v = y[2 * GDN_HEADS + h]
        q = q * lax.rsqrt(jnp.sum(q * q, axis=-1, keepdims=True) + NORM_EPS) * (GDN_DK ** -0.5)
        k = k * lax.rsqrt(jnp.sum(k * k, axis=-1, keepdims=True) + NORM_EPS)
        gc = gcum[:, h:h + 1]
        gr = gt[h:h + 1, 0:L]
        bc = gbv[:, GDN_HEADS + h:GDN_HEADS + h + 1]
        dec = jnp.where(incl, jnp.exp(jnp.where(incl, gc - gr, 0.0)), 0.0)
        kb = k.astype(BF16)
        qb = q.astype(BF16)
        kk = _dot_nt(kb, kb)
        a = jnp.where(strict, bc * kk * dec, 0.0)
        t = _unit_lower_inverse(a, eye, bd)
        eg = jnp.exp(gc)
        rhs = jnp.concatenate([bc * v, (bc * eg) * k], axis=1)
        sol = _dot_x3(t, rhs)
        s = s_ref[0, h]
        sb = s.astype(BF16)
        u = sol[:, :LANES] - _dot(sol[:, LANES:].astype(BF16), sb)
        ub = u.astype(BF16)
        qk = _dot_nt(qb, kb) * dec
        o = _dot(qb, sb) * eg + _dot(qk.astype(BF16), ub)
        gl = gc[L - 1:L, :]
        kd = (k * jnp.exp(gl - gc)).astype(BF16)
        s_ref[0, h] = s * jnp.exp(gl) + _dot_tn(kd, ub)
        o_ref[0, :, sl] = (_rms(o, og) * za_ref[0, :, sl]).astype(BF16)


def _gdn(u, gb, za, cinit, sinit, cw, og, L):
    b, s, _ = u.shape
    nc = s // L
    ns = 3 * GDN_HEADS
    blk = lambda wd: pl.BlockSpec((1, L, wd), lambda i, j: (i, j, 0))
    return pl.pallas_call(
        functools.partial(_gdn_body, L=L),
        grid=(b, nc),
        in_specs=[blk(ns * LANES), blk(LANES), blk(GDN_HEADS * LANES),
                  pl.BlockSpec((1, 8, ns * LANES), lambda i, j: (i, 0, 0)),
                  pl.BlockSpec((1, GDN_HEADS, GDN_DK, LANES), lambda i, j: (i, 0, 0, 0)),
                  _const_spec((ns, 8, LANES)), _const_spec((1, LANES))],
        out_specs=[blk(GDN_HEADS * LANES),
                   pl.BlockSpec((1, GDN_HEADS, GDN_DK, LANES), lambda i, j: (i, 0, 0, 0))],
        out_shape=[jax.ShapeDtypeStruct((b, s, GDN_HEADS * LANES), BF16),
                   jax.ShapeDtypeStruct((b, GDN_HEADS, GDN_DK, LANES), F32)],
        scratch_shapes=[pltpu.VMEM((ns, 8 + L, LANES), F32)],
        compiler_params=_params(("arbitrary", "arbitrary")),
        name="gdn",
    )(u, gb, za, cinit, sinit, cw, og)


def _head_rms64(x, g, seg, expand):
    h2, l2 = _split2(x * x)
    ss = _dot(h2, seg) + _dot(l2, seg)
    r = lax.rsqrt(ss * (1.0 / BAND_HD) + NORM_EPS)
    rh, rl = _split2(r)
    return x * (_dot(rh, expand) + _dot(rl, expand)) * g


def _band_body(q_ref, k_ref, v_ref, zb_ref, ck_ref, cv_ref, qg_ref, kg_ref, bias_ref, seg_ref, exp_ref,
               o_ref, kn_ref, kscr, vlo, vhi, *, L, W, masked):
    c = pl.program_id(1)
    dm = BAND_HEADS * BAND_HD
    lane = lax.broadcasted_iota(jnp.int32, (1, dm), 1)
    lo_full = (lane & (LANES - 1)) < BAND_HD
    lo = lo_full[:, :LANES]

    @pl.when(c == 0)
    def _():
        kscr[0:BAND_PAST, :] = ck_ref[0].astype(BF16)
        cv = cv_ref[0]
        vlo[0:BAND_PAST, :] = jnp.where(lo_full, cv, 0.0).astype(BF16)
        vhi[0:BAND_PAST, :] = jnp.where(lo_full, 0.0, cv).astype(BF16)

    seg = seg_ref[...]
    expand = exp_ref[...]
    qn = _head_rms64(q_ref[0], qg_ref[...], seg, expand)
    kn = _head_rms64(k_ref[0], kg_ref[...], seg, expand)
    kn_ref[0] = kn
    new0 = pl.multiple_of(BAND_PAST + c * L, L)
    kscr[pl.ds(new0, L), :] = kn.astype(BF16)
    v = v_ref[0]
    vlo[pl.ds(new0, L), :] = jnp.where(lo_full, v, 0.0).astype(BF16)
    vhi[pl.ds(new0, L), :] = jnp.where(lo_full, 0.0, v).astype(BF16)

    w0 = pl.multiple_of(c * L, L)
    if masked:
        wcol = lax.broadcasted_iota(jnp.int32, (1, W), 1)
        valid = (wcol + c * L) >= BAND_PAST
    scale = BAND_HD ** -0.5
    for s in range(BAND_HEADS // 2):
        sl = slice(s * LANES, (s + 1) * LANES)
        ks = kscr[pl.ds(w0, W), sl]
        qs = qn[:, sl]
        acc = jnp.zeros((L, LANES), F32)
        for t, vref in enumerate((vlo, vhi)):
            qh = (jnp.where(lo, qs, 0.0) if t == 0 else jnp.where(lo, 0.0, qs)).astype(BF16)
            sc = _dot_nt(qh, ks) * scale + bias_ref[2 * s + t]
            if masked:
                sc = jnp.where(valid, sc, -jnp.inf)
            p = _softmax_rows(sc).astype(BF16)
            acc = acc + _dot(p, vref[pl.ds(w0, W), sl])
        o_ref[0, :, sl] = (acc * zb_ref[0, :, sl]).astype(BF16)


def _band(q, k, v, zb, ck, cv, qg, kg, bias, seg, expand, L, masked):
    b, s, dm = q.shape
    nc = s // L
    W = BAND_PAST + L
    blk = pl.BlockSpec((1, L, dm), lambda i, j: (i, j, 0))
    cache = pl.BlockSpec((1, BAND_PAST, dm), lambda i, j: (i, 0, 0))
    return pl.pallas_call(
        functools.partial(_band_body, L=L, W=W, masked=masked),
        grid=(b, nc),
        in_specs=[blk, blk, blk, blk, cache, cache, _const_spec((1, dm)), _const_spec((1, dm)),
                  _const_spec((BAND_HEADS, L, W)), _const_spec((dm, LANES)), _const_spec((LANES, dm))],
        out_specs=[blk, blk],
        out_shape=[jax.ShapeDtypeStruct((b, s, dm), BF16), jax.ShapeDtypeStruct((b, s, dm), F32)],
        scratch_shapes=[pltpu.VMEM((BAND_PAST + s, dm), BF16)] * 3,
        compiler_params=_params(("arbitrary", "arbitrary")),
        name="band",
    )(q, k, v, zb, ck, cv, qg, kg, bias, seg, expand)


def _gla_body(q_ref, k_ref, v_ref, lg_ref, z_ref, sinit_ref, og_ref, o_ref, st_ref, *, L):
    c = pl.program_id(1)

    @pl.when(c == 0)
    def _():
        st_ref[...] = sinit_ref[...]

    nb = L // GLA_SUB
    row = lax.broadcasted_iota(jnp.int32, (L, L), 0)
    col = lax.broadcasted_iota(jnp.int32, (L, L), 1)
    tri = jnp.where(row >= col, 1.0, 0.0).astype(BF16)
    cb_all = _dot_exact_lhs(tri, lg_ref[0])
    lane = lax.broadcasted_iota(jnp.int32, (GLA_SUB, LANES), 1)
    sub_row = lax.broadcasted_iota(jnp.int32, (GLA_SUB, LANES), 0)
    og = og_ref[...]
    scale = GLA_DK ** -0.5

    for h in range(GLA_HEADS):
        sl = slice(h * GLA_DK, (h + 1) * GLA_DK)
        vsl = slice(h * GLA_DV, (h + 1) * GLA_DV)
        cb = cb_all[:, sl]
        q = q_ref[0, :, sl] * scale
        k = k_ref[0, :, sl]
        vb = v_ref[0, :, vsl].astype(BF16)
        st = st_ref[0, h]
        o = _dot_nt((q * jnp.exp(cb)).astype(BF16), st.astype(BF16))
        strips = []
        for ib in range(nb):
            r0 = ib * GLA_SUB
            qi = q[r0:r0 + GLA_SUB]
            cbi = cb[r0:r0 + GLA_SUB]
            strip = jnp.zeros((GLA_SUB, LANES), F32)
            if ib > 0:
                ref = cb[r0 - 1:r0]
                qe = (qi * jnp.exp(cbi - ref)).astype(BF16)
                ke = (k * jnp.exp(jnp.minimum(ref - cb, 0.0))).astype(BF16)
                off = _dot_nt(qe, ke)
                if L < LANES:
                    off = jnp.concatenate([off, jnp.zeros((GLA_SUB, LANES - L), F32)], axis=1)
                strip = jnp.where(lane < r0, off, 0.0)
            for jj in range(GLA_SUB):
                j = r0 + jj
                w = qi * k[j:j + 1] * jnp.exp(jnp.minimum(cbi - cb[j:j + 1], 0.0))
                colv = jnp.sum(w, axis=-1, keepdims=True)
                strip = strip + jnp.where((lane == j) & (sub_row >= jj), colv, 0.0)
            strips.append(strip)
        att = jnp.concatenate(strips, axis=0)[:, :L]
        o = o + _dot(att.astype(BF16), vb)
        cl = cb[L - 1:L]
        ke = (k * jnp.exp(cl - cb)).astype(BF16)
        st_ref[0, h] = st * jnp.exp(cl) + _dot_tn(vb, ke)
        o_ref[0, :, vsl] = (_rms(o, og) * z_ref[0, :, vsl]).astype(BF16)


def _gla(q, k, v, lg, z, sinit_t, og, L):
    b, s, _ = q.shape
    nc = s // L
    qk = pl.BlockSpec((1, L, GLA_HEADS * GLA_DK), lambda i, j: (i, j, 0))
    vv = pl.BlockSpec((1, L, GLA_HEADS * GLA_DV), lambda i, j: (i, j, 0))
    st = pl.BlockSpec((1, GLA_HEADS, GLA_DV, GLA_DK), lambda i, j: (i, 0, 0, 0))
    return pl.pallas_call(
        functools.partial(_gla_body, L=L),
        grid=(b, nc),
        in_specs=[qk, qk, vv, qk, vv, st, _const_spec((1, GLA_DV))],
        out_specs=[vv, st],
        out_shape=[jax.ShapeDtypeStruct((b, s, GLA_HEADS * GLA_DV), BF16),
                   jax.ShapeDtypeStruct((b, GLA_HEADS, GLA_DV, GLA_DK), F32)],
        compiler_params=_params(("arbitrary", "arbitrary")),
        name="gla",
    )(q, k, v, lg, z, sinit_t, og)


def _post_body(*refs, n_o):
    x_ref = refs[0]
    o_refs = refs[1:1 + n_o]
    wout_ref, mg_ref, wmq_ref, mqg_ref, mk_ref, mv_ref, wmo_ref, y_ref = refs[1 + n_o:]
    acc = x_ref[0]
    off = 0
    for o_ref in o_refs:
        kd = o_ref.shape[-1]
        acc = acc + _dot(o_ref[0], wout_ref[off:off + kd, :])
        off += kd
    hm = _rms(acc, mg_ref[...]).astype(BF16)
    qz = _dot(hm, wmq_ref[...])
    hw = MEM_HEADS * MEM_HD
    outs = []
    for h in range(MEM_HEADS):
        sl = slice(h * MEM_HD, (h + 1) * MEM_HD)
        qn = _rms(qz[:, sl], mqg_ref[...]).astype(BF16)
        sc = _dot_nt(qn, mk_ref[0, :, sl].astype(BF16)) * (MEM_HD ** -0.5)
        p = _softmax_rows(sc).astype(BF16)
        oh = _dot(p, mv_ref[0, :, sl].astype(BF16))
        outs.append(oh * _silu(qz[:, hw + h * MEM_HD:hw + (h + 1) * MEM_HD]))
    om = jnp.concatenate(outs, axis=1).astype(BF16)
    y_ref[0] = acc + _dot(om, wmo_ref[...])


def _post(x, os_, wout, mg, wmq, mqg, mk, mv, wmo):
    b, s, d = x.shape
    tm = _row_tile(s, 256)
    hw = MEM_HEADS * MEM_HD
    nm = mk.shape[1]
    blk = lambda wd: pl.BlockSpec((1, tm, wd), lambda i, j: (i, j, 0))
    mem = pl.BlockSpec((1, nm, hw), lambda i, j: (i, 0, 0))
    kin = sum(o.shape[-1] for o in os_)
    return pl.pallas_call(
        functools.partial(_post_body, n_o=len(os_)),
        grid=(b, s // tm),
        in_specs=[blk(d)] + [blk(o.shape[-1]) for o in os_]
        + [_const_spec((kin, d)), _const_spec((1, d)), _const_spec((d, 2 * hw)), _const_spec((1, MEM_HD)),
           mem, mem, _const_spec((hw, d))],
        out_specs=blk(d),
        out_shape=jax.ShapeDtypeStruct((b, s, d), F32),
        compiler_params=_params(("arbitrary", "arbitrary")),
        name="post",
    )(x, *os_, wout, mg, wmq, mqg, mk, mv, wmo)


def _pad_cols(w, n):
    return jnp.pad(w, ((0, 0), (0, n - w.shape[1])))


def _prep_l0(w_in, conv_w, a_log, dt_bias, rel_bias, q_g, k_g):
    c0 = 3 * 1024
    ab = w_in[:, c0:c0 + 2 * GDN_HEADS]
    c1 = c0 + 2 * GDN_HEADS
    cols = lambda i: w_in[:, c1 + i * 1024:c1 + (i + 1) * 1024].astype(BF16)
    w = dict(
        wu=w_in[:, :c0].astype(BF16), wab=_pad_cols(ab, LANES).astype(BF16),
        wza=cols(0), wq=cols(1), wk=cols(2), wv=cols(3), wzb=cols(4),
        alog=jnp.pad(a_log, (0, LANES - GDN_HEADS)).reshape(1, LANES).astype(F32),
        dtb=jnp.pad(dt_bias, (0, LANES - GDN_HEADS)).reshape(1, LANES).astype(F32),
    )
    ns = 3 * GDN_HEADS
    cw = jnp.pad(conv_w.astype(F32), ((0, 8 - CONV_TAPS), (0, 0))).reshape(8, ns, LANES).transpose(1, 0, 2)
    dm = BAND_HEADS * BAND_HD
    head_of = np.arange(dm) // BAND_HD
    seg = jnp.asarray(head_of[:, None] == np.arange(LANES)[None, :], BF16)
    expand = jnp.asarray(np.arange(LANES)[:, None] == head_of[None, :], BF16)
    qg = jnp.tile(q_g.astype(F32), BAND_HEADS).reshape(1, dm)
    kg = jnp.tile(k_g.astype(F32), BAND_HEADS).reshape(1, dm)
    return w, cw, seg, expand, qg, kg


def _band_bias_table(rel_bias, L):
    W = BAND_PAST + L
    dist = np.arange(L)[:, None] + BAND_PAST - np.arange(W)[None, :]
    idx = np.clip(dist, -BAND_MAX_REL, BAND_MAX_REL) + BAND_MAX_REL
    return rel_bias.astype(F32)[:, idx]


def _prep_l1(w_in, w_gate_up, gate_bias):
    qk = GLA_HEADS * GLA_DK
    vw = GLA_HEADS * GLA_DV
    o = np.cumsum([0, qk, qk, vw, GLA_RANK, vw])
    return dict(
        wq=w_in[:, o[0]:o[1]].astype(BF16), wk=w_in[:, o[1]:o[2]].astype(BF16),
        wv=w_in[:, o[2]:o[3]].astype(BF16), wlr=_pad_cols(w_in[:, o[3]:o[4]], LANES).astype(BF16),
        wz=w_in[:, o[4]:o[5]].astype(BF16),
        wg=jnp.pad(w_gate_up, ((0, LANES - GLA_RANK), (0, 0))).astype(BF16),
        gbias=gate_bias.reshape(1, qk).astype(F32),
    )


def _layer0(x, conv_state, gdn_state, band_k, band_v, mk, mv, norm_g, pw, cw, seg, expand, qg, kg, bias,
            a_onorm_g, w_out, mnorm_g, w_mq, mq_g, w_mo, L, masked):
    b, s, d = x.shape
    u, gb, za, q, k, v, zb = _proj0(x.reshape(b * s, d), norm_g.reshape(1, d), pw)
    r3 = lambda t: t.reshape(b, s, t.shape[-1])
    u, gb, za, q, k, v, zb = map(r3, (u, gb, za, q, k, v, zb))
    cinit = jnp.pad(conv_state.astype(F32), ((0, 0), (8 - (CONV_TAPS - 1), 0), (0, 0)))
    o_a, s_new = _gdn(u, gb, za, cinit, gdn_state.astype(F32), cw, a_onorm_g.reshape(1, LANES), L)
    o_b, kn = _band(q, k, v, zb, band_k, band_v, qg, kg, bias, seg, expand, L, masked)
    y = _post(x, (o_a, o_b), w_out, mnorm_g.reshape(1, d), w_mq, mq_g.reshape(1, MEM_HD), mk, mv, w_mo)
    return y, u[:, s - (CONV_TAPS - 1):, :], s_new, kn, v


def _layer1(x, gla_state, mk, mv, norm_g, pw, c_onorm_g, w_out, mnorm_g, w_mq, mq_g, w_mo, L):
    b, s, d = x.shape
    q, k, v, lg, z = _proj1(x.reshape(b * s, d), norm_g.reshape(1, d), pw)
    r3 = lambda t: t.reshape(b, s, t.shape[-1])
    q, k, v, lg, z = map(r3, (q, k, v, lg, z))
    st0 = jnp.swapaxes(gla_state.astype(F32), 2, 3)
    o, st = _gla(q, k, v, lg, z, st0, c_onorm_g.reshape(1, GLA_DV), L)
    y = _post(x, (o,), w_out, mnorm_g.reshape(1, d), w_mq, mq_g.reshape(1, MEM_HD), mk, mv, w_mo)
    return y, jnp.swapaxes(st, 2, 3)


def kernel(x_prompt, x_sample, mem_prompt, state_l0_gdn_conv, state_l0_gdn, cache_l0_band_k, cache_l0_band_v, cache_l0_mem_k, cache_l0_mem_v, state_l1_gla, cache_l1_mem_k, cache_l1_mem_v, l0_norm_g, l0_w_in, l0_conv_w, l0_a_log, l0_dt_bias, l0_a_onorm_g, l0_b_q_g, l0_b_k_g, l0_b_rel_bias, l0_w_out, l0_mnorm_g, l0_mem_norm_g, l0_w_mkv, l0_mk_g, l0_w_mq, l0_mq_g, l0_w_mo, l1_norm_g, l1_w_in, l1_w_gate_up, l1_gate_bias, l1_c_onorm_g, l1_w_out, l1_mnorm_g, l1_mem_norm_g, l1_w_mkv, l1_mk_g, l1_w_mq, l1_mq_g, l1_w_mo):
    bp, sp, d = x_prompt.shape
    bs, ss, _ = x_sample.shape
    nm = mem_prompt.shape[1]
    hw = MEM_HEADS * MEM_HD
    dm = BAND_HEADS * BAND_HD
    assert sp % CHUNK_ == 0 and ss % INV_SUB == 0 and ss <= CHUNK_
    assert cache_l0_band_k.shape[1] == BAND_PAST

    pw0, cw, seg, expand, qg, kg = _prep_l0(l0_w_in, l0_conv_w, l0_a_log, l0_dt_bias, l0_b_rel_bias,
                                            l0_b_q_g, l0_b_k_g)
    pw1 = _prep_l1(l1_w_in, l1_w_gate_up, l1_gate_bias)
    bias_p = _band_bias_table(l0_b_rel_bias, CHUNK_)
    bias_s = _band_bias_table(l0_b_rel_bias, ss)
    bf = lambda w: w.astype(BF16)
    mem2 = mem_prompt.reshape(bp * nm, d)

    mk0, mv0 = _memkv(mem2, l0_mem_norm_g.reshape(1, d), bf(l0_w_mkv), l0_mk_g.reshape(1, MEM_HD))
    mk0 = mk0.reshape(bp, nm, hw)
    mv0 = mv0.reshape(bp, nm, hw)
    l0_shared = (l0_norm_g, pw0, cw, seg, expand, qg, kg)
    l0_tail = (l0_a_onorm_g, bf(l0_w_out), l0_mnorm_g, bf(l0_w_mq), l0_mq_g, bf(l0_w_mo))
    zeros_conv = jnp.zeros((bp, CONV_TAPS - 1, 3 * GDN_HEADS * LANES), F32)
    zeros_gdn = jnp.zeros((bp, GDN_HEADS, GDN_DK, LANES), F32)
    zeros_band = jnp.zeros((bp, BAND_PAST, dm), F32)
    yp, p_conv, p_gdn, p_kn, p_v = _layer0(
        x_prompt, zeros_conv, zeros_gdn, zeros_band, zeros_band, mk0, mv0,
        *l0_shared, bias_p, *l0_tail, CHUNK_, True)
    ys, s_conv, s_gdn, s_kn, s_v = _layer0(
        x_sample, state_l0_gdn_conv, state_l0_gdn,
        cache_l0_band_k.reshape(bs, BAND_PAST, dm), cache_l0_band_v.reshape(bs, BAND_PAST, dm),
        cache_l0_mem_k.reshape(bs, nm, hw), cache_l0_mem_v.reshape(bs, nm, hw),
        *l0_shared, bias_s, *l0_tail, ss, False)

    mk1, mv1 = _memkv(mem2, l1_mem_norm_g.reshape(1, d), bf(l1_w_mkv), l1_mk_g.reshape(1, MEM_HD))
    mk1 = mk1.reshape(bp, nm, hw)
    mv1 = mv1.reshape(bp, nm, hw)
    l1_tail = (l1_c_onorm_g, bf(l1_w_out), l1_mnorm_g, bf(l1_w_mq), l1_mq_g, bf(l1_w_mo))
    zeros_gla = jnp.zeros((bp, GLA_HEADS, GLA_DK, GLA_DV), F32)
    yp, p_gla = _layer1(yp, zeros_gla, mk1, mv1, l1_norm_g, pw1, *l1_tail, CHUNK_)
    ys, s_gla = _layer1(ys, state_l1_gla, cache_l1_mem_k.reshape(bs, nm, hw),
                        cache_l1_mem_v.reshape(bs, nm, hw), l1_norm_g, pw1, *l1_tail, ss)

    keep = min(BAND_PAST, sp)
    h4 = lambda t: t.reshape(t.shape[0], t.shape[1], BAND_HEADS, BAND_HD)
    m4 = lambda t: t.reshape(bp, nm, MEM_HEADS, MEM_HD)
    return (yp, ys, p_conv, p_gdn, h4(p_kn[:, sp - keep:]), h4(p_v[:, sp - keep:]),
            m4(mk0), m4(mv0), p_gla, m4(mk1), m4(mv1),
            s_conv, s_gdn, h4(s_kn), h4(s_v), s_gla)
```

```python
import functools

import jax
import jax.numpy as jnp
import numpy as np
from jax import lax
from jax.experimental import pallas as pl
from jax.experimental.pallas import tpu as pltpu

F32 = jnp.float32
BF16 = jnp.bfloat16
NORM_EPS = 1e-6

D_MODEL_ = 1024
CHUNK_ = 64
CONV_TAPS = 4
GDN_HEADS = 8
GDN_DK = 128
BAND_HEADS = 16
BAND_HD = 64
BAND_PAST = 512
BAND_MAX_REL = 128
GLA_HEADS = 8
GLA_DK = 128
GLA_DV = 256
GLA_RANK = 16
GLA_TAU = 16.0
GLA_SUB = 16
MEM_HEADS = 4
MEM_HD = 128
INV_SUB = 16
LANES = 128
VMEM_LIMIT = 56 * 1024 * 1024


def _dot(a, b):
    return jnp.dot(a, b, preferred_element_type=F32)


def _dot_nt(a, b):
    return lax.dot_general(a, b, (((1,), (1,)), ((), ())), preferred_element_type=F32)


def _dot_tn(a, b):
    return lax.dot_general(a, b, (((0,), (0,)), ((), ())), preferred_element_type=F32)


def _split2(x):
    hi = x.astype(BF16)
    lo = (x - hi.astype(F32)).astype(BF16)
    return hi, lo


def _split3(x):
    hi = x.astype(BF16)
    r = x - hi.astype(F32)
    mid = r.astype(BF16)
    lo = (r - mid.astype(F32)).astype(BF16)
    return hi, mid, lo


def _dot_x3(a, b):
    ah, al = _split2(a)
    bh, bl = _split2(b)
    return _dot(ah, bh) + (_dot(ah, bl) + _dot(al, bh))


def _dot_exact_lhs(a_bf, b):
    h, m, l = _split3(b)
    return _dot(a_bf, h) + (_dot(a_bf, m) + _dot(a_bf, l))


def _rms(x, g):
    ms = jnp.mean(x * x, axis=-1, keepdims=True)
    return x * lax.rsqrt(ms + NORM_EPS) * g


def _silu(x):
    return x * jax.nn.sigmoid(x)


def _softplus(x):
    return jnp.maximum(x, 0.0) + jnp.log1p(jnp.exp(-jnp.abs(x)))


def _softmax_rows(s):
    m = jnp.max(s, axis=-1, keepdims=True)
    e = jnp.exp(s - m)
    return e * (1.0 / jnp.sum(e, axis=-1, keepdims=True))


def _const_spec(shape):
    nd = len(shape)
    return pl.BlockSpec(shape, lambda *_: (0,) * nd, pipeline_mode=pl.Buffered(1))


def _params(sem):
    return pltpu.CompilerParams(dimension_semantics=sem, vmem_limit_bytes=VMEM_LIMIT)


def _row_tile(n, want):
    t = min(n, want)
    assert n % t == 0
    return t


def _proj0_body(x_ref, g_ref, wu_ref, wab_ref, wza_ref, wq_ref, wk_ref, wv_ref, wzb_ref,
                alog_ref, dtb_ref, u_ref, gb_ref, za_ref, q_ref, k_ref, v_ref, zb_ref):
    h = _rms(x_ref[...], g_ref[...]).astype(BF16)
    u_ref[...] = _dot(h, wu_ref[...])
    ab = _dot(h, wab_ref[...])
    lane = lax.broadcasted_iota(jnp.int32, ab.shape, 1)
    gval = -jnp.exp(alog_ref[...]) * _softplus(ab + dtb_ref[...])
    gb_ref[...] = jnp.where(lane < GDN_HEADS, gval, jax.nn.sigmoid(ab))
    za_ref[...] = _silu(_dot(h, wza_ref[...]))
    q_ref[...] = _dot(h, wq_ref[...])
    k_ref[...] = _dot(h, wk_ref[...])
    v_ref[...] = _dot(h, wv_ref[...])
    zb_ref[...] = _silu(_dot(h, wzb_ref[...]))


def _proj0(x2, g, w):
    n, d = x2.shape
    tm = _row_tile(n, 256)
    widths = (3 * 1024, LANES, 1024, 1024, 1024, 1024, 1024)
    row = lambda wd: pl.BlockSpec((tm, wd), lambda i: (i, 0))
    return pl.pallas_call(
        _proj0_body,
        grid=(n // tm,),
        in_specs=[row(d), _const_spec((1, d))]
        + [_const_spec((d, wd)) for wd in widths]
        + [_const_spec((1, LANES)), _const_spec((1, LANES))],
        out_specs=[row(wd) for wd in widths],
        out_shape=[jax.ShapeDtypeStruct((n, wd), F32) for wd in widths],
        compiler_params=_params(("arbitrary",)),
        name="proj0",
    )(x2, g, w["wu"], w["wab"], w["wza"], w["wq"], w["wk"], w["wv"], w["wzb"], w["alog"], w["dtb"])


def _proj1_body(x_ref, g_ref, wq_ref, wk_ref, wv_ref, wlr_ref, wz_ref, wg_ref, gbias_ref,
                q_ref, k_ref, v_ref, lg_ref, z_ref):
    h = _rms(x_ref[...], g_ref[...]).astype(BF16)
    q_ref[...] = _dot(h, wq_ref[...])
    k_ref[...] = _dot(h, wk_ref[...])
    v_ref[...] = _dot(h, wv_ref[...])
    lr = _dot(h, wlr_ref[...])
    pre = _dot(lr.astype(BF16), wg_ref[...]) + gbias_ref[...]
    lg_ref[...] = -_softplus(-pre) * (1.0 / GLA_TAU)
    z_ref[...] = _silu(_dot(h, wz_ref[...]))


def _proj1(x2, g, w):
    n, d = x2.shape
    tm = _row_tile(n, 256)
    row = lambda wd: pl.BlockSpec((tm, wd), lambda i: (i, 0))
    outw = (1024, 1024, 2048, 1024, 2048)
    return pl.pallas_call(
        _proj1_body,
        grid=(n // tm,),
        in_specs=[row(d), _const_spec((1, d)), _const_spec((d, 1024)), _const_spec((d, 1024)),
                  _const_spec((d, 2048)), _const_spec((d, LANES)), _const_spec((d, 2048)),
                  _const_spec((LANES, 1024)), _const_spec((1, 1024))],
        out_specs=[row(wd) for wd in outw],
        out_shape=[jax.ShapeDtypeStruct((n, wd), F32) for wd in outw],
        compiler_params=_params(("arbitrary",)),
        name="proj1",
    )(x2, g, w["wq"], w["wk"], w["wv"], w["wlr"], w["wz"], w["wg"], w["gbias"])


def _memkv_body(m_ref, g_ref, w_ref, kg_ref, k_ref, v_ref):
    h = _rms(m_ref[...], g_ref[...]).astype(BF16)
    kv = _dot(h, w_ref[...])
    hw = MEM_HEADS * MEM_HD
    for hh in range(MEM_HEADS):
        sl = slice(hh * MEM_HD, (hh + 1) * MEM_HD)
        k_ref[:, sl] = _rms(kv[:, sl], kg_ref[...])
    v_ref[...] = kv[:, hw:]


def _memkv(m2, g, w_bf, kg):
    n, d = m2.shape
    tm = _row_tile(n, 256)
    hw = MEM_HEADS * MEM_HD
    row = lambda wd: pl.BlockSpec((tm, wd), lambda i: (i, 0))
    return pl.pallas_call(
        _memkv_body,
        grid=(n // tm,),
        in_specs=[row(d), _const_spec((1, d)), _const_spec((d, 2 * hw)), _const_spec((1, MEM_HD))],
        out_specs=[row(hw), row(hw)],
        out_shape=[jax.ShapeDtypeStruct((n, hw), F32)] * 2,
        compiler_params=_params(("arbitrary",)),
        name="memkv",
    )(m2, g, w_bf, kg)


def _each(fn, *lists):
    return [fn(*xs) for xs in zip(*lists)]


def _unit_lower_inverse(a, eye, bd):
    mm = lambda x, y: _dot(x.astype(BF16), y.astype(BF16))
    d = _each(lambda x: jnp.where(bd, x, 0.0), a)
    nl = _each(lambda x, y: x - y, a, d)
    d2 = _each(mm, d, d)
    d4 = _each(mm, d2, d2)
    d8 = _each(mm, d4, d4)
    td = _each(lambda x, y: mm(eye - x, eye + y), d, d2)
    td = _each(lambda x, y: mm(x, eye + y), td, d4)
    td = _each(lambda x, y: mm(x, eye + y), td, d8)
    m = _each(mm, td, nl)
    m2 = _each(mm, m, m)
    t0 = _each(lambda x, y: mm(eye - x, eye + y), m, m2)
    t0 = _each(mm, t0, td)
    at = _each(_dot_x3, a, t0)
    r = _each(lambda x, y: eye - x - y, t0, at)
    tr = _each(mm, t0, r)
    return _each(lambda x, y: x + y, t0, tr)


def _gdn_body(u_ref, gb_ref, za_ref, cinit_ref, sinit_ref, cw_ref, og_ref,
              o_ref, s_ref, ubuf, *, L):
    c = pl.program_id(1)
    ns = 3 * GDN_HEADS
    hist = 8

    @pl.when(c == 0)
    def _():
        for j in range(ns):
            ubuf[j, 0:hist, :] = cinit_ref[0, :, j * LANES:(j + 1) * LANES]
        s_ref[...] = sinit_ref[...]

    for j in range(ns):
        ubuf[j, hist:hist + L, :] = u_ref[0, :, j * LANES:(j + 1) * LANES]
    base = hist - (CONV_TAPS - 1)
    y = ubuf[:, base:base + L, :] * cw_ref[:, 0:1, :]
    for i in range(1, CONV_TAPS):
        y = y + ubuf[:, base + i:base + i + L, :] * cw_ref[:, i:i + 1, :]
    ubuf[:, base:hist, :] = ubuf[:, base + L:hist + L, :]
    y = _silu(y)

    gbv = gb_ref[0]
    row = lax.broadcasted_iota(jnp.int32, (L, L), 0)
    col = lax.broadcasted_iota(jnp.int32, (L, L), 1)
    incl = row >= col
    strict = row > col
    sub_shift = INV_SUB.bit_length() - 1
    bd = lax.shift_right_logical(row, sub_shift) == lax.shift_right_logical(col, sub_shift)
    eye = jnp.where(row == col, 1.0, 0.0).astype(F32)
    tri = jnp.where(incl, 1.0, 0.0).astype(BF16)
    gcum = _dot_exact_lhs(tri, gbv)
    gpad = jnp.concatenate([gcum, jnp.zeros((LANES - L, LANES), F32)], axis=0)
    gt = gpad.T
    og = og_ref[...]
    za = za_ref[0]
    heads = range(GDN_HEADS)
    l2n = lambda x: x * lax.rsqrt(jnp.sum(x * x, axis=-1, keepdims=True) + NORM_EPS)
    q = [l2n(y[h]) * (GDN_DK ** -0.5) for h in heads]
    k = [l2n(y[GDN_HEADS + h]) for h in heads]
    v = [y[2 * GDN_HEADS + h] for h in heads]
    gc = [gcum[:, h:h + 1] for h in heads]
    gr = [gt[h:h + 1, 0:L] for h in heads]
    bc = [gbv[:, GDN_HEADS + h:GDN_HEADS + h + 1] for h in heads]
    dec = _each(lambda c_, r_: jnp.where(incl, jnp.exp(jnp.where(incl, c_ - r_, 0.0)), 0.0), gc, gr)
    kb = _each(lambda x: x.astype(BF16), k)
    qb = _each(lambda x: x.astype(BF16), q)
    s = [s_ref[0, h] for h in heads]
    sb = _each(lambda x: x.astype(BF16), s)
    kk = _each(_dot_nt, kb, kb)
    qk = _each(_dot_nt, qb, kb)
    qs = _each(_dot, qb, sb)
    a = _each(lambda b_, kk_, d_: jnp.where(strict, b_ * kk_ * d_, 0.0), bc, kk, dec)
    t = _unit_lower_inverse(a, eye, bd)
    eg = _each(jnp.exp, gc)
    rhs = _each(lambda b_, v_, e_, k_: jnp.concatenate([b_ * v_, (b_ * e_) * k_], axis=1), bc, v, eg, k)
    sol = _each(_dot_x3, t, rhs)
    ks = _each(lambda x, y_: _dot(x[:, LANES:].astype(BF16), y_), sol, sb)
    ub = _each(lambda x, y_: (x[:, :LANES] - y_).astype(BF16), sol, ks)
    qku = _each(lambda x, d_, u_: _dot((x * d_).astype(BF16), u_), qk, dec, ub)
    gl = [c_[L - 1:L, :] for c_ in gc]
    kd = _each(lambda k_, l_, c_: (k_ * jnp.exp(l_ - c_)).astype(BF16), k, gl, gc)
    ktu = _each(_dot_tn, kd, ub)
    for h in heads:
        s_ref[0, h] = s[h] * jnp.exp(gl[h]) + ktu[h]
    outs = [(_rms(qs[h] * eg[h] + qku[h], og) * za[:, h * LANES:(h + 1) * LANES]).astype(BF16) for h in heads]
    o_ref[0] = jnp.concatenate(outs, axis=1)


def _gdn(u, gb, za, cinit, sinit, cw, og, L):
    b, s, _ = u.shape
    nc = s // L
    ns = 3 * GDN_HEADS
    blk = lambda wd: pl.BlockSpec((1, L, wd), lambda i, j: (i, j, 0))
    return pl.pallas_call(
        functools.partial(_gdn_body, L=L),
        grid=(b, nc),
        in_specs=[blk(ns * LANES), blk(LANES), blk(GDN_HEADS * LANES),
                  pl.BlockSpec((1, 8, ns * LANES), lambda i, j: (i, 0, 0)),
                  pl.BlockSpec((1, GDN_HEADS, GDN_DK, LANES), lambda i, j: (i, 0, 0, 0)),
                  _const_spec((ns, 8, LANES)), _const_spec((1, LANES))],
        out_specs=[blk(GDN_HEADS * LANES),
                   pl.BlockSpec((1, GDN_HEADS, GDN_DK, LANES), lambda i, j: (i, 0, 0, 0))],
        out_shape=[jax.ShapeDtypeStruct((b, s, GDN_HEADS * LANES), BF16),
                   jax.ShapeDtypeStruct((b, GDN_HEADS, GDN_DK, LANES), F32)],
        scratch_shapes=[pltpu.VMEM((ns, 8 + L, LANES), F32)],
        compiler_params=_params(("arbitrary", "arbitrary")),
        name="gdn",
    )(u, gb, za, cinit, sinit, cw, og)


def _head_rms64(x, g, seg, expand):
    h2, l2 = _split2(x * x)
    ss = _dot(h2, seg) + _dot(l2, seg)
    r = lax.rsqrt(ss * (1.0 / BAND_HD) + NORM_EPS)
    rh, rl = _split2(r)
    return x * (_dot(rh, expand) + _dot(rl, expand)) * g


def _band_body(q_ref, k_ref, v_ref, zb_ref, ck_ref, cv_ref, qg_ref, kg_ref, bias_ref, seg_ref, exp_ref,
               o_ref, kn_ref, kscr, vlo, vhi, *, L, W, masked):
    c = pl.program_id(1)
    dm = BAND_HEADS * BAND_HD
    lane = lax.broadcasted_iota(jnp.int32, (1, dm), 1)
    lo_full = (lane & (LANES - 1)) < BAND_HD
    lo = lo_full[:, :LANES]

    @pl.when(c == 0)
    def _():
        kscr[0:BAND_PAST, :] = ck_ref[0].astype(BF16)
        cv = cv_ref[0]
        vlo[0:BAND_PAST, :] = jnp.where(lo_full, cv, 0.0).astype(BF16)
        vhi[0:BAND_PAST, :] = jnp.where(lo_full, 0.0, cv).astype(BF16)

    seg = seg_ref[...]
    expand = exp_ref[...]
    qn = _head_rms64(q_ref[0], qg_ref[...], seg, expand)
    kn = _head_rms64(k_ref[0], kg_ref[...], seg, expand)
    kn_ref[0] = kn
    new0 = pl.multiple_of(BAND_PAST + c * L, L)
    kscr[pl.ds(new0, L), :] = kn.astype(BF16)
    v = v_ref[0]
    vlo[pl.ds(new0, L), :] = jnp.where(lo_full, v, 0.0).astype(BF16)
    vhi[pl.ds(new0, L), :] = jnp.where(lo_full, 0.0, v).astype(BF16)

    w0 = pl.multiple_of(c * L, L)
    if masked:
        wcol = lax.broadcasted_iota(jnp.int32, (1, W), 1)
        valid = (wcol + c * L) >= BAND_PAST
    scale = BAND_HD ** -0.5
    zb = zb_ref[0]
    slabs = [slice(s * LANES, (s + 1) * LANES) for s in range(BAND_HEADS // 2)]
    sc = []
    for sl in slabs:
        ks = kscr[pl.ds(w0, W), sl]
        qs = qn[:, sl]
        sc.append(_dot_nt(jnp.where(lo, qs, 0.0).astype(BF16), ks))
        sc.append(_dot_nt(jnp.where(lo, 0.0, qs).astype(BF16), ks))
    ps = []
    for h in range(BAND_HEADS):
        x = sc[h] * scale + bias_ref[h]
        if masked:
            x = jnp.where(valid, x, -jnp.inf)
        ps.append(_softmax_rows(x).astype(BF16))
    pv = []
    for i, sl in enumerate(slabs):
        pv.append(_dot(ps[2 * i], vlo[pl.ds(w0, W), sl]))
        pv.append(_dot(ps[2 * i + 1], vhi[pl.ds(w0, W), sl]))
    outs = [((pv[2 * i] + pv[2 * i + 1]) * zb[:, sl]).astype(BF16) for i, sl in enumerate(slabs)]
    o_ref[0] = jnp.concatenate(outs, axis=1)


def _band(q, k, v, zb, ck, cv, qg, kg, bias, seg, expand, L, masked):
    b, s, dm = q.shape
    nc = s // L
    W = BAND_PAST + L
    blk = pl.BlockSpec((1, L, dm), lambda i, j: (i, j, 0))
    cache = pl.BlockSpec((1, BAND_PAST, dm), lambda i, j: (i, 0, 0))
    return pl.pallas_call(
        functools.partial(_band_body, L=L, W=W, masked=masked),
        grid=(b, nc),
        in_specs=[blk, blk, blk, blk, cache, cache, _const_spec((1, dm)), _const_spec((1, dm)),
                  _const_spec((BAND_HEADS, L, W)), _const_spec((dm, LANES)), _const_spec((LANES, dm))],
        out_specs=[blk, blk],
        out_shape=[jax.ShapeDtypeStruct((b, s, dm), BF16), jax.ShapeDtypeStruct((b, s, dm), F32)],
        scratch_shapes=[pltpu.VMEM((BAND_PAST + s, dm), BF16)] * 3,
        compiler_params=_params(("arbitrary", "arbitrary")),
        name="band",
    )(q, k, v, zb, ck, cv, qg, kg, bias, seg, expand)


def _gla_body(q_ref, k_ref, v_ref, lg_ref, z_ref, sinit_ref, og_ref, o_ref, st_ref, *, L):
    c = pl.program_id(1)

    @pl.when(c == 0)
    def _():
        st_ref[...] = sinit_ref[...]

    nb = L // GLA_SUB
    row = lax.broadcasted_iota(jnp.int32, (L, L), 0)
    col = lax.broadcasted_iota(jnp.int32, (L, L), 1)
    tri = jnp.where(row >= col, 1.0, 0.0).astype(BF16)
    cb_all = _dot_exact_lhs(tri, lg_ref[0])
    lane = lax.broadcasted_iota(jnp.int32, (GLA_SUB, LANES), 1)
    sub_row = lax.broadcasted_iota(jnp.int32, (GLA_SUB, LANES), 0)
    og = og_ref[...]
    scale = GLA_DK ** -0.5

    heads = range(GLA_HEADS)
    sls = [slice(h * GLA_DK, (h + 1) * GLA_DK) for h in heads]
    vsls = [slice(h * GLA_DV, (h + 1) * GLA_DV) for h in heads]
    cb = [cb_all[:, sl] for sl in sls]
    q = [q_ref[0, :, sl] * scale for sl in sls]
    k = [k_ref[0, :, sl] for sl in sls]
    vb = [v_ref[0, :, vsl].astype(BF16) for vsl in vsls]
    st = [st_ref[0, h] for h in heads]
    z = z_ref[0]
    o_inter = _each(lambda q_, c_, s_: _dot_nt((q_ * jnp.exp(c_)).astype(BF16), s_.astype(BF16)), q, cb, st)
    cl = [c_[L - 1:L] for c_ in cb]
    ke = _each(lambda k_, l_, c_: (k_ * jnp.exp(l_ - c_)).astype(BF16), k, cl, cb)
    vtk = _each(_dot_tn, vb, ke)
    off = [[None] * nb for _ in heads]
    for ib in range(1, nb):
        r0 = ib * GLA_SUB
        for h in heads:
            ref = cb[h][r0 - 1:r0]
            qe = (q[h][r0:r0 + GLA_SUB] * jnp.exp(cb[h][r0:r0 + GLA_SUB] - ref)).astype(BF16)
            kx = (k[h] * jnp.exp(jnp.minimum(ref - cb[h], 0.0))).astype(BF16)
            x = _dot_nt(qe, kx)
            if L < LANES:
                x = jnp.concatenate([x, jnp.zeros((GLA_SUB, LANES - L), F32)], axis=1)
            off[h][ib] = jnp.where(lane < r0, x, 0.0)
    strips = [[None] * nb for _ in heads]
    for ib in range(nb):
        r0 = ib * GLA_SUB
        for h in heads:
            qi = q[h][r0:r0 + GLA_SUB]
            cbi = cb[h][r0:r0 + GLA_SUB]
            strip = off[h][ib] if ib > 0 else jnp.zeros((GLA_SUB, LANES), F32)
            for jj in range(GLA_SUB):
                j = r0 + jj
                w = qi * k[h][j:j + 1] * jnp.exp(jnp.minimum(cbi - cb[h][j:j + 1], 0.0))
                colv = jnp.sum(w, axis=-1, keepdims=True)
                strip = strip + jnp.where((lane == j) & (sub_row >= jj), colv, 0.0)
            strips[h][ib] = strip
    att = [jnp.concatenate(strips[h], axis=0)[:, :L].astype(BF16) for h in heads]
    o_intra = _each(_dot, att, vb)
    for h in heads:
        st_ref[0, h] = st[h] * jnp.exp(cl[h]) + vtk[h]
    outs = [(_rms(o_inter[h] + o_intra[h], og) * z[:, vsls[h]]).astype(BF16) for h in heads]
    o_ref[0] = jnp.concatenate(outs, axis=1)


def _gla(q, k, v, lg, z, sinit_t, og, L):
    b, s, _ = q.shape
    nc = s // L
    qk = pl.BlockSpec((1, L, GLA_HEADS * GLA_DK), lambda i, j: (i, j, 0))
    vv = pl.BlockSpec((1, L, GLA_HEADS * GLA_DV), lambda i, j: (i, j, 0))
    st = pl.BlockSpec((1, GLA_HEADS, GLA_DV, GLA_DK), lambda i, j: (i, 0, 0, 0))
    return pl.pallas_call(
        functools.partial(_gla_body, L=L),
        grid=(b, nc),
        in_specs=[qk, qk, vv, qk, vv, st, _const_spec((1, GLA_DV))],
        out_specs=[vv, st],
        out_shape=[jax.ShapeDtypeStruct((b, s, GLA_HEADS * GLA_DV), BF16),
                   jax.ShapeDtypeStruct((b, GLA_HEADS, GLA_DV, GLA_DK), F32)],
        compiler_params=_params(("arbitrary", "arbitrary")),
        name="gla",
    )(q, k, v, lg, z, sinit_t, og)


def _post_body(*refs, n_o):
    x_ref = refs[0]
    o_refs = refs[1:1 + n_o]
    wout_ref, mg_ref, wmq_ref, mqg_ref, mk_ref, mv_ref, wmo_ref, y_ref = refs[1 + n_o:]
    acc = x_ref[0]
    off = 0
    for o_ref in o_refs:
        kd = o_ref.shape[-1]
        acc = acc + _dot(o_ref[0], wout_ref[off:off + kd, :])
        off += kd
    hm = _rms(acc, mg_ref[...]).astype(BF16)
    qz = _dot(hm, wmq_ref[...])
    hw = MEM_HEADS * MEM_HD
    sls = [slice(h * MEM_HD, (h + 1) * MEM_HD) for h in range(MEM_HEADS)]
    qn = [_rms(qz[:, sl], mqg_ref[...]).astype(BF16) for sl in sls]
    sc = [_dot_nt(qn[h], mk_ref[0, :, sl].astype(BF16)) for h, sl in enumerate(sls)]
    p = [_softmax_rows(x * (MEM_HD ** -0.5)).astype(BF16) for x in sc]
    oh = [_dot(p[h], mv_ref[0, :, sl].astype(BF16)) for h, sl in enumerate(sls)]
    outs = [oh[h] * _silu(qz[:, hw + h * MEM_HD:hw + (h + 1) * MEM_HD]) for h in range(MEM_HEADS)]
    om = jnp.concatenate(outs, axis=1).astype(BF16)
    y_ref[0] = acc + _dot(om, wmo_ref[...])


def _post(x, os_, wout, mg, wmq, mqg, mk, mv, wmo):
    b, s, d = x.shape
    tm = _row_tile(s, 256)
    hw = MEM_HEADS * MEM_HD
    nm = mk.shape[1]
    blk = lambda wd: pl.BlockSpec((1, tm, wd), lambda i, j: (i, j, 0))
    mem = pl.BlockSpec((1, nm, hw), lambda i, j: (i, 0, 0))
    kin = sum(o.shape[-1] for o in os_)
    return pl.pallas_call(
        functools.partial(_post_body, n_o=len(os_)),
        grid=(b, s // tm),
        in_specs=[blk(d)] + [blk(o.shape[-1]) for o in os_]
        + [_const_spec((kin, d)), _const_spec((1, d)), _const_spec((d, 2 * hw)), _const_spec((1, MEM_HD)),
           mem, mem, _const_spec((hw, d))],
        out_specs=blk(d),
        out_shape=jax.ShapeDtypeStruct((b, s, d), F32),
        compiler_params=_params(("arbitrary", "arbitrary")),
        name="post",
    )(x, *os_, wout, mg, wmq, mqg, mk, mv, wmo)


def _pad_cols(w, n):
    return jnp.pad(w, ((0, 0), (0, n - w.shape[1])))


def _prep_l0(w_in, conv_w, a_log, dt_bias, rel_bias, q_g, k_g):
    c0 = 3 * 1024
    ab = w_in[:, c0:c0 + 2 * GDN_HEADS]
    c1 = c0 + 2 * GDN_HEADS
    cols = lambda i: w_in[:, c1 + i * 1024:c1 + (i + 1) * 1024].astype(BF16)
    w = dict(
        wu=w_in[:, :c0].astype(BF16), wab=_pad_cols(ab, LANES).astype(BF16),
        wza=cols(0), wq=cols(1), wk=cols(2), wv=cols(3), wzb=cols(4),
        alog=jnp.pad(a_log, (0, LANES - GDN_HEADS)).reshape(1, LANES).astype(F32),
        dtb=jnp.pad(dt_bias, (0, LANES - GDN_HEADS)).reshape(1, LANES).astype(F32),
    )
    ns = 3 * GDN_HEADS
    cw = jnp.pad(conv_w.astype(F32), ((0, 8 - CONV_TAPS), (0, 0))).reshape(8, ns, LANES).transpose(1, 0, 2)
    dm = BAND_HEADS * BAND_HD
    head_of = np.arange(dm) // BAND_HD
    seg = jnp.asarray(head_of[:, None] == np.arange(LANES)[None, :], BF16)
    expand = jnp.asarray(np.arange(LANES)[:, None] == head_of[None, :], BF16)
    qg = jnp.tile(q_g.astype(F32), BAND_HEADS).reshape(1, dm)
    kg = jnp.tile(k_g.astype(F32), BAND_HEADS).reshape(1, dm)
    return w, cw, seg, expand, qg, kg


def _band_bias_table(rel_bias, L):
    W = BAND_PAST + L
    dist = np.arange(L)[:, None] + BAND_PAST - np.arange(W)[None, :]
    idx = np.clip(dist, -BAND_MAX_REL, BAND_MAX_REL) + BAND_MAX_REL
    return rel_bias.astype(F32)[:, idx]


def _prep_l1(w_in, w_gate_up, gate_bias):
    qk = GLA_HEADS * GLA_DK
    vw = GLA_HEADS * GLA_DV
    o = np.cumsum([0, qk, qk, vw, GLA_RANK, vw])
    return dict(
        wq=w_in[:, o[0]:o[1]].astype(BF16), wk=w_in[:, o[1]:o[2]].astype(BF16),
        wv=w_in[:, o[2]:o[3]].astype(BF16), wlr=_pad_cols(w_in[:, o[3]:o[4]], LANES).astype(BF16),
        wz=w_in[:, o[4]:o[5]].astype(BF16),
        wg=jnp.pad(w_gate_up, ((0, LANES - GLA_RANK), (0, 0))).astype(BF16),
        gbias=gate_bias.reshape(1, qk).astype(F32),
    )


def _layer0(x, conv_state, gdn_state, band_k, band_v, mk, mv, norm_g, pw, cw, seg, expand, qg, kg, bias,
            a_onorm_g, w_out, mnorm_g, w_mq, mq_g, w_mo, L, masked):
    b, s, d = x.shape
    u, gb, za, q, k, v, zb = _proj0(x.reshape(b * s, d), norm_g.reshape(1, d), pw)
    r3 = lambda t: t.reshape(b, s, t.shape[-1])
    u, gb, za, q, k, v, zb = map(r3, (u, gb, za, q, k, v, zb))
    cinit = jnp.pad(conv_state.astype(F32), ((0, 0), (8 - (CONV_TAPS - 1), 0), (0, 0)))
    o_a, s_new = _gdn(u, gb, za, cinit, gdn_state.astype(F32), cw, a_onorm_g.reshape(1, LANES), L)
    o_b, kn = _band(q, k, v, zb, band_k, band_v, qg, kg, bias, seg, expand, L, masked)
    y = _post(x, (o_a, o_b), w_out, mnorm_g.reshape(1, d), w_mq, mq_g.reshape(1, MEM_HD), mk, mv, w_mo)
    return y, u[:, s - (CONV_TAPS - 1):, :], s_new, kn, v


def _layer1(x, gla_state, mk, mv, norm_g, pw, c_onorm_g, w_out, mnorm_g, w_mq, mq_g, w_mo, L):
    b, s, d = x.shape
    q, k, v, lg, z = _proj1(x.reshape(b * s, d), norm_g.reshape(1, d), pw)
    r3 = lambda t: t.reshape(b, s, t.shape[-1])
    q, k, v, lg, z = map(r3, (q, k, v, lg, z))
    st0 = jnp.swapaxes(gla_state.astype(F32), 2, 3)
    o, st = _gla(q, k, v, lg, z, st0, c_onorm_g.reshape(1, GLA_DV), L)
    y = _post(x, (o,), w_out, mnorm_g.reshape(1, d), w_mq, mq_g.reshape(1, MEM_HD), mk, mv, w_mo)
    return y, jnp.swapaxes(st, 2, 3)


def kernel(x_prompt, x_sample, mem_prompt, state_l0_gdn_conv, state_l0_gdn, cache_l0_band_k, cache_l0_band_v, cache_l0_mem_k, cache_l0_mem_v, state_l1_gla, cache_l1_mem_k, cache_l1_mem_v, l0_norm_g, l0_w_in, l0_conv_w, l0_a_log, l0_dt_bias, l0_a_onorm_g, l0_b_q_g, l0_b_k_g, l0_b_rel_bias, l0_w_out, l0_mnorm_g, l0_mem_norm_g, l0_w_mkv, l0_mk_g, l0_w_mq, l0_mq_g, l0_w_mo, l1_norm_g, l1_w_in, l1_w_gate_up, l1_gate_bias, l1_c_onorm_g, l1_w_out, l1_mnorm_g, l1_mem_norm_g, l1_w_mkv, l1_mk_g, l1_w_mq, l1_mq_g, l1_w_mo):
    bp, sp, d = x_prompt.shape
    bs, ss, _ = x_sample.shape
    nm = mem_prompt.shape[1]
    hw = MEM_HEADS * MEM_HD
    dm = BAND_HEADS * BAND_HD
    assert sp % CHUNK_ == 0 and ss % INV_SUB == 0 and ss <= CHUNK_
    assert cache_l0_band_k.shape[1] == BAND_PAST

    pw0, cw, seg, expand, qg, kg = _prep_l0(l0_w_in, l0_conv_w, l0_a_log, l0_dt_bias, l0_b_rel_bias,
                                            l0_b_q_g, l0_b_k_g)
    pw1 = _prep_l1(l1_w_in, l1_w_gate_up, l1_gate_bias)
    bias_p = _band_bias_table(l0_b_rel_bias, CHUNK_)
    bias_s = _band_bias_table(l0_b_rel_bias, ss)
    bf = lambda w: w.astype(BF16)
    mem2 = mem_prompt.reshape(bp * nm, d)

    mk0, mv0 = _memkv(mem2, l0_mem_norm_g.reshape(1, d), bf(l0_w_mkv), l0_mk_g.reshape(1, MEM_HD))
    mk0 = mk0.reshape(bp, nm, hw)
    mv0 = mv0.reshape(bp, nm, hw)
    l0_shared = (l0_norm_g, pw0, cw, seg, expand, qg, kg)
    l0_tail = (l0_a_onorm_g, bf(l0_w_out), l0_mnorm_g, bf(l0_w_mq), l0_mq_g, bf(l0_w_mo))
    zeros_conv = jnp.zeros((bp, CONV_TAPS - 1, 3 * GDN_HEADS * LANES), F32)
    zeros_gdn = jnp.zeros((bp, GDN_HEADS, GDN_DK, LANES), F32)
    zeros_band = jnp.zeros((bp, BAND_PAST, dm), F32)
    yp, p_conv, p_gdn, p_kn, p_v = _layer0(
        x_prompt, zeros_conv, zeros_gdn, zeros_band, zeros_band, mk0, mv0,
        *l0_shared, bias_p, *l0_tail, CHUNK_, True)
    ys, s_conv, s_gdn, s_kn, s_v = _layer0(
        x_sample, state_l0_gdn_conv, state_l0_gdn,
        cache_l0_band_k.reshape(bs, BAND_PAST, dm), cache_l0_band_v.reshape(bs, BAND_PAST, dm),
        cache_l0_mem_k.reshape(bs, nm, hw), cache_l0_mem_v.reshape(bs, nm, hw),
        *l0_shared, bias_s, *l0_tail, ss, False)

    mk1, mv1 = _memkv(mem2, l1_mem_norm_g.reshape(1, d), bf(l1_w_mkv), l1_mk_g.reshape(1, MEM_HD))
    mk1 = mk1.reshape(bp, nm, hw)
    mv1 = mv1.reshape(bp, nm, hw)
    l1_tail = (l1_c_onorm_g, bf(l1_w_out), l1_mnorm_g, bf(l1_w_mq), l1_mq_g, bf(l1_w_mo))
    zeros_gla = jnp.zeros((bp, GLA_HEADS, GLA_DK, GLA_DV), F32)
    yp, p_gla = _layer1(yp, zeros_gla, mk1, mv1, l1_norm_g, pw1, *l1_tail, CHUNK_)
    ys, s_gla = _layer1(ys, state_l1_gla, cache_l1_mem_k.reshape(bs, nm, hw),
                        cache_l1_mem_v.reshape(bs, nm, hw), l1_norm_g, pw1, *l1_tail, ss)

    keep = min(BAND_PAST, sp)
    h4 = lambda t: t.reshape(t.shape[0], t.shape[1], BAND_HEADS, BAND_HD)
    m4 = lambda t: t.reshape(bp, nm, MEM_HEADS, MEM_HD)
    return (yp, ys, p_conv, p_gdn, h4(p_kn[:, sp - keep:]), h4(p_v[:, sp - keep:]),
            m4(mk0), m4(mv0), p_gla, m4(mk1), m4(mv1),
            s_conv, s_gdn, h4(s_kn), h4(s_v), s_gla)
```

```python
import functools

import jax
import jax.numpy as jnp
import numpy as np
from jax import lax
from jax.experimental import pallas as pl
from jax.experimental.pallas import tpu as pltpu

F32 = jnp.float32
BF16 = jnp.bfloat16
NORM_EPS = 1e-6
LOG2_E = 1.4426950408889634

D_MODEL_ = 1024
CHUNK_ = 64
CONV_TAPS = 4
GDN_HEADS = 8
GDN_DK = 128
BAND_HEADS = 16
BAND_HD = 64
BAND_PAST = 512
BAND_MAX_REL = 128
GLA_HEADS = 8
GLA_DK = 128
GLA_DV = 256
GLA_RANK = 16
GLA_TAU = 16.0
GLA_SUB = 16
MEM_HEADS = 4
MEM_HD = 128
INV_SUB = 16
GDN_GROUP = 2
POST_SPLIT = 2
LANES = 128
VMEM_LIMIT = 56 * 1024 * 1024


def _dot(a, b):
    return jnp.dot(a, b, preferred_element_type=F32)


def _dot_nt(a, b):
    return lax.dot_general(a, b, (((1,), (1,)), ((), ())), preferred_element_type=F32)


def _dot_tn(a, b):
    return lax.dot_general(a, b, (((0,), (0,)), ((), ())), preferred_element_type=F32)


def _split2(x):
    hi = x.astype(BF16)
    lo = (x - hi.astype(F32)).astype(BF16)
    return hi, lo


def _split3(x):
    hi = x.astype(BF16)
    r = x - hi.astype(F32)
    mid = r.astype(BF16)
    lo = (r - mid.astype(F32)).astype(BF16)
    return hi, mid, lo


def _dot_x3(a, b):
    ah, al = _split2(a)
    bh, bl = _split2(b)
    return _dot(ah, bh) + (_dot(ah, bl) + _dot(al, bh))


def _dot_exact_lhs(a_bf, b):
    h, m, l = _split3(b)
    return _dot(a_bf, h) + (_dot(a_bf, m) + _dot(a_bf, l))


def _rms(x, g):
    ms = jnp.mean(x * x, axis=-1, keepdims=True)
    return x * lax.rsqrt(ms + NORM_EPS) * g


def _silu(x):
    return x * jax.nn.sigmoid(x)


def _softplus(x):
    return jnp.maximum(x, 0.0) + jnp.log1p(jnp.exp(-jnp.abs(x)))


def _softmax_rows(s):
    m = jnp.max(s, axis=-1, keepdims=True)
    e = jnp.exp(s - m)
    return e * (1.0 / jnp.sum(e, axis=-1, keepdims=True))


def _const_spec(shape):
    nd = len(shape)
    return pl.BlockSpec(shape, lambda *_: (0,) * nd, pipeline_mode=pl.Buffered(1))


def _params(sem):
    return pltpu.CompilerParams(dimension_semantics=sem, vmem_limit_bytes=VMEM_LIMIT)


def _row_tile(n, want):
    t = min(n, want)
    assert n % t == 0
    return t


def _proj0_body(x_ref, g_ref, wu_ref, wab_ref, wza_ref, wq_ref, wk_ref, wv_ref, wzb_ref,
                alog_ref, dtb_ref, u_ref, gb_ref, za_ref, q_ref, k_ref, v_ref, zb_ref):
    h = _rms(x_ref[...], g_ref[...]).astype(BF16)
    u_ref[...] = _dot(h, wu_ref[...])
    ab = _dot(h, wab_ref[...])
    lane = lax.broadcasted_iota(jnp.int32, ab.shape, 1)
    gval = -jnp.exp(alog_ref[...]) * _softplus(ab + dtb_ref[...])
    gb_ref[...] = jnp.where(lane < GDN_HEADS, gval, jax.nn.sigmoid(ab))
    za_ref[...] = _silu(_dot(h, wza_ref[...]))
    q_ref[...] = _dot(h, wq_ref[...])
    k_ref[...] = _dot(h, wk_ref[...])
    v_ref[...] = _dot(h, wv_ref[...])
    zb_ref[...] = _silu(_dot(h, wzb_ref[...]))


def _proj0(x2, g, w):
    n, d = x2.shape
    tm = _row_tile(n, 256)
    widths = (3 * 1024, LANES, 1024, 1024, 1024, 1024, 1024)
    row = lambda wd: pl.BlockSpec((tm, wd), lambda i: (i, 0))
    return pl.pallas_call(
        _proj0_body,
        grid=(n // tm,),
        in_specs=[row(d), _const_spec((1, d))]
        + [_const_spec((d, wd)) for wd in widths]
        + [_const_spec((1, LANES)), _const_spec((1, LANES))],
        out_specs=[row(wd) for wd in widths],
        out_shape=[jax.ShapeDtypeStruct((n, wd), F32) for wd in widths],
        compiler_params=_params(("arbitrary",)),
        name="proj0",
    )(x2, g, w["wu"], w["wab"], w["wza"], w["wq"], w["wk"], w["wv"], w["wzb"], w["alog"], w["dtb"])


def _proj1_body(x_ref, g_ref, wq_ref, wk_ref, wv_ref, wlr_ref, wz_ref, wg_ref, gbias_ref,
                q_ref, k_ref, v_ref, lg_ref, z_ref):
    h = _rms(x_ref[...], g_ref[...]).astype(BF16)
    q_ref[...] = _dot(h, wq_ref[...])
    k_ref[...] = _dot(h, wk_ref[...])
    v_ref[...] = _dot(h, wv_ref[...])
    lr = _dot(h, wlr_ref[...])
    pre = _dot(lr.astype(BF16), wg_ref[...]) + gbias_ref[...]
    lg_ref[...] = -_softplus(-pre) * (1.0 / GLA_TAU)
    z_ref[...] = _silu(_dot(h, wz_ref[...]))


def _proj1(x2, g, w):
    n, d = x2.shape
    tm = _row_tile(n, 256)
    row = lambda wd: pl.BlockSpec((tm, wd), lambda i: (i, 0))
    outw = (1024, 1024, 2048, 1024, 2048)
    return pl.pallas_call(
        _proj1_body,
        grid=(n // tm,),
        in_specs=[row(d), _const_spec((1, d)), _const_spec((d, 1024)), _const_spec((d, 1024)),
                  _const_spec((d, 2048)), _const_spec((d, LANES)), _const_spec((d, 2048)),
                  _const_spec((LANES, 1024)), _const_spec((1, 1024))],
        out_specs=[row(wd) for wd in outw],
        out_shape=[jax.ShapeDtypeStruct((n, wd), F32) for wd in outw],
        compiler_params=_params(("arbitrary",)),
        name="proj1",
    )(x2, g, w["wq"], w["wk"], w["wv"], w["wlr"], w["wz"], w["wg"], w["gbias"])


def _memkv_body(m_ref, g_ref, w_ref, kg_ref, k_ref, v_ref):
    h = _rms(m_ref[...], g_ref[...]).astype(BF16)
    kv = _dot(h, w_ref[...])
    hw = MEM_HEADS * MEM_HD
    for hh in range(MEM_HEADS):
        sl = slice(hh * MEM_HD, (hh + 1) * MEM_HD)
        k_ref[:, sl] = _rms(kv[:, sl], kg_ref[...])
    v_ref[...] = kv[:, hw:]


def _memkv(m2, g, w_bf, kg):
    n, d = m2.shape
    tm = _row_tile(n, 256)
    hw = MEM_HEADS * MEM_HD
    row = lambda wd: pl.BlockSpec((tm, wd), lambda i: (i, 0))
    return pl.pallas_call(
        _memkv_body,
        grid=(n // tm,),
        in_specs=[row(d), _const_spec((1, d)), _const_spec((d, 2 * hw)), _const_spec((1, MEM_HD))],
        out_specs=[row(hw), row(hw)],
        out_shape=[jax.ShapeDtypeStruct((n, hw), F32)] * 2,
        compiler_params=_params(("arbitrary",)),
        name="memkv",
    )(m2, g, w_bf, kg)


def _each(fn, *lists):
    return [fn(*xs) for xs in zip(*lists)]


def _unit_lower_inverse(a, eye, bd):
    mm = lambda x, y: _dot(x.astype(BF16), y.astype(BF16))
    d = _each(lambda x: jnp.where(bd, x, 0.0), a)
    nl = _each(lambda x, y: x - y, a, d)
    d2 = _each(mm, d, d)
    d4 = _each(mm, d2, d2)
    td = _each(lambda x, y: mm(eye - x, eye + y), d, d2)
    d8 = _each(mm, d4, d4)
    td = _each(lambda x, y: mm(x, eye + y), td, d4)
    td = _each(lambda x, y: mm(x, eye + y), td, d8)
    m = _each(mm, td, nl)
    m2 = _each(mm, m, m)
    mt = _each(mm, m, td)
    return _each(lambda x, y, z: mm(eye + x, y - z), m2, td, mt)


def _gdn_body(u_ref, gb_ref, za_ref, cinit_ref, sinit_ref, cw_ref, og_ref,
              o_ref, s_ref, ubuf, *, L, NB):
    c = pl.program_id(1)
    ns = 3 * GDN_HEADS
    hist = 8

    @pl.when(c == 0)
    def _():
        for b in range(NB):
            for j in range(ns):
                ubuf[b * ns + j, 0:hist, :] = cinit_ref[b, :, j * LANES:(j + 1) * LANES]
        s_ref[...] = sinit_ref[...]

    for b in range(NB):
        for j in range(ns):
            ubuf[b * ns + j, hist:hist + L, :] = u_ref[b, :, j * LANES:(j + 1) * LANES]
    base = hist - (CONV_TAPS - 1)
    ys = []
    for b in range(NB):
        bsl = slice(b * ns, (b + 1) * ns)
        yb = ubuf[bsl, base:base + L, :] * cw_ref[:, 0:1, :]
        for i in range(1, CONV_TAPS):
            yb = yb + ubuf[bsl, base + i:base + i + L, :] * cw_ref[:, i:i + 1, :]
        ys.append(_silu(yb))
    ubuf[:, base:hist, :] = ubuf[:, base + L:hist + L, :]

    row = lax.broadcasted_iota(jnp.int32, (L, L), 0)
    col = lax.broadcasted_iota(jnp.int32, (L, L), 1)
    incl = row >= col
    strict = row > col
    sub_shift = INV_SUB.bit_length() - 1
    bd = lax.shift_right_logical(row, sub_shift) == lax.shift_right_logical(col, sub_shift)
    eye = jnp.where(row == col, 1.0, 0.0).astype(F32)
    tri = jnp.where(incl, 1.0, 0.0).astype(BF16)
    og = og_ref[...]
    heads = [(b, h) for b in range(NB) for h in range(GDN_HEADS)]
    l2n = lambda x: x * lax.rsqrt(jnp.sum(x * x, axis=-1, keepdims=True) + NORM_EPS)
    q = [l2n(ys[b][h]) * (GDN_DK ** -0.5) for b, h in heads]
    k = [l2n(ys[b][GDN_HEADS + h]) for b, h in heads]
    v = [ys[b][2 * GDN_HEADS + h] for b, h in heads]
    gc, gr, bc = [], [], []
    for b in range(NB):
        gbv = gb_ref[b]
        gcum = _dot_exact_lhs(tri, gbv)
        gpad = jnp.concatenate([gcum, jnp.zeros((LANES - L, LANES), F32)], axis=0)
        gt = gpad.T
        for h in range(GDN_HEADS):
            gc.append(gcum[:, h:h + 1])
            gr.append(gt[h:h + 1, 0:L])
            bc.append(gbv[:, GDN_HEADS + h:GDN_HEADS + h + 1])
    dec = _each(lambda c_, r_: jnp.where(incl, jnp.exp(jnp.where(incl, c_ - r_, 0.0)), 0.0), gc, gr)
    kb = _each(lambda x: x.astype(BF16), k)
    qb = _each(lambda x: x.astype(BF16), q)
    s = [s_ref[b, h] for b, h in heads]
    sb = _each(lambda x: x.astype(BF16), s)
    kq = _each(lambda k_, q_: _dot_nt(jnp.concatenate([k_, q_], axis=0), k_), kb, qb)
    a = _each(lambda b_, x, d_: jnp.where(strict, b_ * x[:L] * d_, 0.0), bc, kq, dec)
    t = _unit_lower_inverse(a, eye, bd)
    eg = _each(jnp.exp, gc)
    rhs = _each(lambda b_, v_, e_, k_: jnp.concatenate([b_ * v_, (b_ * e_) * k_], axis=1).astype(BF16),
                bc, v, eg, k)
    sol = _each(lambda x, y_: _dot(x.astype(BF16), y_), t, rhs)
    ksq = _each(lambda x, q_, y_: _dot(jnp.concatenate([x[:, LANES:].astype(BF16), q_], axis=0), y_),
                sol, qb, sb)
    ub = _each(lambda x, y_: (x[:, :LANES] - y_[:L]).astype(BF16), sol, ksq)
    qs = [x[L:] for x in ksq]
    qku = _each(lambda x, d_, u_: _dot((x[L:] * d_).astype(BF16), u_), kq, dec, ub)
    gl = [c_[L - 1:L, :] for c_ in gc]
    kd = _each(lambda k_, l_, c_: (k_ * jnp.exp(l_ - c_)).astype(BF16), k, gl, gc)
    ktu = _each(_dot_tn, kd, ub)
    for i, (b, h) in enumerate(heads):
        s_ref[b, h] = s[i] * jnp.exp(gl[i]) + ktu[i]
    for b in range(NB):
        za = za_ref[b]
        outs = [(_rms(qs[i] * eg[i] + qku[i], og) * za[:, h * LANES:(h + 1) * LANES]).astype(BF16)
                for i, (b_, h) in enumerate(heads) if b_ == b]
        o_ref[b] = jnp.concatenate(outs, axis=1)


def _gdn(u, gb, za, cinit, sinit, cw, og, L):
    b, s, _ = u.shape
    nc = s // L
    ns = 3 * GDN_HEADS
    nb = GDN_GROUP if b % GDN_GROUP == 0 else 1
    blk = lambda wd: pl.BlockSpec((nb, L, wd), lambda i, j: (i, j, 0))
    return pl.pallas_call(
        functools.partial(_gdn_body, L=L, NB=nb),
        grid=(b // nb, nc),
        in_specs=[blk(ns * LANES), blk(LANES), blk(GDN_HEADS * LANES),
                  pl.BlockSpec((nb, 8, ns * LANES), lambda i, j: (i, 0, 0)),
                  pl.BlockSpec((nb, GDN_HEADS, GDN_DK, LANES), lambda i, j: (i, 0, 0, 0)),
                  _const_spec((ns, 8, LANES)), _const_spec((1, LANES))],
        out_specs=[blk(GDN_HEADS * LANES),
                   pl.BlockSpec((nb, GDN_HEADS, GDN_DK, LANES), lambda i, j: (i, 0, 0, 0))],
        out_shape=[jax.ShapeDtypeStruct((b, s, GDN_HEADS * LANES), BF16),
                   jax.ShapeDtypeStruct((b, GDN_HEADS, GDN_DK, LANES), F32)],
        scratch_shapes=[pltpu.VMEM((nb * ns, 8 + L, LANES), F32)],
        compiler_params=_params(("arbitrary", "arbitrary")),
        name="gdn",
    )(u, gb, za, cinit, sinit, cw, og)


def _head_rms64(x, g, seg, expand):
    h2, l2 = _split2(x * x)
    ss = _dot(h2, seg) + _dot(l2, seg)
    r = lax.rsqrt(ss * (1.0 / BAND_HD) + NORM_EPS)
    rh, rl = _split2(r)
    return x * (_dot(rh, expand) + _dot(rl, expand)) * g


def _band_body(q_ref, k_ref, v_ref, zb_ref, ck_ref, cv_ref, qg_ref, kg_ref, bias_ref, seg_ref, exp_ref,
               o_ref, kn_ref, kscr, vlo, vhi, *, L, W, masked):
    c = pl.program_id(1)
    dm = BAND_HEADS * BAND_HD
    lane = lax.broadcasted_iota(jnp.int32, (1, dm), 1)
    lo_full = (lane & (LANES - 1)) < BAND_HD
    lo = lo_full[:, :LANES]

    @pl.when(c == 0)
    def _():
        kscr[0:BAND_PAST, :] = ck_ref[0].astype(BF16)
        cv = cv_ref[0]
        vlo[0:BAND_PAST, :] = jnp.where(lo_full, cv, 0.0).astype(BF16)
        vhi[0:BAND_PAST, :] = jnp.where(lo_full, 0.0, cv).astype(BF16)

    seg = seg_ref[...]
    expand = exp_ref[...]
    qn = _head_rms64(q_ref[0], qg_ref[...], seg, expand)
    kn = _head_rms64(k_ref[0], kg_ref[...], seg, expand)
    kn_ref[0] = kn
    new0 = pl.multiple_of(BAND_PAST + c * L, L)
    kscr[pl.ds(new0, L), :] = kn.astype(BF16)
    v = v_ref[0]
    vlo[pl.ds(new0, L), :] = jnp.where(lo_full, v, 0.0).astype(BF16)
    vhi[pl.ds(new0, L), :] = jnp.where(lo_full, 0.0, v).astype(BF16)

    w0 = pl.multiple_of(c * L, L)
    if masked:
        wcol = lax.broadcasted_iota(jnp.int32, (1, W), 1)
        valid = (wcol + c * L) >= BAND_PAST
    scale = BAND_HD ** -0.5
    zb = zb_ref[0]
    slabs = [slice(s * LANES, (s + 1) * LANES) for s in range(BAND_HEADS // 2)]
    sc = []
    for sl in slabs:
        ks = kscr[pl.ds(w0, W), sl]
        qs = qn[:, sl] * scale
        sc.append(_dot_nt(jnp.where(lo, qs, 0.0).astype(BF16), ks))
        sc.append(_dot_nt(jnp.where(lo, 0.0, qs).astype(BF16), ks))
    ps, rs = [], []
    for h in range(BAND_HEADS):
        x = sc[h] + bias_ref[h]
        if masked:
            x = jnp.where(valid, x, -jnp.inf)
        e = jnp.exp(x - jnp.max(x, axis=-1, keepdims=True))
        rs.append(1.0 / jnp.sum(e, axis=-1, keepdims=True))
        ps.append(e.astype(BF16))
    pv = []
    for i, sl in enumerate(slabs):
        pv.append(_dot(ps[2 * i], vlo[pl.ds(w0, W), sl]))
        pv.append(_dot(ps[2 * i + 1], vhi[pl.ds(w0, W), sl]))
    outs = [((pv[2 * i] * rs[2 * i] + pv[2 * i + 1] * rs[2 * i + 1]) * zb[:, sl]).astype(BF16)
            for i, sl in enumerate(slabs)]
    o_ref[0] = jnp.concatenate(outs, axis=1)


def _band(q, k, v, zb, ck, cv, qg, kg, bias, seg, expand, L, masked):
    b, s, dm = q.shape
    nc = s // L
    W = BAND_PAST + L
    blk = pl.BlockSpec((1, L, dm), lambda i, j: (i, j, 0))
    cache = pl.BlockSpec((1, BAND_PAST, dm), lambda i, j: (i, 0, 0))
    return pl.pallas_call(
        functools.partial(_band_body, L=L, W=W, masked=masked),
        grid=(b, nc),
        in_specs=[blk, blk, blk, blk, cache, cache, _const_spec((1, dm)), _const_spec((1, dm)),
                  _const_spec((BAND_HEADS, L, W)), _const_spec((dm, LANES)), _const_spec((LANES, dm))],
        out_specs=[blk, blk],
        out_shape=[jax.ShapeDtypeStruct((b, s, dm), BF16), jax.ShapeDtypeStruct((b, s, dm), F32)],
        scratch_shapes=[pltpu.VMEM((BAND_PAST + s, dm), BF16)] * 3,
        compiler_params=_params(("arbitrary", "arbitrary")),
        name="band",
    )(q, k, v, zb, ck, cv, qg, kg, bias, seg, expand)


def _gla_body(q_ref, k_ref, v_ref, lg_ref, z_ref, sinit_ref, og_ref, o_ref, st_ref, cbs, *, L):
    c = pl.program_id(1)

    @pl.when(c == 0)
    def _():
        st_ref[...] = sinit_ref[...]

    nb = L // GLA_SUB
    half = GLA_SUB // 2
    row = lax.broadcasted_iota(jnp.int32, (L, L), 0)
    col = lax.broadcasted_iota(jnp.int32, (L, L), 1)
    tri = jnp.where(row >= col, 1.0, 0.0).astype(BF16)
    cb_all = _dot_exact_lhs(tri, lg_ref[0]) * LOG2_E
    cbs[...] = cb_all
    ex = jnp.exp2
    lane = lax.broadcasted_iota(jnp.int32, (GLA_SUB, LANES), 1)
    lane8 = lax.broadcasted_iota(jnp.int32, (half, LANES), 1)
    row8 = lax.broadcasted_iota(jnp.int32, (half, LANES), 0)
    og = og_ref[...]
    scale = GLA_DK ** -0.5

    heads = range(GLA_HEADS)
    sls = [slice(h * GLA_DK, (h + 1) * GLA_DK) for h in heads]
    vsls = [slice(h * GLA_DV, (h + 1) * GLA_DV) for h in heads]
    cb = [cb_all[:, sl] for sl in sls]
    q = [q_ref[0, :, sl] * scale for sl in sls]
    k = [k_ref[0, :, sl] for sl in sls]
    vb = [v_ref[0, :, vsl].astype(BF16) for vsl in vsls]
    st = [st_ref[0, h] for h in heads]
    z = z_ref[0]
    o_inter = _each(lambda q_, c_, s_: _dot_nt((q_ * ex(c_)).astype(BF16), s_.astype(BF16)), q, cb, st)
    cl = [c_[L - 1:L] for c_ in cb]
    ke = _each(lambda k_, l_, c_: (k_ * ex(l_ - c_)).astype(BF16), k, cl, cb)
    vtk = _each(_dot_tn, vb, ke)
    off = [[None] * nb for _ in heads]
    for ib in range(1, nb):
        r0 = ib * GLA_SUB
        for h in heads:
            ref = cb[h][r0 - 1:r0]
            qe = (q[h][r0:r0 + GLA_SUB] * ex(cb[h][r0:r0 + GLA_SUB] - ref)).astype(BF16)
            kx = (k[h] * ex(jnp.minimum(ref - cb[h], 0.0))).astype(BF16)
            x = _dot_nt(qe, kx)
            if L < LANES:
                x = jnp.concatenate([x, jnp.zeros((GLA_SUB, LANES - L), F32)], axis=1)
            off[h][ib] = jnp.where(lane < r0, x, 0.0)
    strips = [[None] * nb for _ in heads]
    for ib in range(nb):
        r0 = ib * GLA_SUB
        take_lo = [(lane8 == r0 + jj) & (row8 >= jj) for jj in range(half)]
        take_mid = [lane8 == r0 + jj for jj in range(half)]
        take_hi = [(lane8 == r0 + half + jj) & (row8 >= jj) for jj in range(half)]
        for h in heads:
            sl = sls[h]
            q_lo, q_hi = q[h][r0:r0 + half], q[h][r0 + half:r0 + GLA_SUB]
            c_lo, c_hi = cb[h][r0:r0 + half], cb[h][r0 + half:r0 + GLA_SUB]
            if ib > 0:
                s_lo, s_hi = off[h][ib][:half], off[h][ib][half:]
            else:
                s_lo = s_hi = jnp.zeros((half, LANES), F32)
            pair = lambda qq, cc, kj, cj: jnp.sum(qq * kj * ex(cc - cj), axis=-1, keepdims=True)
            for jj in range(half):
                kj, cj = k_ref[0, r0 + jj:r0 + jj + 1, sl], cbs[r0 + jj:r0 + jj + 1, sl]
                s_lo = jnp.where(take_lo[jj], pair(q_lo, c_lo, kj, cj), s_lo)
                s_hi = jnp.where(take_mid[jj], pair(q_hi, c_hi, kj, cj), s_hi)
            for jj in range(half):
                j = r0 + half + jj
                kj, cj = k_ref[0, j:j + 1, sl], cbs[j:j + 1, sl]
                s_hi = jnp.where(take_hi[jj], pair(q_hi, c_hi, kj, cj), s_hi)
            strips[h][ib] = jnp.concatenate([s_lo, s_hi], axis=0)
    att = [jnp.concatenate(strips[h], axis=0)[:, :L].astype(BF16) for h in heads]
    o_intra = _each(_dot, att, vb)
    for h in heads:
        st_ref[0, h] = st[h] * ex(cl[h]) + vtk[h]
    outs = [(_rms(o_inter[h] + o_intra[h], og) * z[:, vsls[h]]).astype(BF16) for h in heads]
    o_ref[0] = jnp.concatenate(outs, axis=1)


def _gla(q, k, v, lg, z, sinit_t, og, L):
    b, s, _ = q.shape
    nc = s // L
    qk = pl.BlockSpec((1, L, GLA_HEADS * GLA_DK), lambda i, j: (i, j, 0))
    vv = pl.BlockSpec((1, L, GLA_HEADS * GLA_DV), lambda i, j: (i, j, 0))
    st = pl.BlockSpec((1, GLA_HEADS, GLA_DV, GLA_DK), lambda i, j: (i, 0, 0, 0))
    return pl.pallas_call(
        functools.partial(_gla_body, L=L),
        grid=(b, nc),
        in_specs=[qk, qk, vv, qk, vv, st, _const_spec((1, GLA_DV))],
        out_specs=[vv, st],
        out_shape=[jax.ShapeDtypeStruct((b, s, GLA_HEADS * GLA_DV), BF16),
                   jax.ShapeDtypeStruct((b, GLA_HEADS, GLA_DV, GLA_DK), F32)],
        scratch_shapes=[pltpu.VMEM((L, GLA_HEADS * GLA_DK), F32)],
        compiler_params=_params(("arbitrary", "arbitrary")),
        name="gla",
    )(q, k, v, lg, z, sinit_t, og)


def _post_body(*refs, n_o):
    x_ref = refs[0]
    o_refs = refs[1:1 + n_o]
    wout_ref, mg_ref, wmq_ref, mqg_ref, mk_ref, mv_ref, wmo_ref, y_ref = refs[1 + n_o:]
    tm = x_ref.shape[1]
    ts = tm // POST_SPLIT if tm % (8 * POST_SPLIT) == 0 else tm
    rows = [slice(t * ts, (t + 1) * ts) for t in range(tm // ts)]
    acc = [x_ref[0, r, :] for r in rows]
    off = 0
    for o_ref in o_refs:
        kd = o_ref.shape[-1]
        w = wout_ref[off:off + kd, :]
        acc = [a + _dot(o_ref[0, r, :], w) for a, r in zip(acc, rows)]
        off += kd
    hm = [_rms(a, mg_ref[...]).astype(BF16) for a in acc]
    qz = [_dot(h_, wmq_ref[...]) for h_ in hm]
    hw = MEM_HEADS * MEM_HD
    sls = [slice(h * MEM_HD, (h + 1) * MEM_HD) for h in range(MEM_HEADS)]
    mkb = [mk_ref[0, :, sl].astype(BF16) for sl in sls]
    mvb = [mv_ref[0, :, sl].astype(BF16) for sl in sls]
    qn = [[_rms(z[:, sl], mqg_ref[...]).astype(BF16) for sl in sls] for z in qz]
    sc = [[_dot_nt(qh, kh) for qh, kh in zip(qt, mkb)] for qt in qn]
    p = [[_softmax_rows(x * (MEM_HD ** -0.5)).astype(BF16) for x in st] for st in sc]
    oh = [[_dot(ph, vh) for ph, vh in zip(pt, mvb)] for pt in p]
    for t, r in enumerate(rows):
        outs = [oh[t][h] * _silu(qz[t][:, hw + h * MEM_HD:hw + (h + 1) * MEM_HD]) for h in range(MEM_HEADS)]
        oh[t] = jnp.concatenate(outs, axis=1).astype(BF16)
    ym = [_dot(om, wmo_ref[...]) for om in oh]
    for t, r in enumerate(rows):
        y_ref[0, r, :] = acc[t] + ym[t]


def _post(x, os_, wout, mg, wmq, mqg, mk, mv, wmo):
    b, s, d = x.shape
    tm = _row_tile(s, 512)
    hw = MEM_HEADS * MEM_HD
    nm = mk.shape[1]
    blk = lambda wd: pl.BlockSpec((1, tm, wd), lambda i, j: (i, j, 0))
    mem = pl.BlockSpec((1, nm, hw), lambda i, j: (i, 0, 0))
    kin = sum(o.shape[-1] for o in os_)
    return pl.pallas_call(
        functools.partial(_post_body, n_o=len(os_)),
        grid=(b, s // tm),
        in_specs=[blk(d)] + [blk(o.shape[-1]) for o in os_]
        + [_const_spec((kin, d)), _const_spec((1, d)), _const_spec((d, 2 * hw)), _const_spec((1, MEM_HD)),
           mem, mem, _const_spec((hw, d))],
        out_specs=blk(d),
        out_shape=jax.ShapeDtypeStruct((b, s, d), F32),
        compiler_params=_params(("arbitrary", "arbitrary")),
        name="post",
    )(x, *os_, wout, mg, wmq, mqg, mk, mv, wmo)


def _pad_cols(w, n):
    return jnp.pad(w, ((0, 0), (0, n - w.shape[1])))


def _prep_l0(w_in, conv_w, a_log, dt_bias, rel_bias, q_g, k_g):
    c0 = 3 * 1024
    ab = w_in[:, c0:c0 + 2 * GDN_HEADS]
    c1 = c0 + 2 * GDN_HEADS
    cols = lambda i: w_in[:, c1 + i * 1024:c1 + (i + 1) * 1024].astype(BF16)
    w = dict(
        wu=w_in[:, :c0].astype(BF16), wab=_pad_cols(ab, LANES).astype(BF16),
        wza=cols(0), wq=cols(1), wk=cols(2), wv=cols(3), wzb=cols(4),
        alog=jnp.pad(a_log, (0, LANES - GDN_HEADS)).reshape(1, LANES).astype(F32),
        dtb=jnp.pad(dt_bias, (0, LANES - GDN_HEADS)).reshape(1, LANES).astype(F32),
    )
    ns = 3 * GDN_HEADS
    cw = jnp.pad(conv_w.astype(F32), ((0, 8 - CONV_TAPS), (0, 0))).reshape(8, ns, LANES).transpose(1, 0, 2)
    dm = BAND_HEADS * BAND_HD
    head_of = np.arange(dm) // BAND_HD
    seg = jnp.asarray(head_of[:, None] == np.arange(LANES)[None, :], BF16)
    expand = jnp.asarray(np.arange(LANES)[:, None] == head_of[None, :], BF16)
    qg = jnp.tile(q_g.astype(F32), BAND_HEADS).reshape(1, dm)
    kg = jnp.tile(k_g.astype(F32), BAND_HEADS).reshape(1, dm)
    return w, cw, seg, expand, qg, kg


def _band_bias_table(rel_bias, L):
    W = BAND_PAST + L
    dist = np.arange(L)[:, None] + BAND_PAST - np.arange(W)[None, :]
    idx = np.clip(dist, -BAND_MAX_REL, BAND_MAX_REL) + BAND_MAX_REL
    return rel_bias.astype(F32)[:, idx]


def _prep_l1(w_in, w_gate_up, gate_bias):
    qk = GLA_HEADS * GLA_DK
    vw = GLA_HEADS * GLA_DV
    o = np.cumsum([0, qk, qk, vw, GLA_RANK, vw])
    return dict(
        wq=w_in[:, o[0]:o[1]].astype(BF16), wk=w_in[:, o[1]:o[2]].astype(BF16),
        wv=w_in[:, o[2]:o[3]].astype(BF16), wlr=_pad_cols(w_in[:, o[3]:o[4]], LANES).astype(BF16),
        wz=w_in[:, o[4]:o[5]].astype(BF16),
        wg=jnp.pad(w_gate_up, ((0, LANES - GLA_RANK), (0, 0))).astype(BF16),
        gbias=gate_bias.reshape(1, qk).astype(F32),
    )


def _layer0(x, conv_state, gdn_state, band_k, band_v, mk, mv, norm_g, pw, cw, seg, expand, qg, kg, bias,
            a_onorm_g, w_out, mnorm_g, w_mq, mq_g, w_mo, L, masked):
    b, s, d = x.shape
    u, gb, za, q, k, v, zb = _proj0(x.reshape(b * s, d), norm_g.reshape(1, d), pw)
    r3 = lambda t: t.reshape(b, s, t.shape[-1])
    u, gb, za, q, k, v, zb = map(r3, (u, gb, za, q, k, v, zb))
    cinit = jnp.pad(conv_state.astype(F32), ((0, 0), (8 - (CONV_TAPS - 1), 0), (0, 0)))
    o_a, s_new = _gdn(u, gb, za, cinit, gdn_state.astype(F32), cw, a_onorm_g.reshape(1, LANES), L)
    o_b, kn = _band(q, k, v, zb, band_k, band_v, qg, kg, bias, seg, expand, L, masked)
    y = _post(x, (o_a, o_b), w_out, mnorm_g.reshape(1, d), w_mq, mq_g.reshape(1, MEM_HD), mk, mv, w_mo)
    return y, u[:, s - (CONV_TAPS - 1):, :], s_new, kn, v


def _layer1(x, gla_state, mk, mv, norm_g, pw, c_onorm_g, w_out, mnorm_g, w_mq, mq_g, w_mo, L):
    b, s, d = x.shape
    q, k, v, lg, z = _proj1(x.reshape(b * s, d), norm_g.reshape(1, d), pw)
    r3 = lambda t: t.reshape(b, s, t.shape[-1])
    q, k, v, lg, z = map(r3, (q, k, v, lg, z))
    st0 = jnp.swapaxes(gla_state.astype(F32), 2, 3)
    o, st = _gla(q, k, v, lg, z, st0, c_onorm_g.reshape(1, GLA_DV), L)
    y = _post(x, (o,), w_out, mnorm_g.reshape(1, d), w_mq, mq_g.reshape(1, MEM_HD), mk, mv, w_mo)
    return y, jnp.swapaxes(st, 2, 3)


def kernel(x_prompt, x_sample, mem_prompt, state_l0_gdn_conv, state_l0_gdn, cache_l0_band_k, cache_l0_band_v, cache_l0_mem_k, cache_l0_mem_v, state_l1_gla, cache_l1_mem_k, cache_l1_mem_v, l0_norm_g, l0_w_in, l0_conv_w, l0_a_log, l0_dt_bias, l0_a_onorm_g, l0_b_q_g, l0_b_k_g, l0_b_rel_bias, l0_w_out, l0_mnorm_g, l0_mem_norm_g, l0_w_mkv, l0_mk_g, l0_w_mq, l0_mq_g, l0_w_mo, l1_norm_g, l1_w_in, l1_w_gate_up, l1_gate_bias, l1_c_onorm_g, l1_w_out, l1_mnorm_g, l1_mem_norm_g, l1_w_mkv, l1_mk_g, l1_w_mq, l1_mq_g, l1_w_mo):
    bp, sp, d = x_prompt.shape
    bs, ss, _ = x_sample.shape
    nm = mem_prompt.shape[1]
    hw = MEM_HEADS * MEM_HD
    dm = BAND_HEADS * BAND_HD
    assert sp % CHUNK_ == 0 and ss % INV_SUB == 0 and ss <= CHUNK_
    assert cache_l0_band_k.shape[1] == BAND_PAST

    pw0, cw, seg, expand, qg, kg = _prep_l0(l0_w_in, l0_conv_w, l0_a_log, l0_dt_bias, l0_b_rel_bias,
                                            l0_b_q_g, l0_b_k_g)
    pw1 = _prep_l1(l1_w_in, l1_w_gate_up, l1_gate_bias)
    bias_p = _band_bias_table(l0_b_rel_bias, CHUNK_)
    bias_s = _band_bias_table(l0_b_rel_bias, ss)
    bf = lambda w: w.astype(BF16)
    mem2 = mem_prompt.reshape(bp * nm, d)

    mk0, mv0 = _memkv(mem2, l0_mem_norm_g.reshape(1, d), bf(l0_w_mkv), l0_mk_g.reshape(1, MEM_HD))
    mk0 = mk0.reshape(bp, nm, hw)
    mv0 = mv0.reshape(bp, nm, hw)
    l0_shared = (l0_norm_g, pw0, cw, seg, expand, qg, kg)
    l0_tail = (l0_a_onorm_g, bf(l0_w_out), l0_mnorm_g, bf(l0_w_mq), l0_mq_g, bf(l0_w_mo))
    zeros_conv = jnp.zeros((bp, CONV_TAPS - 1, 3 * GDN_HEADS * LANES), F32)
    zeros_gdn = jnp.zeros((bp, GDN_HEADS, GDN_DK, LANES), F32)
    zeros_band = jnp.zeros((bp, BAND_PAST, dm), F32)
    yp, p_conv, p_gdn, p_kn, p_v = _layer0(
        x_prompt, zeros_conv, zeros_gdn, zeros_band, zeros_band, mk0, mv0,
        *l0_shared, bias_p, *l0_tail, CHUNK_, True)
    ys, s_conv, s_gdn, s_kn, s_v = _layer0(
        x_sample, state_l0_gdn_conv, state_l0_gdn,
        cache_l0_band_k.reshape(bs, BAND_PAST, dm), cache_l0_band_v.reshape(bs, BAND_PAST, dm),
        cache_l0_mem_k.reshape(bs, nm, hw), cache_l0_mem_v.reshape(bs, nm, hw),
        *l0_shared, bias_s, *l0_tail, ss, False)

    mk1, mv1 = _memkv(mem2, l1_mem_norm_g.reshape(1, d), bf(l1_w_mkv), l1_mk_g.reshape(1, MEM_HD))
    mk1 = mk1.reshape(bp, nm, hw)
    mv1 = mv1.reshape(bp, nm, hw)
    l1_tail = (l1_c_onorm_g, bf(l1_w_out), l1_mnorm_g, bf(l1_w_mq), l1_mq_g, bf(l1_w_mo))
    zeros_gla = jnp.zeros((bp, GLA_HEADS, GLA_DK, GLA_DV), F32)
    yp, p_gla = _layer1(yp, zeros_gla, mk1, mv1, l1_norm_g, pw1, *l1_tail, CHUNK_)
    ys, s_gla = _layer1(ys, state_l1_gla, cache_l1_mem_k.reshape(bs, nm, hw),
                        cache_l1_mem_v.reshape(bs, nm, hw), l1_norm_g, pw1, *l1_tail, ss)

    keep = min(BAND_PAST, sp)
    h4 = lambda t: t.reshape(t.shape[0], t.shape[1], BAND_HEADS, BAND_HD)
    m4 = lambda t: t.reshape(bp, nm, MEM_HEADS, MEM_HD)
    return (yp, ys, p_conv, p_gdn, h4(p_kn[:, sp - keep:]), h4(p_v[:, sp - keep:]),
            m4(mk0), m4(mv0), p_gla, m4(mk1), m4(mv1),
            s_conv, s_gdn, h4(s_kn), h4(s_v), s_gla)
```

```python
import functools

import jax
import jax.numpy as jnp
import numpy as np
from jax import lax
from jax.experimental import pallas as pl
from jax.experimental.pallas import tpu as pltpu

F32 = jnp.float32
BF16 = jnp.bfloat16
NORM_EPS = 1e-6
LOG2_E = 1.4426950408889634

D_MODEL_ = 1024
CHUNK_ = 64
CONV_TAPS = 4
GDN_HEADS = 8
GDN_DK = 128
BAND_HEADS = 16
BAND_HD = 64
BAND_PAST = 512
BAND_MAX_REL = 128
GLA_HEADS = 8
GLA_DK = 128
GLA_DV = 256
GLA_RANK = 16
GLA_TAU = 16.0
GLA_SUB = 16
MEM_HEADS = 4
MEM_HD = 128
INV_SUB = 16
GDN_GROUP = 2
POST_SPLIT = 2
LANES = 128
VMEM_LIMIT = 56 * 1024 * 1024


def _dot(a, b):
    return jnp.dot(a, b, preferred_element_type=F32)


def _dot_nt(a, b):
    return lax.dot_general(a, b, (((1,), (1,)), ((), ())), preferred_element_type=F32)


def _dot_tn(a, b):
    return lax.dot_general(a, b, (((0,), (0,)), ((), ())), preferred_element_type=F32)


def _split2(x):
    hi = x.astype(BF16)
    lo = (x - hi.astype(F32)).astype(BF16)
    return hi, lo


def _split3(x):
    hi = x.astype(BF16)
    r = x - hi.astype(F32)
    mid = r.astype(BF16)
    lo = (r - mid.astype(F32)).astype(BF16)
    return hi, mid, lo


def _dot_x3(a, b):
    ah, al = _split2(a)
    bh, bl = _split2(b)
    return _dot(ah, bh) + (_dot(ah, bl) + _dot(al, bh))


def _dot_exact_lhs(a_bf, b):
    h, m, l = _split3(b)
    return _dot(a_bf, h) + (_dot(a_bf, m) + _dot(a_bf, l))


def _rms(x, g):
    ms = jnp.mean(x * x, axis=-1, keepdims=True)
    return x * lax.rsqrt(ms + NORM_EPS) * g


def _silu(x):
    return x * jax.nn.sigmoid(x)


def _softplus(x):
    return jnp.maximum(x, 0.0) + jnp.log1p(jnp.exp(-jnp.abs(x)))


def _softmax_rows(s):
    m = jnp.max(s, axis=-1, keepdims=True)
    e = jnp.exp(s - m)
    return e * (1.0 / jnp.sum(e, axis=-1, keepdims=True))


def _const_spec(shape):
    nd = len(shape)
    return pl.BlockSpec(shape, lambda *_: (0,) * nd, pipeline_mode=pl.Buffered(1))


def _params(sem):
    return pltpu.CompilerParams(dimension_semantics=sem, vmem_limit_bytes=VMEM_LIMIT)


def _row_tile(n, want):
    t = min(n, want)
    assert n % t == 0
    return t


def _proj0_body(x_ref, g_ref, wu_ref, wab_ref, wza_ref, wq_ref, wk_ref, wv_ref, wzb_ref,
                alog_ref, dtb_ref, qg_ref, kg_ref, seg_ref,
                u_ref, gb_ref, za_ref, qlo_ref, qhi_ref, kn_ref, klast_ref, vlo_ref, vhi_ref, vlast_ref, zb_ref):
    h = _rms(x_ref[...], g_ref[...]).astype(BF16)
    dm = BAND_HEADS * BAND_HD
    lo = (lax.broadcasted_iota(jnp.int32, (1, dm), 1) & (LANES - 1)) < BAND_HD
    seg = seg_ref[...]
    q = _dot(h, wq_ref[...])
    k = _dot(h, wk_ref[...])
    u_ref[...] = _dot(h, wu_ref[...])

    def head_rsqrt(x):
        ss = _dot((x * x).astype(BF16), seg)
        return lax.rsqrt(ss * (1.0 / BAND_HD) + NORM_EPS)

    rq = head_rsqrt(q)
    rk = head_rsqrt(k)
    za_ref[...] = _silu(_dot(h, wza_ref[...]))
    v = _dot(h, wv_ref[...])
    lo1 = lo[:, :LANES]

    def spread(r):
        return jnp.concatenate([jnp.where(lo1, r[:, 2 * s:2 * s + 1], r[:, 2 * s + 1:2 * s + 2])
                                for s in range(BAND_HEADS // 2)], axis=1)

    qn = q * spread(rq) * qg_ref[...] * (BAND_HD ** -0.5)
    kn = k * spread(rk) * kg_ref[...]
    zb_ref[...] = _silu(_dot(h, wzb_ref[...]))
    ab = _dot(h, wab_ref[...])
    qlo_ref[...] = jnp.where(lo, qn, 0.0).astype(BF16)
    qhi_ref[...] = jnp.where(lo, 0.0, qn).astype(BF16)
    kn_ref[...] = kn.astype(BF16)
    klast_ref[...] = kn
    vlo_ref[...] = jnp.where(lo, v, 0.0).astype(BF16)
    vhi_ref[...] = jnp.where(lo, 0.0, v).astype(BF16)
    vlast_ref[...] = v
    lane = lax.broadcasted_iota(jnp.int32, ab.shape, 1)
    gval = -jnp.exp(alog_ref[...]) * _softplus(ab + dtb_ref[...])
    gb_ref[...] = jnp.where(lane < GDN_HEADS, gval, jax.nn.sigmoid(ab))


def _proj0(x2, g, w, qg, kg, seg, rows_per_batch):
    n, d = x2.shape
    tm = _row_tile(n, 256)
    dm = BAND_HEADS * BAND_HD
    widths = (3 * 1024, LANES, 1024, dm, dm, dm, dm)
    row = lambda wd: pl.BlockSpec((tm, wd), lambda i: (i, 0))
    keep = min(BAND_PAST, rows_per_batch)
    if rows_per_batch > keep:
        assert rows_per_batch % tm == 0 and keep % tm == 0
        tpb, kt = rows_per_batch // tm, keep // tm
        last = pl.BlockSpec((tm, dm), lambda i: ((i // tpb) * kt + jnp.maximum(i % tpb - (tpb - kt), 0), 0))
        n_last = (n // rows_per_batch) * keep
    else:
        last, n_last = row(dm), n
    f32 = lambda rows, wd: jax.ShapeDtypeStruct((rows, wd), F32)
    bf16 = lambda wd: jax.ShapeDtypeStruct((n, wd), BF16)
    return pl.pallas_call(
        _proj0_body,
        grid=(n // tm,),
        in_specs=[row(d), _const_spec((1, d))]
        + [_const_spec((d, wd)) for wd in widths]
        + [_const_spec((1, LANES)), _const_spec((1, LANES)), _const_spec((1, dm)), _const_spec((1, dm)),
           _const_spec((dm, LANES))],
        out_specs=[row(3 * 1024), row(LANES), row(1024), row(dm), row(dm), row(dm), last, row(dm), row(dm), last,
                   row(dm)],
        out_shape=[f32(n, 3 * 1024), f32(n, LANES), f32(n, 1024), bf16(dm), bf16(dm), bf16(dm), f32(n_last, dm),
                   bf16(dm), bf16(dm), f32(n_last, dm), f32(n, dm)],
        compiler_params=_params(("arbitrary",)),
        name="proj0",
    )(x2, g, w["wu"], w["wab"], w["wza"], w["wq"], w["wk"], w["wv"], w["wzb"], w["alog"], w["dtb"],
      qg, kg, seg)


def _proj1_body(x_ref, g_ref, wq_ref, wk_ref, wv_ref, wlr_ref, wz_ref, wg_ref, gbias_ref,
                q_ref, k_ref, v_ref, lg_ref, z_ref):
    h = _rms(x_ref[...], g_ref[...]).astype(BF16)
    q_ref[...] = _dot(h, wq_ref[...])
    k_ref[...] = _dot(h, wk_ref[...])
    v_ref[...] = _dot(h, wv_ref[...])
    lr = _dot(h, wlr_ref[...])
    pre = _dot(lr.astype(BF16), wg_ref[...]) + gbias_ref[...]
    lg_ref[...] = -_softplus(-pre) * (1.0 / GLA_TAU)
    z_ref[...] = _silu(_dot(h, wz_ref[...]))


def _proj1(x2, g, w):
    n, d = x2.shape
    tm = _row_tile(n, 256)
    row = lambda wd: pl.BlockSpec((tm, wd), lambda i: (i, 0))
    outw = (1024, 1024, 2048, 1024, 2048)
    return pl.pallas_call(
        _proj1_body,
        grid=(n // tm,),
        in_specs=[row(d), _const_spec((1, d)), _const_spec((d, 1024)), _const_spec((d, 1024)),
                  _const_spec((d, 2048)), _const_spec((d, LANES)), _const_spec((d, 2048)),
                  _const_spec((LANES, 1024)), _const_spec((1, 1024))],
        out_specs=[row(wd) for wd in outw],
        out_shape=[jax.ShapeDtypeStruct((n, wd), F32) for wd in outw],
        compiler_params=_params(("arbitrary",)),
        name="proj1",
    )(x2, g, w["wq"], w["wk"], w["wv"], w["wlr"], w["wz"], w["wg"], w["gbias"])


def _memkv_body(m_ref, g_ref, w_ref, kg_ref, k_ref, v_ref):
    h = _rms(m_ref[...], g_ref[...]).astype(BF16)
    kv = _dot(h, w_ref[...])
    hw = MEM_HEADS * MEM_HD
    for hh in range(MEM_HEADS):
        sl = slice(hh * MEM_HD, (hh + 1) * MEM_HD)
        k_ref[:, sl] = _rms(kv[:, sl], kg_ref[...])
    v_ref[...] = kv[:, hw:]


def _memkv(m2, g, w_bf, kg):
    n, d = m2.shape
    tm = _row_tile(n, 256)
    hw = MEM_HEADS * MEM_HD
    row = lambda wd: pl.BlockSpec((tm, wd), lambda i: (i, 0))
    return pl.pallas_call(
        _memkv_body,
        grid=(n // tm,),
        in_specs=[row(d), _const_spec((1, d)), _const_spec((d, 2 * hw)), _const_spec((1, MEM_HD))],
        out_specs=[row(hw), row(hw)],
        out_shape=[jax.ShapeDtypeStruct((n, hw), F32)] * 2,
        compiler_params=_params(("arbitrary",)),
        name="memkv",
    )(m2, g, w_bf, kg)


def _each(fn, *lists):
    return [fn(*xs) for xs in zip(*lists)]


def _unit_lower_inverse(a, eye, bd):
    mm = lambda x, y: _dot(x.astype(BF16), y.astype(BF16))
    d = _each(lambda x: jnp.where(bd, x, 0.0), a)
    nl = _each(lambda x, y: x - y, a, d)
    d2 = _each(mm, d, d)
    d4 = _each(mm, d2, d2)
    td = _each(lambda x, y: mm(eye - x, eye + y), d, d2)
    d8 = _each(mm, d4, d4)
    td = _each(lambda x, y: mm(x, eye + y), td, d4)
    td = _each(lambda x, y: mm(x, eye + y), td, d8)
    m = _each(mm, td, nl)
    m2 = _each(mm, m, m)
    mt = _each(mm, m, td)
    return _each(lambda x, y, z: mm(eye + x, y - z), m2, td, mt)


def _gdn_body(u_ref, gb_ref, za_ref, cinit_ref, sinit_ref, cw_ref, og_ref,
              o_ref, s_ref, ubuf, *, L, NB):
    c = pl.program_id(1)
    ns = 3 * GDN_HEADS
    hist = 8

    @pl.when(c == 0)
    def _():
        for b in range(NB):
            for j in range(ns):
                ubuf[b * ns + j, 0:hist, :] = cinit_ref[b, :, j * LANES:(j + 1) * LANES]
        s_ref[...] = sinit_ref[...]

    for b in range(NB):
        for j in range(ns):
            ubuf[b * ns + j, hist:hist + L, :] = u_ref[b, :, j * LANES:(j + 1) * LANES]
    base = hist - (CONV_TAPS - 1)
    ys = []
    for b in range(NB):
        bsl = slice(b * ns, (b + 1) * ns)
        yb = ubuf[bsl, base:base + L, :] * cw_ref[:, 0:1, :]
        for i in range(1, CONV_TAPS):
            yb = yb + ubuf[bsl, base + i:base + i + L, :] * cw_ref[:, i:i + 1, :]
        ys.append(_silu(yb))
    ubuf[:, base:hist, :] = ubuf[:, base + L:hist + L, :]

    row = lax.broadcasted_iota(jnp.int32, (L, L), 0)
    col = lax.broadcasted_iota(jnp.int32, (L, L), 1)
    incl = row >= col
    strict = row > col
    sub_shift = INV_SUB.bit_length() - 1
    bd = lax.shift_right_logical(row, sub_shift) == lax.shift_right_logical(col, sub_shift)
    eye = jnp.where(row == col, 1.0, 0.0).astype(F32)
    tri = jnp.where(incl, 1.0, 0.0).astype(BF16)
    og = og_ref[...]
    heads = [(b, h) for b in range(NB) for h in range(GDN_HEADS)]
    l2n = lambda x: x * lax.rsqrt(jnp.sum(x * x, axis=-1, keepdims=True) + NORM_EPS)
    q = [l2n(ys[b][h]) * (GDN_DK ** -0.5) for b, h in heads]
    k = [l2n(ys[b][GDN_HEADS + h]) for b, h in heads]
    v = [ys[b][2 * GDN_HEADS + h] for b, h in heads]
    gc, gr, bc = [], [], []
    for b in range(NB):
        gbv = gb_ref[b]
        gcum = _dot_exact_lhs(tri, gbv)
        gpad = jnp.concatenate([gcum, jnp.zeros((LANES - L, LANES), F32)], axis=0)
        gt = gpad.T
        for h in range(GDN_HEADS):
            gc.append(gcum[:, h:h + 1])
            gr.append(gt[h:h + 1, 0:L])
            bc.append(gbv[:, GDN_HEADS + h:GDN_HEADS + h + 1])
    dec = _each(lambda c_, r_: jnp.where(incl, jnp.exp(jnp.where(incl, c_ - r_, 0.0)), 0.0), gc, gr)
    kb = _each(lambda x: x.astype(BF16), k)
    qb = _each(lambda x: x.astype(BF16), q)
    s = [s_ref[b, h] for b, h in heads]
    sb = _each(lambda x: x.astype(BF16), s)
    kq = _each(lambda k_, q_: _dot_nt(jnp.concatenate([k_, q_], axis=0), k_), kb, qb)
    a = _each(lambda b_, x, d_: jnp.where(strict, b_ * x[:L] * d_, 0.0), bc, kq, dec)
    t = _unit_lower_inverse(a, eye, bd)
    eg = _each(jnp.exp, gc)
    rhs = _each(lambda b_, v_, e_, k_: jnp.concatenate([b_ * v_, (b_ * e_) * k_], axis=1).astype(BF16),
                bc, v, eg, k)
    sol = _each(lambda x, y_: _dot(x.astype(BF16), y_), t, rhs)
    ksq = _each(lambda x, q_, y_: _dot(jnp.concatenate([x[:, LANES:].astype(BF16), q_], axis=0), y_),
                sol, qb, sb)
    ub = _each(lambda x, y_: (x[:, :LANES] - y_[:L]).astype(BF16), sol, ksq)
    qs = [x[L:] for x in ksq]
    qku = _each(lambda x, d_, u_: _dot((x[L:] * d_).astype(BF16), u_), kq, dec, ub)
    gl = [c_[L - 1:L, :] for c_ in gc]
    kd = _each(lambda k_, l_, c_: (k_ * jnp.exp(l_ - c_)).astype(BF16), k, gl, gc)
    ktu = _each(_dot_tn, kd, ub)
    for i, (b, h) in enumerate(heads):
        s_ref[b, h] = s[i] * jnp.exp(gl[i]) + ktu[i]
    for b in range(NB):
        za = za_ref[b]
        outs = [(_rms(qs[i] * eg[i] + qku[i], og) * za[:, h * LANES:(h + 1) * LANES]).astype(BF16)
                for i, (b_, h) in enumerate(heads) if b_ == b]
        o_ref[b] = jnp.concatenate(outs, axis=1)


def _gdn(u, gb, za, cinit, sinit, cw, og, L):
    b, s, _ = u.shape
    nc = s // L
    ns = 3 * GDN_HEADS
    nb = GDN_GROUP if b % GDN_GROUP == 0 else 1
    blk = lambda wd: pl.BlockSpec((nb, L, wd), lambda i, j: (i, j, 0))
    return pl.pallas_call(
        functools.partial(_gdn_body, L=L, NB=nb),
        grid=(b // nb, nc),
        in_specs=[blk(ns * LANES), blk(LANES), blk(GDN_HEADS * LANES),
                  pl.BlockSpec((nb, 8, ns * LANES), lambda i, j: (i, 0, 0)),
                  pl.BlockSpec((nb, GDN_HEADS, GDN_DK, LANES), lambda i, j: (i, 0, 0, 0)),
                  _const_spec((ns, 8, LANES)), _const_spec((1, LANES))],
        out_specs=[blk(GDN_HEADS * LANES),
                   pl.BlockSpec((nb, GDN_HEADS, GDN_DK, LANES), lambda i, j: (i, 0, 0, 0))],
        out_shape=[jax.ShapeDtypeStruct((b, s, GDN_HEADS * LANES), BF16),
                   jax.ShapeDtypeStruct((b, GDN_HEADS, GDN_DK, LANES), F32)],
        scratch_shapes=[pltpu.VMEM((nb * ns, 8 + L, LANES), F32)],
        compiler_params=_params(("arbitrary", "arbitrary")),
        name="gdn",
    )(u, gb, za, cinit, sinit, cw, og)


def _band_body(*refs, L, W, has_cache):
    if has_cache:
        (qlo_ref, qhi_ref, k_ref, vlo_ref, vhi_ref, zb_ref, bias_ref, ck_ref, cv_ref,
         o_ref, kscr, vlo, vhi) = refs
    else:
        (qlo_ref, qhi_ref, k_ref, vlo_ref, vhi_ref, zb_ref, bias_ref,
         o_ref, kscr, vlo, vhi, biasm) = refs
    c = pl.program_id(1)
    dm = BAND_HEADS * BAND_HD
    pad_chunks = BAND_PAST // L

    @pl.when(c == 0)
    def _():
        if has_cache:
            lo = (lax.broadcasted_iota(jnp.int32, (1, dm), 1) & (LANES - 1)) < BAND_HD
            kscr[0:BAND_PAST, :] = ck_ref[0].astype(BF16)
            cv = cv_ref[0]
            vlo[0:BAND_PAST, :] = jnp.where(lo, cv, 0.0).astype(BF16)
            vhi[0:BAND_PAST, :] = jnp.where(lo, 0.0, cv).astype(BF16)
        else:
            zero = jnp.zeros((BAND_PAST, dm), BF16)
            kscr[0:BAND_PAST, :] = zero
            vlo[0:BAND_PAST, :] = zero
            vhi[0:BAND_PAST, :] = zero

    new0 = pl.multiple_of(BAND_PAST + c * L, L)
    kscr[pl.ds(new0, L), :] = k_ref[0]
    vlo[pl.ds(new0, L), :] = vlo_ref[0]
    vhi[pl.ds(new0, L), :] = vhi_ref[0]

    if has_cache:
        bias_src = bias_ref
    else:
        @pl.when(c < pad_chunks)
        def _():
            wcol = lax.broadcasted_iota(jnp.int32, (1, W), 1)
            valid = (wcol + c * L) >= BAND_PAST
            for h in range(BAND_HEADS):
                biasm[h] = jnp.where(valid, bias_ref[h], -jnp.inf)

        @pl.when(c == pad_chunks)
        def _():
            biasm[...] = bias_ref[...]

        bias_src = biasm

    w0 = pl.multiple_of(c * L, L)
    zb = zb_ref[0]
    slabs = [slice(s * LANES, (s + 1) * LANES) for s in range(BAND_HEADS // 2)]
    sc = []
    for sl in slabs:
        ks = kscr[pl.ds(w0, W), sl]
        sc.append(_dot_nt(qlo_ref[0, :, sl], ks))
        sc.append(_dot_nt(qhi_ref[0, :, sl], ks))
    ps, rs = [], []
    for h in range(BAND_HEADS):
        x = sc[h] + bias_src[h]
        e = jnp.exp(x - jnp.max(x, axis=-1, keepdims=True))
        rs.append(1.0 / jnp.sum(e, axis=-1, keepdims=True))
        ps.append(e.astype(BF16))
    pv = []
    for i, sl in enumerate(slabs):
        pv.append(_dot(ps[2 * i], vlo[pl.ds(w0, W), sl]))
        pv.append(_dot(ps[2 * i + 1], vhi[pl.ds(w0, W), sl]))
    outs = [((pv[2 * i] * rs[2 * i] + pv[2 * i + 1] * rs[2 * i + 1]) * zb[:, sl]).astype(BF16)
            for i, sl in enumerate(slabs)]
    o_ref[0] = jnp.concatenate(outs, axis=1)


def _band(qlo, qhi, kn, vlo, vhi, zb, bias, caches, L):
    b, s, dm = qlo.shape
    nc = s // L
    W = BAND_PAST + L
    blk = pl.BlockSpec((1, L, dm), lambda i, j: (i, j, 0))
    cache = pl.BlockSpec((1, BAND_PAST, dm), lambda i, j: (i, 0, 0))
    has_cache = caches is not None
    scratch = [pltpu.VMEM((BAND_PAST + s, dm), BF16)] * 3
    if not has_cache:
        scratch = scratch + [pltpu.VMEM((BAND_HEADS, L, W), F32)]
    return pl.pallas_call(
        functools.partial(_band_body, L=L, W=W, has_cache=has_cache),
        grid=(b, nc),
        in_specs=[blk] * 6 + [_const_spec((BAND_HEADS, L, W))] + ([cache, cache] if has_cache else []),
        out_specs=blk,
        out_shape=jax.ShapeDtypeStruct((b, s, dm), BF16),
        scratch_shapes=scratch,
        compiler_params=_params(("arbitrary", "arbitrary")),
        name="band",
    )(qlo, qhi, kn, vlo, vhi, zb, bias, *(caches if has_cache else ()))


def _gla_body(q_ref, k_ref, v_ref, lg_ref, z_ref, sinit_ref, og_ref, o_ref, st_ref, cbs, *, L):
    c = pl.program_id(1)

    @pl.when(c == 0)
    def _():
        st_ref[...] = sinit_ref[...]

    nb = L // GLA_SUB
    half = GLA_SUB // 2
    row = lax.broadcasted_iota(jnp.int32, (L, L), 0)
    col = lax.broadcasted_iota(jnp.int32, (L, L), 1)
    tri = jnp.where(row >= col, 1.0, 0.0).astype(BF16)
    cb_all = _dot_exact_lhs(tri, lg_ref[0]) * LOG2_E
    cbs[...] = cb_all
    ex = jnp.exp2
    lane = lax.broadcasted_iota(jnp.int32, (GLA_SUB, LANES), 1)
    lane8 = lax.broadcasted_iota(jnp.int32, (half, LANES), 1)
    row8 = lax.broadcasted_iota(jnp.int32, (half, LANES), 0)
    og = og_ref[...]
    scale = GLA_DK ** -0.5

    heads = range(GLA_HEADS)
    sls = [slice(h * GLA_DK, (h + 1) * GLA_DK) for h in heads]
    vsls = [slice(h * GLA_DV, (h + 1) * GLA_DV) for h in heads]
    cb = [cb_all[:, sl] for sl in sls]
    q = [q_ref[0, :, sl] * scale for sl in sls]
    k = [k_ref[0, :, sl] for sl in sls]
    vb = [v_ref[0, :, vsl].astype(BF16) for vsl in vsls]
    st = [st_ref[0, h] for h in heads]
    z = z_ref[0]
    o_inter = _each(lambda q_, c_, s_: _dot_nt((q_ * ex(c_)).astype(BF16), s_.astype(BF16)), q, cb, st)
    cl = [c_[L - 1:L] for c_ in cb]
    ke = _each(lambda k_, l_, c_: (k_ * ex(l_ - c_)).astype(BF16), k, cl, cb)
    vtk = _each(_dot_tn, vb, ke)
    off = [[None] * nb for _ in heads]
    for ib in range(1, nb):
        r0 = ib * GLA_SUB
        for h in heads:
            ref = cb[h][r0 - 1:r0]
            qe = (q[h][r0:r0 + GLA_SUB] * ex(cb[h][r0:r0 + GLA_SUB] - ref)).astype(BF16)
            kx = (k[h] * ex(jnp.minimum(ref - cb[h], 0.0))).astype(BF16)
            x = _dot_nt(qe, kx)
            if L < LANES:
                x = jnp.concatenate([x, jnp.zeros((GLA_SUB, LANES - L), F32)], axis=1)
            off[h][ib] = jnp.where(lane < r0, x, 0.0)
    strips = [[None] * nb for _ in heads]
    for ib in range(nb):
        r0 = ib * GLA_SUB
        take_lo = [(lane8 == r0 + jj) & (row8 >= jj) for jj in range(half)]
        take_mid = [lane8 == r0 + jj for jj in range(half)]
        take_hi = [(lane8 == r0 + half + jj) & (row8 >= jj) for jj in range(half)]
        for h in heads:
            sl = sls[h]
            q_lo, q_hi = q[h][r0:r0 + half], q[h][r0 + half:r0 + GLA_SUB]
            c_lo, c_hi = cb[h][r0:r0 + half], cb[h][r0 + half:r0 + GLA_SUB]
            if ib > 0:
                s_lo, s_hi = off[h][ib][:half], off[h][ib][half:]
            else:
                s_lo = s_hi = jnp.zeros((half, LANES), F32)
            pair = lambda qq, cc, kj, cj: jnp.sum(qq * kj * ex(cc - cj), axis=-1, keepdims=True)
            for jj in range(half):
                kj, cj = k_ref[0, r0 + jj:r0 + jj + 1, sl], cbs[r0 + jj:r0 + jj + 1, sl]
                s_lo = jnp.where(take_lo[jj], pair(q_lo, c_lo, kj, cj), s_lo)
                s_hi = jnp.where(take_mid[jj], pair(q_hi, c_hi, kj, cj), s_hi)
            for jj in range(half):
                j = r0 + half + jj
                kj, cj = k_ref[0, j:j + 1, sl], cbs[j:j + 1, sl]
                s_hi = jnp.where(take_hi[jj], pair(q_hi, c_hi, kj, cj), s_hi)
            strips[h][ib] = jnp.concatenate([s_lo, s_hi], axis=0)
    att = [jnp.concatenate(strips[h], axis=0)[:, :L].astype(BF16) for h in heads]
    o_intra = _each(_dot, att, vb)
    for h in heads:
        st_ref[0, h] = st[h] * ex(cl[h]) + vtk[h]
    outs = [(_rms(o_inter[h] + o_intra[h], og) * z[:, vsls[h]]).astype(BF16) for h in heads]
    o_ref[0] = jnp.concatenate(outs, axis=1)


def _gla(q, k, v, lg, z, sinit_t, og, L):
    b, s, _ = q.shape
    nc = s // L
    qk = pl.BlockSpec((1, L, GLA_HEADS * GLA_DK), lambda i, j: (i, j, 0))
    vv = pl.BlockSpec((1, L, GLA_HEADS * GLA_DV), lambda i, j: (i, j, 0))
    st = pl.BlockSpec((1, GLA_HEADS, GLA_DV, GLA_DK), lambda i, j: (i, 0, 0, 0))
    return pl.pallas_call(
        functools.partial(_gla_body, L=L),
        grid=(b, nc),
        in_specs=[qk, qk, vv, qk, vv, st, _const_spec((1, GLA_DV))],
        out_specs=[vv, st],
        out_shape=[jax.ShapeDtypeStruct((b, s, GLA_HEADS * GLA_DV), BF16),
                   jax.ShapeDtypeStruct((b, GLA_HEADS, GLA_DV, GLA_DK), F32)],
        scratch_shapes=[pltpu.VMEM((L, GLA_HEADS * GLA_DK), F32)],
        compiler_params=_params(("arbitrary", "arbitrary")),
        name="gla",
    )(q, k, v, lg, z, sinit_t, og)


def _post_body(*refs, n_o):
    x_ref = refs[0]
    o_refs = refs[1:1 + n_o]
    wout_ref, mg_ref, wmq_ref, mqg_ref, mk_ref, mv_ref, wmo_ref, y_ref = refs[1 + n_o:]
    tm = x_ref.shape[1]
    ts = tm // POST_SPLIT if tm % (8 * POST_SPLIT) == 0 else tm
    rows = [slice(t * ts, (t + 1) * ts) for t in range(tm // ts)]
    acc = [x_ref[0, r, :] for r in rows]
    off = 0
    for o_ref in o_refs:
        kd = o_ref.shape[-1]
        w = wout_ref[off:off + kd, :]
        acc = [a + _dot(o_ref[0, r, :], w) for a, r in zip(acc, rows)]
        off += kd
    hm = [_rms(a, mg_ref[...]).astype(BF16) for a in acc]
    qz = [_dot(h_, wmq_ref[...]) for h_ in hm]
    hw = MEM_HEADS * MEM_HD
    sls = [slice(h * MEM_HD, (h + 1) * MEM_HD) for h in range(MEM_HEADS)]
    mkb = [mk_ref[0, :, sl].astype(BF16) for sl in sls]
    mvb = [mv_ref[0, :, sl].astype(BF16) for sl in sls]
    qn = [[_rms(z[:, sl], mqg_ref[...]).astype(BF16) for sl in sls] for z in qz]
    sc = [[_dot_nt(qh, kh) for qh, kh in zip(qt, mkb)] for qt in qn]
    p = [[_softmax_rows(x * (MEM_HD ** -0.5)).astype(BF16) for x in st] for st in sc]
    oh = [[_dot(ph, vh) for ph, vh in zip(pt, mvb)] for pt in p]
    for t, r in enumerate(rows):
        outs = [oh[t][h] * _silu(qz[t][:, hw + h * MEM_HD:hw + (h + 1) * MEM_HD]) for h in range(MEM_HEADS)]
        oh[t] = jnp.concatenate(outs, axis=1).astype(BF16)
    ym = [_dot(om, wmo_ref[...]) for om in oh]
    for t, r in enumerate(rows):
        y_ref[0, r, :] = acc[t] + ym[t]


def _post(x, os_, wout, mg, wmq, mqg, mk, mv, wmo):
    b, s, d = x.shape
    tm = _row_tile(s, 512)
    hw = MEM_HEADS * MEM_HD
    nm = mk.shape[1]
    blk = lambda wd: pl.BlockSpec((1, tm, wd), lambda i, j: (i, j, 0))
    mem = pl.BlockSpec((1, nm, hw), lambda i, j: (i, 0, 0))
    kin = sum(o.shape[-1] for o in os_)
    return pl.pallas_call(
        functools.partial(_post_body, n_o=len(os_)),
        grid=(b, s // tm),
        in_specs=[blk(d)] + [blk(o.shape[-1]) for o in os_]
        + [_const_spec((kin, d)), _const_spec((1, d)), _const_spec((d, 2 * hw)), _const_spec((1, MEM_HD)),
           mem, mem, _const_spec((hw, d))],
        out_specs=blk(d),
        out_shape=jax.ShapeDtypeStruct((b, s, d), F32),
        compiler_params=_params(("arbitrary", "arbitrary")),
        name="post",
    )(x, *os_, wout, mg, wmq, mqg, mk, mv, wmo)


def _pad_cols(w, n):
    return jnp.pad(w, ((0, 0), (0, n - w.shape[1])))


def _prep_l0(w_in, conv_w, a_log, dt_bias, q_g, k_g):
    c0 = 3 * 1024
    ab = w_in[:, c0:c0 + 2 * GDN_HEADS]
    c1 = c0 + 2 * GDN_HEADS
    cols = lambda i: w_in[:, c1 + i * 1024:c1 + (i + 1) * 1024].astype(BF16)
    w = dict(
        wu=w_in[:, :c0].astype(BF16), wab=_pad_cols(ab, LANES).astype(BF16),
        wza=cols(0), wq=cols(1), wk=cols(2), wv=cols(3), wzb=cols(4),
        alog=jnp.pad(a_log, (0, LANES - GDN_HEADS)).reshape(1, LANES).astype(F32),
        dtb=jnp.pad(dt_bias, (0, LANES - GDN_HEADS)).reshape(1, LANES).astype(F32),
    )
    ns = 3 * GDN_HEADS
    cw = jnp.pad(conv_w.astype(F32), ((0, 8 - CONV_TAPS), (0, 0))).reshape(8, ns, LANES).transpose(1, 0, 2)
    dm = BAND_HEADS * BAND_HD
    head_of = np.arange(dm) // BAND_HD
    seg = jnp.asarray(head_of[:, None] == np.arange(LANES)[None, :], BF16)
    qg = jnp.tile(q_g.astype(F32), BAND_HEADS).reshape(1, dm)
    kg = jnp.tile(k_g.astype(F32), BAND_HEADS).reshape(1, dm)
    return w, cw, seg, qg, kg


def _band_bias_table(rel_bias, L):
    W = BAND_PAST + L
    n = np.arange(W + L - 1)
    idx = np.clip(BAND_PAST + (L - 1) - n, -BAND_MAX_REL, BAND_MAX_REL) + BAND_MAX_REL
    strip = rel_bias.astype(F32)[:, idx]
    return jnp.stack([strip[:, L - 1 - a:L - 1 - a + W] for a in range(L)], axis=1)


def _prep_l1(w_in, w_gate_up, gate_bias):
    qk = GLA_HEADS * GLA_DK
    vw = GLA_HEADS * GLA_DV
    o = np.cumsum([0, qk, qk, vw, GLA_RANK, vw])
    return dict(
        wq=w_in[:, o[0]:o[1]].astype(BF16), wk=w_in[:, o[1]:o[2]].astype(BF16),
        wv=w_in[:, o[2]:o[3]].astype(BF16), wlr=_pad_cols(w_in[:, o[3]:o[4]], LANES).astype(BF16),
        wz=w_in[:, o[4]:o[5]].astype(BF16),
        wg=jnp.pad(w_gate_up, ((0, LANES - GLA_RANK), (0, 0))).astype(BF16),
        gbias=gate_bias.reshape(1, qk).astype(F32),
    )


def _layer0(x, conv_state, gdn_state, band_caches, mk, mv, norm_g, pw, cw, seg, qg, kg, bias,
            a_onorm_g, w_out, mnorm_g, w_mq, mq_g, w_mo, L):
    b, s, d = x.shape
    outs = _proj0(x.reshape(b * s, d), norm_g.reshape(1, d), pw, qg, kg, seg, s)
    u, gb, za, qlo, qhi, kn, klast, vlo, vhi, vlast, zb = (t.reshape(b, -1, t.shape[-1]) for t in outs)
    cinit = jnp.pad(conv_state.astype(F32), ((0, 0), (8 - (CONV_TAPS - 1), 0), (0, 0)))
    o_a, s_new = _gdn(u, gb, za, cinit, gdn_state.astype(F32), cw, a_onorm_g.reshape(1, LANES), L)
    o_b = _band(qlo, qhi, kn, vlo, vhi, zb, bias, band_caches, L)
    y = _post(x, (o_a, o_b), w_out, mnorm_g.reshape(1, d), w_mq, mq_g.reshape(1, MEM_HD), mk, mv, w_mo)
    return y, u[:, s - (CONV_TAPS - 1):, :], s_new, klast, vlast


def _layer1(x, gla_state, mk, mv, norm_g, pw, c_onorm_g, w_out, mnorm_g, w_mq, mq_g, w_mo, L):
    b, s, d = x.shape
    q, k, v, lg, z = _proj1(x.reshape(b * s, d), norm_g.reshape(1, d), pw)
    r3 = lambda t: t.reshape(b, s, t.shape[-1])
    q, k, v, lg, z = map(r3, (q, k, v, lg, z))
    st0 = jnp.swapaxes(gla_state.astype(F32), 2, 3)
    o, st = _gla(q, k, v, lg, z, st0, c_onorm_g.reshape(1, GLA_DV), L)
    y = _post(x, (o,), w_out, mnorm_g.reshape(1, d), w_mq, mq_g.reshape(1, MEM_HD), mk, mv, w_mo)
    return y, jnp.swapaxes(st, 2, 3)


def kernel(x_prompt, x_sample, mem_prompt, state_l0_gdn_conv, state_l0_gdn, cache_l0_band_k, cache_l0_band_v, cache_l0_mem_k, cache_l0_mem_v, state_l1_gla, cache_l1_mem_k, cache_l1_mem_v, l0_norm_g, l0_w_in, l0_conv_w, l0_a_log, l0_dt_bias, l0_a_onorm_g, l0_b_q_g, l0_b_k_g, l0_b_rel_bias, l0_w_out, l0_mnorm_g, l0_mem_norm_g, l0_w_mkv, l0_mk_g, l0_w_mq, l0_mq_g, l0_w_mo, l1_norm_g, l1_w_in, l1_w_gate_up, l1_gate_bias, l1_c_onorm_g, l1_w_out, l1_mnorm_g, l1_mem_norm_g, l1_w_mkv, l1_mk_g, l1_w_mq, l1_mq_g, l1_w_mo):
    bp, sp, d = x_prompt.shape
    bs, ss, _ = x_sample.shape
    nm = mem_prompt.shape[1]
    hw = MEM_HEADS * MEM_HD
    dm = BAND_HEADS * BAND_HD
    assert sp % CHUNK_ == 0 and ss % INV_SUB == 0 and ss <= CHUNK_
    assert cache_l0_band_k.shape[1] == BAND_PAST

    pw0, cw, seg, qg, kg = _prep_l0(l0_w_in, l0_conv_w, l0_a_log, l0_dt_bias, l0_b_q_g, l0_b_k_g)
    pw1 = _prep_l1(l1_w_in, l1_w_gate_up, l1_gate_bias)
    bias_p = _band_bias_table(l0_b_rel_bias, CHUNK_)
    bias_s = _band_bias_table(l0_b_rel_bias, ss)
    bf = lambda w: w.astype(BF16)
    mem2 = mem_prompt.reshape(bp * nm, d)

    mk0, mv0 = _memkv(mem2, l0_mem_norm_g.reshape(1, d), bf(l0_w_mkv), l0_mk_g.reshape(1, MEM_HD))
    mk0 = mk0.reshape(bp, nm, hw)
    mv0 = mv0.reshape(bp, nm, hw)
    l0_shared = (l0_norm_g, pw0, cw, seg, qg, kg)
    l0_tail = (l0_a_onorm_g, bf(l0_w_out), l0_mnorm_g, bf(l0_w_mq), l0_mq_g, bf(l0_w_mo))
    zeros_conv = jnp.zeros((bp, CONV_TAPS - 1, 3 * GDN_HEADS * LANES), F32)
    zeros_gdn = jnp.zeros((bp, GDN_HEADS, GDN_DK, LANES), F32)
    yp, p_conv, p_gdn, p_kn, p_v = _layer0(
        x_prompt, zeros_conv, zeros_gdn, None, mk0, mv0,
        *l0_shared, bias_p, *l0_tail, CHUNK_)
    ys, s_conv, s_gdn, s_kn, s_v = _layer0(
        x_sample, state_l0_gdn_conv, state_l0_gdn,
        (cache_l0_band_k.reshape(bs, BAND_PAST, dm), cache_l0_band_v.reshape(bs, BAND_PAST, dm)),
        cache_l0_mem_k.reshape(bs, nm, hw), cache_l0_mem_v.reshape(bs, nm, hw),
        *l0_shared, bias_s, *l0_tail, ss)

    mk1, mv1 = _memkv(mem2, l1_mem_norm_g.reshape(1, d), bf(l1_w_mkv), l1_mk_g.reshape(1, MEM_HD))
    mk1 = mk1.reshape(bp, nm, hw)
    mv1 = mv1.reshape(bp, nm, hw)
    l1_tail = (l1_c_onorm_g, bf(l1_w_out), l1_mnorm_g, bf(l1_w_mq), l1_mq_g, bf(l1_w_mo))
    zeros_gla = jnp.zeros((bp, GLA_HEADS, GLA_DK, GLA_DV), F32)
    yp, p_gla = _layer1(yp, zeros_gla, mk1, mv1, l1_norm_g, pw1, *l1_tail, CHUNK_)
    ys, s_gla = _layer1(ys, state_l1_gla, cache_l1_mem_k.reshape(bs, nm, hw),
                        cache_l1_mem_v.reshape(bs, nm, hw), l1_norm_g, pw1, *l1_tail, ss)

    h4 = lambda t: t.reshape(t.shape[0], t.shape[1], BAND_HEADS, BAND_HD)
    m4 = lambda t: t.reshape(bp, nm, MEM_HEADS, MEM_HD)
    return (yp, ys, p_conv, p_gdn, h4(p_kn), h4(p_v),
            m4(mk0), m4(mv0), p_gla, m4(mk1), m4(mv1),
            s_conv, s_gdn, h4(s_kn), h4(s_v), s_gla)
```

```python
import functools

import jax
import jax.numpy as jnp
import numpy as np
from jax import lax
from jax.experimental import pallas as pl
from jax.experimental.pallas import tpu as pltpu

F32 = jnp.float32
BF16 = jnp.bfloat16
NORM_EPS = 1e-6
LOG2_E = 1.4426950408889634

D_MODEL_ = 1024
CHUNK_ = 64
CONV_TAPS = 4
GDN_HEADS = 8
GDN_DK = 128
BAND_HEADS = 16
BAND_HD = 64
BAND_PAST = 512
BAND_MAX_REL = 128
GLA_HEADS = 8
GLA_DK = 128
GLA_DV = 256
GLA_RANK = 16
GLA_TAU = 16.0
GLA_SUB = 16
MEM_HEADS = 4
MEM_HD = 128
INV_SUB = 16
GDN_GROUP = 2
POST_SPLIT = 2
BAND_STEP_CHUNKS = 2
LANES = 128
VMEM_LIMIT = 56 * 1024 * 1024


def _dot(a, b):
    return jnp.dot(a, b, preferred_element_type=F32)


def _dot_nt(a, b):
    return lax.dot_general(a, b, (((1,), (1,)), ((), ())), preferred_element_type=F32)


def _dot_tn(a, b):
    return lax.dot_general(a, b, (((0,), (0,)), ((), ())), preferred_element_type=F32)


def _split2(x):
    hi = x.astype(BF16)
    lo = (x - hi.astype(F32)).astype(BF16)
    return hi, lo


def _split3(x):
    hi = x.astype(BF16)
    r = x - hi.astype(F32)
    mid = r.astype(BF16)
    lo = (r - mid.astype(F32)).astype(BF16)
    return hi, mid, lo


def _dot_x3(a, b):
    ah, al = _split2(a)
    bh, bl = _split2(b)
    return _dot(ah, bh) + (_dot(ah, bl) + _dot(al, bh))


def _dot_exact_lhs(a_bf, b):
    h, m, l = _split3(b)
    return _dot(a_bf, h) + (_dot(a_bf, m) + _dot(a_bf, l))


def _rms(x, g):
    ms = jnp.mean(x * x, axis=-1, keepdims=True)
    return x * lax.rsqrt(ms + NORM_EPS) * g


def _silu(x):
    return x * jax.nn.sigmoid(x)


def _softplus(x):
    return jnp.maximum(x, 0.0) + jnp.log1p(jnp.exp(-jnp.abs(x)))


def _softmax_rows(s):
    m = jnp.max(s, axis=-1, keepdims=True)
    e = jnp.exp(s - m)
    return e * (1.0 / jnp.sum(e, axis=-1, keepdims=True))


def _const_spec(shape):
    nd = len(shape)
    return pl.BlockSpec(shape, lambda *_: (0,) * nd, pipeline_mode=pl.Buffered(1))


def _params(sem):
    return pltpu.CompilerParams(dimension_semantics=sem, vmem_limit_bytes=VMEM_LIMIT)


def _row_tile(n, want):
    t = min(n, want)
    assert n % t == 0
    return t


def _proj0_body(x_ref, g_ref, wu_ref, wab_ref, wza_ref, wq_ref, wk_ref, wv_ref, wzb_ref,
                alog_ref, dtb_ref, qg_ref, kg_ref, seg_ref,
                u_ref, gb_ref, za_ref, qlo_ref, qhi_ref, kn_ref, klast_ref, vlo_ref, vhi_ref, vlast_ref, zb_ref):
    h = _rms(x_ref[...], g_ref[...]).astype(BF16)
    dm = BAND_HEADS * BAND_HD
    lo = (lax.broadcasted_iota(jnp.int32, (1, dm), 1) & (LANES - 1)) < BAND_HD
    seg = seg_ref[...]
    q = _dot(h, wq_ref[...])
    k = _dot(h, wk_ref[...])
    u_ref[...] = _dot(h, wu_ref[...])

    def head_rsqrt(x):
        ss = _dot((x * x).astype(BF16), seg)
        return lax.rsqrt(ss * (1.0 / BAND_HD) + NORM_EPS)

    rq = head_rsqrt(q)
    rk = head_rsqrt(k)
    za_ref[...] = _silu(_dot(h, wza_ref[...]))
    v = _dot(h, wv_ref[...])
    lo1 = lo[:, :LANES]

    def spread(r):
        return jnp.concatenate([jnp.where(lo1, r[:, 2 * s:2 * s + 1], r[:, 2 * s + 1:2 * s + 2])
                                for s in range(BAND_HEADS // 2)], axis=1)

    qn = q * spread(rq) * qg_ref[...] * (BAND_HD ** -0.5)
    kn = k * spread(rk) * kg_ref[...]
    zb_ref[...] = _silu(_dot(h, wzb_ref[...]))
    ab = _dot(h, wab_ref[...])
    qlo_ref[...] = jnp.where(lo, qn, 0.0).astype(BF16)
    qhi_ref[...] = jnp.where(lo, 0.0, qn).astype(BF16)
    kn_ref[...] = kn.astype(BF16)
    klast_ref[...] = kn
    vlo_ref[...] = jnp.where(lo, v, 0.0).astype(BF16)
    vhi_ref[...] = jnp.where(lo, 0.0, v).astype(BF16)
    vlast_ref[...] = v
    lane = lax.broadcasted_iota(jnp.int32, ab.shape, 1)
    gval = -jnp.exp(alog_ref[...]) * _softplus(ab + dtb_ref[...])
    gb_ref[...] = jnp.where(lane < GDN_HEADS, gval, jax.nn.sigmoid(ab))


def _proj0(x2, g, w, qg, kg, seg, rows_per_batch):
    n, d = x2.shape
    tm = _row_tile(n, 256)
    dm = BAND_HEADS * BAND_HD
    widths = (3 * 1024, LANES, 1024, dm, dm, dm, dm)
    row = lambda wd: pl.BlockSpec((tm, wd), lambda i: (i, 0))
    keep = min(BAND_PAST, rows_per_batch)
    if rows_per_batch > keep:
        assert rows_per_batch % tm == 0 and keep % tm == 0
        tpb, kt = rows_per_batch // tm, keep // tm
        last = pl.BlockSpec((tm, dm), lambda i: ((i // tpb) * kt + jnp.maximum(i % tpb - (tpb - kt), 0), 0))
        n_last = (n // rows_per_batch) * keep
    else:
        last, n_last = row(dm), n
    f32 = lambda rows, wd: jax.ShapeDtypeStruct((rows, wd), F32)
    bf16 = lambda wd: jax.ShapeDtypeStruct((n, wd), BF16)
    return pl.pallas_call(
        _proj0_body,
        grid=(n // tm,),
        in_specs=[row(d), _const_spec((1, d))]
        + [_const_spec((d, wd)) for wd in widths]
        + [_const_spec((1, LANES)), _const_spec((1, LANES)), _const_spec((1, dm)), _const_spec((1, dm)),
           _const_spec((dm, LANES))],
        out_specs=[row(3 * 1024), row(LANES), row(1024), row(dm), row(dm), row(dm), last, row(dm), row(dm), last,
                   row(dm)],
        out_shape=[f32(n, 3 * 1024), f32(n, LANES), f32(n, 1024), bf16(dm), bf16(dm), bf16(dm), f32(n_last, dm),
                   bf16(dm), bf16(dm), f32(n_last, dm), f32(n, dm)],
        compiler_params=_params(("arbitrary",)),
        name="proj0",
    )(x2, g, w["wu"], w["wab"], w["wza"], w["wq"], w["wk"], w["wv"], w["wzb"], w["alog"], w["dtb"],
      qg, kg, seg)


def _proj1_body(x_ref, g_ref, wq_ref, wk_ref, wv_ref, wlr_ref, wz_ref, wg_ref, gbias_ref,
                q_ref, k_ref, v_ref, lg_ref, z_ref):
    h = _rms(x_ref[...], g_ref[...]).astype(BF16)
    q_ref[...] = _dot(h, wq_ref[...])
    k_ref[...] = _dot(h, wk_ref[...])
    v_ref[...] = _dot(h, wv_ref[...])
    lr = _dot(h, wlr_ref[...])
    pre = _dot(lr.astype(BF16), wg_ref[...]) + gbias_ref[...]
    lg_ref[...] = -_softplus(-pre) * (1.0 / GLA_TAU)
    z_ref[...] = _silu(_dot(h, wz_ref[...]))


def _proj1(x2, g, w):
    n, d = x2.shape
    tm = _row_tile(n, 256)
    row = lambda wd: pl.BlockSpec((tm, wd), lambda i: (i, 0))
    outw = (1024, 1024, 2048, 1024, 2048)
    return pl.pallas_call(
        _proj1_body,
        grid=(n // tm,),
        in_specs=[row(d), _const_spec((1, d)), _const_spec((d, 1024)), _const_spec((d, 1024)),
                  _const_spec((d, 2048)), _const_spec((d, LANES)), _const_spec((d, 2048)),
                  _const_spec((LANES, 1024)), _const_spec((1, 1024))],
        out_specs=[row(wd) for wd in outw],
        out_shape=[jax.ShapeDtypeStruct((n, wd), F32) for wd in outw],
        compiler_params=_params(("arbitrary",)),
        name="proj1",
    )(x2, g, w["wq"], w["wk"], w["wv"], w["wlr"], w["wz"], w["wg"], w["gbias"])


def _memkv_body(m_ref, g_ref, w_ref, kg_ref, k_ref, v_ref):
    h = _rms(m_ref[...], g_ref[...]).astype(BF16)
    kv = _dot(h, w_ref[...])
    hw = MEM_HEADS * MEM_HD
    for hh in range(MEM_HEADS):
        sl = slice(hh * MEM_HD, (hh + 1) * MEM_HD)
        k_ref[:, sl] = _rms(kv[:, sl], kg_ref[...])
    v_ref[...] = kv[:, hw:]


def _memkv(m2, g, w_bf, kg):
    n, d = m2.shape
    tm = _row_tile(n, 256)
    hw = MEM_HEADS * MEM_HD
    row = lambda wd: pl.BlockSpec((tm, wd), lambda i: (i, 0))
    return pl.pallas_call(
        _memkv_body,
        grid=(n // tm,),
        in_specs=[row(d), _const_spec((1, d)), _const_spec((d, 2 * hw)), _const_spec((1, MEM_HD))],
        out_specs=[row(hw), row(hw)],
        out_shape=[jax.ShapeDtypeStruct((n, hw), F32)] * 2,
        compiler_params=_params(("arbitrary",)),
        name="memkv",
    )(m2, g, w_bf, kg)


def _each(fn, *lists):
    return [fn(*xs) for xs in zip(*lists)]


def _unit_lower_inverse(a, eye, bd):
    mm = lambda x, y: _dot(x.astype(BF16), y.astype(BF16))
    d = _each(lambda x: jnp.where(bd, x, 0.0), a)
    nl = _each(lambda x, y: x - y, a, d)
    d2 = _each(mm, d, d)
    d4 = _each(mm, d2, d2)
    td = _each(lambda x, y: mm(eye - x, eye + y), d, d2)
    d8 = _each(mm, d4, d4)
    td = _each(lambda x, y: mm(x, eye + y), td, d4)
    td = _each(lambda x, y: mm(x, eye + y), td, d8)
    m = _each(mm, td, nl)
    m2 = _each(mm, m, m)
    mt = _each(mm, m, td)
    return _each(lambda x, y, z: mm(eye + x, y - z), m2, td, mt)


def _gdn_body(u_ref, gb_ref, za_ref, cinit_ref, sinit_ref, cw_ref, og_ref,
              o_ref, s_ref, ubuf, *, L, NB):
    c = pl.program_id(1)
    ns = 3 * GDN_HEADS
    hist = 8

    @pl.when(c == 0)
    def _():
        for b in range(NB):
            for j in range(ns):
                ubuf[b * ns + j, 0:hist, :] = cinit_ref[b, :, j * LANES:(j + 1) * LANES]
        s_ref[...] = sinit_ref[...]

    for b in range(NB):
        for j in range(ns):
            ubuf[b * ns + j, hist:hist + L, :] = u_ref[b, :, j * LANES:(j + 1) * LANES]
    base = hist - (CONV_TAPS - 1)
    ys = []
    for b in range(NB):
        bsl = slice(b * ns, (b + 1) * ns)
        yb = ubuf[bsl, base:base + L, :] * cw_ref[:, 0:1, :]
        for i in range(1, CONV_TAPS):
            yb = yb + ubuf[bsl, base + i:base + i + L, :] * cw_ref[:, i:i + 1, :]
        ys.append(_silu(yb))
    ubuf[:, base:hist, :] = ubuf[:, base + L:hist + L, :]

    row = lax.broadcasted_iota(jnp.int32, (L, L), 0)
    col = lax.broadcasted_iota(jnp.int32, (L, L), 1)
    incl = row >= col
    strict = row > col
    sub_shift = INV_SUB.bit_length() - 1
    bd = lax.shift_right_logical(row, sub_shift) == lax.shift_right_logical(col, sub_shift)
    eye = jnp.where(row == col, 1.0, 0.0).astype(F32)
    tri = jnp.where(incl, 1.0, 0.0).astype(BF16)
    og = og_ref[...]
    heads = [(b, h) for b in range(NB) for h in range(GDN_HEADS)]
    l2n = lambda x: x * lax.rsqrt(jnp.sum(x * x, axis=-1, keepdims=True) + NORM_EPS)
    q = [l2n(ys[b][h]) * (GDN_DK ** -0.5) for b, h in heads]
    k = [l2n(ys[b][GDN_HEADS + h]) for b, h in heads]
    v = [ys[b][2 * GDN_HEADS + h] for b, h in heads]
    gc, gr, bc = [], [], []
    for b in range(NB):
        gbv = gb_ref[b]
        gcum = _dot_exact_lhs(tri, gbv)
        gpad = jnp.concatenate([gcum, jnp.zeros((LANES - L, LANES), F32)], axis=0)
        gt = gpad.T
        for h in range(GDN_HEADS):
            gc.append(gcum[:, h:h + 1])
            gr.append(gt[h:h + 1, 0:L])
            bc.append(gbv[:, GDN_HEADS + h:GDN_HEADS + h + 1])
    dec = _each(lambda c_, r_: jnp.where(incl, jnp.exp(jnp.where(incl, c_ - r_, 0.0)), 0.0), gc, gr)
    kb = _each(lambda x: x.astype(BF16), k)
    qb = _each(lambda x: x.astype(BF16), q)
    s = [s_ref[b, h] for b, h in heads]
    sb = _each(lambda x: x.astype(BF16), s)
    kq = _each(lambda k_, q_: _dot_nt(jnp.concatenate([k_, q_], axis=0), k_), kb, qb)
    a = _each(lambda b_, x, d_: jnp.where(strict, b_ * x[:L] * d_, 0.0), bc, kq, dec)
    t = _unit_lower_inverse(a, eye, bd)
    eg = _each(jnp.exp, gc)
    rhs = _each(lambda b_, v_, e_, k_: jnp.concatenate([b_ * v_, (b_ * e_) * k_], axis=1).astype(BF16),
                bc, v, eg, k)
    sol = _each(lambda x, y_: _dot(x.astype(BF16), y_), t, rhs)
    ksq = _each(lambda x, q_, y_: _dot(jnp.concatenate([x[:, LANES:].astype(BF16), q_], axis=0), y_),
                sol, qb, sb)
    ub = _each(lambda x, y_: (x[:, :LANES] - y_[:L]).astype(BF16), sol, ksq)
    qs = [x[L:] for x in ksq]
    qku = _each(lambda x, d_, u_: _dot((x[L:] * d_).astype(BF16), u_), kq, dec, ub)
    gl = [c_[L - 1:L, :] for c_ in gc]
    kd = _each(lambda k_, l_, c_: (k_ * jnp.exp(l_ - c_)).astype(BF16), k, gl, gc)
    ktu = _each(_dot_tn, kd, ub)
    for i, (b, h) in enumerate(heads):
        s_ref[b, h] = s[i] * jnp.exp(gl[i]) + ktu[i]
    for b in range(NB):
        za = za_ref[b]
        outs = [(_rms(qs[i] * eg[i] + qku[i], og) * za[:, h * LANES:(h + 1) * LANES]).astype(BF16)
                for i, (b_, h) in enumerate(heads) if b_ == b]
        o_ref[b] = jnp.concatenate(outs, axis=1)


def _gdn(u, gb, za, cinit, sinit, cw, og, L):
    b, s, _ = u.shape
    nc = s // L
    ns = 3 * GDN_HEADS
    nb = GDN_GROUP if b % GDN_GROUP == 0 else 1
    blk = lambda wd: pl.BlockSpec((nb, L, wd), lambda i, j: (i, j, 0))
    return pl.pallas_call(
        functools.partial(_gdn_body, L=L, NB=nb),
        grid=(b // nb, nc),
        in_specs=[blk(ns * LANES), blk(LANES), blk(GDN_HEADS * LANES),
                  pl.BlockSpec((nb, 8, ns * LANES), lambda i, j: (i, 0, 0)),
                  pl.BlockSpec((nb, GDN_HEADS, GDN_DK, LANES), lambda i, j: (i, 0, 0, 0)),
                  _const_spec((ns, 8, LANES)), _const_spec((1, LANES))],
        out_specs=[blk(GDN_HEADS * LANES),
                   pl.BlockSpec((nb, GDN_HEADS, GDN_DK, LANES), lambda i, j: (i, 0, 0, 0))],
        out_shape=[jax.ShapeDtypeStruct((b, s, GDN_HEADS * LANES), BF16),
                   jax.ShapeDtypeStruct((b, GDN_HEADS, GDN_DK, LANES), F32)],
        scratch_shapes=[pltpu.VMEM((nb * ns, 8 + L, LANES), F32)],
        compiler_params=_params(("arbitrary", "arbitrary")),
        name="gdn",
    )(u, gb, za, cinit, sinit, cw, og)


def _band_body(*refs, L, W, has_cache):
    if has_cache:
        (qlo_ref, qhi_ref, k_ref, vlo_ref, vhi_ref, zb_ref, bias_ref, ck_ref, cv_ref,
         o_ref, kscr, vlo, vhi) = refs
    else:
        (qlo_ref, qhi_ref, k_ref, vlo_ref, vhi_ref, zb_ref, bias_ref,
         o_ref, kscr, vlo, vhi, biasm) = refs
    c = pl.program_id(1)
    dm = BAND_HEADS * BAND_HD
    pad_chunks = BAND_PAST // L

    @pl.when(c == 0)
    def _():
        if has_cache:
            lo = (lax.broadcasted_iota(jnp.int32, (1, dm), 1) & (LANES - 1)) < BAND_HD
            kscr[0:BAND_PAST, :] = ck_ref[0].astype(BF16)
            cv = cv_ref[0]
            vlo[0:BAND_PAST, :] = jnp.where(lo, cv, 0.0).astype(BF16)
            vhi[0:BAND_PAST, :] = jnp.where(lo, 0.0, cv).astype(BF16)
        else:
            zero = jnp.zeros((BAND_PAST, dm), BF16)
            kscr[0:BAND_PAST, :] = zero
            vlo[0:BAND_PAST, :] = zero
            vhi[0:BAND_PAST, :] = zero

    new0 = pl.multiple_of(BAND_PAST + c * L, L)
    kscr[pl.ds(new0, L), :] = k_ref[0]
    vlo[pl.ds(new0, L), :] = vlo_ref[0]
    vhi[pl.ds(new0, L), :] = vhi_ref[0]

    if has_cache:
        bias_src = bias_ref
    else:
        @pl.when(c < pad_chunks)
        def _():
            wcol = lax.broadcasted_iota(jnp.int32, (1, W), 1)
            valid = (wcol + c * L) >= BAND_PAST
            for h in range(BAND_HEADS):
                biasm[h] = jnp.where(valid, bias_ref[h], -jnp.inf)

        @pl.when(c == pad_chunks)
        def _():
            biasm[...] = bias_ref[...]

        bias_src = biasm

    w0 = pl.multiple_of(c * L, L)
    zb = zb_ref[0]
    slabs = [slice(s * LANES, (s + 1) * LANES) for s in range(BAND_HEADS // 2)]
    sc = []
    for sl in slabs:
        ks = kscr[pl.ds(w0, W), sl]
        sc.append(_dot_nt(qlo_ref[0, :, sl], ks))
        sc.append(_dot_nt(qhi_ref[0, :, sl], ks))
    ps, rs = [], []
    for h in range(BAND_HEADS):
        x = sc[h] + bias_src[h]
        e = jnp.exp(x - jnp.max(x, axis=-1, keepdims=True))
        rs.append(1.0 / jnp.sum(e, axis=-1, keepdims=True))
        ps.append(e.astype(BF16))
    pv = []
    for i, sl in enumerate(slabs):
        pv.append(_dot(ps[2 * i], vlo[pl.ds(w0, W), sl]))
        pv.append(_dot(ps[2 * i + 1], vhi[pl.ds(w0, W), sl]))
    outs = [((pv[2 * i] * rs[2 * i] + pv[2 * i + 1] * rs[2 * i + 1]) * zb[:, sl]).astype(BF16)
            for i, sl in enumerate(slabs)]
    o_ref[0] = jnp.concatenate(outs, axis=1)


def _band(qlo, qhi, kn, vlo, vhi, zb, bias, caches, L):
    b, s, dm = qlo.shape
    nc = s // L
    W = BAND_PAST + L
    blk = pl.BlockSpec((1, L, dm), lambda i, j: (i, j, 0))
    cache = pl.BlockSpec((1, BAND_PAST, dm), lambda i, j: (i, 0, 0))
    has_cache = caches is not None
    scratch = [pltpu.VMEM((BAND_PAST + s, dm), BF16)] * 3
    if not has_cache:
        scratch = scratch + [pltpu.VMEM((BAND_HEADS, L, W), F32)]
    return pl.pallas_call(
        functools.partial(_band_body, L=L, W=W, has_cache=has_cache),
        grid=(b, nc),
        in_specs=[blk] * 6 + [_const_spec((BAND_HEADS, L, W))] + ([cache, cache] if has_cache else []),
        out_specs=blk,
        out_shape=jax.ShapeDtypeStruct((b, s, dm), BF16),
        scratch_shapes=scratch,
        compiler_params=_params(("arbitrary", "arbitrary")),
        name="band",
    )(qlo, qhi, kn, vlo, vhi, zb, bias, *(caches if has_cache else ()))


def _gla_body(q_ref, k_ref, v_ref, lg_ref, z_ref, sinit_ref, og_ref, o_ref, st_ref, cbs, *, L):
    c = pl.program_id(1)

    @pl.when(c == 0)
    def _():
        st_ref[...] = sinit_ref[...]

    nb = L // GLA_SUB
    half = GLA_SUB // 2
    row = lax.broadcasted_iota(jnp.int32, (L, L), 0)
    col = lax.broadcasted_iota(jnp.int32, (L, L), 1)
    tri = jnp.where(row >= col, 1.0, 0.0).astype(BF16)
    cb_all = _dot_exact_lhs(tri, lg_ref[0]) * LOG2_E
    cbs[...] = cb_all
    ex = jnp.exp2
    lane = lax.broadcasted_iota(jnp.int32, (GLA_SUB, LANES), 1)
    lane8 = lax.broadcasted_iota(jnp.int32, (half, LANES), 1)
    row8 = lax.broadcasted_iota(jnp.int32, (half, LANES), 0)
    og = og_ref[...]
    scale = GLA_DK ** -0.5

    heads = range(GLA_HEADS)
    sls = [slice(h * GLA_DK, (h + 1) * GLA_DK) for h in heads]
    vsls = [slice(h * GLA_DV, (h + 1) * GLA_DV) for h in heads]
    cb = [cb_all[:, sl] for sl in sls]
    q = [q_ref[0, :, sl] * scale for sl in sls]
    k = [k_ref[0, :, sl] for sl in sls]
    vb = [v_ref[0, :, vsl].astype(BF16) for vsl in vsls]
    st = [st_ref[0, h] for h in heads]
    z = z_ref[0]
    o_inter = _each(lambda q_, c_, s_: _dot_nt((q_ * ex(c_)).astype(BF16), s_.astype(BF16)), q, cb, st)
    cl = [c_[L - 1:L] for c_ in cb]
    ke = _each(lambda k_, l_, c_: (k_ * ex(l_ - c_)).astype(BF16), k, cl, cb)
    vtk = _each(_dot_tn, vb, ke)
    off = [[None] * nb for _ in heads]
    for ib in range(1, nb):
        r0 = ib * GLA_SUB
        for h in heads:
            ref = cb[h][r0 - 1:r0]
            qe = (q[h][r0:r0 + GLA_SUB] * ex(cb[h][r0:r0 + GLA_SUB] - ref)).astype(BF16)
            kx = (k[h] * ex(jnp.minimum(ref - cb[h], 0.0))).astype(BF16)
            x = _dot_nt(qe, kx)
            if L < LANES:
                x = jnp.concatenate([x, jnp.zeros((GLA_SUB, LANES - L), F32)], axis=1)
            off[h][ib] = jnp.where(lane < r0, x, 0.0)
    strips = [[None] * nb for _ in heads]
    for ib in range(nb):
        r0 = ib * GLA_SUB
        take_lo = [(lane8 == r0 + jj) & (row8 >= jj) for jj in range(half)]
        take_mid = [lane8 == r0 + jj for jj in range(half)]
        take_hi = [(lane8 == r0 + half + jj) & (row8 >= jj) for jj in range(half)]
        for h in heads:
            sl = sls[h]
            q_lo, q_hi = q[h][r0:r0 + half], q[h][r0 + half:r0 + GLA_SUB]
            c_lo, c_hi = cb[h][r0:r0 + half], cb[h][r0 + half:r0 + GLA_SUB]
            if ib > 0:
                s_lo, s_hi = off[h][ib][:half], off[h][ib][half:]
            else:
                s_lo = s_hi = jnp.zeros((half, LANES), F32)
            pair = lambda qq, cc, kj, cj: jnp.sum(qq * kj * ex(cc - cj), axis=-1, keepdims=True)
            for jj in range(half):
                kj, cj = k_ref[0, r0 + jj:r0 + jj + 1, sl], cbs[r0 + jj:r0 + jj + 1, sl]
                s_lo = jnp.where(take_lo[jj], pair(q_lo, c_lo, kj, cj), s_lo)
                s_hi = jnp.where(take_mid[jj], pair(q_hi, c_hi, kj, cj), s_hi)
            for jj in range(half):
                j = r0 + half + jj
                kj, cj = k_ref[0, j:j + 1, sl], cbs[j:j + 1, sl]
                s_hi = jnp.where(take_hi[jj], pair(q_hi, c_hi, kj, cj), s_hi)
            strips[h][ib] = jnp.concatenate([s_lo, s_hi], axis=0)
    att = [jnp.concatenate(strips[h], axis=0)[:, :L].astype(BF16) for h in heads]
    o_intra = _each(_dot, att, vb)
    for h in heads:
        st_ref[0, h] = st[h] * ex(cl[h]) + vtk[h]
    outs = [(_rms(o_inter[h] + o_intra[h], og) * z[:, vsls[h]]).astype(BF16) for h in heads]
    o_ref[0] = jnp.concatenate(outs, axis=1)


def _gla(q, k, v, lg, z, sinit_t, og, L):
    b, s, _ = q.shape
    nc = s // L
    qk = pl.BlockSpec((1, L, GLA_HEADS * GLA_DK), lambda i, j: (i, j, 0))
    vv = pl.BlockSpec((1, L, GLA_HEADS * GLA_DV), lambda i, j: (i, j, 0))
    st = pl.BlockSpec((1, GLA_HEADS, GLA_DV, GLA_DK), lambda i, j: (i, 0, 0, 0))
    return pl.pallas_call(
        functools.partial(_gla_body, L=L),
        grid=(b, nc),
        in_specs=[qk, qk, vv, qk, vv, st, _const_spec((1, GLA_DV))],
        out_specs=[vv, st],
        out_shape=[jax.ShapeDtypeStruct((b, s, GLA_HEADS * GLA_DV), BF16),
                   jax.ShapeDtypeStruct((b, GLA_HEADS, GLA_DV, GLA_DK), F32)],
        scratch_shapes=[pltpu.VMEM((L, GLA_HEADS * GLA_DK), F32)],
        compiler_params=_params(("arbitrary", "arbitrary")),
        name="gla",
    )(q, k, v, lg, z, sinit_t, og)


def _post_body(*refs, n_o):
    x_ref = refs[0]
    o_refs = refs[1:1 + n_o]
    wout_ref, mg_ref, wmq_ref, mqg_ref, mk_ref, mv_ref, wmo_ref, y_ref = refs[1 + n_o:]
    tm = x_ref.shape[1]
    ts = tm // POST_SPLIT if tm % (8 * POST_SPLIT) == 0 else tm
    rows = [slice(t * ts, (t + 1) * ts) for t in range(tm // ts)]
    acc = [x_ref[0, r, :] for r in rows]
    off = 0
    for o_ref in o_refs:
        kd = o_ref.shape[-1]
        w = wout_ref[off:off + kd, :]
        acc = [a + _dot(o_ref[0, r, :], w) for a, r in zip(acc, rows)]
        off += kd
    hm = [_rms(a, mg_ref[...]).astype(BF16) for a in acc]
    qz = [_dot(h_, wmq_ref[...]) for h_ in hm]
    hw = MEM_HEADS * MEM_HD
    sls = [slice(h * MEM_HD, (h + 1) * MEM_HD) for h in range(MEM_HEADS)]
    mkb = [mk_ref[0, :, sl].astype(BF16) for sl in sls]
    mvb = [mv_ref[0, :, sl].astype(BF16) for sl in sls]
    qn = [[_rms(z[:, sl], mqg_ref[...]).astype(BF16) for sl in sls] for z in qz]
    sc = [[_dot_nt(qh, kh) for qh, kh in zip(qt, mkb)] for qt in qn]
    p = [[_softmax_rows(x * (MEM_HD ** -0.5)).astype(BF16) for x in st] for st in sc]
    oh = [[_dot(ph, vh) for ph, vh in zip(pt, mvb)] for pt in p]
    for t, r in enumerate(rows):
        outs = [oh[t][h] * _silu(qz[t][:, hw + h * MEM_HD:hw + (h + 1) * MEM_HD]) for h in range(MEM_HEADS)]
        oh[t] = jnp.concatenate(outs, axis=1).astype(BF16)
    ym = [_dot(om, wmo_ref[...]) for om in oh]
    for t, r in enumerate(rows):
        y_ref[0, r, :] = acc[t] + ym[t]


def _post(x, os_, wout, mg, wmq, mqg, mk, mv, wmo):
    b, s, d = x.shape
    tm = _row_tile(s, 512)
    hw = MEM_HEADS * MEM_HD
    nm = mk.shape[1]
    blk = lambda wd: pl.BlockSpec((1, tm, wd), lambda i, j: (i, j, 0))
    mem = pl.BlockSpec((1, nm, hw), lambda i, j: (i, 0, 0))
    kin = sum(o.shape[-1] for o in os_)
    return pl.pallas_call(
        functools.partial(_post_body, n_o=len(os_)),
        grid=(b, s // tm),
        in_specs=[blk(d)] + [blk(o.shape[-1]) for o in os_]
        + [_const_spec((kin, d)), _const_spec((1, d)), _const_spec((d, 2 * hw)), _const_spec((1, MEM_HD)),
           mem, mem, _const_spec((hw, d))],
        out_specs=blk(d),
        out_shape=jax.ShapeDtypeStruct((b, s, d), F32),
        compiler_params=_params(("arbitrary", "arbitrary")),
        name="post",
    )(x, *os_, wout, mg, wmq, mqg, mk, mv, wmo)


def _pad_cols(w, n):
    return jnp.pad(w, ((0, 0), (0, n - w.shape[1])))


def _prep_l0(w_in, conv_w, a_log, dt_bias, q_g, k_g):
    c0 = 3 * 1024
    ab = w_in[:, c0:c0 + 2 * GDN_HEADS]
    c1 = c0 + 2 * GDN_HEADS
    cols = lambda i: w_in[:, c1 + i * 1024:c1 + (i + 1) * 1024].astype(BF16)
    w = dict(
        wu=w_in[:, :c0].astype(BF16), wab=_pad_cols(ab, LANES).astype(BF16),
        wza=cols(0), wq=cols(1), wk=cols(2), wv=cols(3), wzb=cols(4),
        alog=jnp.pad(a_log, (0, LANES - GDN_HEADS)).reshape(1, LANES).astype(F32),
        dtb=jnp.pad(dt_bias, (0, LANES - GDN_HEADS)).reshape(1, LANES).astype(F32),
    )
    ns = 3 * GDN_HEADS
    cw = jnp.pad(conv_w.astype(F32), ((0, 8 - CONV_TAPS), (0, 0))).reshape(8, ns, LANES).transpose(1, 0, 2)
    dm = BAND_HEADS * BAND_HD
    head_of = np.arange(dm) // BAND_HD
    seg = jnp.asarray(head_of[:, None] == np.arange(LANES)[None, :], BF16)
    qg = jnp.tile(q_g.astype(F32), BAND_HEADS).reshape(1, dm)
    kg = jnp.tile(k_g.astype(F32), BAND_HEADS).reshape(1, dm)
    return w, cw, seg, qg, kg


def _band_bias_table(rel_bias, L, chunk):
    W = BAND_PAST + L
    n = np.arange(W + L - 1)
    idx = np.clip(BAND_PAST + (L - 1) - n, -BAND_MAX_REL, BAND_MAX_REL) + BAND_MAX_REL
    strip = rel_bias.astype(F32)[:, idx]
    bias = jnp.stack([strip[:, L - 1 - a:L - 1 - a + W] for a in range(L)], axis=1)
    back = np.arange(L)[:, None] // chunk - (np.arange(W)[None, :] - BAND_PAST) // chunk
    readable = (back >= 0) & (back <= BAND_PAST // chunk)
    return bias if readable.all() else jnp.where(jnp.asarray(readable)[None], bias, -jnp.inf)


def _prep_l1(w_in, w_gate_up, gate_bias):
    qk = GLA_HEADS * GLA_DK
    vw = GLA_HEADS * GLA_DV
    o = np.cumsum([0, qk, qk, vw, GLA_RANK, vw])
    return dict(
        wq=w_in[:, o[0]:o[1]].astype(BF16), wk=w_in[:, o[1]:o[2]].astype(BF16),
        wv=w_in[:, o[2]:o[3]].astype(BF16), wlr=_pad_cols(w_in[:, o[3]:o[4]], LANES).astype(BF16),
        wz=w_in[:, o[4]:o[5]].astype(BF16),
        wg=jnp.pad(w_gate_up, ((0, LANES - GLA_RANK), (0, 0))).astype(BF16),
        gbias=gate_bias.reshape(1, qk).astype(F32),
    )


def _layer0(x, conv_state, gdn_state, band_caches, mk, mv, norm_g, pw, cw, seg, qg, kg, bias,
            a_onorm_g, w_out, mnorm_g, w_mq, mq_g, w_mo, L):
    b, s, d = x.shape
    outs = _proj0(x.reshape(b * s, d), norm_g.reshape(1, d), pw, qg, kg, seg, s)
    u, gb, za, qlo, qhi, kn, klast, vlo, vhi, vlast, zb = (t.reshape(b, -1, t.shape[-1]) for t in outs)
    cinit = jnp.pad(conv_state.astype(F32), ((0, 0), (8 - (CONV_TAPS - 1), 0), (0, 0)))
    o_a, s_new = _gdn(u, gb, za, cinit, gdn_state.astype(F32), cw, a_onorm_g.reshape(1, LANES), L)
    o_b = _band(qlo, qhi, kn, vlo, vhi, zb, bias, band_caches, bias.shape[1])
    y = _post(x, (o_a, o_b), w_out, mnorm_g.reshape(1, d), w_mq, mq_g.reshape(1, MEM_HD), mk, mv, w_mo)
    return y, u[:, s - (CONV_TAPS - 1):, :], s_new, klast, vlast


def _layer1(x, gla_state, mk, mv, norm_g, pw, c_onorm_g, w_out, mnorm_g, w_mq, mq_g, w_mo, L):
    b, s, d = x.shape
    q, k, v, lg, z = _proj1(x.reshape(b * s, d), norm_g.reshape(1, d), pw)
    r3 = lambda t: t.reshape(b, s, t.shape[-1])
    q, k, v, lg, z = map(r3, (q, k, v, lg, z))
    st0 = jnp.swapaxes(gla_state.astype(F32), 2, 3)
    o, st = _gla(q, k, v, lg, z, st0, c_onorm_g.reshape(1, GLA_DV), L)
    y = _post(x, (o,), w_out, mnorm_g.reshape(1, d), w_mq, mq_g.reshape(1, MEM_HD), mk, mv, w_mo)
    return y, jnp.swapaxes(st, 2, 3)


def kernel(x_prompt, x_sample, mem_prompt, state_l0_gdn_conv, state_l0_gdn, cache_l0_band_k, cache_l0_band_v, cache_l0_mem_k, cache_l0_mem_v, state_l1_gla, cache_l1_mem_k, cache_l1_mem_v, l0_norm_g, l0_w_in, l0_conv_w, l0_a_log, l0_dt_bias, l0_a_onorm_g, l0_b_q_g, l0_b_k_g, l0_b_rel_bias, l0_w_out, l0_mnorm_g, l0_mem_norm_g, l0_w_mkv, l0_mk_g, l0_w_mq, l0_mq_g, l0_w_mo, l1_norm_g, l1_w_in, l1_w_gate_up, l1_gate_bias, l1_c_onorm_g, l1_w_out, l1_mnorm_g, l1_mem_norm_g, l1_w_mkv, l1_mk_g, l1_w_mq, l1_mq_g, l1_w_mo):
    bp, sp, d = x_prompt.shape
    bs, ss, _ = x_sample.shape
    nm = mem_prompt.shape[1]
    hw = MEM_HEADS * MEM_HD
    dm = BAND_HEADS * BAND_HD
    assert sp % CHUNK_ == 0 and ss % INV_SUB == 0 and ss <= CHUNK_
    assert cache_l0_band_k.shape[1] == BAND_PAST

    pw0, cw, seg, qg, kg = _prep_l0(l0_w_in, l0_conv_w, l0_a_log, l0_dt_bias, l0_b_q_g, l0_b_k_g)
    pw1 = _prep_l1(l1_w_in, l1_w_gate_up, l1_gate_bias)
    band_rows = BAND_STEP_CHUNKS * CHUNK_ if sp % (BAND_STEP_CHUNKS * CHUNK_) == 0 else CHUNK_
    bias_p = _band_bias_table(l0_b_rel_bias, band_rows, CHUNK_)
    bias_s = _band_bias_table(l0_b_rel_bias, ss, ss)
    bf = lambda w: w.astype(BF16)
    mem2 = mem_prompt.reshape(bp * nm, d)

    mk0, mv0 = _memkv(mem2, l0_mem_norm_g.reshape(1, d), bf(l0_w_mkv), l0_mk_g.reshape(1, MEM_HD))
    mk0 = mk0.reshape(bp, nm, hw)
    mv0 = mv0.reshape(bp, nm, hw)
    l0_shared = (l0_norm_g, pw0, cw, seg, qg, kg)
    l0_tail = (l0_a_onorm_g, bf(l0_w_out), l0_mnorm_g, bf(l0_w_mq), l0_mq_g, bf(l0_w_mo))
    zeros_conv = jnp.zeros((bp, CONV_TAPS - 1, 3 * GDN_HEADS * LANES), F32)
    zeros_gdn = jnp.zeros((bp, GDN_HEADS, GDN_DK, LANES), F32)
    yp, p_conv, p_gdn, p_kn, p_v = _layer0(
        x_prompt, zeros_conv, zeros_gdn, None, mk0, mv0,
        *l0_shared, bias_p, *l0_tail, CHUNK_)
    ys, s_conv, s_gdn, s_kn, s_v = _layer0(
        x_sample, state_l0_gdn_conv, state_l0_gdn,
        (cache_l0_band_k.reshape(bs, BAND_PAST, dm), cache_l0_band_v.reshape(bs, BAND_PAST, dm)),
        cache_l0_mem_k.reshape(bs, nm, hw), cache_l0_mem_v.reshape(bs, nm, hw),
        *l0_shared, bias_s, *l0_tail, ss)

    mk1, mv1 = _memkv(mem2, l1_mem_norm_g.reshape(1, d), bf(l1_w_mkv), l1_mk_g.reshape(1, MEM_HD))
    mk1 = mk1.reshape(bp, nm, hw)
    mv1 = mv1.reshape(bp, nm, hw)
    l1_tail = (l1_c_onorm_g, bf(l1_w_out), l1_mnorm_g, bf(l1_w_mq), l1_mq_g, bf(l1_w_mo))
    zeros_gla = jnp.zeros((bp, GLA_HEADS, GLA_DK, GLA_DV), F32)
    yp, p_gla = _layer1(yp, zeros_gla, mk1, mv1, l1_norm_g, pw1, *l1_tail, CHUNK_)
    ys, s_gla = _layer1(ys, state_l1_gla, cache_l1_mem_k.reshape(bs, nm, hw),
                        cache_l1_mem_v.reshape(bs, nm, hw), l1_norm_g, pw1, *l1_tail, ss)

    h4 = lambda t: t.reshape(t.shape[0], t.shape[1], BAND_HEADS, BAND_HD)
    m4 = lambda t: t.reshape(bp, nm, MEM_HEADS, MEM_HD)
    return (yp, ys, p_conv, p_gdn, h4(p_kn), h4(p_v),
            m4(mk0), m4(mv0), p_gla, m4(mk1), m4(mv1),
            s_conv, s_gdn, h4(s_kn), h4(s_v), s_gla)
```

```python
import functools

import jax
import jax.numpy as jnp
import numpy as np
from jax import lax
from jax.experimental import pallas as pl
from jax.experimental.pallas import tpu as pltpu

F32 = jnp.float32
BF16 = jnp.bfloat16
NORM_EPS = 1e-6
LOG2_E = 1.4426950408889634

D_MODEL_ = 1024
CHUNK_ = 64
CONV_TAPS = 4
GDN_HEADS = 8
GDN_DK = 128
BAND_HEADS = 16
BAND_HD = 64
BAND_PAST = 512
BAND_MAX_REL = 128
GLA_HEADS = 8
GLA_DK = 128
GLA_DV = 256
GLA_RANK = 16
GLA_TAU = 16.0
MEM_HEADS = 4
MEM_HD = 128
INV_SUB = 16
GDN_GROUP = 2
GLA_GROUP = 2
POST_SPLIT = 2
BAND_STEP_CHUNKS = 2
LANES = 128
VMEM_LIMIT = 56 * 1024 * 1024


def _dot(a, b):
    return jnp.dot(a, b, preferred_element_type=F32)


def _dot_nt(a, b):
    return lax.dot_general(a, b, (((1,), (1,)), ((), ())), preferred_element_type=F32)


def _dot_tn(a, b):
    return lax.dot_general(a, b, (((0,), (0,)), ((), ())), preferred_element_type=F32)


def _split2(x):
    hi = x.astype(BF16)
    lo = (x - hi.astype(F32)).astype(BF16)
    return hi, lo


def _split3(x):
    hi = x.astype(BF16)
    r = x - hi.astype(F32)
    mid = r.astype(BF16)
    lo = (r - mid.astype(F32)).astype(BF16)
    return hi, mid, lo


def _dot_x3(a, b):
    ah, al = _split2(a)
    bh, bl = _split2(b)
    return _dot(ah, bh) + (_dot(ah, bl) + _dot(al, bh))


def _dot_exact_lhs(a_bf, b):
    h, m, l = _split3(b)
    return _dot(a_bf, h) + (_dot(a_bf, m) + _dot(a_bf, l))


def _rms(x, g):
    ms = jnp.mean(x * x, axis=-1, keepdims=True)
    return x * lax.rsqrt(ms + NORM_EPS) * g


def _silu(x):
    return x * jax.nn.sigmoid(x)


def _softplus(x):
    return jnp.maximum(x, 0.0) + jnp.log1p(jnp.exp(-jnp.abs(x)))


def _softmax_rows(s):
    m = jnp.max(s, axis=-1, keepdims=True)
    e = jnp.exp(s - m)
    return e * (1.0 / jnp.sum(e, axis=-1, keepdims=True))


def _const_spec(shape):
    nd = len(shape)
    return pl.BlockSpec(shape, lambda *_: (0,) * nd, pipeline_mode=pl.Buffered(1))


def _params(sem):
    return pltpu.CompilerParams(dimension_semantics=sem, vmem_limit_bytes=VMEM_LIMIT)


def _row_tile(n, want):
    t = min(n, want)
    assert n % t == 0
    return t


def _proj0_body(x_ref, g_ref, wu_ref, wab_ref, wza_ref, wq_ref, wk_ref, wv_ref, wzb_ref,
                alog_ref, dtb_ref, qg_ref, kg_ref, seg_ref,
                u_ref, gb_ref, za_ref, qlo_ref, qhi_ref, kn_ref, klast_ref, vlo_ref, vhi_ref, vlast_ref, zb_ref):
    h = _rms(x_ref[...], g_ref[...]).astype(BF16)
    dm = BAND_HEADS * BAND_HD
    lo = (lax.broadcasted_iota(jnp.int32, (1, dm), 1) & (LANES - 1)) < BAND_HD
    seg = seg_ref[...]
    q = _dot(h, wq_ref[...])
    k = _dot(h, wk_ref[...])
    u_ref[...] = _dot(h, wu_ref[...])

    def head_rsqrt(x):
        ss = _dot((x * x).astype(BF16), seg)
        return lax.rsqrt(ss * (1.0 / BAND_HD) + NORM_EPS)

    rq = head_rsqrt(q)
    rk = head_rsqrt(k)
    za_ref[...] = _silu(_dot(h, wza_ref[...]))
    v = _dot(h, wv_ref[...])
    lo1 = lo[:, :LANES]

    def spread(r):
        return jnp.concatenate([jnp.where(lo1, r[:, 2 * s:2 * s + 1], r[:, 2 * s + 1:2 * s + 2])
                                for s in range(BAND_HEADS // 2)], axis=1)

    qn = q * spread(rq) * qg_ref[...] * (BAND_HD ** -0.5)
    kn = k * spread(rk) * kg_ref[...]
    zb_ref[...] = _silu(_dot(h, wzb_ref[...]))
    ab = _dot(h, wab_ref[...])
    qlo_ref[...] = jnp.where(lo, qn, 0.0).astype(BF16)
    qhi_ref[...] = jnp.where(lo, 0.0, qn).astype(BF16)
    kn_ref[...] = kn.astype(BF16)
    klast_ref[...] = kn
    vlo_ref[...] = jnp.where(lo, v, 0.0).astype(BF16)
    vhi_ref[...] = jnp.where(lo, 0.0, v).astype(BF16)
    vlast_ref[...] = v
    lane = lax.broadcasted_iota(jnp.int32, ab.shape, 1)
    gval = -jnp.exp(alog_ref[...]) * _softplus(ab + dtb_ref[...])
    gb_ref[...] = jnp.where(lane < GDN_HEADS, gval, jax.nn.sigmoid(ab))


def _proj0(x2, g, w, qg, kg, seg, rows_per_batch):
    n, d = x2.shape
    tm = _row_tile(n, 256)
    dm = BAND_HEADS * BAND_HD
    widths = (3 * 1024, LANES, 1024, dm, dm, dm, dm)
    row = lambda wd: pl.BlockSpec((tm, wd), lambda i: (i, 0))
    keep = min(BAND_PAST, rows_per_batch)
    if rows_per_batch > keep:
        assert rows_per_batch % tm == 0 and keep % tm == 0
        tpb, kt = rows_per_batch // tm, keep // tm
        last = pl.BlockSpec((tm, dm), lambda i: ((i // tpb) * kt + jnp.maximum(i % tpb - (tpb - kt), 0), 0))
        n_last = (n // rows_per_batch) * keep
    else:
        last, n_last = row(dm), n
    f32 = lambda rows, wd: jax.ShapeDtypeStruct((rows, wd), F32)
    bf16 = lambda wd: jax.ShapeDtypeStruct((n, wd), BF16)
    return pl.pallas_call(
        _proj0_body,
        grid=(n // tm,),
        in_specs=[row(d), _const_spec((1, d))]
        + [_const_spec((d, wd)) for wd in widths]
        + [_const_spec((1, LANES)), _const_spec((1, LANES)), _const_spec((1, dm)), _const_spec((1, dm)),
           _const_spec((dm, LANES))],
        out_specs=[row(3 * 1024), row(LANES), row(1024), row(dm), row(dm), row(dm), last, row(dm), row(dm), last,
                   row(dm)],
        out_shape=[f32(n, 3 * 1024), f32(n, LANES), f32(n, 1024), bf16(dm), bf16(dm), bf16(dm), f32(n_last, dm),
                   bf16(dm), bf16(dm), f32(n_last, dm), f32(n, dm)],
        compiler_params=_params(("arbitrary",)),
        name="proj0",
    )(x2, g, w["wu"], w["wab"], w["wza"], w["wq"], w["wk"], w["wv"], w["wzb"], w["alog"], w["dtb"],
      qg, kg, seg)


def _proj1_body(x_ref, g_ref, wq_ref, wk_ref, wv_ref, wlr_ref, wz_ref, wg_ref, gbias_ref,
                q_ref, k_ref, v_ref, lg_ref, z_ref):
    h = _rms(x_ref[...], g_ref[...]).astype(BF16)
    q_ref[...] = _dot(h, wq_ref[...])
    k_ref[...] = _dot(h, wk_ref[...])
    v_ref[...] = _dot(h, wv_ref[...])
    lr = _dot(h, wlr_ref[...])
    pre = _dot(lr.astype(BF16), wg_ref[...]) + gbias_ref[...]
    lg_ref[...] = -_softplus(-pre) * (1.0 / GLA_TAU)
    z_ref[...] = _silu(_dot(h, wz_ref[...]))


def _proj1(x2, g, w):
    n, d = x2.shape
    tm = _row_tile(n, 256)
    row = lambda wd: pl.BlockSpec((tm, wd), lambda i: (i, 0))
    outw = (1024, 1024, 2048, 1024, 2048)
    return pl.pallas_call(
        _proj1_body,
        grid=(n // tm,),
        in_specs=[row(d), _const_spec((1, d)), _const_spec((d, 1024)), _const_spec((d, 1024)),
                  _const_spec((d, 2048)), _const_spec((d, LANES)), _const_spec((d, 2048)),
                  _const_spec((LANES, 1024)), _const_spec((1, 1024))],
        out_specs=[row(wd) for wd in outw],
        out_shape=[jax.ShapeDtypeStruct((n, wd), F32) for wd in outw],
        compiler_params=_params(("arbitrary",)),
        name="proj1",
    )(x2, g, w["wq"], w["wk"], w["wv"], w["wlr"], w["wz"], w["wg"], w["gbias"])


def _memkv_body(m_ref, g_ref, w_ref, kg_ref, k_ref, v_ref):
    h = _rms(m_ref[...], g_ref[...]).astype(BF16)
    kv = _dot(h, w_ref[...])
    hw = MEM_HEADS * MEM_HD
    for hh in range(MEM_HEADS):
        sl = slice(hh * MEM_HD, (hh + 1) * MEM_HD)
        k_ref[:, sl] = _rms(kv[:, sl], kg_ref[...])
    v_ref[...] = kv[:, hw:]


def _memkv(m2, g, w_bf, kg):
    n, d = m2.shape
    tm = _row_tile(n, 256)
    hw = MEM_HEADS * MEM_HD
    row = lambda wd: pl.BlockSpec((tm, wd), lambda i: (i, 0))
    return pl.pallas_call(
        _memkv_body,
        grid=(n // tm,),
        in_specs=[row(d), _const_spec((1, d)), _const_spec((d, 2 * hw)), _const_spec((1, MEM_HD))],
        out_specs=[row(hw), row(hw)],
        out_shape=[jax.ShapeDtypeStruct((n, hw), F32)] * 2,
        compiler_params=_params(("arbitrary",)),
        name="memkv",
    )(m2, g, w_bf, kg)


def _each(fn, *lists):
    return [fn(*xs) for xs in zip(*lists)]


def _unit_lower_inverse(a, eye, bd):
    mm = lambda x, y: _dot(x.astype(BF16), y.astype(BF16))
    d = _each(lambda x: jnp.where(bd, x, 0.0), a)
    nl = _each(lambda x, y: x - y, a, d)
    d2 = _each(mm, d, d)
    d4 = _each(mm, d2, d2)
    td = _each(lambda x, y: mm(eye - x, eye + y), d, d2)
    d8 = _each(mm, d4, d4)
    td = _each(lambda x, y: mm(x, eye + y), td, d4)
    td = _each(lambda x, y: mm(x, eye + y), td, d8)
    m = _each(mm, td, nl)
    m2 = _each(mm, m, m)
    mt = _each(mm, m, td)
    return _each(lambda x, y, z: mm(eye + x, y - z), m2, td, mt)


def _gdn_body(u_ref, gb_ref, za_ref, cinit_ref, sinit_ref, cw_ref, og_ref,
              o_ref, s_ref, ubuf, *, L, NB):
    c = pl.program_id(1)
    ns = 3 * GDN_HEADS
    hist = 8

    @pl.when(c == 0)
    def _():
        for b in range(NB):
            for j in range(ns):
                ubuf[b * ns + j, 0:hist, :] = cinit_ref[b, :, j * LANES:(j + 1) * LANES]
        s_ref[...] = sinit_ref[...]

    for b in range(NB):
        for j in range(ns):
            ubuf[b * ns + j, hist:hist + L, :] = u_ref[b, :, j * LANES:(j + 1) * LANES]
    base = hist - (CONV_TAPS - 1)
    ys = []
    for b in range(NB):
        bsl = slice(b * ns, (b + 1) * ns)
        yb = ubuf[bsl, base:base + L, :] * cw_ref[:, 0:1, :]
        for i in range(1, CONV_TAPS):
            yb = yb + ubuf[bsl, base + i:base + i + L, :] * cw_ref[:, i:i + 1, :]
        ys.append(_silu(yb))
    ubuf[:, base:hist, :] = ubuf[:, base + L:hist + L, :]

    row = lax.broadcasted_iota(jnp.int32, (L, L), 0)
    col = lax.broadcasted_iota(jnp.int32, (L, L), 1)
    incl = row >= col
    strict = row > col
    sub_shift = INV_SUB.bit_length() - 1
    bd = lax.shift_right_logical(row, sub_shift) == lax.shift_right_logical(col, sub_shift)
    eye = jnp.where(row == col, 1.0, 0.0).astype(F32)
    tri = jnp.where(incl, 1.0, 0.0).astype(BF16)
    og = og_ref[...]
    heads = [(b, h) for b in range(NB) for h in range(GDN_HEADS)]
    l2n = lambda x: x * lax.rsqrt(jnp.sum(x * x, axis=-1, keepdims=True) + NORM_EPS)
    q = [l2n(ys[b][h]) * (GDN_DK ** -0.5) for b, h in heads]
    k = [l2n(ys[b][GDN_HEADS + h]) for b, h in heads]
    v = [ys[b][2 * GDN_HEADS + h] for b, h in heads]
    gc, gr, bc = [], [], []
    for b in range(NB):
        gbv = gb_ref[b]
        gcum = _dot_exact_lhs(tri, gbv)
        gpad = jnp.concatenate([gcum, jnp.zeros((LANES - L, LANES), F32)], axis=0)
        gt = gpad.T
        for h in range(GDN_HEADS):
            gc.append(gcum[:, h:h + 1])
            gr.append(gt[h:h + 1, 0:L])
            bc.append(gbv[:, GDN_HEADS + h:GDN_HEADS + h + 1])
    dec = _each(lambda c_, r_: jnp.where(incl, jnp.exp(jnp.where(incl, c_ - r_, 0.0)), 0.0), gc, gr)
    kb = _each(lambda x: x.astype(BF16), k)
    qb = _each(lambda x: x.astype(BF16), q)
    s = [s_ref[b, h] for b, h in heads]
    sb = _each(lambda x: x.astype(BF16), s)
    kq = _each(lambda k_, q_: _dot_nt(jnp.concatenate([k_, q_], axis=0), k_), kb, qb)
    a = _each(lambda b_, x, d_: jnp.where(strict, b_ * x[:L] * d_, 0.0), bc, kq, dec)
    t = _unit_lower_inverse(a, eye, bd)
    eg = _each(jnp.exp, gc)
    rhs = _each(lambda b_, v_, e_, k_: jnp.concatenate([b_ * v_, (b_ * e_) * k_], axis=1).astype(BF16),
                bc, v, eg, k)
    sol = _each(lambda x, y_: _dot(x.astype(BF16), y_), t, rhs)
    ksq = _each(lambda x, q_, y_: _dot(jnp.concatenate([x[:, LANES:].astype(BF16), q_], axis=0), y_),
                sol, qb, sb)
    ub = _each(lambda x, y_: (x[:, :LANES] - y_[:L]).astype(BF16), sol, ksq)
    qs = [x[L:] for x in ksq]
    qku = _each(lambda x, d_, u_: _dot((x[L:] * d_).astype(BF16), u_), kq, dec, ub)
    gl = [c_[L - 1:L, :] for c_ in gc]
    kd = _each(lambda k_, l_, c_: (k_ * jnp.exp(l_ - c_)).astype(BF16), k, gl, gc)
    ktu = _each(_dot_tn, kd, ub)
    for i, (b, h) in enumerate(heads):
        s_ref[b, h] = s[i] * jnp.exp(gl[i]) + ktu[i]
    for b in range(NB):
        za = za_ref[b]
        outs = [(_rms(qs[i] * eg[i] + qku[i], og) * za[:, h * LANES:(h + 1) * LANES]).astype(BF16)
                for i, (b_, h) in enumerate(heads) if b_ == b]
        o_ref[b] = jnp.concatenate(outs, axis=1)


def _gdn(u, gb, za, cinit, sinit, cw, og, L):
    b, s, _ = u.shape
    nc = s // L
    ns = 3 * GDN_HEADS
    nb = GDN_GROUP if b % GDN_GROUP == 0 else 1
    blk = lambda wd: pl.BlockSpec((nb, L, wd), lambda i, j: (i, j, 0))
    return pl.pallas_call(
        functools.partial(_gdn_body, L=L, NB=nb),
        grid=(b // nb, nc),
        in_specs=[blk(ns * LANES), blk(LANES), blk(GDN_HEADS * LANES),
                  pl.BlockSpec((nb, 8, ns * LANES), lambda i, j: (i, 0, 0)),
                  pl.BlockSpec((nb, GDN_HEADS, GDN_DK, LANES), lambda i, j: (i, 0, 0, 0)),
                  _const_spec((ns, 8, LANES)), _const_spec((1, LANES))],
        out_specs=[blk(GDN_HEADS * LANES),
                   pl.BlockSpec((nb, GDN_HEADS, GDN_DK, LANES), lambda i, j: (i, 0, 0, 0))],
        out_shape=[jax.ShapeDtypeStruct((b, s, GDN_HEADS * LANES), BF16),
                   jax.ShapeDtypeStruct((b, GDN_HEADS, GDN_DK, LANES), F32)],
        scratch_shapes=[pltpu.VMEM((nb * ns, 8 + L, LANES), F32)],
        compiler_params=_params(("arbitrary", "arbitrary")),
        name="gdn",
    )(u, gb, za, cinit, sinit, cw, og)


def _band_body(*refs, L, W, has_cache):
    if has_cache:
        (qlo_ref, qhi_ref, k_ref, vlo_ref, vhi_ref, zb_ref, bias_ref, ck_ref, cv_ref,
         o_ref, kscr, vlo, vhi) = refs
    else:
        (qlo_ref, qhi_ref, k_ref, vlo_ref, vhi_ref, zb_ref, bias_ref,
         o_ref, kscr, vlo, vhi, biasm) = refs
    c = pl.program_id(1)
    dm = BAND_HEADS * BAND_HD
    pad_chunks = BAND_PAST // L

    @pl.when(c == 0)
    def _():
        if has_cache:
            lo = (lax.broadcasted_iota(jnp.int32, (1, dm), 1) & (LANES - 1)) < BAND_HD
            kscr[0:BAND_PAST, :] = ck_ref[0].astype(BF16)
            cv = cv_ref[0]
            vlo[0:BAND_PAST, :] = jnp.where(lo, cv, 0.0).astype(BF16)
            vhi[0:BAND_PAST, :] = jnp.where(lo, 0.0, cv).astype(BF16)
        else:
            zero = jnp.zeros((BAND_PAST, dm), BF16)
            kscr[0:BAND_PAST, :] = zero
            vlo[0:BAND_PAST, :] = zero
            vhi[0:BAND_PAST, :] = zero

    new0 = pl.multiple_of(BAND_PAST + c * L, L)
    kscr[pl.ds(new0, L), :] = k_ref[0]
    vlo[pl.ds(new0, L), :] = vlo_ref[0]
    vhi[pl.ds(new0, L), :] = vhi_ref[0]

    if has_cache:
        bias_src = bias_ref
    else:
        @pl.when(c < pad_chunks)
        def _():
            wcol = lax.broadcasted_iota(jnp.int32, (1, W), 1)
            valid = (wcol + c * L) >= BAND_PAST
            for h in range(BAND_HEADS):
                biasm[h] = jnp.where(valid, bias_ref[h], -jnp.inf)

        @pl.when(c == pad_chunks)
        def _():
            biasm[...] = bias_ref[...]

        bias_src = biasm

    w0 = pl.multiple_of(c * L, L)
    zb = zb_ref[0]
    slabs = [slice(s * LANES, (s + 1) * LANES) for s in range(BAND_HEADS // 2)]
    sc = []
    for sl in slabs:
        ks = kscr[pl.ds(w0, W), sl]
        sc.append(_dot_nt(qlo_ref[0, :, sl], ks))
        sc.append(_dot_nt(qhi_ref[0, :, sl], ks))
    ps, rs = [], []
    for h in range(BAND_HEADS):
        x = sc[h] + bias_src[h]
        e = jnp.exp(x - jnp.max(x, axis=-1, keepdims=True))
        rs.append(1.0 / jnp.sum(e, axis=-1, keepdims=True))
        ps.append(e.astype(BF16))
    pv = []
    for i, sl in enumerate(slabs):
        pv.append(_dot(ps[2 * i], vlo[pl.ds(w0, W), sl]))
        pv.append(_dot(ps[2 * i + 1], vhi[pl.ds(w0, W), sl]))
    outs = [((pv[2 * i] * rs[2 * i] + pv[2 * i + 1] * rs[2 * i + 1]) * zb[:, sl]).astype(BF16)
            for i, sl in enumerate(slabs)]
    o_ref[0] = jnp.concatenate(outs, axis=1)


def _band(qlo, qhi, kn, vlo, vhi, zb, bias, caches, L):
    b, s, dm = qlo.shape
    nc = s // L
    W = BAND_PAST + L
    blk = pl.BlockSpec((1, L, dm), lambda i, j: (i, j, 0))
    cache = pl.BlockSpec((1, BAND_PAST, dm), lambda i, j: (i, 0, 0))
    has_cache = caches is not None
    scratch = [pltpu.VMEM((BAND_PAST + s, dm), BF16)] * 3
    if not has_cache:
        scratch = scratch + [pltpu.VMEM((BAND_HEADS, L, W), F32)]
    return pl.pallas_call(
        functools.partial(_band_body, L=L, W=W, has_cache=has_cache),
        grid=(b, nc),
        in_specs=[blk] * 6 + [_const_spec((BAND_HEADS, L, W))] + ([cache, cache] if has_cache else []),
        out_specs=blk,
        out_shape=jax.ShapeDtypeStruct((b, s, dm), BF16),
        scratch_shapes=scratch,
        compiler_params=_params(("arbitrary", "arbitrary")),
        name="band",
    )(qlo, qhi, kn, vlo, vhi, zb, bias, *(caches if has_cache else ()))


def _gla_body(q_ref, k_ref, v_ref, lg_ref, z_ref, sinit_ref, og_ref, o_ref, st_ref, cbs, *, L, NB):
    c = pl.program_id(1)

    @pl.when(c == 0)
    def _():
        st_ref[...] = sinit_ref[...]

    levels = [L >> (t + 1) for t in range(L.bit_length() - 1)]
    small = [s for s in levels if 2 * s < 16]
    row = lax.broadcasted_iota(jnp.int32, (L, L), 0)
    col = lax.broadcasted_iota(jnp.int32, (L, L), 1)
    anchor = lambda s: lax.shift_left(lax.shift_right_logical(row, s.bit_length()), s.bit_length()) + (s - 1)
    between = lambda s: (col > jnp.minimum(row, anchor(s))) & (col <= jnp.maximum(row, anchor(s)))
    onehot = lambda m: jnp.where(m, 1.0, 0.0).astype(BF16)
    tri = onehot(row >= col)
    stack = jnp.concatenate([onehot(between(s)) for s in small], axis=0)
    ex = jnp.exp2
    og = og_ref[...]
    scale = GLA_DK ** -0.5
    pieces = [_split3(lg_ref[b] * LOG2_E) for b in range(NB)]
    cb_all = [_dot(tri, p[0]) + (_dot(tri, p[1]) + _dot(tri, p[2])) for p in pieces]
    y_small = [_dot(stack, p[0]) + _dot(stack, p[1]) for p in pieces]
    ydec = [dict() for _ in range(NB)]
    for b in range(NB):
        cbs[b] = cb_all[b]
        for t, s in enumerate(small):
            ydec[b][s] = y_small[b][t * L:(t + 1) * L]
        for s in levels:
            if s not in ydec[b]:
                anc = jnp.concatenate(
                    [jnp.broadcast_to(cbs[b, p * 2 * s + s - 1:p * 2 * s + s, :], (2 * s, cbs.shape[2]))
                     for p in range(L // (2 * s))], axis=0)
                ydec[b][s] = -jnp.abs(cb_all[b] - anc)

    heads = [(b, h) for b in range(NB) for h in range(GLA_HEADS)]
    pairs = range(len(heads) // 2)
    ksl = lambda h: slice(h * GLA_DK, (h + 1) * GLA_DK)
    vsl = lambda h: slice(h * GLA_DV, (h + 1) * GLA_DV)
    cb = [cb_all[b][:, ksl(h)] for b, h in heads]
    q = [q_ref[b, :, ksl(h)] * scale for b, h in heads]
    k = [k_ref[b, :, ksl(h)] for b, h in heads]
    vb = [v_ref[b, :, vsl(h)].astype(BF16) for b, h in heads]
    st = [st_ref[b, h] for b, h in heads]
    o_inter = _each(lambda q_, c_, s_: _dot_nt((q_ * ex(c_)).astype(BF16), s_.astype(BF16)), q, cb, st)
    cl = [c_[L - 1:L] for c_ in cb]
    ke = _each(lambda k_, l_, c_: (k_ * ex(l_ - c_)).astype(BF16), k, cl, cb)
    vtk = _each(_dot_tn, vb, ke)

    zk = jnp.zeros((L, GLA_DK), BF16)
    zv = jnp.zeros((L, GLA_DV), BF16)

    def side_by_side(qs, ks):
        lhs = jnp.concatenate(qs, axis=1).astype(BF16)
        rhs = jnp.concatenate([jnp.concatenate([ks[0].astype(BF16), zk], axis=1),
                               jnp.concatenate([zk, ks[1].astype(BF16)], axis=1)], axis=0)
        return _dot_nt(lhs, rhs)

    prow = lax.broadcasted_iota(jnp.int32, (L, 2 * L), 0)
    pcol = lax.broadcasted_iota(jnp.int32, (L, 2 * L), 1) & (L - 1)
    att = [jnp.where(prow == pcol, side_by_side((q[2 * p], q[2 * p + 1]), (k[2 * p], k[2 * p + 1])), 0.0)
           for p in pairs]
    for s in levels:
        sh = s.bit_length() - 1
        same_parent = lax.shift_right_logical(prow, sh + 1) == lax.shift_right_logical(pcol, sh + 1)
        take = same_parent & ((lax.shift_right_logical(prow, sh) & 1) == 1) & \
            ((lax.shift_right_logical(pcol, sh) & 1) == 0)
        f = [ex(ydec[b][s][:, ksl(h)]) for b, h in heads]
        prod = [side_by_side((q[2 * p] * f[2 * p], q[2 * p + 1] * f[2 * p + 1]),
                             (k[2 * p] * f[2 * p], k[2 * p + 1] * f[2 * p + 1])) for p in pairs]
        att = [jnp.where(take, prod[p], att[p]) for p in pairs]
    vpair = [jnp.concatenate([jnp.concatenate([vb[2 * p], zv], axis=1),
                              jnp.concatenate([zv, vb[2 * p + 1]], axis=1)], axis=0) for p in pairs]
    o_intra = [_dot(att[p].astype(BF16), vpair[p]) for p in pairs]
    for i, (b, h) in enumerate(heads):
        st_ref[b, h] = st[i] * ex(cl[i]) + vtk[i]
    for b in range(NB):
        z = z_ref[b]
        outs = [(_rms(o_inter[i] + o_intra[i // 2][:, (i % 2) * GLA_DV:(i % 2 + 1) * GLA_DV], og)
                 * z[:, vsl(h)]).astype(BF16) for i, (b_, h) in enumerate(heads) if b_ == b]
        o_ref[b] = jnp.concatenate(outs, axis=1)


def _gla(q, k, v, lg, z, sinit_t, og, L):
    b, s, _ = q.shape
    nc = s // L
    nb = GLA_GROUP if b % GLA_GROUP == 0 else 1
    qk = pl.BlockSpec((nb, L, GLA_HEADS * GLA_DK), lambda i, j: (i, j, 0))
    vv = pl.BlockSpec((nb, L, GLA_HEADS * GLA_DV), lambda i, j: (i, j, 0))
    st = pl.BlockSpec((nb, GLA_HEADS, GLA_DV, GLA_DK), lambda i, j: (i, 0, 0, 0))
    return pl.pallas_call(
        functools.partial(_gla_body, L=L, NB=nb),
        grid=(b // nb, nc),
        in_specs=[qk, qk, vv, qk, vv, st, _const_spec((1, GLA_DV))],
        out_specs=[vv, st],
        out_shape=[jax.ShapeDtypeStruct((b, s, GLA_HEADS * GLA_DV), BF16),
                   jax.ShapeDtypeStruct((b, GLA_HEADS, GLA_DV, GLA_DK), F32)],
        scratch_shapes=[pltpu.VMEM((nb, L, GLA_HEADS * GLA_DK), F32)],
        compiler_params=_params(("arbitrary", "arbitrary")),
        name="gla",
    )(q, k, v, lg, z, sinit_t, og)


def _post_body(*refs, n_o):
    x_ref = refs[0]
    o_refs = refs[1:1 + n_o]
    wout_ref, mg_ref, wmq_ref, mqg_ref, mk_ref, mv_ref, wmo_ref, y_ref = refs[1 + n_o:]
    tm = x_ref.shape[1]
    ts = tm // POST_SPLIT if tm % (8 * POST_SPLIT) == 0 else tm
    rows = [slice(t * ts, (t + 1) * ts) for t in range(tm // ts)]
    acc = [x_ref[0, r, :] for r in rows]
    off = 0
    for o_ref in o_refs:
        kd = o_ref.shape[-1]
        w = wout_ref[off:off + kd, :]
        acc = [a + _dot(o_ref[0, r, :], w) for a, r in zip(acc, rows)]
        off += kd
    hm = [_rms(a, mg_ref[...]).astype(BF16) for a in acc]
    qz = [_dot(h_, wmq_ref[...]) for h_ in hm]
    hw = MEM_HEADS * MEM_HD
    sls = [slice(h * MEM_HD, (h + 1) * MEM_HD) for h in range(MEM_HEADS)]
    mkb = [mk_ref[0, :, sl].astype(BF16) for sl in sls]
    mvb = [mv_ref[0, :, sl].astype(BF16) for sl in sls]
    qn = [[_rms(z[:, sl], mqg_ref[...]).astype(BF16) for sl in sls] for z in qz]
    sc = [[_dot_nt(qh, kh) for qh, kh in zip(qt, mkb)] for qt in qn]
    p = [[_softmax_rows(x * (MEM_HD ** -0.5)).astype(BF16) for x in st] for st in sc]
    oh = [[_dot(ph, vh) for ph, vh in zip(pt, mvb)] for pt in p]
    for t, r in enumerate(rows):
        outs = [oh[t][h] * _silu(qz[t][:, hw + h * MEM_HD:hw + (h + 1) * MEM_HD]) for h in range(MEM_HEADS)]
        oh[t] = jnp.concatenate(outs, axis=1).astype(BF16)
    ym = [_dot(om, wmo_ref[...]) for om in oh]
    for t, r in enumerate(rows):
        y_ref[0, r, :] = acc[t] + ym[t]


def _post(x, os_, wout, mg, wmq, mqg, mk, mv, wmo):
    b, s, d = x.shape
    tm = _row_tile(s, 512)
    hw = MEM_HEADS * MEM_HD
    nm = mk.shape[1]
    blk = lambda wd: pl.BlockSpec((1, tm, wd), lambda i, j: (i, j, 0))
    mem = pl.BlockSpec((1, nm, hw), lambda i, j: (i, 0, 0))
    kin = sum(o.shape[-1] for o in os_)
    return pl.pallas_call(
        functools.partial(_post_body, n_o=len(os_)),
        grid=(b, s // tm),
        in_specs=[blk(d)] + [blk(o.shape[-1]) for o in os_]
        + [_const_spec((kin, d)), _const_spec((1, d)), _const_spec((d, 2 * hw)), _const_spec((1, MEM_HD)),
           mem, mem, _const_spec((hw, d))],
        out_specs=blk(d),
        out_shape=jax.ShapeDtypeStruct((b, s, d), F32),
        compiler_params=_params(("arbitrary", "arbitrary")),
        name="post",
    )(x, *os_, wout, mg, wmq, mqg, mk, mv, wmo)


def _pad_cols(w, n):
    return jnp.pad(w, ((0, 0), (0, n - w.shape[1])))


def _prep_l0(w_in, conv_w, a_log, dt_bias, q_g, k_g):
    c0 = 3 * 1024
    ab = w_in[:, c0:c0 + 2 * GDN_HEADS]
    c1 = c0 + 2 * GDN_HEADS
    cols = lambda i: w_in[:, c1 + i * 1024:c1 + (i + 1) * 1024].astype(BF16)
    w = dict(
        wu=w_in[:, :c0].astype(BF16), wab=_pad_cols(ab, LANES).astype(BF16),
        wza=cols(0), wq=cols(1), wk=cols(2), wv=cols(3), wzb=cols(4),
        alog=jnp.pad(a_log, (0, LANES - GDN_HEADS)).reshape(1, LANES).astype(F32),
        dtb=jnp.pad(dt_bias, (0, LANES - GDN_HEADS)).reshape(1, LANES).astype(F32),
    )
    ns = 3 * GDN_HEADS
    cw = jnp.pad(conv_w.astype(F32), ((0, 8 - CONV_TAPS), (0, 0))).reshape(8, ns, LANES).transpose(1, 0, 2)
    dm = BAND_HEADS * BAND_HD
    head_of = np.arange(dm) // BAND_HD
    seg = jnp.asarray(head_of[:, None] == np.arange(LANES)[None, :], BF16)
    qg = jnp.tile(q_g.astype(F32), BAND_HEADS).reshape(1, dm)
    kg = jnp.tile(k_g.astype(F32), BAND_HEADS).reshape(1, dm)
    return w, cw, seg, qg, kg


def _band_bias_table(rel_bias, L, chunk):
    W = BAND_PAST + L
    n = np.arange(W + L - 1)
    idx = np.clip(BAND_PAST + (L - 1) - n, -BAND_MAX_REL, BAND_MAX_REL) + BAND_MAX_REL
    strip = rel_bias.astype(F32)[:, idx]
    bias = jnp.stack([strip[:, L - 1 - a:L - 1 - a + W] for a in range(L)], axis=1)
    back = np.arange(L)[:, None] // chunk - (np.arange(W)[None, :] - BAND_PAST) // chunk
    readable = (back >= 0) & (back <= BAND_PAST // chunk)
    return bias if readable.all() else jnp.where(jnp.asarray(readable)[None], bias, -jnp.inf)


def _prep_l1(w_in, w_gate_up, gate_bias):
    qk = GLA_HEADS * GLA_DK
    vw = GLA_HEADS * GLA_DV
    o = np.cumsum([0, qk, qk, vw, GLA_RANK, vw])
    return dict(
        wq=w_in[:, o[0]:o[1]].astype(BF16), wk=w_in[:, o[1]:o[2]].astype(BF16),
        wv=w_in[:, o[2]:o[3]].astype(BF16), wlr=_pad_cols(w_in[:, o[3]:o[4]], LANES).astype(BF16),
        wz=w_in[:, o[4]:o[5]].astype(BF16),
        wg=jnp.pad(w_gate_up, ((0, LANES - GLA_RANK), (0, 0))).astype(BF16),
        gbias=gate_bias.reshape(1, qk).astype(F32),
    )


def _layer0(x, conv_state, gdn_state, band_caches, mk, mv, norm_g, pw, cw, seg, qg, kg, bias,
            a_onorm_g, w_out, mnorm_g, w_mq, mq_g, w_mo, L):
    b, s, d = x.shape
    outs = _proj0(x.reshape(b * s, d), norm_g.reshape(1, d), pw, qg, kg, seg, s)
    u, gb, za, qlo, qhi, kn, klast, vlo, vhi, vlast, zb = (t.reshape(b, -1, t.shape[-1]) for t in outs)
    cinit = jnp.pad(conv_state.astype(F32), ((0, 0), (8 - (CONV_TAPS - 1), 0), (0, 0)))
    o_a, s_new = _gdn(u, gb, za, cinit, gdn_state.astype(F32), cw, a_onorm_g.reshape(1, LANES), L)
    o_b = _band(qlo, qhi, kn, vlo, vhi, zb, bias, band_caches, bias.shape[1])
    y = _post(x, (o_a, o_b), w_out, mnorm_g.reshape(1, d), w_mq, mq_g.reshape(1, MEM_HD), mk, mv, w_mo)
    return y, u[:, s - (CONV_TAPS - 1):, :], s_new, klast, vlast


def _layer1(x, gla_state, mk, mv, norm_g, pw, c_onorm_g, w_out, mnorm_g, w_mq, mq_g, w_mo, L):
    b, s, d = x.shape
    q, k, v, lg, z = _proj1(x.reshape(b * s, d), norm_g.reshape(1, d), pw)
    r3 = lambda t: t.reshape(b, s, t.shape[-1])
    q, k, v, lg, z = map(r3, (q, k, v, lg, z))
    st0 = jnp.swapaxes(gla_state.astype(F32), 2, 3)
    o, st = _gla(q, k, v, lg, z, st0, c_onorm_g.reshape(1, GLA_DV), L)
    y = _post(x, (o,), w_out, mnorm_g.reshape(1, d), w_mq, mq_g.reshape(1, MEM_HD), mk, mv, w_mo)
    return y, jnp.swapaxes(st, 2, 3)


def kernel(x_prompt, x_sample, mem_prompt, state_l0_gdn_conv, state_l0_gdn, cache_l0_band_k, cache_l0_band_v, cache_l0_mem_k, cache_l0_mem_v, state_l1_gla, cache_l1_mem_k, cache_l1_mem_v, l0_norm_g, l0_w_in, l0_conv_w, l0_a_log, l0_dt_bias, l0_a_onorm_g, l0_b_q_g, l0_b_k_g, l0_b_rel_bias, l0_w_out, l0_mnorm_g, l0_mem_norm_g, l0_w_mkv, l0_mk_g, l0_w_mq, l0_mq_g, l0_w_mo, l1_norm_g, l1_w_in, l1_w_gate_up, l1_gate_bias, l1_c_onorm_g, l1_w_out, l1_mnorm_g, l1_mem_norm_g, l1_w_mkv, l1_mk_g, l1_w_mq, l1_mq_g, l1_w_mo):
    bp, sp, d = x_prompt.shape
    bs, ss, _ = x_sample.shape
    nm = mem_prompt.shape[1]
    hw = MEM_HEADS * MEM_HD
    dm = BAND_HEADS * BAND_HD
    assert sp % CHUNK_ == 0 and ss % INV_SUB == 0 and ss <= CHUNK_
    assert cache_l0_band_k.shape[1] == BAND_PAST

    pw0, cw, seg, qg, kg = _prep_l0(l0_w_in, l0_conv_w, l0_a_log, l0_dt_bias, l0_b_q_g, l0_b_k_g)
    pw1 = _prep_l1(l1_w_in, l1_w_gate_up, l1_gate_bias)
    band_rows = BAND_STEP_CHUNKS * CHUNK_ if sp % (BAND_STEP_CHUNKS * CHUNK_) == 0 else CHUNK_
    bias_p = _band_bias_table(l0_b_rel_bias, band_rows, CHUNK_)
    bias_s = _band_bias_table(l0_b_rel_bias, ss, ss)
    bf = lambda w: w.astype(BF16)
    mem2 = mem_prompt.reshape(bp * nm, d)

    mk0, mv0 = _memkv(mem2, l0_mem_norm_g.reshape(1, d), bf(l0_w_mkv), l0_mk_g.reshape(1, MEM_HD))
    mk0 = mk0.reshape(bp, nm, hw)
    mv0 = mv0.reshape(bp, nm, hw)
    l0_shared = (l0_norm_g, pw0, cw, seg, qg, kg)
    l0_tail = (l0_a_onorm_g, bf(l0_w_out), l0_mnorm_g, bf(l0_w_mq), l0_mq_g, bf(l0_w_mo))
    zeros_conv = jnp.zeros((bp, CONV_TAPS - 1, 3 * GDN_HEADS * LANES), F32)
    zeros_gdn = jnp.zeros((bp, GDN_HEADS, GDN_DK, LANES), F32)
    yp, p_conv, p_gdn, p_kn, p_v = _layer0(
        x_prompt, zeros_conv, zeros_gdn, None, mk0, mv0,
        *l0_shared, bias_p, *l0_tail, CHUNK_)
    ys, s_conv, s_gdn, s_kn, s_v = _layer0(
        x_sample, state_l0_gdn_conv, state_l0_gdn,
        (cache_l0_band_k.reshape(bs, BAND_PAST, dm), cache_l0_band_v.reshape(bs, BAND_PAST, dm)),
        cache_l0_mem_k.reshape(bs, nm, hw), cache_l0_mem_v.reshape(bs, nm, hw),
        *l0_shared, bias_s, *l0_tail, ss)

    mk1, mv1 = _memkv(mem2, l1_mem_norm_g.reshape(1, d), bf(l1_w_mkv), l1_mk_g.reshape(1, MEM_HD))
    mk1 = mk1.reshape(bp, nm, hw)
    mv1 = mv1.reshape(bp, nm, hw)
    l1_tail = (l1_c_onorm_g, bf(l1_w_out), l1_mnorm_g, bf(l1_w_mq), l1_mq_g, bf(l1_w_mo))
    zeros_gla = jnp.zeros((bp, GLA_HEADS, GLA_DK, GLA_DV), F32)
    yp, p_gla = _layer1(yp, zeros_gla, mk1, mv1, l1_norm_g, pw1, *l1_tail, CHUNK_)
    ys, s_gla = _layer1(ys, state_l1_gla, cache_l1_mem_k.reshape(bs, nm, hw),
                        cache_l1_mem_v.reshape(bs, nm, hw), l1_norm_g, pw1, *l1_tail, ss)

    h4 = lambda t: t.reshape(t.shape[0], t.shape[1], BAND_HEADS, BAND_HD)
    m4 = lambda t: t.reshape(bp, nm, MEM_HEADS, MEM_HD)
    return (yp, ys, p_conv, p_gdn, h4(p_kn), h4(p_v),
            m4(mk0), m4(mv0), p_gla, m4(mk1), m4(mv1),
            s_conv, s_gdn, h4(s_kn), h4(s_v), s_gla)
```

```python
import functools

import jax
import jax.numpy as jnp
import numpy as np
from jax import lax
from jax.experimental import pallas as pl
from jax.experimental.pallas import tpu as pltpu

F32 = jnp.float32
BF16 = jnp.bfloat16
NORM_EPS = 1e-6
LOG2_E = 1.4426950408889634

D_MODEL_ = 1024
CHUNK_ = 64
CONV_TAPS = 4
GDN_HEADS = 8
GDN_DK = 128
BAND_HEADS = 16
BAND_HD = 64
BAND_PAST = 512
BAND_MAX_REL = 128
GLA_HEADS = 8
GLA_DK = 128
GLA_DV = 256
GLA_RANK = 16
GLA_TAU = 16.0
MEM_HEADS = 4
MEM_HD = 128
INV_SUB = 16
GDN_GROUP = 2
GDN_STEP_CHUNKS = 1
GLA_GROUP = 2
POST_SPLIT = 2
PROJ_SPLIT = 2
BAND_STEP_CHUNKS = 2
LANES = 128
VMEM_LIMIT = 56 * 1024 * 1024


def _dot(a, b):
    return jnp.dot(a, b, preferred_element_type=F32)


def _dot_nt(a, b):
    return lax.dot_general(a, b, (((1,), (1,)), ((), ())), preferred_element_type=F32)


def _dot_tn(a, b):
    return lax.dot_general(a, b, (((0,), (0,)), ((), ())), preferred_element_type=F32)


def _split2(x):
    hi = x.astype(BF16)
    lo = (x - hi.astype(F32)).astype(BF16)
    return hi, lo


def _split3(x):
    hi = x.astype(BF16)
    r = x - hi.astype(F32)
    mid = r.astype(BF16)
    lo = (r - mid.astype(F32)).astype(BF16)
    return hi, mid, lo


def _dot_x3(a, b):
    ah, al = _split2(a)
    bh, bl = _split2(b)
    return _dot(ah, bh) + (_dot(ah, bl) + _dot(al, bh))


def _dot_exact_lhs(a_bf, b):
    h, m, l = _split3(b)
    return _dot(a_bf, h) + (_dot(a_bf, m) + _dot(a_bf, l))


def _rms(x, g):
    ms = jnp.mean(x * x, axis=-1, keepdims=True)
    return x * lax.rsqrt(ms + NORM_EPS) * g


def _silu(x):
    return x * jax.nn.sigmoid(x)


def _softplus(x):
    return jnp.maximum(x, 0.0) + jnp.log1p(jnp.exp(-jnp.abs(x)))


def _softmax_rows(s):
    m = jnp.max(s, axis=-1, keepdims=True)
    e = jnp.exp(s - m)
    return e * (1.0 / jnp.sum(e, axis=-1, keepdims=True))


def _const_spec(shape):
    nd = len(shape)
    return pl.BlockSpec(shape, lambda *_: (0,) * nd, pipeline_mode=pl.Buffered(1))


def _params(sem):
    return pltpu.CompilerParams(dimension_semantics=sem, vmem_limit_bytes=VMEM_LIMIT)


def _row_tile(n, want):
    t = min(n, want)
    assert n % t == 0
    return t


def _proj0_body(x_ref, g_ref, wu_ref, wab_ref, wza_ref, wq_ref, wk_ref, wv_ref, wzb_ref,
                alog_ref, dtb_ref, qg_ref, kg_ref, seg_ref,
                u_ref, gb_ref, za_ref, qlo_ref, qhi_ref, kn_ref, klast_ref, vlo_ref, vhi_ref, vlast_ref, zb_ref):
    h = _rms(x_ref[...], g_ref[...]).astype(BF16)
    dm = BAND_HEADS * BAND_HD
    lo = (lax.broadcasted_iota(jnp.int32, (1, dm), 1) & (LANES - 1)) < BAND_HD
    seg = seg_ref[...]
    q = _dot(h, wq_ref[...])
    k = _dot(h, wk_ref[...])
    u_ref[...] = _dot(h, wu_ref[...])

    def head_rsqrt(x):
        ss = _dot((x * x).astype(BF16), seg)
        return lax.rsqrt(ss * (1.0 / BAND_HD) + NORM_EPS)

    rq = head_rsqrt(q)
    rk = head_rsqrt(k)
    za_ref[...] = _silu(_dot(h, wza_ref[...]))
    v = _dot(h, wv_ref[...])
    lo1 = lo[:, :LANES]

    def spread(r):
        return jnp.concatenate([jnp.where(lo1, r[:, 2 * s:2 * s + 1], r[:, 2 * s + 1:2 * s + 2])
                                for s in range(BAND_HEADS // 2)], axis=1)

    qn = q * spread(rq) * qg_ref[...] * (BAND_HD ** -0.5)
    kn = k * spread(rk) * kg_ref[...]
    zb_ref[...] = _silu(_dot(h, wzb_ref[...]))
    ab = _dot(h, wab_ref[...])
    qlo_ref[...] = jnp.where(lo, qn, 0.0).astype(BF16)
    qhi_ref[...] = jnp.where(lo, 0.0, qn).astype(BF16)
    kn_ref[...] = kn.astype(BF16)
    klast_ref[...] = kn
    vlo_ref[...] = jnp.where(lo, v, 0.0).astype(BF16)
    vhi_ref[...] = jnp.where(lo, 0.0, v).astype(BF16)
    vlast_ref[...] = v
    lane = lax.broadcasted_iota(jnp.int32, ab.shape, 1)
    gval = -jnp.exp(alog_ref[...]) * _softplus(ab + dtb_ref[...])
    gb_ref[...] = jnp.where(lane < GDN_HEADS, gval, jax.nn.sigmoid(ab))


def _proj0(x2, g, w, qg, kg, seg, rows_per_batch):
    n, d = x2.shape
    tm = _row_tile(n, 256)
    dm = BAND_HEADS * BAND_HD
    widths = (3 * 1024, LANES, 1024, dm, dm, dm, dm)
    row = lambda wd: pl.BlockSpec((tm, wd), lambda i: (i, 0))
    keep = min(BAND_PAST, rows_per_batch)
    if rows_per_batch > keep:
        assert rows_per_batch % tm == 0 and keep % tm == 0
        tpb, kt = rows_per_batch // tm, keep // tm
        last = pl.BlockSpec((tm, dm), lambda i: ((i // tpb) * kt + jnp.maximum(i % tpb - (tpb - kt), 0), 0))
        n_last = (n // rows_per_batch) * keep
    else:
        last, n_last = row(dm), n
    f32 = lambda rows, wd: jax.ShapeDtypeStruct((rows, wd), F32)
    bf16 = lambda wd: jax.ShapeDtypeStruct((n, wd), BF16)
    return pl.pallas_call(
        _proj0_body,
        grid=(n // tm,),
        in_specs=[row(d), _const_spec((1, d))]
        + [_const_spec((d, wd)) for wd in widths]
        + [_const_spec((1, LANES)), _const_spec((1, LANES)), _const_spec((1, dm)), _const_spec((1, dm)),
           _const_spec((dm, LANES))],
        out_specs=[row(3 * 1024), row(LANES), row(1024), row(dm), row(dm), row(dm), last, row(dm), row(dm), last,
                   row(dm)],
        out_shape=[f32(n, 3 * 1024), f32(n, LANES), f32(n, 1024), bf16(dm), bf16(dm), bf16(dm), f32(n_last, dm),
                   bf16(dm), bf16(dm), f32(n_last, dm), f32(n, dm)],
        compiler_params=_params(("arbitrary",)),
        name="proj0",
    )(x2, g, w["wu"], w["wab"], w["wza"], w["wq"], w["wk"], w["wv"], w["wzb"], w["alog"], w["dtb"],
      qg, kg, seg)


def _proj1_body(x_ref, g_ref, wq_ref, wk_ref, wv_ref, wlr_ref, wz_ref, wg_ref, gbias_ref,
                q_ref, k_ref, v_ref, lg_ref, z_ref):
    tm = x_ref.shape[0]
    ts = tm // PROJ_SPLIT if tm % (16 * PROJ_SPLIT) == 0 else tm
    rows = [slice(t * ts, (t + 1) * ts) for t in range(tm // ts)]
    h = [_rms(x_ref[r, :], g_ref[...]).astype(BF16) for r in rows]
    lr = [_dot(h_, wlr_ref[...]) for h_ in h]
    for r, h_ in zip(rows, h):
        q_ref[r, :] = _dot(h_, wq_ref[...])
    pre = [_dot(x.astype(BF16), wg_ref[...]) + gbias_ref[...] for x in lr]
    zs = [_dot(h_, wz_ref[...]) for h_ in h]
    for r, p_ in zip(rows, pre):
        lg_ref[r, :] = -_softplus(-p_) * (1.0 / GLA_TAU)
    for r, h_ in zip(rows, h):
        k_ref[r, :] = _dot(h_, wk_ref[...])
    for r, z_ in zip(rows, zs):
        z_ref[r, :] = _silu(z_)
    for r, h_ in zip(rows, h):
        v_ref[r, :] = _dot(h_, wv_ref[...])


def _proj1(x2, g, w):
    n, d = x2.shape
    tm = _row_tile(n, 512)
    row = lambda wd: pl.BlockSpec((tm, wd), lambda i: (i, 0))
    outw = (1024, 1024, 2048, 1024, 2048)
    return pl.pallas_call(
        _proj1_body,
        grid=(n // tm,),
        in_specs=[row(d), _const_spec((1, d)), _const_spec((d, 1024)), _const_spec((d, 1024)),
                  _const_spec((d, 2048)), _const_spec((d, LANES)), _const_spec((d, 2048)),
                  _const_spec((LANES, 1024)), _const_spec((1, 1024))],
        out_specs=[row(wd) for wd in outw],
        out_shape=[jax.ShapeDtypeStruct((n, wd), F32) for wd in outw],
        compiler_params=_params(("arbitrary",)),
        name="proj1",
    )(x2, g, w["wq"], w["wk"], w["wv"], w["wlr"], w["wz"], w["wg"], w["gbias"])


def _memkv_body(m_ref, g_ref, w_ref, kg_ref, k_ref, v_ref):
    h = _rms(m_ref[...], g_ref[...]).astype(BF16)
    kv = _dot(h, w_ref[...])
    hw = MEM_HEADS * MEM_HD
    for hh in range(MEM_HEADS):
        sl = slice(hh * MEM_HD, (hh + 1) * MEM_HD)
        k_ref[:, sl] = _rms(kv[:, sl], kg_ref[...])
    v_ref[...] = kv[:, hw:]


def _memkv(m2, g, w_bf, kg):
    n, d = m2.shape
    tm = _row_tile(n, 256)
    hw = MEM_HEADS * MEM_HD
    row = lambda wd: pl.BlockSpec((tm, wd), lambda i: (i, 0))
    return pl.pallas_call(
        _memkv_body,
        grid=(n // tm,),
        in_specs=[row(d), _const_spec((1, d)), _const_spec((d, 2 * hw)), _const_spec((1, MEM_HD))],
        out_specs=[row(hw), row(hw)],
        out_shape=[jax.ShapeDtypeStruct((n, hw), F32)] * 2,
        compiler_params=_params(("arbitrary",)),
        name="memkv",
    )(m2, g, w_bf, kg)


def _each(fn, *lists):
    return [fn(*xs) for xs in zip(*lists)]


def _unit_lower_inverse(a, eye, bd, mm):
    d = _each(lambda x: jnp.where(bd, x, 0.0), a)
    nl = _each(lambda x, y: x - y, a, d)
    d2 = _each(mm, d, d)
    d4 = _each(mm, d2, d2)
    td = _each(lambda x, y: mm(eye - x, eye + y), d, d2)
    d8 = _each(mm, d4, d4)
    td = _each(lambda x, y: mm(x, eye + y), td, d4)
    td = _each(lambda x, y: mm(x, eye + y), td, d8)
    m = _each(mm, td, nl)
    m2 = _each(mm, m, m)
    mt = _each(mm, m, td)
    return _each(lambda x, y, z: mm(eye + x, y - z), m2, td, mt)


def _gdn_body(u_ref, gb_ref, za_ref, cinit_ref, sinit_ref, cw_ref, og_ref,
              o_ref, s_ref, ubuf, *, L, NB, G):
    c = pl.program_id(1)
    ns = 3 * GDN_HEADS
    hist = 8
    R = G * L

    @pl.when(c == 0)
    def _():
        for b in range(NB):
            for j in range(ns):
                ubuf[b * ns + j, 0:hist, :] = cinit_ref[b, :, j * LANES:(j + 1) * LANES]
        s_ref[...] = sinit_ref[...]

    for b in range(NB):
        for j in range(ns):
            ubuf[b * ns + j, hist:hist + R, :] = u_ref[b, :, j * LANES:(j + 1) * LANES]
    base = hist - (CONV_TAPS - 1)
    ys = []
    for b in range(NB):
        bsl = slice(b * ns, (b + 1) * ns)
        yb = ubuf[bsl, base:base + R, :] * cw_ref[:, 0:1, :]
        for i in range(1, CONV_TAPS):
            yb = yb + ubuf[bsl, base + i:base + i + R, :] * cw_ref[:, i:i + 1, :]
        ys.append(_silu(yb))
    ubuf[:, base:hist, :] = ubuf[:, base + R:hist + R, :]

    row = lax.broadcasted_iota(jnp.int32, (L, 2 * L), 0)
    lane = lax.broadcasted_iota(jnp.int32, (L, 2 * L), 1)
    col = lane & (L - 1)
    left = lane < L
    incl = row >= col
    strict = row > col
    sub_shift = INV_SUB.bit_length() - 1
    bd = lax.shift_right_logical(row, sub_shift) == lax.shift_right_logical(col, sub_shift)
    eye = jnp.where(row == col, 1.0, 0.0).astype(F32)
    trow = lax.broadcasted_iota(jnp.int32, (L, L), 0)
    tcol = lax.broadcasted_iota(jnp.int32, (L, L), 1)
    tri = jnp.where(trow >= tcol, 1.0, 0.0).astype(BF16)
    og = og_ref[...]
    heads = [(b, g, h) for b in range(NB) for g in range(G) for h in range(GDN_HEADS)]
    pairs = range(len(heads) // 2)
    rows_of = lambda g: slice(g * L, (g + 1) * L)
    l2n = lambda x: x * lax.rsqrt(jnp.sum(x * x, axis=-1, keepdims=True) + NORM_EPS)
    q = [l2n(ys[b][h][rows_of(g)]) * (GDN_DK ** -0.5) for b, g, h in heads]
    k = [l2n(ys[b][GDN_HEADS + h][rows_of(g)]) for b, g, h in heads]
    v = [ys[b][2 * GDN_HEADS + h][rows_of(g)] for b, g, h in heads]
    gc, gr, bc = [], [], []
    for b in range(NB):
        for g in range(G):
            gbv = gb_ref[b, rows_of(g), :]
            gcum = _dot_exact_lhs(tri, gbv)
            gpad = jnp.concatenate([gcum, jnp.zeros((LANES - L, LANES), F32)], axis=0)
            gt = gpad.T
            for h in range(GDN_HEADS):
                gc.append(gcum[:, h:h + 1])
                gr.append(gt[h:h + 1, 0:L])
                bc.append(gbv[:, GDN_HEADS + h:GDN_HEADS + h + 1])
    side = lambda x0, x1: jnp.where(left, x0, x1)
    gcp = [side(gc[2 * p], gc[2 * p + 1]) for p in pairs]
    grp = [jnp.concatenate([gr[2 * p], gr[2 * p + 1]], axis=1) for p in pairs]
    bcp = [side(bc[2 * p], bc[2 * p + 1]) for p in pairs]
    dec = _each(lambda c_, r_: jnp.where(incl, jnp.exp(jnp.where(incl, c_ - r_, 0.0)), 0.0), gcp, grp)
    kb = _each(lambda x: x.astype(BF16), k)
    qb = _each(lambda x: x.astype(BF16), q)

    def blockdiag(y0, y1):
        z0 = jnp.zeros(y1.shape, y1.dtype)
        z1 = jnp.zeros(y0.shape, y0.dtype)
        return jnp.concatenate([jnp.concatenate([y0, z0], axis=1), jnp.concatenate([z1, y1], axis=1)], axis=0)

    def mm(x, y):
        yb = y.astype(BF16)
        zero = jnp.zeros_like(yb)
        return _dot(x.astype(BF16), jnp.concatenate([jnp.where(left, yb, zero), jnp.where(left, zero, yb)], axis=0))

    kq = [_dot_nt(jnp.concatenate([jnp.concatenate([kb[2 * p], qb[2 * p]], axis=0),
                                   jnp.concatenate([kb[2 * p + 1], qb[2 * p + 1]], axis=0)], axis=1),
                  blockdiag(kb[2 * p], kb[2 * p + 1])) for p in pairs]
    a = _each(lambda b_, x, d_: jnp.where(strict, b_ * x[:L] * d_, 0.0), bcp, kq, dec)
    t = _unit_lower_inverse(a, eye, bd, mm)
    eg = _each(jnp.exp, gc)
    rhs = _each(lambda b_, v_, e_, k_: jnp.concatenate([b_ * v_, (b_ * e_) * k_], axis=1).astype(BF16),
                bc, v, eg, k)
    sol = [_dot(t[p].astype(BF16), blockdiag(rhs[2 * p], rhs[2 * p + 1])) for p in pairs]
    solk = lambda i: sol[i // 2][:, 2 * (i % 2) * LANES + LANES:2 * (i % 2 + 1) * LANES]
    solv = lambda i: sol[i // 2][:, 2 * (i % 2) * LANES:2 * (i % 2) * LANES + LANES]
    qkd = [(kq[p][L:] * dec[p]).astype(BF16) for p in pairs]
    gl = [c_[L - 1:L, :] for c_ in gc]
    kd = _each(lambda k_, l_, c_: (k_ * jnp.exp(l_ - c_)).astype(BF16), k, gl, gc)

    item = lambda b, g, h: (b * G + g) * GDN_HEADS + h
    half = lambda x, i: x[:, (i % 2) * LANES:(i % 2 + 1) * LANES]
    s = {(b, h): s_ref[b, h] for b in range(NB) for h in range(GDN_HEADS)}
    for g in range(G):
        ids = [item(b, g, h) for b in range(NB) for h in range(GDN_HEADS)]
        sb = {i: s[(heads[i][0], heads[i][2])].astype(BF16) for i in ids}
        ksq = {i: _dot(jnp.concatenate([jnp.concatenate([solk(i).astype(BF16), qb[i]], axis=0),
                                        jnp.concatenate([solk(i + 1).astype(BF16), qb[i + 1]], axis=0)], axis=1),
                       blockdiag(sb[i], sb[i + 1])) for i in ids[::2]}
        ub = {i: (solv(i) - half(ksq[i - i % 2][:L], i)).astype(BF16) for i in ids}
        qku = {i: _dot(qkd[i // 2], blockdiag(ub[i], ub[i + 1])) for i in ids[::2]}
        ktu = {i: _dot_tn(kd[i], ub[i]) for i in ids}
        for i in ids:
            b, _, h = heads[i]
            s[(b, h)] = s[(b, h)] * jnp.exp(gl[i]) + ktu[i]
        for b in range(NB):
            za = za_ref[b, rows_of(g), :]
            outs = [(_rms(half(ksq[i - i % 2][L:], i) * eg[i] + half(qku[i - i % 2], i), og)
                     * za[:, heads[i][2] * LANES:(heads[i][2] + 1) * LANES]).astype(BF16)
                    for i in ids if heads[i][0] == b]
            o_ref[b, rows_of(g), :] = jnp.concatenate(outs, axis=1)
    for (b, h), val in s.items():
        s_ref[b, h] = val


def _gdn(u, gb, za, cinit, sinit, cw, og, L):
    b, s, _ = u.shape
    nc = s // L
    ns = 3 * GDN_HEADS
    nb = GDN_GROUP if b % GDN_GROUP == 0 else 1
    g = GDN_STEP_CHUNKS if nc % GDN_STEP_CHUNKS == 0 else 1
    blk = lambda wd: pl.BlockSpec((nb, g * L, wd), lambda i, j: (i, j, 0))
    return pl.pallas_call(
        functools.partial(_gdn_body, L=L, NB=nb, G=g),
        grid=(b // nb, nc // g),
        in_specs=[blk(ns * LANES), blk(LANES), blk(GDN_HEADS * LANES),
                  pl.BlockSpec((nb, 8, ns * LANES), lambda i, j: (i, 0, 0)),
                  pl.BlockSpec((nb, GDN_HEADS, GDN_DK, LANES), lambda i, j: (i, 0, 0, 0)),
                  _const_spec((ns, 8, LANES)), _const_spec((1, LANES))],
        out_specs=[blk(GDN_HEADS * LANES),
                   pl.BlockSpec((nb, GDN_HEADS, GDN_DK, LANES), lambda i, j: (i, 0, 0, 0))],
        out_shape=[jax.ShapeDtypeStruct((b, s, GDN_HEADS * LANES), BF16),
                   jax.ShapeDtypeStruct((b, GDN_HEADS, GDN_DK, LANES), F32)],
        scratch_shapes=[pltpu.VMEM((nb * ns, 8 + g * L, LANES), F32)],
        compiler_params=_params(("arbitrary", "arbitrary")),
        name="gdn",
    )(u, gb, za, cinit, sinit, cw, og)


def _band_body(*refs, L, W, has_cache):
    if has_cache:
        (qlo_ref, qhi_ref, k_ref, vlo_ref, vhi_ref, zb_ref, bias_ref, ck_ref, cv_ref,
         o_ref, kscr, vlo, vhi) = refs
    else:
        (qlo_ref, qhi_ref, k_ref, vlo_ref, vhi_ref, zb_ref, bias_ref,
         o_ref, kscr, vlo, vhi, biasm) = refs
    c = pl.program_id(1)
    dm = BAND_HEADS * BAND_HD
    pad_chunks = BAND_PAST // L

    @pl.when(c == 0)
    def _():
        if has_cache:
            lo = (lax.broadcasted_iota(jnp.int32, (1, dm), 1) & (LANES - 1)) < BAND_HD
            kscr[0:BAND_PAST, :] = ck_ref[0].astype(BF16)
            cv = cv_ref[0]
            vlo[0:BAND_PAST, :] = jnp.where(lo, cv, 0.0).astype(BF16)
            vhi[0:BAND_PAST, :] = jnp.where(lo, 0.0, cv).astype(BF16)
        else:
            zero = jnp.zeros((BAND_PAST, dm), BF16)
            kscr[0:BAND_PAST, :] = zero
            vlo[0:BAND_PAST, :] = zero
            vhi[0:BAND_PAST, :] = zero

    new0 = pl.multiple_of(BAND_PAST + c * L, L)
    kscr[pl.ds(new0, L), :] = k_ref[0]
    vlo[pl.ds(new0, L), :] = vlo_ref[0]
    vhi[pl.ds(new0, L), :] = vhi_ref[0]

    if has_cache:
        bias_src = bias_ref
    else:
        @pl.when(c < pad_chunks)
        def _():
            wcol = lax.broadcasted_iota(jnp.int32, (1, W), 1)
            valid = (wcol + c * L) >= BAND_PAST
            for h in range(BAND_HEADS):
                biasm[h] = jnp.where(valid, bias_ref[h], -jnp.inf)

        @pl.when(c == pad_chunks)
        def _():
            biasm[...] = bias_ref[...]

        bias_src = biasm

    w0 = pl.multiple_of(c * L, L)
    zb = zb_ref[0]
    slabs = [slice(s * LANES, (s + 1) * LANES) for s in range(BAND_HEADS // 2)]
    sc = []
    for sl in slabs:
        ks = kscr[pl.ds(w0, W), sl]
        sc.append(_dot_nt(qlo_ref[0, :, sl], ks))
        sc.append(_dot_nt(qhi_ref[0, :, sl], ks))
    ps, rs = [], []
    for h in range(BAND_HEADS):
        x = sc[h] + bias_src[h]
        e = jnp.exp(x - jnp.max(x, axis=-1, keepdims=True))
        rs.append(1.0 / jnp.sum(e, axis=-1, keepdims=True))
        ps.append(e.astype(BF16))
    pv = []
    for i, sl in enumerate(slabs):
        pv.append(_dot(ps[2 * i], vlo[pl.ds(w0, W), sl]))
        pv.append(_dot(ps[2 * i + 1], vhi[pl.ds(w0, W), sl]))
    outs = [((pv[2 * i] * rs[2 * i] + pv[2 * i + 1] * rs[2 * i + 1]) * zb[:, sl]).astype(BF16)
            for i, sl in enumerate(slabs)]
    o_ref[0] = jnp.concatenate(outs, axis=1)


def _band(qlo, qhi, kn, vlo, vhi, zb, bias, caches, L):
    b, s, dm = qlo.shape
    nc = s // L
    W = BAND_PAST + L
    blk = pl.BlockSpec((1, L, dm), lambda i, j: (i, j, 0))
    cache = pl.BlockSpec((1, BAND_PAST, dm), lambda i, j: (i, 0, 0))
    has_cache = caches is not None
    scratch = [pltpu.VMEM((BAND_PAST + s, dm), BF16)] * 3
    if not has_cache:
        scratch = scratch + [pltpu.VMEM((BAND_HEADS, L, W), F32)]
    return pl.pallas_call(
        functools.partial(_band_body, L=L, W=W, has_cache=has_cache),
        grid=(b, nc),
        in_specs=[blk] * 6 + [_const_spec((BAND_HEADS, L, W))] + ([cache, cache] if has_cache else []),
        out_specs=blk,
        out_shape=jax.ShapeDtypeStruct((b, s, dm), BF16),
        scratch_shapes=scratch,
        compiler_params=_params(("arbitrary", "arbitrary")),
        name="band",
    )(qlo, qhi, kn, vlo, vhi, zb, bias, *(caches if has_cache else ()))


def _gla_body(q_ref, k_ref, v_ref, lg_ref, z_ref, sinit_ref, og_ref, o_ref, st_ref, cbs, *, L, NB):
    c = pl.program_id(1)

    @pl.when(c == 0)
    def _():
        st_ref[...] = sinit_ref[...]

    levels = [L >> (t + 1) for t in range(L.bit_length() - 1)]
    small = [s for s in levels if 2 * s < 16]
    row = lax.broadcasted_iota(jnp.int32, (L, L), 0)
    col = lax.broadcasted_iota(jnp.int32, (L, L), 1)
    anchor = lambda s: lax.shift_left(lax.shift_right_logical(row, s.bit_length()), s.bit_length()) + (s - 1)
    between = lambda s: (col > jnp.minimum(row, anchor(s))) & (col <= jnp.maximum(row, anchor(s)))
    onehot = lambda m: jnp.where(m, 1.0, 0.0).astype(BF16)
    tri = onehot(row >= col)
    stack = jnp.concatenate([onehot(between(s)) for s in small], axis=0)
    ex = jnp.exp2
    og = og_ref[...]
    scale = GLA_DK ** -0.5
    pieces = [_split3(lg_ref[b] * LOG2_E) for b in range(NB)]
    cb_all = [_dot(tri, p[0]) + (_dot(tri, p[1]) + _dot(tri, p[2])) for p in pieces]
    y_small = [_dot(stack, p[0]) + _dot(stack, p[1]) for p in pieces]
    ydec = [dict() for _ in range(NB)]
    for b in range(NB):
        cbs[b] = cb_all[b]
        for t, s in enumerate(small):
            ydec[b][s] = y_small[b][t * L:(t + 1) * L]
        for s in levels:
            if s not in ydec[b]:
                anc = jnp.concatenate(
                    [jnp.broadcast_to(cbs[b, p * 2 * s + s - 1:p * 2 * s + s, :], (2 * s, cbs.shape[2]))
                     for p in range(L // (2 * s))], axis=0)
                ydec[b][s] = -jnp.abs(cb_all[b] - anc)

    heads = [(b, h) for b in range(NB) for h in range(GLA_HEADS)]
    pairs = range(len(heads) // 2)
    ksl = lambda h: slice(h * GLA_DK, (h + 1) * GLA_DK)
    vsl = lambda h: slice(h * GLA_DV, (h + 1) * GLA_DV)
    cb = [cb_all[b][:, ksl(h)] for b, h in heads]
    q = [q_ref[b, :, ksl(h)] * scale for b, h in heads]
    k = [k_ref[b, :, ksl(h)] for b, h in heads]
    vb = [v_ref[b, :, vsl(h)].astype(BF16) for b, h in heads]
    st = [st_ref[b, h] for b, h in heads]
    o_inter = _each(lambda q_, c_, s_: _dot_nt((q_ * ex(c_)).astype(BF16), s_.astype(BF16)), q, cb, st)
    cl = [c_[L - 1:L] for c_ in cb]
    ke = _each(lambda k_, l_, c_: (k_ * ex(l_ - c_)).astype(BF16), k, cl, cb)
    vtk = _each(_dot_tn, vb, ke)

    zk = jnp.zeros((L, GLA_DK), BF16)
    zv = jnp.zeros((L, GLA_DV), BF16)

    def side_by_side(qs, ks):
        lhs = jnp.concatenate(qs, axis=1).astype(BF16)
        rhs = jnp.concatenate([jnp.concatenate([ks[0].astype(BF16), zk], axis=1),
                               jnp.concatenate([zk, ks[1].astype(BF16)], axis=1)], axis=0)
        return _dot_nt(lhs, rhs)

    prow = lax.broadcasted_iota(jnp.int32, (L, 2 * L), 0)
    pcol = lax.broadcasted_iota(jnp.int32, (L, 2 * L), 1) & (L - 1)
    att = [jnp.where(prow == pcol, side_by_side((q[2 * p], q[2 * p + 1]), (k[2 * p], k[2 * p + 1])), 0.0)
           for p in pairs]
    for s in levels:
        sh = s.bit_length() - 1
        same_parent = lax.shift_right_logical(prow, sh + 1) == lax.shift_right_logical(pcol, sh + 1)
        take = same_parent & ((lax.shift_right_logical(prow, sh) & 1) == 1) & \
            ((lax.shift_right_logical(pcol, sh) & 1) == 0)
        f = [ex(ydec[b][s][:, ksl(h)]) for b, h in heads]
        prod = [side_by_side((q[2 * p] * f[2 * p], q[2 * p + 1] * f[2 * p + 1]),
                             (k[2 * p] * f[2 * p], k[2 * p + 1] * f[2 * p + 1])) for p in pairs]
        att = [jnp.where(take, prod[p], att[p]) for p in pairs]
    vpair = [jnp.concatenate([jnp.concatenate([vb[2 * p], zv], axis=1),
                              jnp.concatenate([zv, vb[2 * p + 1]], axis=1)], axis=0) for p in pairs]
    o_intra = [_dot(att[p].astype(BF16), vpair[p]) for p in pairs]
    for i, (b, h) in enumerate(heads):
        st_ref[b, h] = st[i] * ex(cl[i]) + vtk[i]
    for b in range(NB):
        z = z_ref[b]
        outs = [(_rms(o_inter[i] + o_intra[i // 2][:, (i % 2) * GLA_DV:(i % 2 + 1) * GLA_DV], og)
                 * z[:, vsl(h)]).astype(BF16) for i, (b_, h) in enumerate(heads) if b_ == b]
        o_ref[b] = jnp.concatenate(outs, axis=1)


def _gla(q, k, v, lg, z, sinit_t, og, L):
    b, s, _ = q.shape
    nc = s // L
    nb = GLA_GROUP if b % GLA_GROUP == 0 else 1
    qk = pl.BlockSpec((nb, L, GLA_HEADS * GLA_DK), lambda i, j: (i, j, 0))
    vv = pl.BlockSpec((nb, L, GLA_HEADS * GLA_DV), lambda i, j: (i, j, 0))
    st = pl.BlockSpec((nb, GLA_HEADS, GLA_DV, GLA_DK), lambda i, j: (i, 0, 0, 0))
    return pl.pallas_call(
        functools.partial(_gla_body, L=L, NB=nb),
        grid=(b // nb, nc),
        in_specs=[qk, qk, vv, qk, vv, st, _const_spec((1, GLA_DV))],
        out_specs=[vv, st],
        out_shape=[jax.ShapeDtypeStruct((b, s, GLA_HEADS * GLA_DV), BF16),
                   jax.ShapeDtypeStruct((b, GLA_HEADS, GLA_DV, GLA_DK), F32)],
        scratch_shapes=[pltpu.VMEM((nb, L, GLA_HEADS * GLA_DK), F32)],
        compiler_params=_params(("arbitrary", "arbitrary")),
        name="gla",
    )(q, k, v, lg, z, sinit_t, og)


def _post_body(*refs, n_o):
    x_ref = refs[0]
    o_refs = refs[1:1 + n_o]
    wout_ref, mg_ref, wmq_ref, mqg_ref, mk_ref, mv_ref, wmo_ref, y_ref = refs[1 + n_o:]
    tm = x_ref.shape[1]
    ts = tm // POST_SPLIT if tm % (8 * POST_SPLIT) == 0 else tm
    rows = [slice(t * ts, (t + 1) * ts) for t in range(tm // ts)]
    acc = [x_ref[0, r, :] for r in rows]
    off = 0
    for o_ref in o_refs:
        kd = o_ref.shape[-1]
        w = wout_ref[off:off + kd, :]
        acc = [a + _dot(o_ref[0, r, :], w) for a, r in zip(acc, rows)]
        off += kd
    hm = [_rms(a, mg_ref[...]).astype(BF16) for a in acc]
    qz = [_dot(h_, wmq_ref[...]) for h_ in hm]
    hw = MEM_HEADS * MEM_HD
    sls = [slice(h * MEM_HD, (h + 1) * MEM_HD) for h in range(MEM_HEADS)]
    mkb = [mk_ref[0, :, sl].astype(BF16) for sl in sls]
    mvb = [mv_ref[0, :, sl].astype(BF16) for sl in sls]
    qn = [[_rms(z[:, sl], mqg_ref[...]).astype(BF16) for sl in sls] for z in qz]
    sc = [[_dot_nt(qh, kh) for qh, kh in zip(qt, mkb)] for qt in qn]
    p = [[_softmax_rows(x * (MEM_HD ** -0.5)).astype(BF16) for x in st] for st in sc]
    oh = [[_dot(ph, vh) for ph, vh in zip(pt, mvb)] for pt in p]
    for t, r in enumerate(rows):
        outs = [oh[t][h] * _silu(qz[t][:, hw + h * MEM_HD:hw + (h + 1) * MEM_HD]) for h in range(MEM_HEADS)]
        oh[t] = jnp.concatenate(outs, axis=1).astype(BF16)
    ym = [_dot(om, wmo_ref[...]) for om in oh]
    for t, r in enumerate(rows):
        y_ref[0, r, :] = acc[t] + ym[t]


def _post(x, os_, wout, mg, wmq, mqg, mk, mv, wmo):
    b, s, d = x.shape
    tm = _row_tile(s, 512)
    hw = MEM_HEADS * MEM_HD
    nm = mk.shape[1]
    blk = lambda wd: pl.BlockSpec((1, tm, wd), lambda i, j: (i, j, 0))
    mem = pl.BlockSpec((1, nm, hw), lambda i, j: (i, 0, 0))
    kin = sum(o.shape[-1] for o in os_)
    return pl.pallas_call(
        functools.partial(_post_body, n_o=len(os_)),
        grid=(b, s // tm),
        in_specs=[blk(d)] + [blk(o.shape[-1]) for o in os_]
        + [_const_spec((kin, d)), _const_spec((1, d)), _const_spec((d, 2 * hw)), _const_spec((1, MEM_HD)),
           mem, mem, _const_spec((hw, d))],
        out_specs=blk(d),
        out_shape=jax.ShapeDtypeStruct((b, s, d), F32),
        compiler_params=_params(("arbitrary", "arbitrary")),
        name="post",
    )(x, *os_, wout, mg, wmq, mqg, mk, mv, wmo)


def _pad_cols(w, n):
    return jnp.pad(w, ((0, 0), (0, n - w.shape[1])))


def _prep_l0(w_in, conv_w, a_log, dt_bias, q_g, k_g):
    c0 = 3 * 1024
    ab = w_in[:, c0:c0 + 2 * GDN_HEADS]
    c1 = c0 + 2 * GDN_HEADS
    cols = lambda i: w_in[:, c1 + i * 1024:c1 + (i + 1) * 1024].astype(BF16)
    w = dict(
        wu=w_in[:, :c0].astype(BF16), wab=_pad_cols(ab, LANES).astype(BF16),
        wza=cols(0), wq=cols(1), wk=cols(2), wv=cols(3), wzb=cols(4),
        alog=jnp.pad(a_log, (0, LANES - GDN_HEADS)).reshape(1, LANES).astype(F32),
        dtb=jnp.pad(dt_bias, (0, LANES - GDN_HEADS)).reshape(1, LANES).astype(F32),
    )
    ns = 3 * GDN_HEADS
    cw = jnp.pad(conv_w.astype(F32), ((0, 8 - CONV_TAPS), (0, 0))).reshape(8, ns, LANES).transpose(1, 0, 2)
    dm = BAND_HEADS * BAND_HD
    head_of = np.arange(dm) // BAND_HD
    seg = jnp.asarray(head_of[:, None] == np.arange(LANES)[None, :], BF16)
    qg = jnp.tile(q_g.astype(F32), BAND_HEADS).reshape(1, dm)
    kg = jnp.tile(k_g.astype(F32), BAND_HEADS).reshape(1, dm)
    return w, cw, seg, qg, kg


def _band_bias_table(rel_bias, L, chunk):
    W = BAND_PAST + L
    n = np.arange(W + L - 1)
    idx = np.clip(BAND_PAST + (L - 1) - n, -BAND_MAX_REL, BAND_MAX_REL) + BAND_MAX_REL
    strip = rel_bias.astype(F32)[:, idx]
    bias = jnp.stack([strip[:, L - 1 - a:L - 1 - a + W] for a in range(L)], axis=1)
    back = np.arange(L)[:, None] // chunk - (np.arange(W)[None, :] - BAND_PAST) // chunk
    readable = (back >= 0) & (back <= BAND_PAST // chunk)
    return bias if readable.all() else jnp.where(jnp.asarray(readable)[None], bias, -jnp.inf)


def _prep_l1(w_in, w_gate_up, gate_bias):
    qk = GLA_HEADS * GLA_DK
    vw = GLA_HEADS * GLA_DV
    o = np.cumsum([0, qk, qk, vw, GLA_RANK, vw])
    return dict(
        wq=w_in[:, o[0]:o[1]].astype(BF16), wk=w_in[:, o[1]:o[2]].astype(BF16),
        wv=w_in[:, o[2]:o[3]].astype(BF16), wlr=_pad_cols(w_in[:, o[3]:o[4]], LANES).astype(BF16),
        wz=w_in[:, o[4]:o[5]].astype(BF16),
        wg=jnp.pad(w_gate_up, ((0, LANES - GLA_RANK), (0, 0))).astype(BF16),
        gbias=gate_bias.reshape(1, qk).astype(F32),
    )


def _layer0(x, conv_state, gdn_state, band_caches, mk, mv, norm_g, pw, cw, seg, qg, kg, bias,
            a_onorm_g, w_out, mnorm_g, w_mq, mq_g, w_mo, L):
    b, s, d = x.shape
    outs = _proj0(x.reshape(b * s, d), norm_g.reshape(1, d), pw, qg, kg, seg, s)
    u, gb, za, qlo, qhi, kn, klast, vlo, vhi, vlast, zb = (t.reshape(b, -1, t.shape[-1]) for t in outs)
    cinit = jnp.pad(conv_state.astype(F32), ((0, 0), (8 - (CONV_TAPS - 1), 0), (0, 0)))
    o_a, s_new = _gdn(u, gb, za, cinit, gdn_state.astype(F32), cw, a_onorm_g.reshape(1, LANES), L)
    o_b = _band(qlo, qhi, kn, vlo, vhi, zb, bias, band_caches, bias.shape[1])
    y = _post(x, (o_a, o_b), w_out, mnorm_g.reshape(1, d), w_mq, mq_g.reshape(1, MEM_HD), mk, mv, w_mo)
    return y, u[:, s - (CONV_TAPS - 1):, :], s_new, klast, vlast


def _layer1(x, gla_state, mk, mv, norm_g, pw, c_onorm_g, w_out, mnorm_g, w_mq, mq_g, w_mo, L):
    b, s, d = x.shape
    q, k, v, lg, z = _proj1(x.reshape(b * s, d), norm_g.reshape(1, d), pw)
    r3 = lambda t: t.reshape(b, s, t.shape[-1])
    q, k, v, lg, z = map(r3, (q, k, v, lg, z))
    st0 = jnp.swapaxes(gla_state.astype(F32), 2, 3)
    o, st = _gla(q, k, v, lg, z, st0, c_onorm_g.reshape(1, GLA_DV), L)
    y = _post(x, (o,), w_out, mnorm_g.reshape(1, d), w_mq, mq_g.reshape(1, MEM_HD), mk, mv, w_mo)
    return y, jnp.swapaxes(st, 2, 3)


def kernel(x_prompt, x_sample, mem_prompt, state_l0_gdn_conv, state_l0_gdn, cache_l0_band_k, cache_l0_band_v, cache_l0_mem_k, cache_l0_mem_v, state_l1_gla, cache_l1_mem_k, cache_l1_mem_v, l0_norm_g, l0_w_in, l0_conv_w, l0_a_log, l0_dt_bias, l0_a_onorm_g, l0_b_q_g, l0_b_k_g, l0_b_rel_bias, l0_w_out, l0_mnorm_g, l0_mem_norm_g, l0_w_mkv, l0_mk_g, l0_w_mq, l0_mq_g, l0_w_mo, l1_norm_g, l1_w_in, l1_w_gate_up, l1_gate_bias, l1_c_onorm_g, l1_w_out, l1_mnorm_g, l1_mem_norm_g, l1_w_mkv, l1_mk_g, l1_w_mq, l1_mq_g, l1_w_mo):
    bp, sp, d = x_prompt.shape
    bs, ss, _ = x_sample.shape
    nm = mem_prompt.shape[1]
    hw = MEM_HEADS * MEM_HD
    dm = BAND_HEADS * BAND_HD
    assert sp % CHUNK_ == 0 and ss % INV_SUB == 0 and ss <= CHUNK_
    assert cache_l0_band_k.shape[1] == BAND_PAST

    pw0, cw, seg, qg, kg = _prep_l0(l0_w_in, l0_conv_w, l0_a_log, l0_dt_bias, l0_b_q_g, l0_b_k_g)
    pw1 = _prep_l1(l1_w_in, l1_w_gate_up, l1_gate_bias)
    band_rows = BAND_STEP_CHUNKS * CHUNK_ if sp % (BAND_STEP_CHUNKS * CHUNK_) == 0 else CHUNK_
    bias_p = _band_bias_table(l0_b_rel_bias, band_rows, CHUNK_)
    bias_s = _band_bias_table(l0_b_rel_bias, ss, ss)
    bf = lambda w: w.astype(BF16)
    mem2 = mem_prompt.reshape(bp * nm, d)

    mk0, mv0 = _memkv(mem2, l0_mem_norm_g.reshape(1, d), bf(l0_w_mkv), l0_mk_g.reshape(1, MEM_HD))
    mk0 = mk0.reshape(bp, nm, hw)
    mv0 = mv0.reshape(bp, nm, hw)
    l0_shared = (l0_norm_g, pw0, cw, seg, qg, kg)
    l0_tail = (l0_a_onorm_g, bf(l0_w_out), l0_mnorm_g, bf(l0_w_mq), l0_mq_g, bf(l0_w_mo))
    zeros_conv = jnp.zeros((bp, CONV_TAPS - 1, 3 * GDN_HEADS * LANES), F32)
    zeros_gdn = jnp.zeros((bp, GDN_HEADS, GDN_DK, LANES), F32)
    yp, p_conv, p_gdn, p_kn, p_v = _layer0(
        x_prompt, zeros_conv, zeros_gdn, None, mk0, mv0,
        *l0_shared, bias_p, *l0_tail, CHUNK_)
    ys, s_conv, s_gdn, s_kn, s_v = _layer0(
        x_sample, state_l0_gdn_conv, state_l0_gdn,
        (cache_l0_band_k.reshape(bs, BAND_PAST, dm), cache_l0_band_v.reshape(bs, BAND_PAST, dm)),
        cache_l0_mem_k.reshape(bs, nm, hw), cache_l0_mem_v.reshape(bs, nm, hw),
        *l0_shared, bias_s, *l0_tail, ss)

    mk1, mv1 = _memkv(mem2, l1_mem_norm_g.reshape(1, d), bf(l1_w_mkv), l1_mk_g.reshape(1, MEM_HD))
    mk1 = mk1.reshape(bp, nm, hw)
    mv1 = mv1.reshape(bp, nm, hw)
    l1_tail = (l1_c_onorm_g, bf(l1_w_out), l1_mnorm_g, bf(l1_w_mq), l1_mq_g, bf(l1_w_mo))
    zeros_gla = jnp.zeros((bp, GLA_HEADS, GLA_DK, GLA_DV), F32)
    yp, p_gla = _layer1(yp, zeros_gla, mk1, mv1, l1_norm_g, pw1, *l1_tail, CHUNK_)
    ys, s_gla = _layer1(ys, state_l1_gla, cache_l1_mem_k.reshape(bs, nm, hw),
                        cache_l1_mem_v.reshape(bs, nm, hw), l1_norm_g, pw1, *l1_tail, ss)

    h4 = lambda t: t.reshape(t.shape[0], t.shape[1], BAND_HEADS, BAND_HD)
    m4 = lambda t: t.reshape(bp, nm, MEM_HEADS, MEM_HD)
    return (yp, ys, p_conv, p_gdn, h4(p_kn), h4(p_v),
            m4(mk0), m4(mv0), p_gla, m4(mk1), m4(mv1),
            s_conv, s_gdn, h4(s_kn), h4(s_v), s_gla)
```

```python
import functools

import jax
import jax.numpy as jnp
import numpy as np
from jax import lax
from jax.experimental import pallas as pl
from jax.experimental.pallas import tpu as pltpu

F32 = jnp.float32
BF16 = jnp.bfloat16
NORM_EPS = 1e-6
LOG2_E = 1.4426950408889634

D_MODEL_ = 1024
CHUNK_ = 64
CONV_TAPS = 4
GDN_HEADS = 8
GDN_DK = 128
BAND_HEADS = 16
BAND_HD = 64
BAND_PAST = 512
BAND_MAX_REL = 128
GLA_HEADS = 8
GLA_DK = 128
GLA_DV = 256
GLA_RANK = 16
GLA_TAU = 16.0
MEM_HEADS = 4
MEM_HD = 128
INV_SUB = 16
GDN_GROUP = 2
GDN_STEP_CHUNKS = 1
GLA_GROUP = 2
POST_SPLIT = 4
PROJ_SPLIT = 2
BAND_STEP_CHUNKS = 2
LANES = 128
VMEM_LIMIT = 56 * 1024 * 1024


def _dot(a, b):
    return jnp.dot(a, b, preferred_element_type=F32)


def _dot_nt(a, b):
    return lax.dot_general(a, b, (((1,), (1,)), ((), ())), preferred_element_type=F32)


def _dot_tn(a, b):
    return lax.dot_general(a, b, (((0,), (0,)), ((), ())), preferred_element_type=F32)


def _split2(x):
    hi = x.astype(BF16)
    lo = (x - hi.astype(F32)).astype(BF16)
    return hi, lo


def _split3(x):
    hi = x.astype(BF16)
    r = x - hi.astype(F32)
    mid = r.astype(BF16)
    lo = (r - mid.astype(F32)).astype(BF16)
    return hi, mid, lo


def _dot_x3(a, b):
    ah, al = _split2(a)
    bh, bl = _split2(b)
    return _dot(ah, bh) + (_dot(ah, bl) + _dot(al, bh))


def _dot_exact_lhs(a_bf, b):
    h, m, l = _split3(b)
    return _dot(a_bf, h) + (_dot(a_bf, m) + _dot(a_bf, l))


def _rms(x, g):
    ms = jnp.mean(x * x, axis=-1, keepdims=True)
    return x * lax.rsqrt(ms + NORM_EPS) * g


def _silu(x):
    return x * jax.nn.sigmoid(x)


def _softplus(x):
    return jnp.maximum(x, 0.0) + jnp.log1p(jnp.exp(-jnp.abs(x)))


def _softmax_rows(s):
    m = jnp.max(s, axis=-1, keepdims=True)
    e = jnp.exp(s - m)
    return e * (1.0 / jnp.sum(e, axis=-1, keepdims=True))


def _const_spec(shape):
    nd = len(shape)
    return pl.BlockSpec(shape, lambda *_: (0,) * nd, pipeline_mode=pl.Buffered(1))


def _params(sem):
    return pltpu.CompilerParams(dimension_semantics=sem, vmem_limit_bytes=VMEM_LIMIT)


def _row_tile(n, want):
    t = min(n, want)
    assert n % t == 0
    return t


def _proj0_body(x_ref, g_ref, wu_ref, wab_ref, wza_ref, wq_ref, wk_ref, wv_ref, wzb_ref,
                alog_ref, dtb_ref, qg_ref, kg_ref, seg_ref,
                u_ref, gb_ref, za_ref, qlo_ref, qhi_ref, kn_ref, klast_ref, vlo_ref, vhi_ref, vlast_ref, zb_ref,
                *, kept_tiles):
    h = _rms(x_ref[...], g_ref[...]).astype(BF16)
    dm = BAND_HEADS * BAND_HD
    lo = (lax.broadcasted_iota(jnp.int32, (1, dm), 1) & (LANES - 1)) < BAND_HD
    seg = seg_ref[...]
    q = _dot(h, wq_ref[...])
    k = _dot(h, wk_ref[...])
    ab = _dot(h, wab_ref[...])
    za = _dot(h, wza_ref[...])

    def head_rsqrt(x):
        ss = _dot((x * x).astype(BF16), seg)
        return lax.rsqrt(ss * (1.0 / BAND_HD) + NORM_EPS)

    rq = head_rsqrt(q)
    rk = head_rsqrt(k)
    zb = _dot(h, wzb_ref[...])
    v = _dot(h, wv_ref[...])
    u_ref[...] = _dot(h, wu_ref[...])
    za_ref[...] = _silu(za)
    zb_ref[...] = _silu(zb)
    lo1 = lo[:, :LANES]

    def spread(r):
        return jnp.concatenate([jnp.where(lo1, r[:, 2 * s:2 * s + 1], r[:, 2 * s + 1:2 * s + 2])
                                for s in range(BAND_HEADS // 2)], axis=1)

    qn = q * spread(rq) * qg_ref[...] * (BAND_HD ** -0.5)
    kn = k * spread(rk) * kg_ref[...]
    qlo_ref[...] = jnp.where(lo, qn, 0.0).astype(BF16)
    qhi_ref[...] = jnp.where(lo, 0.0, qn).astype(BF16)
    kn_ref[...] = kn.astype(BF16)
    vlo_ref[...] = jnp.where(lo, v, 0.0).astype(BF16)
    vhi_ref[...] = jnp.where(lo, 0.0, v).astype(BF16)

    def hand_on():
        klast_ref[...] = kn.reshape(kn.shape[0], BAND_HEADS, BAND_HD)
        vlast_ref[...] = v.reshape(v.shape[0], BAND_HEADS, BAND_HD)

    if kept_tiles is None:
        hand_on()
    else:
        tpb, kt = kept_tiles
        pl.when(pl.program_id(0) % tpb >= tpb - kt)(hand_on)
    lane = lax.broadcasted_iota(jnp.int32, ab.shape, 1)
    gval = -jnp.exp(alog_ref[...]) * _softplus(ab + dtb_ref[...])
    gb_ref[...] = jnp.where(lane < GDN_HEADS, gval, jax.nn.sigmoid(ab))


def _proj0(x2, g, w, qg, kg, seg, rows_per_batch):
    n, d = x2.shape
    tm = _row_tile(n, 256)
    dm = BAND_HEADS * BAND_HD
    widths = (3 * 1024, LANES, 1024, dm, dm, dm, dm)
    row = lambda wd: pl.BlockSpec((tm, wd), lambda i: (i, 0))
    keep = min(BAND_PAST, rows_per_batch)
    if rows_per_batch > keep:
        assert rows_per_batch % tm == 0 and keep % tm == 0
        tpb, kt = rows_per_batch // tm, keep // tm
        last = pl.BlockSpec((tm, BAND_HEADS, BAND_HD),
                            lambda i: ((i // tpb) * kt + jnp.maximum(i % tpb - (tpb - kt), 0), 0, 0))
        n_last = (n // rows_per_batch) * keep
        kept_tiles = (tpb, kt)
    else:
        last, n_last = pl.BlockSpec((tm, BAND_HEADS, BAND_HD), lambda i: (i, 0, 0)), n
        kept_tiles = None
    last_shape = jax.ShapeDtypeStruct((n_last, BAND_HEADS, BAND_HD), F32)
    f32 = lambda rows, wd: jax.ShapeDtypeStruct((rows, wd), F32)
    bf16 = lambda wd: jax.ShapeDtypeStruct((n, wd), BF16)
    return pl.pallas_call(
        functools.partial(_proj0_body, kept_tiles=kept_tiles),
        grid=(n // tm,),
        in_specs=[row(d), _const_spec((1, d))]
        + [_const_spec((d, wd)) for wd in widths]
        + [_const_spec((1, LANES)), _const_spec((1, LANES)), _const_spec((1, dm)), _const_spec((1, dm)),
           _const_spec((dm, LANES))],
        out_specs=[row(3 * 1024), row(LANES), row(1024), row(dm), row(dm), row(dm), last, row(dm), row(dm), last,
                   row(dm)],
        out_shape=[f32(n, 3 * 1024), f32(n, LANES), f32(n, 1024), bf16(dm), bf16(dm), bf16(dm), last_shape,
                   bf16(dm), bf16(dm), last_shape, f32(n, dm)],
        compiler_params=_params(("arbitrary",)),
        name="proj0",
    )(x2, g, w["wu"], w["wab"], w["wza"], w["wq"], w["wk"], w["wv"], w["wzb"], w["alog"], w["dtb"],
      qg, kg, seg)


def _proj1_body(x_ref, g_ref, wq_ref, wk_ref, wv_ref, wlr_ref, wz_ref, wg_ref, gbias_ref,
                q_ref, k_ref, v_ref, lg_ref, z_ref):
    tm = x_ref.shape[0]
    ts = tm // PROJ_SPLIT if tm % (16 * PROJ_SPLIT) == 0 else tm
    rows = [slice(t * ts, (t + 1) * ts) for t in range(tm // ts)]
    h = [_rms(x_ref[r, :], g_ref[...]).astype(BF16) for r in rows]
    lr = [_dot(h_, wlr_ref[...]) for h_ in h]
    for r, h_ in zip(rows, h):
        q_ref[r, :] = _dot(h_, wq_ref[...])
    pre = [_dot(x.astype(BF16), wg_ref[...]) + gbias_ref[...] for x in lr]
    zs = [_dot(h_, wz_ref[...]) for h_ in h]
    for r, p_ in zip(rows, pre):
        lg_ref[r, :] = -_softplus(-p_) * (1.0 / GLA_TAU)
    for r, h_ in zip(rows, h):
        k_ref[r, :] = _dot(h_, wk_ref[...])
    for r, z_ in zip(rows, zs):
        z_ref[r, :] = _silu(z_)
    for r, h_ in zip(rows, h):
        v_ref[r, :] = _dot(h_, wv_ref[...])


def _proj1(x2, g, w):
    n, d = x2.shape
    tm = _row_tile(n, 512)
    row = lambda wd: pl.BlockSpec((tm, wd), lambda i: (i, 0))
    outw = (1024, 1024, 2048, 1024, 2048)
    return pl.pallas_call(
        _proj1_body,
        grid=(n // tm,),
        in_specs=[row(d), _const_spec((1, d)), _const_spec((d, 1024)), _const_spec((d, 1024)),
                  _const_spec((d, 2048)), _const_spec((d, LANES)), _const_spec((d, 2048)),
                  _const_spec((LANES, 1024)), _const_spec((1, 1024))],
        out_specs=[row(wd) for wd in outw],
        out_shape=[jax.ShapeDtypeStruct((n, wd), F32) for wd in outw],
        compiler_params=_params(("arbitrary",)),
        name="proj1",
    )(x2, g, w["wq"], w["wk"], w["wv"], w["wlr"], w["wz"], w["wg"], w["gbias"])


def _memkv_body(m_ref, g_ref, w_ref, kg_ref, k4_ref, v4_ref, kb_ref, vb_ref):
    h = _rms(m_ref[...], g_ref[...]).astype(BF16)
    kv = _dot(h, w_ref[...])
    hw = MEM_HEADS * MEM_HD
    k = jnp.concatenate([_rms(kv[:, hh * MEM_HD:(hh + 1) * MEM_HD], kg_ref[...]) for hh in range(MEM_HEADS)],
                        axis=1)
    v = kv[:, hw:]
    k4_ref[...] = k.reshape(k.shape[0], MEM_HEADS, MEM_HD)
    v4_ref[...] = v.reshape(v.shape[0], MEM_HEADS, MEM_HD)
    kb_ref[...] = k.astype(BF16)
    vb_ref[...] = v.astype(BF16)


def _memkv(m2, g, w_bf, kg):
    n, d = m2.shape
    tm = _row_tile(n, 256)
    hw = MEM_HEADS * MEM_HD
    row = lambda wd: pl.BlockSpec((tm, wd), lambda i: (i, 0))
    row4 = pl.BlockSpec((tm, MEM_HEADS, MEM_HD), lambda i: (i, 0, 0))
    return pl.pallas_call(
        _memkv_body,
        grid=(n // tm,),
        in_specs=[row(d), _const_spec((1, d)), _const_spec((d, 2 * hw)), _const_spec((1, MEM_HD))],
        out_specs=[row4, row4, row(hw), row(hw)],
        out_shape=[jax.ShapeDtypeStruct((n, MEM_HEADS, MEM_HD), F32)] * 2
        + [jax.ShapeDtypeStruct((n, hw), BF16)] * 2,
        compiler_params=_params(("arbitrary",)),
        name="memkv",
    )(m2, g, w_bf, kg)


def _each(fn, *lists):
    return [fn(*xs) for xs in zip(*lists)]


def _unit_lower_inverse(a, eye, bd, mm):
    d = _each(lambda x: jnp.where(bd, x, 0.0), a)
    nl = _each(lambda x, y: x - y, a, d)
    d2 = _each(mm, d, d)
    d4 = _each(mm, d2, d2)
    td = _each(lambda x, y: mm(eye - x, eye + y), d, d2)
    d8 = _each(mm, d4, d4)
    td = _each(lambda x, y: mm(x, eye + y), td, d4)
    td = _each(lambda x, y: mm(x, eye + y), td, d8)
    m = _each(mm, td, nl)
    m2 = _each(mm, m, m)
    mt = _each(mm, m, td)
    return _each(lambda x, y, z: mm(eye + x, y - z), m2, td, mt)


def _gdn_body(u_ref, gb_ref, za_ref, cinit_ref, sinit_ref, cw_ref, og_ref,
              o_ref, s_ref, ubuf, *, L, NB, G):
    c = pl.program_id(1)
    ns = 3 * GDN_HEADS
    hist = 8
    R = G * L

    @pl.when(c == 0)
    def _():
        for b in range(NB):
            for j in range(ns):
                ubuf[b * ns + j, 0:hist, :] = cinit_ref[b, :, j * LANES:(j + 1) * LANES]
        s_ref[...] = sinit_ref[...]

    for b in range(NB):
        for j in range(ns):
            ubuf[b * ns + j, hist:hist + R, :] = u_ref[b, :, j * LANES:(j + 1) * LANES]
    base = hist - (CONV_TAPS - 1)
    ys = []
    for b in range(NB):
        bsl = slice(b * ns, (b + 1) * ns)
        yb = ubuf[bsl, base:base + R, :] * cw_ref[:, 0:1, :]
        for i in range(1, CONV_TAPS):
            yb = yb + ubuf[bsl, base + i:base + i + R, :] * cw_ref[:, i:i + 1, :]
        ys.append(_silu(yb))
    ubuf[:, base:hist, :] = ubuf[:, base + R:hist + R, :]

    row = lax.broadcasted_iota(jnp.int32, (L, 2 * L), 0)
    lane = lax.broadcasted_iota(jnp.int32, (L, 2 * L), 1)
    col = lane & (L - 1)
    left = lane < L
    incl = row >= col
    strict = row > col
    sub_shift = INV_SUB.bit_length() - 1
    bd = lax.shift_right_logical(row, sub_shift) == lax.shift_right_logical(col, sub_shift)
    eye = jnp.where(row == col, 1.0, 0.0).astype(F32)
    trow = lax.broadcasted_iota(jnp.int32, (L, L), 0)
    tcol = lax.broadcasted_iota(jnp.int32, (L, L), 1)
    tri = jnp.where(trow >= tcol, 1.0, 0.0).astype(BF16)
    og = og_ref[...]
    heads = [(b, g, h) for b in range(NB) for g in range(G) for h in range(GDN_HEADS)]
    pairs = range(len(heads) // 2)
    rows_of = lambda g: slice(g * L, (g + 1) * L)
    l2n = lambda x: x * lax.rsqrt(jnp.sum(x * x, axis=-1, keepdims=True) + NORM_EPS)
    q = [l2n(ys[b][h][rows_of(g)]) * (GDN_DK ** -0.5) for b, g, h in heads]
    k = [l2n(ys[b][GDN_HEADS + h][rows_of(g)]) for b, g, h in heads]
    v = [ys[b][2 * GDN_HEADS + h][rows_of(g)] for b, g, h in heads]
    gc, gr, bc = [], [], []
    for b in range(NB):
        for g in range(G):
            gbv = gb_ref[b, rows_of(g), :]
            gcum = _dot_exact_lhs(tri, gbv)
            gpad = jnp.concatenate([gcum, jnp.zeros((LANES - L, LANES), F32)], axis=0)
            gt = gpad.T
            for h in range(GDN_HEADS):
                gc.append(gcum[:, h:h + 1])
                gr.append(gt[h:h + 1, 0:L])
                bc.append(gbv[:, GDN_HEADS + h:GDN_HEADS + h + 1])
    side = lambda x0, x1: jnp.where(left, x0, x1)
    gcp = [side(gc[2 * p], gc[2 * p + 1]) for p in pairs]
    grp = [jnp.concatenate([gr[2 * p], gr[2 * p + 1]], axis=1) for p in pairs]
    bcp = [side(bc[2 * p], bc[2 * p + 1]) for p in pairs]
    dec = _each(lambda c_, r_: jnp.where(incl, jnp.exp(jnp.where(incl, c_ - r_, 0.0)), 0.0), gcp, grp)
    kb = _each(lambda x: x.astype(BF16), k)
    qb = _each(lambda x: x.astype(BF16), q)

    def blockdiag(y0, y1):
        z0 = jnp.zeros(y1.shape, y1.dtype)
        z1 = jnp.zeros(y0.shape, y0.dtype)
        return jnp.concatenate([jnp.concatenate([y0, z0], axis=1), jnp.concatenate([z1, y1], axis=1)], axis=0)

    def mm(x, y):
        yb = y.astype(BF16)
        zero = jnp.zeros_like(yb)
        return _dot(x.astype(BF16), jnp.concatenate([jnp.where(left, yb, zero), jnp.where(left, zero, yb)], axis=0))

    kq = [_dot_nt(jnp.concatenate([jnp.concatenate([kb[2 * p], qb[2 * p]], axis=0),
                                   jnp.concatenate([kb[2 * p + 1], qb[2 * p + 1]], axis=0)], axis=1),
                  blockdiag(kb[2 * p], kb[2 * p + 1])) for p in pairs]
    a = _each(lambda b_, x, d_: jnp.where(strict, b_ * x[:L] * d_, 0.0), bcp, kq, dec)
    t = _unit_lower_inverse(a, eye, bd, mm)
    eg = _each(jnp.exp, gc)
    rhs = _each(lambda b_, v_, e_, k_: jnp.concatenate([b_ * v_, (b_ * e_) * k_], axis=1).astype(BF16),
                bc, v, eg, k)
    sol = [_dot(t[p].astype(BF16), blockdiag(rhs[2 * p], rhs[2 * p + 1])) for p in pairs]
    solk = lambda i: sol[i // 2][:, 2 * (i % 2) * LANES + LANES:2 * (i % 2 + 1) * LANES]
    solv = lambda i: sol[i // 2][:, 2 * (i % 2) * LANES:2 * (i % 2) * LANES + LANES]
    qkd = [(kq[p][L:] * dec[p]).astype(BF16) for p in pairs]
    gl = [c_[L - 1:L, :] for c_ in gc]
    kd = _each(lambda k_, l_, c_: (k_ * jnp.exp(l_ - c_)).astype(BF16), k, gl, gc)

    item = lambda b, g, h: (b * G + g) * GDN_HEADS + h
    half = lambda x, i: x[:, (i % 2) * LANES:(i % 2 + 1) * LANES]
    s = {(b, h): s_ref[b, h] for b in range(NB) for h in range(GDN_HEADS)}
    for g in range(G):
        ids = [item(b, g, h) for b in range(NB) for h in range(GDN_HEADS)]
        sb = {i: s[(heads[i][0], heads[i][2])].astype(BF16) for i in ids}
        ksq = {i: _dot(jnp.concatenate([jnp.concatenate([solk(i).astype(BF16), qb[i]], axis=0),
                                        jnp.concatenate([solk(i + 1).astype(BF16), qb[i + 1]], axis=0)], axis=1),
                       blockdiag(sb[i], sb[i + 1])) for i in ids[::2]}
        ub = {i: (solv(i) - half(ksq[i - i % 2][:L], i)).astype(BF16) for i in ids}
        qku = {i: _dot(qkd[i // 2], blockdiag(ub[i], ub[i + 1])) for i in ids[::2]}
        ktu = {i: _dot_tn(kd[i], ub[i]) for i in ids}
        for i in ids:
            b, _, h = heads[i]
            s[(b, h)] = s[(b, h)] * jnp.exp(gl[i]) + ktu[i]
        for b in range(NB):
            za = za_ref[b, rows_of(g), :]
            outs = [(_rms(half(ksq[i - i % 2][L:], i) * eg[i] + half(qku[i - i % 2], i), og)
                     * za[:, heads[i][2] * LANES:(heads[i][2] + 1) * LANES]).astype(BF16)
                    for i in ids if heads[i][0] == b]
            o_ref[b, rows_of(g), :] = jnp.concatenate(outs, axis=1)
    for (b, h), val in s.items():
        s_ref[b, h] = val


def _gdn(u, gb, za, cinit, sinit, cw, og, L):
    b, s, _ = u.shape
    nc = s // L
    ns = 3 * GDN_HEADS
    nb = GDN_GROUP if b % GDN_GROUP == 0 else 1
    g = GDN_STEP_CHUNKS if nc % GDN_STEP_CHUNKS == 0 else 1
    blk = lambda wd: pl.BlockSpec((nb, g * L, wd), lambda i, j: (i, j, 0))
    return pl.pallas_call(
        functools.partial(_gdn_body, L=L, NB=nb, G=g),
        grid=(b // nb, nc // g),
        in_specs=[blk(ns * LANES), blk(LANES), blk(GDN_HEADS * LANES),
                  pl.BlockSpec((nb, 8, ns * LANES), lambda i, j: (i, 0, 0)),
                  pl.BlockSpec((nb, GDN_HEADS, GDN_DK, LANES), lambda i, j: (i, 0, 0, 0)),
                  _const_spec((ns, 8, LANES)), _const_spec((1, LANES))],
        out_specs=[blk(GDN_HEADS * LANES),
                   pl.BlockSpec((nb, GDN_HEADS, GDN_DK, LANES), lambda i, j: (i, 0, 0, 0))],
        out_shape=[jax.ShapeDtypeStruct((b, s, GDN_HEADS * LANES), BF16),
                   jax.ShapeDtypeStruct((b, GDN_HEADS, GDN_DK, LANES), F32)],
        scratch_shapes=[pltpu.VMEM((nb * ns, 8 + g * L, LANES), F32)],
        compiler_params=_params(("arbitrary", "arbitrary")),
        name="gdn",
    )(u, gb, za, cinit, sinit, cw, og)


def _band_body(*refs, L, W, has_cache):
    if has_cache:
        (qlo_ref, qhi_ref, k_ref, vlo_ref, vhi_ref, zb_ref, bias_ref, ck_ref, cv_ref,
         o_ref, kscr, vlo, vhi) = refs
    else:
        (qlo_ref, qhi_ref, k_ref, vlo_ref, vhi_ref, zb_ref, bias_ref,
         o_ref, kscr, vlo, vhi, biasm) = refs
    c = pl.program_id(1)
    dm = BAND_HEADS * BAND_HD
    pad_chunks = BAND_PAST // L

    @pl.when(c == 0)
    def _():
        if has_cache:
            lo = (lax.broadcasted_iota(jnp.int32, (1, dm), 1) & (LANES - 1)) < BAND_HD
            kscr[0:BAND_PAST, :] = ck_ref[0].astype(BF16)
            cv = cv_ref[0]
            vlo[0:BAND_PAST, :] = jnp.where(lo, cv, 0.0).astype(BF16)
            vhi[0:BAND_PAST, :] = jnp.where(lo, 0.0, cv).astype(BF16)
        else:
            zero = jnp.zeros((BAND_PAST, dm), BF16)
            kscr[0:BAND_PAST, :] = zero
            vlo[0:BAND_PAST, :] = zero
            vhi[0:BAND_PAST, :] = zero

    new0 = pl.multiple_of(BAND_PAST + c * L, L)
    kscr[pl.ds(new0, L), :] = k_ref[0]
    vlo[pl.ds(new0, L), :] = vlo_ref[0]
    vhi[pl.ds(new0, L), :] = vhi_ref[0]

    if has_cache:
        bias_src = bias_ref
    else:
        @pl.when(c < pad_chunks)
        def _():
            wcol = lax.broadcasted_iota(jnp.int32, (1, W), 1)
            valid = (wcol + c * L) >= BAND_PAST
            for h in range(BAND_HEADS):
                biasm[h] = jnp.where(valid, bias_ref[h], -jnp.inf)

        @pl.when(c == pad_chunks)
        def _():
            biasm[...] = bias_ref[...]

        bias_src = biasm

    w0 = pl.multiple_of(c * L, L)
    zb = zb_ref[0]
    slabs = [slice(s * LANES, (s + 1) * LANES) for s in range(BAND_HEADS // 2)]
    sc = []
    for sl in slabs:
        ks = kscr[pl.ds(w0, W), sl]
        sc.append(_dot_nt(qlo_ref[0, :, sl], ks))
        sc.append(_dot_nt(qhi_ref[0, :, sl], ks))
    ps, rs = [], []
    for h in range(BAND_HEADS):
        x = sc[h] + bias_src[h]
        e = jnp.exp(x - jnp.max(x, axis=-1, keepdims=True))
        rs.append(1.0 / jnp.sum(e, axis=-1, keepdims=True))
        ps.append(e.astype(BF16))
    pv = []
    for i, sl in enumerate(slabs):
        pv.append(_dot(ps[2 * i], vlo[pl.ds(w0, W), sl]))
        pv.append(_dot(ps[2 * i + 1], vhi[pl.ds(w0, W), sl]))
    outs = [((pv[2 * i] * rs[2 * i] + pv[2 * i + 1] * rs[2 * i + 1]) * zb[:, sl]).astype(BF16)
            for i, sl in enumerate(slabs)]
    o_ref[0] = jnp.concatenate(outs, axis=1)


def _band(qlo, qhi, kn, vlo, vhi, zb, bias, caches, L):
    b, s, dm = qlo.shape
    nc = s // L
    W = BAND_PAST + L
    blk = pl.BlockSpec((1, L, dm), lambda i, j: (i, j, 0))
    cache = pl.BlockSpec((1, BAND_PAST, dm), lambda i, j: (i, 0, 0))
    has_cache = caches is not None
    scratch = [pltpu.VMEM((BAND_PAST + s, dm), BF16)] * 3
    if not has_cache:
        scratch = scratch + [pltpu.VMEM((BAND_HEADS, L, W), F32)]
    return pl.pallas_call(
        functools.partial(_band_body, L=L, W=W, has_cache=has_cache),
        grid=(b, nc),
        in_specs=[blk] * 6 + [_const_spec((BAND_HEADS, L, W))] + ([cache, cache] if has_cache else []),
        out_specs=blk,
        out_shape=jax.ShapeDtypeStruct((b, s, dm), BF16),
        scratch_shapes=scratch,
        compiler_params=_params(("arbitrary", "arbitrary")),
        name="band",
    )(qlo, qhi, kn, vlo, vhi, zb, bias, *(caches if has_cache else ()))


def _gla_body(q_ref, k_ref, v_ref, lg_ref, z_ref, sinit_ref, og_ref, o_ref, st_ref, cbs, *, L, NB):
    c = pl.program_id(1)

    @pl.when(c == 0)
    def _():
        st_ref[...] = sinit_ref[...]

    levels = [L >> (t + 1) for t in range(L.bit_length() - 1)]
    small = [s for s in levels if 2 * s < 16]
    row = lax.broadcasted_iota(jnp.int32, (L, L), 0)
    col = lax.broadcasted_iota(jnp.int32, (L, L), 1)
    anchor = lambda s: lax.shift_left(lax.shift_right_logical(row, s.bit_length()), s.bit_length()) + (s - 1)
    between = lambda s: (col > jnp.minimum(row, anchor(s))) & (col <= jnp.maximum(row, anchor(s)))
    onehot = lambda m: jnp.where(m, 1.0, 0.0).astype(BF16)
    tri = onehot(row >= col)
    stack = jnp.concatenate([onehot(between(s)) for s in small], axis=0)
    ex = jnp.exp2
    og = og_ref[...]
    scale = GLA_DK ** -0.5
    pieces = [_split3(lg_ref[b] * LOG2_E) for b in range(NB)]
    cb_all = [_dot(tri, p[0]) + (_dot(tri, p[1]) + _dot(tri, p[2])) for p in pieces]
    y_small = [_dot(stack, p[0]) + _dot(stack, p[1]) for p in pieces]
    ydec = [dict() for _ in range(NB)]
    for b in range(NB):
        cbs[b] = cb_all[b]
        for t, s in enumerate(small):
            ydec[b][s] = y_small[b][t * L:(t + 1) * L]
        for s in levels:
            if s not in ydec[b]:
                anc = jnp.concatenate(
                    [jnp.broadcast_to(cbs[b, p * 2 * s + s - 1:p * 2 * s + s, :], (2 * s, cbs.shape[2]))
                     for p in range(L // (2 * s))], axis=0)
                ydec[b][s] = -jnp.abs(cb_all[b] - anc)

    heads = [(b, h) for b in range(NB) for h in range(GLA_HEADS)]
    pairs = range(len(heads) // 2)
    ksl = lambda h: slice(h * GLA_DK, (h + 1) * GLA_DK)
    vsl = lambda h: slice(h * GLA_DV, (h + 1) * GLA_DV)
    cb = [cb_all[b][:, ksl(h)] for b, h in heads]
    q = [q_ref[b, :, ksl(h)] * scale for b, h in heads]
    k = [k_ref[b, :, ksl(h)] for b, h in heads]
    vb = [v_ref[b, :, vsl(h)].astype(BF16) for b, h in heads]
    st = [st_ref[b, h] for b, h in heads]
    o_inter = _each(lambda q_, c_, s_: _dot_nt((q_ * ex(c_)).astype(BF16), s_.astype(BF16)), q, cb, st)
    cl = [c_[L - 1:L] for c_ in cb]
    ke = _each(lambda k_, l_, c_: (k_ * ex(l_ - c_)).astype(BF16), k, cl, cb)
    vtk = _each(_dot_tn, vb, ke)

    zk = jnp.zeros((L, GLA_DK), BF16)
    zv = jnp.zeros((L, GLA_DV), BF16)

    def side_by_side(qs, ks):
        lhs = jnp.concatenate(qs, axis=1).astype(BF16)
        rhs = jnp.concatenate([jnp.concatenate([ks[0].astype(BF16), zk], axis=1),
                               jnp.concatenate([zk, ks[1].astype(BF16)], axis=1)], axis=0)
        return _dot_nt(lhs, rhs)

    prow = lax.broadcasted_iota(jnp.int32, (L, 2 * L), 0)
    pcol = lax.broadcasted_iota(jnp.int32, (L, 2 * L), 1) & (L - 1)
    att = [jnp.where(prow == pcol, side_by_side((q[2 * p], q[2 * p + 1]), (k[2 * p], k[2 * p + 1])), 0.0)
           for p in pairs]
    for s in levels:
        sh = s.bit_length() - 1
        same_parent = lax.shift_right_logical(prow, sh + 1) == lax.shift_right_logical(pcol, sh + 1)
        take = same_parent & ((lax.shift_right_logical(prow, sh) & 1) == 1) & \
            ((lax.shift_right_logical(pcol, sh) & 1) == 0)
        f = [ex(ydec[b][s][:, ksl(h)]) for b, h in heads]
        prod = [side_by_side((q[2 * p] * f[2 * p], q[2 * p + 1] * f[2 * p + 1]),
                             (k[2 * p] * f[2 * p], k[2 * p + 1] * f[2 * p + 1])) for p in pairs]
        att = [jnp.where(take, prod[p], att[p]) for p in pairs]
    vpair = [jnp.concatenate([jnp.concatenate([vb[2 * p], zv], axis=1),
                              jnp.concatenate([zv, vb[2 * p + 1]], axis=1)], axis=0) for p in pairs]
    o_intra = [_dot(att[p].astype(BF16), vpair[p]) for p in pairs]
    for i, (b, h) in enumerate(heads):
        st_ref[b, h] = st[i] * ex(cl[i]) + vtk[i]
    for b in range(NB):
        z = z_ref[b]
        outs = [(_rms(o_inter[i] + o_intra[i // 2][:, (i % 2) * GLA_DV:(i % 2 + 1) * GLA_DV], og)
                 * z[:, vsl(h)]).astype(BF16) for i, (b_, h) in enumerate(heads) if b_ == b]
        o_ref[b] = jnp.concatenate(outs, axis=1)


def _gla(q, k, v, lg, z, sinit_t, og, L):
    b, s, _ = q.shape
    nc = s // L
    nb = GLA_GROUP if b % GLA_GROUP == 0 else 1
    qk = pl.BlockSpec((nb, L, GLA_HEADS * GLA_DK), lambda i, j: (i, j, 0))
    vv = pl.BlockSpec((nb, L, GLA_HEADS * GLA_DV), lambda i, j: (i, j, 0))
    st = pl.BlockSpec((nb, GLA_HEADS, GLA_DV, GLA_DK), lambda i, j: (i, 0, 0, 0))
    return pl.pallas_call(
        functools.partial(_gla_body, L=L, NB=nb),
        grid=(b // nb, nc),
        in_specs=[qk, qk, vv, qk, vv, st, _const_spec((1, GLA_DV))],
        out_specs=[vv, st],
        out_shape=[jax.ShapeDtypeStruct((b, s, GLA_HEADS * GLA_DV), BF16),
                   jax.ShapeDtypeStruct((b, GLA_HEADS, GLA_DV, GLA_DK), F32)],
        scratch_shapes=[pltpu.VMEM((nb, L, GLA_HEADS * GLA_DK), F32)],
        compiler_params=_params(("arbitrary", "arbitrary")),
        name="gla",
    )(q, k, v, lg, z, sinit_t, og)


def _post_body(*refs, n_o):
    x_ref = refs[0]
    o_refs = refs[1:1 + n_o]
    wout_ref, mg_ref, wmq_ref, mqg_ref, mk_ref, mv_ref, wmo_ref, y_ref = refs[1 + n_o:]
    tm = x_ref.shape[1]
    ts = tm // POST_SPLIT if tm % (8 * POST_SPLIT) == 0 else tm
    rows = [slice(t * ts, (t + 1) * ts) for t in range(tm // ts)]
    acc = [x_ref[0, r, :] for r in rows]
    off = 0
    for o_ref in o_refs:
        kd = o_ref.shape[-1]
        w = wout_ref[off:off + kd, :]
        acc = [a + _dot(o_ref[0, r, :], w) for a, r in zip(acc, rows)]
        off += kd
    hm = [_rms(a, mg_ref[...]).astype(BF16) for a in acc]
    qz = [_dot(h_, wmq_ref[...]) for h_ in hm]
    hw = MEM_HEADS * MEM_HD
    sls = [slice(h * MEM_HD, (h + 1) * MEM_HD) for h in range(MEM_HEADS)]
    mkb = [mk_ref[0, :, sl].astype(BF16) for sl in sls]
    mvb = [mv_ref[0, :, sl].astype(BF16) for sl in sls]
    qn = [[_rms(z[:, sl], mqg_ref[...]).astype(BF16) for sl in sls] for z in qz]
    sc = [[_dot_nt(qh, kh) for qh, kh in zip(qt, mkb)] for qt in qn]
    p = [[_softmax_rows(x * (MEM_HD ** -0.5)).astype(BF16) for x in st] for st in sc]
    oh = [[_dot(ph, vh) for ph, vh in zip(pt, mvb)] for pt in p]
    for t, r in enumerate(rows):
        outs = [oh[t][h] * _silu(qz[t][:, hw + h * MEM_HD:hw + (h + 1) * MEM_HD]) for h in range(MEM_HEADS)]
        oh[t] = jnp.concatenate(outs, axis=1).astype(BF16)
    ym = [_dot(om, wmo_ref[...]) for om in oh]
    for t, r in enumerate(rows):
        y_ref[0, r, :] = acc[t] + ym[t]


def _post(x, os_, wout, mg, wmq, mqg, mk, mv, wmo):
    b, s, d = x.shape
    tm = _row_tile(s, 256 * POST_SPLIT)
    hw = MEM_HEADS * MEM_HD
    nm = mk.shape[1]
    blk = lambda wd: pl.BlockSpec((1, tm, wd), lambda i, j: (i, j, 0))
    mem = pl.BlockSpec((1, nm, hw), lambda i, j: (i, 0, 0))
    kin = sum(o.shape[-1] for o in os_)
    return pl.pallas_call(
        functools.partial(_post_body, n_o=len(os_)),
        grid=(b, s // tm),
        in_specs=[blk(d)] + [blk(o.shape[-1]) for o in os_]
        + [_const_spec((kin, d)), _const_spec((1, d)), _const_spec((d, 2 * hw)), _const_spec((1, MEM_HD)),
           mem, mem, _const_spec((hw, d))],
        out_specs=blk(d),
        out_shape=jax.ShapeDtypeStruct((b, s, d), F32),
        compiler_params=_params(("arbitrary", "arbitrary")),
        name="post",
    )(x, *os_, wout, mg, wmq, mqg, mk, mv, wmo)


def _pad_cols(w, n):
    return jnp.pad(w, ((0, 0), (0, n - w.shape[1])))


def _prep_l0(w_in, conv_w, a_log, dt_bias, q_g, k_g):
    c0 = 3 * 1024
    ab = w_in[:, c0:c0 + 2 * GDN_HEADS]
    c1 = c0 + 2 * GDN_HEADS
    cols = lambda i: w_in[:, c1 + i * 1024:c1 + (i + 1) * 1024].astype(BF16)
    w = dict(
        wu=w_in[:, :c0].astype(BF16), wab=_pad_cols(ab, LANES).astype(BF16),
        wza=cols(0), wq=cols(1), wk=cols(2), wv=cols(3), wzb=cols(4),
        alog=jnp.pad(a_log, (0, LANES - GDN_HEADS)).reshape(1, LANES).astype(F32),
        dtb=jnp.pad(dt_bias, (0, LANES - GDN_HEADS)).reshape(1, LANES).astype(F32),
    )
    ns = 3 * GDN_HEADS
    cw = jnp.pad(conv_w.astype(F32), ((0, 8 - CONV_TAPS), (0, 0))).reshape(8, ns, LANES).transpose(1, 0, 2)
    dm = BAND_HEADS * BAND_HD
    head_of = np.arange(dm) // BAND_HD
    seg = jnp.asarray(head_of[:, None] == np.arange(LANES)[None, :], BF16)
    qg = jnp.tile(q_g.astype(F32), BAND_HEADS).reshape(1, dm)
    kg = jnp.tile(k_g.astype(F32), BAND_HEADS).reshape(1, dm)
    return w, cw, seg, qg, kg


def _band_bias_table(rel_bias, L, chunk):
    W = BAND_PAST + L
    n = np.arange(W + L - 1)
    idx = np.clip(BAND_PAST + (L - 1) - n, -BAND_MAX_REL, BAND_MAX_REL) + BAND_MAX_REL
    strip = rel_bias.astype(F32)[:, idx]
    bias = jnp.stack([strip[:, L - 1 - a:L - 1 - a + W] for a in range(L)], axis=1)
    back = np.arange(L)[:, None] // chunk - (np.arange(W)[None, :] - BAND_PAST) // chunk
    readable = (back >= 0) & (back <= BAND_PAST // chunk)
    return bias if readable.all() else jnp.where(jnp.asarray(readable)[None], bias, -jnp.inf)


def _prep_l1(w_in, w_gate_up, gate_bias):
    qk = GLA_HEADS * GLA_DK
    vw = GLA_HEADS * GLA_DV
    o = np.cumsum([0, qk, qk, vw, GLA_RANK, vw])
    return dict(
        wq=w_in[:, o[0]:o[1]].astype(BF16), wk=w_in[:, o[1]:o[2]].astype(BF16),
        wv=w_in[:, o[2]:o[3]].astype(BF16), wlr=_pad_cols(w_in[:, o[3]:o[4]], LANES).astype(BF16),
        wz=w_in[:, o[4]:o[5]].astype(BF16),
        wg=jnp.pad(w_gate_up, ((0, LANES - GLA_RANK), (0, 0))).astype(BF16),
        gbias=gate_bias.reshape(1, qk).astype(F32),
    )


def _layer0(x, conv_state, gdn_state, band_caches, mk, mv, norm_g, pw, cw, seg, qg, kg, bias,
            a_onorm_g, w_out, mnorm_g, w_mq, mq_g, w_mo, L):
    b, s, d = x.shape
    outs = _proj0(x.reshape(b * s, d), norm_g.reshape(1, d), pw, qg, kg, seg, s)
    u, gb, za, qlo, qhi, kn, klast, vlo, vhi, vlast, zb = (t.reshape((b, -1) + t.shape[1:]) for t in outs)
    cinit = jnp.pad(conv_state.astype(F32), ((0, 0), (8 - (CONV_TAPS - 1), 0), (0, 0)))
    o_a, s_new = _gdn(u, gb, za, cinit, gdn_state.astype(F32), cw, a_onorm_g.reshape(1, LANES), L)
    o_b = _band(qlo, qhi, kn, vlo, vhi, zb, bias, band_caches, bias.shape[1])
    y = _post(x, (o_a, o_b), w_out, mnorm_g.reshape(1, d), w_mq, mq_g.reshape(1, MEM_HD), mk, mv, w_mo)
    return y, u[:, s - (CONV_TAPS - 1):, :], s_new, klast, vlast


def _layer1(x, gla_state, mk, mv, norm_g, pw, c_onorm_g, w_out, mnorm_g, w_mq, mq_g, w_mo, L):
    b, s, d = x.shape
    q, k, v, lg, z = _proj1(x.reshape(b * s, d), norm_g.reshape(1, d), pw)
    r3 = lambda t: t.reshape(b, s, t.shape[-1])
    q, k, v, lg, z = map(r3, (q, k, v, lg, z))
    st0 = jnp.swapaxes(gla_state.astype(F32), 2, 3)
    o, st = _gla(q, k, v, lg, z, st0, c_onorm_g.reshape(1, GLA_DV), L)
    y = _post(x, (o,), w_out, mnorm_g.reshape(1, d), w_mq, mq_g.reshape(1, MEM_HD), mk, mv, w_mo)
    return y, jnp.swapaxes(st, 2, 3)


def kernel(x_prompt, x_sample, mem_prompt, state_l0_gdn_conv, state_l0_gdn, cache_l0_band_k, cache_l0_band_v, cache_l0_mem_k, cache_l0_mem_v, state_l1_gla, cache_l1_mem_k, cache_l1_mem_v, l0_norm_g, l0_w_in, l0_conv_w, l0_a_log, l0_dt_bias, l0_a_onorm_g, l0_b_q_g, l0_b_k_g, l0_b_rel_bias, l0_w_out, l0_mnorm_g, l0_mem_norm_g, l0_w_mkv, l0_mk_g, l0_w_mq, l0_mq_g, l0_w_mo, l1_norm_g, l1_w_in, l1_w_gate_up, l1_gate_bias, l1_c_onorm_g, l1_w_out, l1_mnorm_g, l1_mem_norm_g, l1_w_mkv, l1_mk_g, l1_w_mq, l1_mq_g, l1_w_mo):
    bp, sp, d = x_prompt.shape
    bs, ss, _ = x_sample.shape
    nm = mem_prompt.shape[1]
    hw = MEM_HEADS * MEM_HD
    dm = BAND_HEADS * BAND_HD
    assert sp % CHUNK_ == 0 and ss % INV_SUB == 0 and ss <= CHUNK_
    assert cache_l0_band_k.shape[1] == BAND_PAST

    pw0, cw, seg, qg, kg = _prep_l0(l0_w_in, l0_conv_w, l0_a_log, l0_dt_bias, l0_b_q_g, l0_b_k_g)
    pw1 = _prep_l1(l1_w_in, l1_w_gate_up, l1_gate_bias)
    band_rows = BAND_STEP_CHUNKS * CHUNK_ if sp % (BAND_STEP_CHUNKS * CHUNK_) == 0 else CHUNK_
    bias_p = _band_bias_table(l0_b_rel_bias, band_rows, CHUNK_)
    bias_s = _band_bias_table(l0_b_rel_bias, ss, ss)
    bf = lambda w: w.astype(BF16)
    mem2 = mem_prompt.reshape(bp * nm, d)

    p_mk0, p_mv0, mk0, mv0 = _memkv(mem2, l0_mem_norm_g.reshape(1, d), bf(l0_w_mkv), l0_mk_g.reshape(1, MEM_HD))
    mk0 = mk0.reshape(bp, nm, hw)
    mv0 = mv0.reshape(bp, nm, hw)
    l0_shared = (l0_norm_g, pw0, cw, seg, qg, kg)
    l0_tail = (l0_a_onorm_g, bf(l0_w_out), l0_mnorm_g, bf(l0_w_mq), l0_mq_g, bf(l0_w_mo))
    zeros_conv = jnp.zeros((bp, CONV_TAPS - 1, 3 * GDN_HEADS * LANES), F32)
    zeros_gdn = jnp.zeros((bp, GDN_HEADS, GDN_DK, LANES), F32)
    yp, p_conv, p_gdn, p_kn, p_v = _layer0(
        x_prompt, zeros_conv, zeros_gdn, None, mk0, mv0,
        *l0_shared, bias_p, *l0_tail, CHUNK_)
    ys, s_conv, s_gdn, s_kn, s_v = _layer0(
        x_sample, state_l0_gdn_conv, state_l0_gdn,
        (cache_l0_band_k.reshape(bs, BAND_PAST, dm), cache_l0_band_v.reshape(bs, BAND_PAST, dm)),
        cache_l0_mem_k.reshape(bs, nm, hw), cache_l0_mem_v.reshape(bs, nm, hw),
        *l0_shared, bias_s, *l0_tail, ss)

    p_mk1, p_mv1, mk1, mv1 = _memkv(mem2, l1_mem_norm_g.reshape(1, d), bf(l1_w_mkv), l1_mk_g.reshape(1, MEM_HD))
    mk1 = mk1.reshape(bp, nm, hw)
    mv1 = mv1.reshape(bp, nm, hw)
    l1_tail = (l1_c_onorm_g, bf(l1_w_out), l1_mnorm_g, bf(l1_w_mq), l1_mq_g, bf(l1_w_mo))
    zeros_gla = jnp.zeros((bp, GLA_HEADS, GLA_DK, GLA_DV), F32)
    yp, p_gla = _layer1(yp, zeros_gla, mk1, mv1, l1_norm_g, pw1, *l1_tail, CHUNK_)
    ys, s_gla = _layer1(ys, state_l1_gla, cache_l1_mem_k.reshape(bs, nm, hw),
                        cache_l1_mem_v.reshape(bs, nm, hw), l1_norm_g, pw1, *l1_tail, ss)

    m4 = lambda t: t.reshape(bp, nm, MEM_HEADS, MEM_HD)
    return (yp, ys, p_conv, p_gdn, p_kn, p_v,
            m4(p_mk0), m4(p_mv0), p_gla, m4(p_mk1), m4(p_mv1),
            s_conv, s_gdn, s_kn, s_v, s_gla)
```

```python
import functools

import jax
import jax.numpy as jnp
import numpy as np
from jax import lax
from jax.experimental import pallas as pl
from jax.experimental.pallas import tpu as pltpu

F32 = jnp.float32
BF16 = jnp.bfloat16
NORM_EPS = 1e-6
LOG2_E = 1.4426950408889634

CHUNK_ = 64
CONV_TAPS = 4
GDN_HEADS = 8
GDN_DK = 128
BAND_HEADS = 16
BAND_HD = 64
BAND_PAST = 512
BAND_MAX_REL = 128
GLA_HEADS = 8
GLA_DK = 128
GLA_DV = 256
GLA_RANK = 16
GLA_TAU = 16.0
MEM_HEADS = 4
MEM_HD = 128
INV_SUB = 16
GDN_GROUP = 2
GDN_STEP_CHUNKS = 1
GLA_GROUP = 2
POST_SPLIT = 4
PROJ_SPLIT = 2
SUB_TILE_MIN = 128
BAND_STEP_CHUNKS = 2
LANES = 128
VMEM_LIMIT = 56 * 1024 * 1024


def _dot(a, b):
    return jnp.dot(a, b, preferred_element_type=F32)


def _dot_nt(a, b):
    return lax.dot_general(a, b, (((1,), (1,)), ((), ())), preferred_element_type=F32)


def _dot_tn(a, b):
    return lax.dot_general(a, b, (((0,), (0,)), ((), ())), preferred_element_type=F32)


def _split3(x):
    hi = x.astype(BF16)
    r = x - hi.astype(F32)
    mid = r.astype(BF16)
    lo = (r - mid.astype(F32)).astype(BF16)
    return hi, mid, lo


def _dot_exact_lhs(a_bf, b):
    h, m, l = _split3(b)
    return _dot(a_bf, h) + (_dot(a_bf, m) + _dot(a_bf, l))


def _rms(x, g):
    ms = jnp.mean(x * x, axis=-1, keepdims=True)
    return x * lax.rsqrt(ms + NORM_EPS) * g


def _silu(x):
    return x * jax.nn.sigmoid(x)


def _softplus(x):
    return jnp.maximum(x, 0.0) + jnp.log1p(jnp.exp(-jnp.abs(x)))


def _softmax_rows(s):
    m = jnp.max(s, axis=-1, keepdims=True)
    e = jnp.exp(s - m)
    return e * (1.0 / jnp.sum(e, axis=-1, keepdims=True))


def _const_spec(shape):
    nd = len(shape)
    return pl.BlockSpec(shape, lambda *_: (0,) * nd, pipeline_mode=pl.Buffered(1))


def _params(sem):
    return pltpu.CompilerParams(dimension_semantics=sem, vmem_limit_bytes=VMEM_LIMIT)


def _row_tile(n, want):
    t = min(n, want)
    assert n % t == 0
    return t


def _sub_tiles(tm):
    ts = tm // PROJ_SPLIT if tm % (SUB_TILE_MIN * PROJ_SPLIT) == 0 else tm
    return [slice(t * ts, (t + 1) * ts) for t in range(tm // ts)]


def _proj0_body(x_ref, g_ref, wu_ref, wab_ref, wza_ref, wq_ref, wk_ref, wv_ref, wzb_ref,
                alog_ref, dtb_ref, qg_ref, kg_ref, seg_ref,
                u_ref, gb_ref, za_ref, qlo_ref, qhi_ref, kn_ref, klast_ref, vlo_ref, vhi_ref, vlast_ref, zb_ref,
                *, kept_tiles):
    h = _rms(x_ref[...], g_ref[...]).astype(BF16)
    dm = BAND_HEADS * BAND_HD
    lo = (lax.broadcasted_iota(jnp.int32, (1, dm), 1) & (LANES - 1)) < BAND_HD
    seg = seg_ref[...]
    q = _dot(h, wq_ref[...])
    k = _dot(h, wk_ref[...])
    ab = _dot(h, wab_ref[...])
    za = _dot(h, wza_ref[...])

    def head_rsqrt(x):
        ss = _dot((x * x).astype(BF16), seg)
        return lax.rsqrt(ss * (1.0 / BAND_HD) + NORM_EPS)

    rq = head_rsqrt(q)
    rk = head_rsqrt(k)
    zb = _dot(h, wzb_ref[...])
    v = _dot(h, wv_ref[...])
    u_ref[...] = _dot(h, wu_ref[...])
    za_ref[...] = _silu(za)
    zb_ref[...] = _silu(zb)
    lo1 = lo[:, :LANES]

    def spread(r):
        return jnp.concatenate([jnp.where(lo1, r[:, 2 * s:2 * s + 1], r[:, 2 * s + 1:2 * s + 2])
                                for s in range(BAND_HEADS // 2)], axis=1)

    qn = q * spread(rq) * (qg_ref[...] * (BAND_HD ** -0.5 * LOG2_E))
    kn = k * spread(rk) * kg_ref[...]
    qlo_ref[...] = jnp.where(lo, qn, 0.0).astype(BF16)
    qhi_ref[...] = jnp.where(lo, 0.0, qn).astype(BF16)
    kn_ref[...] = kn.astype(BF16)
    vlo_ref[...] = jnp.where(lo, v, 0.0).astype(BF16)
    vhi_ref[...] = jnp.where(lo, 0.0, v).astype(BF16)

    def hand_on():
        klast_ref[...] = kn.reshape(kn.shape[0], BAND_HEADS, BAND_HD)
        vlast_ref[...] = v.reshape(v.shape[0], BAND_HEADS, BAND_HD)

    if kept_tiles is None:
        hand_on()
    else:
        tpb, kt = kept_tiles
        pl.when(pl.program_id(0) % tpb >= tpb - kt)(hand_on)
    lane = lax.broadcasted_iota(jnp.int32, ab.shape, 1)
    gval = -jnp.exp(alog_ref[...]) * _softplus(ab + dtb_ref[...])
    gb_ref[...] = jnp.where(lane < GDN_HEADS, gval, jax.nn.sigmoid(ab))


def _proj0(x2, g, w, qg, kg, seg, rows_per_batch):
    n, d = x2.shape
    tm = _row_tile(n, 256)
    dm = BAND_HEADS * BAND_HD
    widths = (3 * 1024, LANES, 1024, dm, dm, dm, dm)
    row = lambda wd: pl.BlockSpec((tm, wd), lambda i: (i, 0))
    keep = min(BAND_PAST, rows_per_batch)
    if rows_per_batch > keep:
        assert rows_per_batch % tm == 0 and keep % tm == 0
        tpb, kt = rows_per_batch // tm, keep // tm
        last = pl.BlockSpec((tm, BAND_HEADS, BAND_HD),
                            lambda i: ((i // tpb) * kt + jnp.maximum(i % tpb - (tpb - kt), 0), 0, 0))
        n_last = (n // rows_per_batch) * keep
        kept_tiles = (tpb, kt)
    else:
        last, n_last = pl.BlockSpec((tm, BAND_HEADS, BAND_HD), lambda i: (i, 0, 0)), n
        kept_tiles = None
    last_shape = jax.ShapeDtypeStruct((n_last, BAND_HEADS, BAND_HD), F32)
    f32 = lambda rows, wd: jax.ShapeDtypeStruct((rows, wd), F32)
    bf16 = lambda wd: jax.ShapeDtypeStruct((n, wd), BF16)
    return pl.pallas_call(
        functools.partial(_proj0_body, kept_tiles=kept_tiles),
        grid=(n // tm,),
        in_specs=[row(d), _const_spec((1, d))]
        + [_const_spec((d, wd)) for wd in widths]
        + [_const_spec((1, LANES)), _const_spec((1, LANES)), _const_spec((1, dm)), _const_spec((1, dm)),
           _const_spec((dm, LANES))],
        out_specs=[row(3 * 1024), row(LANES), row(1024), row(dm), row(dm), row(dm), last, row(dm), row(dm), last,
                   row(dm)],
        out_shape=[f32(n, 3 * 1024), f32(n, LANES), f32(n, 1024), bf16(dm), bf16(dm), bf16(dm), last_shape,
                   bf16(dm), bf16(dm), last_shape, f32(n, dm)],
        compiler_params=_params(("arbitrary",)),
        name="proj0",
    )(x2, g, w["wu"], w["wab"], w["wza"], w["wq"], w["wk"], w["wv"], w["wzb"], w["alog"], w["dtb"],
      qg, kg, seg)


def _proj1_body(x_ref, g_ref, wq_ref, wk_ref, wv_ref, wlr_ref, wz_ref, wg_ref, gbias_ref,
                q_ref, k_ref, v_ref, lg_ref, z_ref):
    rows = _sub_tiles(x_ref.shape[0])
    h = [_rms(x_ref[r, :], g_ref[...]).astype(BF16) for r in rows]
    lr = [_dot(h_, wlr_ref[...]) for h_ in h]
    for r, h_ in zip(rows, h):
        q_ref[r, :] = _dot(h_, wq_ref[...])
    pre = [_dot(x.astype(BF16), wg_ref[...]) + gbias_ref[...] for x in lr]
    zs = [_dot(h_, wz_ref[...]) for h_ in h]
    for r, p_ in zip(rows, pre):
        lg_ref[r, :] = -_softplus(-p_) * (1.0 / GLA_TAU)
    for r, h_ in zip(rows, h):
        k_ref[r, :] = _dot(h_, wk_ref[...])
    for r, z_ in zip(rows, zs):
        z_ref[r, :] = _silu(z_)
    for r, h_ in zip(rows, h):
        v_ref[r, :] = _dot(h_, wv_ref[...])


def _proj1(x2, g, w):
    n, d = x2.shape
    tm = _row_tile(n, 512)
    row = lambda wd: pl.BlockSpec((tm, wd), lambda i: (i, 0))
    outw = (1024, 1024, 2048, 1024, 2048)
    return pl.pallas_call(
        _proj1_body,
        grid=(n // tm,),
        in_specs=[row(d), _const_spec((1, d)), _const_spec((d, 1024)), _const_spec((d, 1024)),
                  _const_spec((d, 2048)), _const_spec((d, LANES)), _const_spec((d, 2048)),
                  _const_spec((LANES, 1024)), _const_spec((1, 1024))],
        out_specs=[row(wd) for wd in outw],
        out_shape=[jax.ShapeDtypeStruct((n, wd), F32) for wd in outw],
        compiler_params=_params(("arbitrary",)),
        name="proj1",
    )(x2, g, w["wq"], w["wk"], w["wv"], w["wlr"], w["wz"], w["wg"], w["gbias"])


def _memkv_body(m_ref, g_ref, w_ref, kg_ref, k4_ref, v4_ref, kb_ref, vb_ref):
    h = _rms(m_ref[...], g_ref[...]).astype(BF16)
    kv = _dot(h, w_ref[...])
    hw = MEM_HEADS * MEM_HD
    k = jnp.concatenate([_rms(kv[:, hh * MEM_HD:(hh + 1) * MEM_HD], kg_ref[...]) for hh in range(MEM_HEADS)],
                        axis=1)
    v = kv[:, hw:]
    k4_ref[...] = k.reshape(k.shape[0], MEM_HEADS, MEM_HD)
    v4_ref[...] = v.reshape(v.shape[0], MEM_HEADS, MEM_HD)
    kb_ref[...] = k.astype(BF16)
    vb_ref[...] = v.astype(BF16)


def _memkv(m2, g, w_bf, kg):
    n, d = m2.shape
    tm = _row_tile(n, 256)
    hw = MEM_HEADS * MEM_HD
    row = lambda wd: pl.BlockSpec((tm, wd), lambda i: (i, 0))
    row4 = pl.BlockSpec((tm, MEM_HEADS, MEM_HD), lambda i: (i, 0, 0))
    return pl.pallas_call(
        _memkv_body,
        grid=(n // tm,),
        in_specs=[row(d), _const_spec((1, d)), _const_spec((d, 2 * hw)), _const_spec((1, MEM_HD))],
        out_specs=[row4, row4, row(hw), row(hw)],
        out_shape=[jax.ShapeDtypeStruct((n, MEM_HEADS, MEM_HD), F32)] * 2
        + [jax.ShapeDtypeStruct((n, hw), BF16)] * 2,
        compiler_params=_params(("arbitrary",)),
        name="memkv",
    )(m2, g, w_bf, kg)


def _each(fn, *lists):
    return [fn(*xs) for xs in zip(*lists)]


def _unit_lower_inverse(a, eye, bd, mm):
    d = _each(lambda x: jnp.where(bd, x, 0.0), a)
    nl = _each(lambda x, y: x - y, a, d)
    d2 = _each(mm, d, d)
    d4 = _each(mm, d2, d2)
    td = _each(lambda x, y: mm(eye - x, eye + y), d, d2)
    d8 = _each(mm, d4, d4)
    td = _each(lambda x, y: mm(x, eye + y), td, d4)
    td = _each(lambda x, y: mm(x, eye + y), td, d8)
    m = _each(mm, td, nl)
    m2 = _each(mm, m, m)
    mt = _each(mm, m, td)
    return _each(lambda x, y, z: mm(eye + x, y - z), m2, td, mt)


def _gdn_body(u_ref, gb_ref, za_ref, cinit_ref, sinit_ref, cw_ref, og_ref,
              o_ref, s_ref, ubuf, *, L, NB, G):
    c = pl.program_id(1)
    ns = 3 * GDN_HEADS
    hist = 8
    R = G * L

    @pl.when(c == 0)
    def _():
        for b in range(NB):
            for j in range(ns):
                ubuf[b * ns + j, 0:hist, :] = cinit_ref[b, :, j * LANES:(j + 1) * LANES]
        s_ref[...] = sinit_ref[...]

    for b in range(NB):
        for j in range(ns):
            ubuf[b * ns + j, hist:hist + R, :] = u_ref[b, :, j * LANES:(j + 1) * LANES]
    base = hist - (CONV_TAPS - 1)
    ys = []
    for b in range(NB):
        bsl = slice(b * ns, (b + 1) * ns)
        yb = ubuf[bsl, base:base + R, :] * cw_ref[:, 0:1, :]
        for i in range(1, CONV_TAPS):
            yb = yb + ubuf[bsl, base + i:base + i + R, :] * cw_ref[:, i:i + 1, :]
        ys.append(_silu(yb))
    ubuf[:, base:hist, :] = ubuf[:, base + R:hist + R, :]

    row = lax.broadcasted_iota(jnp.int32, (L, 2 * L), 0)
    lane = lax.broadcasted_iota(jnp.int32, (L, 2 * L), 1)
    col = lane & (L - 1)
    left = lane < L
    incl = row >= col
    strict = row > col
    sub_shift = INV_SUB.bit_length() - 1
    bd = lax.shift_right_logical(row, sub_shift) == lax.shift_right_logical(col, sub_shift)
    eye = jnp.where(row == col, 1.0, 0.0).astype(F32)
    trow = lax.broadcasted_iota(jnp.int32, (L, L), 0)
    tcol = lax.broadcasted_iota(jnp.int32, (L, L), 1)
    tri = jnp.where(trow >= tcol, 1.0, 0.0).astype(BF16)
    og = og_ref[...]
    heads = [(b, g, h) for b in range(NB) for g in range(G) for h in range(GDN_HEADS)]
    pairs = range(len(heads) // 2)
    rows_of = lambda g: slice(g * L, (g + 1) * L)
    l2n = lambda x: x * lax.rsqrt(jnp.sum(x * x, axis=-1, keepdims=True) + NORM_EPS)
    q = [l2n(ys[b][h][rows_of(g)]) * (GDN_DK ** -0.5) for b, g, h in heads]
    k = [l2n(ys[b][GDN_HEADS + h][rows_of(g)]) for b, g, h in heads]
    v = [ys[b][2 * GDN_HEADS + h][rows_of(g)] for b, g, h in heads]
    gc, gr, bc = [], [], []
    for b in range(NB):
        for g in range(G):
            gbv = gb_ref[b, rows_of(g), :]
            gcum = _dot_exact_lhs(tri, gbv)
            gpad = jnp.concatenate([gcum, jnp.zeros((LANES - L, LANES), F32)], axis=0)
            gt = gpad.T
            for h in range(GDN_HEADS):
                gc.append(gcum[:, h:h + 1])
                gr.append(gt[h:h + 1, 0:L])
                bc.append(gbv[:, GDN_HEADS + h:GDN_HEADS + h + 1])
    side = lambda x0, x1: jnp.where(left, x0, x1)
    gcp = [side(gc[2 * p], gc[2 * p + 1]) for p in pairs]
    grp = [jnp.concatenate([gr[2 * p], gr[2 * p + 1]], axis=1) for p in pairs]
    bcp = [side(bc[2 * p], bc[2 * p + 1]) for p in pairs]
    dec = _each(lambda c_, r_: jnp.where(incl, jnp.exp(jnp.where(incl, c_ - r_, 0.0)), 0.0), gcp, grp)
    kb = _each(lambda x: x.astype(BF16), k)
    qb = _each(lambda x: x.astype(BF16), q)

    def blockdiag(y0, y1):
        z0 = jnp.zeros(y1.shape, y1.dtype)
        z1 = jnp.zeros(y0.shape, y0.dtype)
        return jnp.concatenate([jnp.concatenate([y0, z0], axis=1), jnp.concatenate([z1, y1], axis=1)], axis=0)

    def mm(x, y):
        yb = y.astype(BF16)
        zero = jnp.zeros_like(yb)
        return _dot(x.astype(BF16), jnp.concatenate([jnp.where(left, yb, zero), jnp.where(left, zero, yb)], axis=0))

    kq = [_dot_nt(jnp.concatenate([jnp.concatenate([kb[2 * p], qb[2 * p]], axis=0),
                                   jnp.concatenate([kb[2 * p + 1], qb[2 * p + 1]], axis=0)], axis=1),
                  blockdiag(kb[2 * p], kb[2 * p + 1])) for p in pairs]
    a = _each(lambda b_, x, d_: jnp.where(strict, b_ * x[:L] * d_, 0.0), bcp, kq, dec)
    t = _unit_lower_inverse(a, eye, bd, mm)
    eg = _each(jnp.exp, gc)
    rhs = _each(lambda b_, v_, e_, k_: jnp.concatenate([b_ * v_, (b_ * e_) * k_], axis=1).astype(BF16),
                bc, v, eg, k)
    sol = [_dot(t[p].astype(BF16), blockdiag(rhs[2 * p], rhs[2 * p + 1])) for p in pairs]
    solk = lambda i: sol[i // 2][:, 2 * (i % 2) * LANES + LANES:2 * (i % 2 + 1) * LANES]
    solv = lambda i: sol[i // 2][:, 2 * (i % 2) * LANES:2 * (i % 2) * LANES + LANES]
    qkd = [(kq[p][L:] * dec[p]).astype(BF16) for p in pairs]
    gl = [c_[L - 1:L, :] for c_ in gc]
    kd = _each(lambda k_, l_, c_: (k_ * jnp.exp(l_ - c_)).astype(BF16), k, gl, gc)

    item = lambda b, g, h: (b * G + g) * GDN_HEADS + h
    half = lambda x, i: x[:, (i % 2) * LANES:(i % 2 + 1) * LANES]
    s = {(b, h): s_ref[b, h] for b in range(NB) for h in range(GDN_HEADS)}
    for g in range(G):
        ids = [item(b, g, h) for b in range(NB) for h in range(GDN_HEADS)]
        sb = {i: s[(heads[i][0], heads[i][2])].astype(BF16) for i in ids}
        ksq = {i: _dot(jnp.concatenate([jnp.concatenate([solk(i).astype(BF16), qb[i]], axis=0),
                                        jnp.concatenate([solk(i + 1).astype(BF16), qb[i + 1]], axis=0)], axis=1),
                       blockdiag(sb[i], sb[i + 1])) for i in ids[::2]}
        ub = {i: (solv(i) - half(ksq[i - i % 2][:L], i)).astype(BF16) for i in ids}
        qku = {i: _dot(qkd[i // 2], blockdiag(ub[i], ub[i + 1])) for i in ids[::2]}
        ktu = {i: _dot_tn(kd[i], ub[i]) for i in ids}
        for i in ids:
            b, _, h = heads[i]
            s[(b, h)] = s[(b, h)] * jnp.exp(gl[i]) + ktu[i]
        for b in range(NB):
            za = za_ref[b, rows_of(g), :]
            outs = [(_rms(half(ksq[i - i % 2][L:], i) * eg[i] + half(qku[i - i % 2], i), og)
                     * za[:, heads[i][2] * LANES:(heads[i][2] + 1) * LANES]).astype(BF16)
                    for i in ids if heads[i][0] == b]
            o_ref[b, rows_of(g), :] = jnp.concatenate(outs, axis=1)
    for (b, h), val in s.items():
        s_ref[b, h] = val


def _gdn(u, gb, za, cinit, sinit, cw, og, L):
    b, s, _ = u.shape
    nc = s // L
    ns = 3 * GDN_HEADS
    nb = GDN_GROUP if b % GDN_GROUP == 0 else 1
    g = GDN_STEP_CHUNKS if nc % GDN_STEP_CHUNKS == 0 else 1
    blk = lambda wd: pl.BlockSpec((nb, g * L, wd), lambda i, j: (i, j, 0))
    return pl.pallas_call(
        functools.partial(_gdn_body, L=L, NB=nb, G=g),
        grid=(b // nb, nc // g),
        in_specs=[blk(ns * LANES), blk(LANES), blk(GDN_HEADS * LANES),
                  pl.BlockSpec((nb, 8, ns * LANES), lambda i, j: (i, 0, 0)),
                  pl.BlockSpec((nb, GDN_HEADS, GDN_DK, LANES), lambda i, j: (i, 0, 0, 0)),
                  _const_spec((ns, 8, LANES)), _const_spec((1, LANES))],
        out_specs=[blk(GDN_HEADS * LANES),
                   pl.BlockSpec((nb, GDN_HEADS, GDN_DK, LANES), lambda i, j: (i, 0, 0, 0))],
        out_shape=[jax.ShapeDtypeStruct((b, s, GDN_HEADS * LANES), BF16),
                   jax.ShapeDtypeStruct((b, GDN_HEADS, GDN_DK, LANES), F32)],
        scratch_shapes=[pltpu.VMEM((nb * ns, 8 + g * L, LANES), F32)],
        compiler_params=_params(("arbitrary", "arbitrary")),
        name="gdn",
    )(u, gb, za, cinit, sinit, cw, og)


def _band_body(*refs, L, W, has_cache):
    if has_cache:
        (qlo_ref, qhi_ref, k_ref, vlo_ref, vhi_ref, zb_ref, bias_ref, ck_ref, cv_ref,
         o_ref, kscr, vlo, vhi, biasm) = refs
    else:
        (qlo_ref, qhi_ref, k_ref, vlo_ref, vhi_ref, zb_ref, bias_ref,
         o_ref, kscr, vlo, vhi, biasm) = refs
    c = pl.program_id(1)
    dm = BAND_HEADS * BAND_HD
    pad_chunks = BAND_PAST // L

    @pl.when(c == 0)
    def _():
        if has_cache:
            lo = (lax.broadcasted_iota(jnp.int32, (1, dm), 1) & (LANES - 1)) < BAND_HD
            kscr[0:BAND_PAST, :] = ck_ref[0].astype(BF16)
            cv = cv_ref[0]
            vlo[0:BAND_PAST, :] = jnp.where(lo, cv, 0.0).astype(BF16)
            vhi[0:BAND_PAST, :] = jnp.where(lo, 0.0, cv).astype(BF16)
        else:
            zero = jnp.zeros((BAND_PAST, dm), BF16)
            kscr[0:BAND_PAST, :] = zero
            vlo[0:BAND_PAST, :] = zero
            vhi[0:BAND_PAST, :] = zero

    new0 = pl.multiple_of(BAND_PAST + c * L, L)
    kscr[pl.ds(new0, L), :] = k_ref[0]
    vlo[pl.ds(new0, L), :] = vlo_ref[0]
    vhi[pl.ds(new0, L), :] = vhi_ref[0]

    if has_cache:
        @pl.when(c == 0)
        def _():
            biasm[...] = bias_ref[...] * LOG2_E
    else:
        @pl.when(c < pad_chunks)
        def _():
            wcol = lax.broadcasted_iota(jnp.int32, (1, W), 1)
            valid = (wcol + c * L) >= BAND_PAST
            for h in range(BAND_HEADS):
                biasm[h] = jnp.where(valid, bias_ref[h] * LOG2_E, -jnp.inf)

        @pl.when(c == pad_chunks)
        def _():
            biasm[...] = bias_ref[...] * LOG2_E
    bias_src = biasm

    w0 = pl.multiple_of(c * L, L)
    zb = zb_ref[0]
    slabs = [slice(s * LANES, (s + 1) * LANES) for s in range(BAND_HEADS // 2)]
    sc = []
    for sl in slabs:
        ks = kscr[pl.ds(w0, W), sl]
        sc.append(_dot_nt(qlo_ref[0, :, sl], ks))
        sc.append(_dot_nt(qhi_ref[0, :, sl], ks))
    ps, rs = [], []
    for h in range(BAND_HEADS):
        x = sc[h] + bias_src[h]
        e = jnp.exp2(x - jnp.max(x, axis=-1, keepdims=True))
        rs.append(1.0 / jnp.sum(e, axis=-1, keepdims=True))
        ps.append(e.astype(BF16))
    pv = []
    for i, sl in enumerate(slabs):
        pv.append(_dot(ps[2 * i], vlo[pl.ds(w0, W), sl]))
        pv.append(_dot(ps[2 * i + 1], vhi[pl.ds(w0, W), sl]))
    outs = [((pv[2 * i] * rs[2 * i] + pv[2 * i + 1] * rs[2 * i + 1]) * zb[:, sl]).astype(BF16)
            for i, sl in enumerate(slabs)]
    o_ref[0] = jnp.concatenate(outs, axis=1)


def _band(qlo, qhi, kn, vlo, vhi, zb, bias, caches, L):
    b, s, dm = qlo.shape
    nc = s // L
    W = BAND_PAST + L
    blk = pl.BlockSpec((1, L, dm), lambda i, j: (i, j, 0))
    cache = pl.BlockSpec((1, BAND_PAST, dm), lambda i, j: (i, 0, 0))
    has_cache = caches is not None
    scratch = [pltpu.VMEM((BAND_PAST + s, dm), BF16)] * 3 + [pltpu.VMEM((BAND_HEADS, L, W), F32)]
    return pl.pallas_call(
        functools.partial(_band_body, L=L, W=W, has_cache=has_cache),
        grid=(b, nc),
        in_specs=[blk] * 6 + [_const_spec((BAND_HEADS, L, W))] + ([cache, cache] if has_cache else []),
        out_specs=blk,
        out_shape=jax.ShapeDtypeStruct((b, s, dm), BF16),
        scratch_shapes=scratch,
        compiler_params=_params(("arbitrary", "arbitrary")),
        name="band",
    )(qlo, qhi, kn, vlo, vhi, zb, bias, *(caches if has_cache else ()))


def _gla_body(q_ref, k_ref, v_ref, lg_ref, z_ref, sinit_ref, og_ref, o_ref, st_ref, cbs, *, L, NB):
    c = pl.program_id(1)

    @pl.when(c == 0)
    def _():
        st_ref[...] = sinit_ref[...]

    levels = [L >> (t + 1) for t in range(L.bit_length() - 1)]
    small = [s for s in levels if 2 * s < 16]
    row = lax.broadcasted_iota(jnp.int32, (L, L), 0)
    col = lax.broadcasted_iota(jnp.int32, (L, L), 1)
    anchor = lambda s: lax.shift_left(lax.shift_right_logical(row, s.bit_length()), s.bit_length()) + (s - 1)
    between = lambda s: (col > jnp.minimum(row, anchor(s))) & (col <= jnp.maximum(row, anchor(s)))
    onehot = lambda m: jnp.where(m, 1.0, 0.0).astype(BF16)
    tri = onehot(row >= col)
    stack = jnp.concatenate([onehot(between(s)) for s in small], axis=0)
    ex = jnp.exp2
    og = og_ref[...]
    scale = GLA_DK ** -0.5
    pieces = [_split3(lg_ref[b] * LOG2_E) for b in range(NB)]
    cb_all = [_dot(tri, p[0]) + (_dot(tri, p[1]) + _dot(tri, p[2])) for p in pieces]
    y_small = [_dot(stack, p[0]) + _dot(stack, p[1]) for p in pieces]
    ydec = [dict() for _ in range(NB)]
    for b in range(NB):
        cbs[b] = cb_all[b]
        for t, s in enumerate(small):
            ydec[b][s] = y_small[b][t * L:(t + 1) * L]
        for s in levels:
            if s not in ydec[b]:
                anc = jnp.concatenate(
                    [jnp.broadcast_to(cbs[b, p * 2 * s + s - 1:p * 2 * s + s, :], (2 * s, cbs.shape[2]))
                     for p in range(L // (2 * s))], axis=0)
                ydec[b][s] = -jnp.abs(cb_all[b] - anc)

    heads = [(b, h) for b in range(NB) for h in range(GLA_HEADS)]
    pairs = range(len(heads) // 2)
    ksl = lambda h: slice(h * GLA_DK, (h + 1) * GLA_DK)
    vsl = lambda h: slice(h * GLA_DV, (h + 1) * GLA_DV)
    cb = [cb_all[b][:, ksl(h)] for b, h in heads]
    q = [q_ref[b, :, ksl(h)] * scale for b, h in heads]
    k = [k_ref[b, :, ksl(h)] for b, h in heads]
    vb = [v_ref[b, :, vsl(h)].astype(BF16) for b, h in heads]
    st = [st_ref[b, h] for b, h in heads]
    o_inter = _each(lambda q_, c_, s_: _dot_nt((q_ * ex(c_)).astype(BF16), s_.astype(BF16)), q, cb, st)
    cl = [c_[L - 1:L] for c_ in cb]
    ke = _each(lambda k_, l_, c_: (k_ * ex(l_ - c_)).astype(BF16), k, cl, cb)
    vtk = _each(_dot_tn, vb, ke)

    zk = jnp.zeros((L, GLA_DK), BF16)
    zv = jnp.zeros((L, GLA_DV), BF16)

    def side_by_side(qs, ks):
        lhs = jnp.concatenate(qs, axis=1).astype(BF16)
        rhs = jnp.concatenate([jnp.concatenate([ks[0].astype(BF16), zk], axis=1),
                               jnp.concatenate([zk, ks[1].astype(BF16)], axis=1)], axis=0)
        return _dot_nt(lhs, rhs)

    prow = lax.broadcasted_iota(jnp.int32, (L, 2 * L), 0)
    pcol = lax.broadcasted_iota(jnp.int32, (L, 2 * L), 1) & (L - 1)
    att = [jnp.where(prow == pcol, side_by_side((q[2 * p], q[2 * p + 1]), (k[2 * p], k[2 * p + 1])), 0.0)
           for p in pairs]
    for s in levels:
        sh = s.bit_length() - 1
        same_parent = lax.shift_right_logical(prow, sh + 1) == lax.shift_right_logical(pcol, sh + 1)
        take = same_parent & ((lax.shift_right_logical(prow, sh) & 1) == 1) & \
            ((lax.shift_right_logical(pcol, sh) & 1) == 0)
        f = [ex(ydec[b][s][:, ksl(h)]) for b, h in heads]
        prod = [side_by_side((q[2 * p] * f[2 * p], q[2 * p + 1] * f[2 * p + 1]),
                             (k[2 * p] * f[2 * p], k[2 * p + 1] * f[2 * p + 1])) for p in pairs]
        att = [jnp.where(take, prod[p], att[p]) for p in pairs]
    vpair = [jnp.concatenate([jnp.concatenate([vb[2 * p], zv], axis=1),
                              jnp.concatenate([zv, vb[2 * p + 1]], axis=1)], axis=0) for p in pairs]
    o_intra = [_dot(att[p].astype(BF16), vpair[p]) for p in pairs]
    for i, (b, h) in enumerate(heads):
        st_ref[b, h] = st[i] * ex(cl[i]) + vtk[i]
    for b in range(NB):
        z = z_ref[b]
        outs = [(_rms(o_inter[i] + o_intra[i // 2][:, (i % 2) * GLA_DV:(i % 2 + 1) * GLA_DV], og)
                 * z[:, vsl(h)]).astype(BF16) for i, (b_, h) in enumerate(heads) if b_ == b]
        o_ref[b] = jnp.concatenate(outs, axis=1)


def _gla(q, k, v, lg, z, sinit_t, og, L):
    b, s, _ = q.shape
    nc = s // L
    nb = GLA_GROUP if b % GLA_GROUP == 0 else 1
    qk = pl.BlockSpec((nb, L, GLA_HEADS * GLA_DK), lambda i, j: (i, j, 0))
    vv = pl.BlockSpec((nb, L, GLA_HEADS * GLA_DV), lambda i, j: (i, j, 0))
    st = pl.BlockSpec((nb, GLA_HEADS, GLA_DV, GLA_DK), lambda i, j: (i, 0, 0, 0))
    return pl.pallas_call(
        functools.partial(_gla_body, L=L, NB=nb),
        grid=(b // nb, nc),
        in_specs=[qk, qk, vv, qk, vv, st, _const_spec((1, GLA_DV))],
        out_specs=[vv, st],
        out_shape=[jax.ShapeDtypeStruct((b, s, GLA_HEADS * GLA_DV), BF16),
                   jax.ShapeDtypeStruct((b, GLA_HEADS, GLA_DV, GLA_DK), F32)],
        scratch_shapes=[pltpu.VMEM((nb, L, GLA_HEADS * GLA_DK), F32)],
        compiler_params=_params(("arbitrary", "arbitrary")),
        name="gla",
    )(q, k, v, lg, z, sinit_t, og)


def _post_body(*refs, n_o):
    x_ref = refs[0]
    o_refs = refs[1:1 + n_o]
    wout_ref, mg_ref, wmq_ref, mqg_ref, mk_ref, mv_ref, wmo_ref, y_ref = refs[1 + n_o:]
    tm = x_ref.shape[1]
    ts = tm // POST_SPLIT if tm % (SUB_TILE_MIN * POST_SPLIT) == 0 else tm
    rows = [slice(t * ts, (t + 1) * ts) for t in range(tm // ts)]
    acc = [x_ref[0, r, :] for r in rows]
    off = 0
    for o_ref in o_refs:
        kd = o_ref.shape[-1]
        w = wout_ref[off:off + kd, :]
        acc = [a + _dot(o_ref[0, r, :], w) for a, r in zip(acc, rows)]
        off += kd
    hm = [_rms(a, mg_ref[...]).astype(BF16) for a in acc]
    qz = [_dot(h_, wmq_ref[...]) for h_ in hm]
    hw = MEM_HEADS * MEM_HD
    sls = [slice(h * MEM_HD, (h + 1) * MEM_HD) for h in range(MEM_HEADS)]
    mkb = [mk_ref[0, :, sl].astype(BF16) for sl in sls]
    mvb = [mv_ref[0, :, sl].astype(BF16) for sl in sls]
    qn = [[_rms(z[:, sl], mqg_ref[...]).astype(BF16) for sl in sls] for z in qz]
    sc = [[_dot_nt(qh, kh) for qh, kh in zip(qt, mkb)] for qt in qn]
    p = [[_softmax_rows(x * (MEM_HD ** -0.5)).astype(BF16) for x in st] for st in sc]
    oh = [[_dot(ph, vh) for ph, vh in zip(pt, mvb)] for pt in p]
    for t, r in enumerate(rows):
        outs = [oh[t][h] * _silu(qz[t][:, hw + h * MEM_HD:hw + (h + 1) * MEM_HD]) for h in range(MEM_HEADS)]
        oh[t] = jnp.concatenate(outs, axis=1).astype(BF16)
    ym = [_dot(om, wmo_ref[...]) for om in oh]
    for t, r in enumerate(rows):
        y_ref[0, r, :] = acc[t] + ym[t]


def _post(x, os_, wout, mg, wmq, mqg, mk, mv, wmo):
    b, s, d = x.shape
    tm = _row_tile(s, 256 * POST_SPLIT)
    hw = MEM_HEADS * MEM_HD
    nm = mk.shape[1]
    blk = lambda wd: pl.BlockSpec((1, tm, wd), lambda i, j: (i, j, 0))
    mem = pl.BlockSpec((1, nm, hw), lambda i, j: (i, 0, 0))
    kin = sum(o.shape[-1] for o in os_)
    return pl.pallas_call(
        functools.partial(_post_body, n_o=len(os_)),
        grid=(b, s // tm),
        in_specs=[blk(d)] + [blk(o.shape[-1]) for o in os_]
        + [_const_spec((kin, d)), _const_spec((1, d)), _const_spec((d, 2 * hw)), _const_spec((1, MEM_HD)),
           mem, mem, _const_spec((hw, d))],
        out_specs=blk(d),
        out_shape=jax.ShapeDtypeStruct((b, s, d), F32),
        compiler_params=_params(("arbitrary", "arbitrary")),
        name="post",
    )(x, *os_, wout, mg, wmq, mqg, mk, mv, wmo)


def _pad_cols(w, n):
    return jnp.pad(w, ((0, 0), (0, n - w.shape[1])))


def _prep_l0(w_in, conv_w, a_log, dt_bias, q_g, k_g):
    c0 = 3 * 1024
    ab = w_in[:, c0:c0 + 2 * GDN_HEADS]
    c1 = c0 + 2 * GDN_HEADS
    cols = lambda i: w_in[:, c1 + i * 1024:c1 + (i + 1) * 1024].astype(BF16)
    w = dict(
        wu=w_in[:, :c0].astype(BF16), wab=_pad_cols(ab, LANES).astype(BF16),
        wza=cols(0), wq=cols(1), wk=cols(2), wv=cols(3), wzb=cols(4),
        alog=jnp.pad(a_log, (0, LANES - GDN_HEADS)).reshape(1, LANES).astype(F32),
        dtb=jnp.pad(dt_bias, (0, LANES - GDN_HEADS)).reshape(1, LANES).astype(F32),
    )
    ns = 3 * GDN_HEADS
    cw = jnp.pad(conv_w.astype(F32), ((0, 8 - CONV_TAPS), (0, 0))).reshape(8, ns, LANES).transpose(1, 0, 2)
    dm = BAND_HEADS * BAND_HD
    head_of = np.arange(dm) // BAND_HD
    seg = jnp.asarray(head_of[:, None] == np.arange(LANES)[None, :], BF16)
    qg = jnp.tile(q_g.astype(F32), BAND_HEADS).reshape(1, dm)
    kg = jnp.tile(k_g.astype(F32), BAND_HEADS).reshape(1, dm)
    return w, cw, seg, qg, kg


def _band_bias_table(rel_bias, L, chunk):
    W = BAND_PAST + L
    n = np.arange(W + L - 1)
    idx = np.clip(BAND_PAST + (L - 1) - n, -BAND_MAX_REL, BAND_MAX_REL) + BAND_MAX_REL
    strip = rel_bias.astype(F32)[:, idx]
    bias = jnp.stack([strip[:, L - 1 - a:L - 1 - a + W] for a in range(L)], axis=1)
    back = np.arange(L)[:, None] // chunk - (np.arange(W)[None, :] - BAND_PAST) // chunk
    readable = (back >= 0) & (back <= BAND_PAST // chunk)
    return bias if readable.all() else jnp.where(jnp.asarray(readable)[None], bias, -jnp.inf)


def _prep_l1(w_in, w_gate_up, gate_bias):
    qk = GLA_HEADS * GLA_DK
    vw = GLA_HEADS * GLA_DV
    o = np.cumsum([0, qk, qk, vw, GLA_RANK, vw])
    return dict(
        wq=w_in[:, o[0]:o[1]].astype(BF16), wk=w_in[:, o[1]:o[2]].astype(BF16),
        wv=w_in[:, o[2]:o[3]].astype(BF16), wlr=_pad_cols(w_in[:, o[3]:o[4]], LANES).astype(BF16),
        wz=w_in[:, o[4]:o[5]].astype(BF16),
        wg=jnp.pad(w_gate_up, ((0, LANES - GLA_RANK), (0, 0))).astype(BF16),
        gbias=gate_bias.reshape(1, qk).astype(F32),
    )


def _layer0(x, conv_state, gdn_state, band_caches, mk, mv, norm_g, pw, cw, seg, qg, kg, bias,
            a_onorm_g, w_out, mnorm_g, w_mq, mq_g, w_mo, L):
    b, s, d = x.shape
    outs = _proj0(x.reshape(b * s, d), norm_g.reshape(1, d), pw, qg, kg, seg, s)
    u, gb, za, qlo, qhi, kn, klast, vlo, vhi, vlast, zb = (t.reshape((b, -1) + t.shape[1:]) for t in outs)
    cinit = jnp.pad(conv_state.astype(F32), ((0, 0), (8 - (CONV_TAPS - 1), 0), (0, 0)))
    o_a, s_new = _gdn(u, gb, za, cinit, gdn_state.astype(F32), cw, a_onorm_g.reshape(1, LANES), L)
    o_b = _band(qlo, qhi, kn, vlo, vhi, zb, bias, band_caches, bias.shape[1])
    y = _post(x, (o_a, o_b), w_out, mnorm_g.reshape(1, d), w_mq, mq_g.reshape(1, MEM_HD), mk, mv, w_mo)
    return y, u[:, s - (CONV_TAPS - 1):, :], s_new, klast, vlast


def _layer1(x, gla_state, mk, mv, norm_g, pw, c_onorm_g, w_out, mnorm_g, w_mq, mq_g, w_mo, L):
    b, s, d = x.shape
    q, k, v, lg, z = _proj1(x.reshape(b * s, d), norm_g.reshape(1, d), pw)
    r3 = lambda t: t.reshape(b, s, t.shape[-1])
    q, k, v, lg, z = map(r3, (q, k, v, lg, z))
    st0 = jnp.swapaxes(gla_state.astype(F32), 2, 3)
    o, st = _gla(q, k, v, lg, z, st0, c_onorm_g.reshape(1, GLA_DV), L)
    y = _post(x, (o,), w_out, mnorm_g.reshape(1, d), w_mq, mq_g.reshape(1, MEM_HD), mk, mv, w_mo)
    return y, jnp.swapaxes(st, 2, 3)


def kernel(x_prompt, x_sample, mem_prompt, state_l0_gdn_conv, state_l0_gdn, cache_l0_band_k, cache_l0_band_v, cache_l0_mem_k, cache_l0_mem_v, state_l1_gla, cache_l1_mem_k, cache_l1_mem_v, l0_norm_g, l0_w_in, l0_conv_w, l0_a_log, l0_dt_bias, l0_a_onorm_g, l0_b_q_g, l0_b_k_g, l0_b_rel_bias, l0_w_out, l0_mnorm_g, l0_mem_norm_g, l0_w_mkv, l0_mk_g, l0_w_mq, l0_mq_g, l0_w_mo, l1_norm_g, l1_w_in, l1_w_gate_up, l1_gate_bias, l1_c_onorm_g, l1_w_out, l1_mnorm_g, l1_mem_norm_g, l1_w_mkv, l1_mk_g, l1_w_mq, l1_mq_g, l1_w_mo):
    bp, sp, d = x_prompt.shape
    bs, ss, _ = x_sample.shape
    nm = mem_prompt.shape[1]
    hw = MEM_HEADS * MEM_HD
    dm = BAND_HEADS * BAND_HD
    assert sp % CHUNK_ == 0 and ss % INV_SUB == 0 and ss <= CHUNK_
    assert cache_l0_band_k.shape[1] == BAND_PAST

    pw0, cw, seg, qg, kg = _prep_l0(l0_w_in, l0_conv_w, l0_a_log, l0_dt_bias, l0_b_q_g, l0_b_k_g)
    pw1 = _prep_l1(l1_w_in, l1_w_gate_up, l1_gate_bias)
    band_rows = BAND_STEP_CHUNKS * CHUNK_ if sp % (BAND_STEP_CHUNKS * CHUNK_) == 0 else CHUNK_
    bias_p = _band_bias_table(l0_b_rel_bias, band_rows, CHUNK_)
    bias_s = _band_bias_table(l0_b_rel_bias, ss, ss)
    bf = lambda w: w.astype(BF16)
    mem2 = mem_prompt.reshape(bp * nm, d)

    p_mk0, p_mv0, mk0, mv0 = _memkv(mem2, l0_mem_norm_g.reshape(1, d), bf(l0_w_mkv), l0_mk_g.reshape(1, MEM_HD))
    mk0 = mk0.reshape(bp, nm, hw)
    mv0 = mv0.reshape(bp, nm, hw)
    l0_shared = (l0_norm_g, pw0, cw, seg, qg, kg)
    l0_tail = (l0_a_onorm_g, bf(l0_w_out), l0_mnorm_g, bf(l0_w_mq), l0_mq_g, bf(l0_w_mo))
    zeros_conv = jnp.zeros((bp, CONV_TAPS - 1, 3 * GDN_HEADS * LANES), F32)
    zeros_gdn = jnp.zeros((bp, GDN_HEADS, GDN_DK, LANES), F32)
    yp, p_conv, p_gdn, p_kn, p_v = _layer0(
        x_prompt, zeros_conv, zeros_gdn, None, mk0, mv0,
        *l0_shared, bias_p, *l0_tail, CHUNK_)
    ys, s_conv, s_gdn, s_kn, s_v = _layer0(
        x_sample, state_l0_gdn_conv, state_l0_gdn,
        (cache_l0_band_k.reshape(bs, BAND_PAST, dm), cache_l0_band_v.reshape(bs, BAND_PAST, dm)),
        cache_l0_mem_k.reshape(bs, nm, hw), cache_l0_mem_v.reshape(bs, nm, hw),
        *l0_shared, bias_s, *l0_tail, ss)

    p_mk1, p_mv1, mk1, mv1 = _memkv(mem2, l1_mem_norm_g.reshape(1, d), bf(l1_w_mkv), l1_mk_g.reshape(1, MEM_HD))
    mk1 = mk1.reshape(bp, nm, hw)
    mv1 = mv1.reshape(bp, nm, hw)
    l1_tail = (l1_c_onorm_g, bf(l1_w_out), l1_mnorm_g, bf(l1_w_mq), l1_mq_g, bf(l1_w_mo))
    zeros_gla = jnp.zeros((bp, GLA_HEADS, GLA_DK, GLA_DV), F32)
    yp, p_gla = _layer1(yp, zeros_gla, mk1, mv1, l1_norm_g, pw1, *l1_tail, CHUNK_)
    ys, s_gla = _layer1(ys, state_l1_gla, cache_l1_mem_k.reshape(bs, nm, hw),
                        cache_l1_mem_v.reshape(bs, nm, hw), l1_norm_g, pw1, *l1_tail, ss)

    m4 = lambda t: t.reshape(bp, nm, MEM_HEADS, MEM_HD)
    return (yp, ys, p_conv, p_gdn, p_kn, p_v,
            m4(p_mk0), m4(p_mv0), p_gla, m4(p_mk1), m4(p_mv1),
            s_conv, s_gdn, s_kn, s_v, s_gla)
```

```python
import functools

import jax
import jax.numpy as jnp
import numpy as np
from jax import lax
from jax.experimental import pallas as pl
from jax.experimental.pallas import tpu as pltpu

F32 = jnp.float32
BF16 = jnp.bfloat16
NORM_EPS = 1e-6
LOG2_E = 1.4426950408889634

CHUNK_ = 64
CONV_TAPS = 4
GDN_HEADS = 8
GDN_DK = 128
BAND_HEADS = 16
BAND_HD = 64
BAND_PAST = 512
BAND_MAX_REL = 128
GLA_HEADS = 8
GLA_DK = 128
GLA_DV = 256
GLA_RANK = 16
GLA_TAU = 16.0
MEM_HEADS = 4
MEM_HD = 128
INV_SUB = 16
GDN_GROUP = 2
GDN_STEP_CHUNKS = 1
GLA_GROUP = 2
POST_SPLIT = 4
PROJ_SPLIT = 2
SUB_TILE_MIN = 128
BAND_STEP_CHUNKS = 2
LANES = 128
VMEM_LIMIT = 56 * 1024 * 1024


def _dot(a, b):
    return jnp.dot(a, b, preferred_element_type=F32)


def _dot_nt(a, b):
    return lax.dot_general(a, b, (((1,), (1,)), ((), ())), preferred_element_type=F32)


def _dot_tn(a, b):
    return lax.dot_general(a, b, (((0,), (0,)), ((), ())), preferred_element_type=F32)


def _split3(x):
    hi = x.astype(BF16)
    r = x - hi.astype(F32)
    mid = r.astype(BF16)
    lo = (r - mid.astype(F32)).astype(BF16)
    return hi, mid, lo


def _dot_exact_lhs(a_bf, b):
    h, m, l = _split3(b)
    return _dot(a_bf, h) + (_dot(a_bf, m) + _dot(a_bf, l))


def _rms(x, g):
    ms = jnp.mean(x * x, axis=-1, keepdims=True)
    return x * lax.rsqrt(ms + NORM_EPS) * g


def _silu(x):
    return x * jax.nn.sigmoid(x)


def _softplus(x):
    return jnp.maximum(x, 0.0) + jnp.log1p(jnp.exp(-jnp.abs(x)))


def _softmax2_rows(s):
    m = jnp.max(s, axis=-1, keepdims=True)
    e = jnp.exp2(s - m)
    return e * (1.0 / jnp.sum(e, axis=-1, keepdims=True))


def _const_spec(shape):
    nd = len(shape)
    return pl.BlockSpec(shape, lambda *_: (0,) * nd, pipeline_mode=pl.Buffered(1))


def _params(sem):
    return pltpu.CompilerParams(dimension_semantics=sem, vmem_limit_bytes=VMEM_LIMIT)


def _row_tile(n, want):
    t = min(n, want)
    assert n % t == 0
    return t


def _sub_tiles(tm):
    ts = tm // PROJ_SPLIT if tm % (SUB_TILE_MIN * PROJ_SPLIT) == 0 else tm
    return [slice(t * ts, (t + 1) * ts) for t in range(tm // ts)]


def _proj0_body(x_ref, g_ref, wu_ref, wab_ref, wza_ref, wq_ref, wk_ref, wv_ref, wzb_ref,
                alog_ref, dtb_ref, qg_ref, kg_ref, seg_ref,
                u_ref, gb_ref, za_ref, qlo_ref, qhi_ref, kn_ref, klast_ref, vlo_ref, vhi_ref, vlast_ref, zb_ref,
                *, kept_tiles):
    h = _rms(x_ref[...], g_ref[...]).astype(BF16)
    dm = BAND_HEADS * BAND_HD
    lo = (lax.broadcasted_iota(jnp.int32, (1, dm), 1) & (LANES - 1)) < BAND_HD
    seg = seg_ref[...]
    q = _dot(h, wq_ref[...])
    k = _dot(h, wk_ref[...])
    ab = _dot(h, wab_ref[...])
    za = _dot(h, wza_ref[...])

    def head_rsqrt(x):
        ss = _dot((x * x).astype(BF16), seg)
        return lax.rsqrt(ss * (1.0 / BAND_HD) + NORM_EPS)

    rq = head_rsqrt(q)
    rk = head_rsqrt(k)
    zb = _dot(h, wzb_ref[...])
    v = _dot(h, wv_ref[...])
    u_ref[...] = _dot(h, wu_ref[...])
    za_ref[...] = _silu(za)
    zb_ref[...] = _silu(zb)
    lo1 = lo[:, :LANES]

    def spread(r):
        return jnp.concatenate([jnp.where(lo1, r[:, 2 * s:2 * s + 1], r[:, 2 * s + 1:2 * s + 2])
                                for s in range(BAND_HEADS // 2)], axis=1)

    qn = q * spread(rq) * (qg_ref[...] * (BAND_HD ** -0.5 * LOG2_E))
    kn = k * spread(rk) * kg_ref[...]
    qlo_ref[...] = jnp.where(lo, qn, 0.0).astype(BF16)
    qhi_ref[...] = jnp.where(lo, 0.0, qn).astype(BF16)
    kn_ref[...] = kn.astype(BF16)
    vlo_ref[...] = jnp.where(lo, v, 0.0).astype(BF16)
    vhi_ref[...] = jnp.where(lo, 0.0, v).astype(BF16)

    def hand_on():
        klast_ref[...] = kn.reshape(kn.shape[0], BAND_HEADS, BAND_HD)
        vlast_ref[...] = v.reshape(v.shape[0], BAND_HEADS, BAND_HD)

    if kept_tiles is None:
        hand_on()
    else:
        tpb, kt = kept_tiles
        pl.when(pl.program_id(0) % tpb >= tpb - kt)(hand_on)
    lane = lax.broadcasted_iota(jnp.int32, ab.shape, 1)
    gval = -jnp.exp(alog_ref[...]) * _softplus(ab + dtb_ref[...])
    gb_ref[...] = jnp.where(lane < GDN_HEADS, gval, jax.nn.sigmoid(ab))


def _proj0(x2, g, w, qg, kg, seg, rows_per_batch):
    n, d = x2.shape
    tm = _row_tile(n, 256)
    dm = BAND_HEADS * BAND_HD
    widths = (3 * 1024, LANES, 1024, dm, dm, dm, dm)
    row = lambda wd: pl.BlockSpec((tm, wd), lambda i: (i, 0))
    keep = min(BAND_PAST, rows_per_batch)
    if rows_per_batch > keep:
        assert rows_per_batch % tm == 0 and keep % tm == 0
        tpb, kt = rows_per_batch // tm, keep // tm
        last = pl.BlockSpec((tm, BAND_HEADS, BAND_HD),
                            lambda i: ((i // tpb) * kt + jnp.maximum(i % tpb - (tpb - kt), 0), 0, 0))
        n_last = (n // rows_per_batch) * keep
        kept_tiles = (tpb, kt)
    else:
        last, n_last = pl.BlockSpec((tm, BAND_HEADS, BAND_HD), lambda i: (i, 0, 0)), n
        kept_tiles = None
    last_shape = jax.ShapeDtypeStruct((n_last, BAND_HEADS, BAND_HD), F32)
    f32 = lambda rows, wd: jax.ShapeDtypeStruct((rows, wd), F32)
    bf16 = lambda wd: jax.ShapeDtypeStruct((n, wd), BF16)
    return pl.pallas_call(
        functools.partial(_proj0_body, kept_tiles=kept_tiles),
        grid=(n // tm,),
        in_specs=[row(d), _const_spec((1, d))]
        + [_const_spec((d, wd)) for wd in widths]
        + [_const_spec((1, LANES)), _const_spec((1, LANES)), _const_spec((1, dm)), _const_spec((1, dm)),
           _const_spec((dm, LANES))],
        out_specs=[row(3 * 1024), row(LANES), row(1024), row(dm), row(dm), row(dm), last, row(dm), row(dm), last,
                   row(dm)],
        out_shape=[f32(n, 3 * 1024), f32(n, LANES), f32(n, 1024), bf16(dm), bf16(dm), bf16(dm), last_shape,
                   bf16(dm), bf16(dm), last_shape, f32(n, dm)],
        compiler_params=_params(("arbitrary",)),
        name="proj0",
    )(x2, g, w["wu"], w["wab"], w["wza"], w["wq"], w["wk"], w["wv"], w["wzb"], w["alog"], w["dtb"],
      qg, kg, seg)


def _proj1_body(x_ref, g_ref, wq_ref, wk_ref, wv_ref, wlr_ref, wz_ref, wg_ref, gbias_ref,
                q_ref, k_ref, v_ref, lg_ref, z_ref):
    rows = _sub_tiles(x_ref.shape[0])
    h = [_rms(x_ref[r, :], g_ref[...]).astype(BF16) for r in rows]
    lr = [_dot(h_, wlr_ref[...]) for h_ in h]
    for r, h_ in zip(rows, h):
        q_ref[r, :] = _dot(h_, wq_ref[...])
    pre = [_dot(x.astype(BF16), wg_ref[...]) + gbias_ref[...] for x in lr]
    zs = [_dot(h_, wz_ref[...]) for h_ in h]
    for r, p_ in zip(rows, pre):
        lg_ref[r, :] = -_softplus(-p_) * (1.0 / GLA_TAU)
    for r, h_ in zip(rows, h):
        k_ref[r, :] = _dot(h_, wk_ref[...])
    for r, z_ in zip(rows, zs):
        z_ref[r, :] = _silu(z_)
    for r, h_ in zip(rows, h):
        v_ref[r, :] = _dot(h_, wv_ref[...])


def _proj1(x2, g, w):
    n, d = x2.shape
    tm = _row_tile(n, 512)
    row = lambda wd: pl.BlockSpec((tm, wd), lambda i: (i, 0))
    outw = (1024, 1024, 2048, 1024, 2048)
    return pl.pallas_call(
        _proj1_body,
        grid=(n // tm,),
        in_specs=[row(d), _const_spec((1, d)), _const_spec((d, 1024)), _const_spec((d, 1024)),
                  _const_spec((d, 2048)), _const_spec((d, LANES)), _const_spec((d, 2048)),
                  _const_spec((LANES, 1024)), _const_spec((1, 1024))],
        out_specs=[row(wd) for wd in outw],
        out_shape=[jax.ShapeDtypeStruct((n, wd), F32) for wd in outw],
        compiler_params=_params(("arbitrary",)),
        name="proj1",
    )(x2, g, w["wq"], w["wk"], w["wv"], w["wlr"], w["wz"], w["wg"], w["gbias"])


def _memkv_body(m_ref, g_ref, w_ref, kg_ref, k4_ref, v4_ref, kb_ref, vb_ref):
    h = _rms(m_ref[...], g_ref[...]).astype(BF16)
    kv = _dot(h, w_ref[...])
    hw = MEM_HEADS * MEM_HD
    k = jnp.concatenate([_rms(kv[:, hh * MEM_HD:(hh + 1) * MEM_HD], kg_ref[...]) for hh in range(MEM_HEADS)],
                        axis=1)
    v = kv[:, hw:]
    k4_ref[...] = k.reshape(k.shape[0], MEM_HEADS, MEM_HD)
    v4_ref[...] = v.reshape(v.shape[0], MEM_HEADS, MEM_HD)
    kb_ref[...] = k.astype(BF16)
    vb_ref[...] = v.astype(BF16)


def _memkv(m2, g, w_bf, kg):
    n, d = m2.shape
    tm = _row_tile(n, 256)
    hw = MEM_HEADS * MEM_HD
    row = lambda wd: pl.BlockSpec((tm, wd), lambda i: (i, 0))
    row4 = pl.BlockSpec((tm, MEM_HEADS, MEM_HD), lambda i: (i, 0, 0))
    return pl.pallas_call(
        _memkv_body,
        grid=(n // tm,),
        in_specs=[row(d), _const_spec((1, d)), _const_spec((d, 2 * hw)), _const_spec((1, MEM_HD))],
        out_specs=[row4, row4, row(hw), row(hw)],
        out_shape=[jax.ShapeDtypeStruct((n, MEM_HEADS, MEM_HD), F32)] * 2
        + [jax.ShapeDtypeStruct((n, hw), BF16)] * 2,
        compiler_params=_params(("arbitrary",)),
        name="memkv",
    )(m2, g, w_bf, kg)


def _each(fn, *lists):
    return [fn(*xs) for xs in zip(*lists)]


def _unit_lower_inverse(a, eye, bd, mm):
    d = _each(lambda x: jnp.where(bd, x, 0.0), a)
    nl = _each(lambda x, y: x - y, a, d)
    d2 = _each(mm, d, d)
    d4 = _each(mm, d2, d2)
    td = _each(lambda x, y: mm(eye - x, eye + y), d, d2)
    d8 = _each(mm, d4, d4)
    td = _each(lambda x, y: mm(x, eye + y), td, d4)
    td = _each(lambda x, y: mm(x, eye + y), td, d8)
    m = _each(mm, td, nl)
    m2 = _each(mm, m, m)
    mt = _each(mm, m, td)
    return _each(lambda x, y, z: mm(eye + x, y - z), m2, td, mt)


def _gdn_body(*refs, L, NB, G, has_state):
    if has_state:
        u_ref, gb_ref, za_ref, cw_ref, og_ref, cinit_ref, sinit_ref, o_ref, s_ref, ubuf = refs
    else:
        u_ref, gb_ref, za_ref, cw_ref, og_ref, o_ref, s_ref, ubuf = refs
    c = pl.program_id(1)
    ns = 3 * GDN_HEADS
    hist = 8
    R = G * L

    @pl.when(c == 0)
    def _():
        if has_state:
            for b in range(NB):
                for j in range(ns):
                    ubuf[b * ns + j, 0:hist, :] = cinit_ref[b, :, j * LANES:(j + 1) * LANES]
            s_ref[...] = sinit_ref[...]
        else:
            ubuf[:, 0:hist, :] = jnp.zeros((NB * ns, hist, LANES), F32)
            s_ref[...] = jnp.zeros(s_ref.shape, F32)

    for b in range(NB):
        for j in range(ns):
            ubuf[b * ns + j, hist:hist + R, :] = u_ref[b, :, j * LANES:(j + 1) * LANES]
    base = hist - (CONV_TAPS - 1)
    ys = []
    for b in range(NB):
        bsl = slice(b * ns, (b + 1) * ns)
        yb = ubuf[bsl, base:base + R, :] * cw_ref[:, 0:1, :]
        for i in range(1, CONV_TAPS):
            yb = yb + ubuf[bsl, base + i:base + i + R, :] * cw_ref[:, i:i + 1, :]
        ys.append(_silu(yb))
    ubuf[:, base:hist, :] = ubuf[:, base + R:hist + R, :]

    row = lax.broadcasted_iota(jnp.int32, (L, 2 * L), 0)
    lane = lax.broadcasted_iota(jnp.int32, (L, 2 * L), 1)
    col = lane & (L - 1)
    left = lane < L
    incl = row >= col
    strict = row > col
    sub_shift = INV_SUB.bit_length() - 1
    bd = lax.shift_right_logical(row, sub_shift) == lax.shift_right_logical(col, sub_shift)
    eye = jnp.where(row == col, 1.0, 0.0).astype(F32)
    trow = lax.broadcasted_iota(jnp.int32, (L, L), 0)
    tcol = lax.broadcasted_iota(jnp.int32, (L, L), 1)
    tri = jnp.where(trow >= tcol, 1.0, 0.0).astype(BF16)
    og = og_ref[...]
    heads = [(b, g, h) for b in range(NB) for g in range(G) for h in range(GDN_HEADS)]
    pairs = range(len(heads) // 2)
    rows_of = lambda g: slice(g * L, (g + 1) * L)
    l2n = lambda x: x * lax.rsqrt(jnp.sum(x * x, axis=-1, keepdims=True) + NORM_EPS)
    q = [l2n(ys[b][h][rows_of(g)]) * (GDN_DK ** -0.5) for b, g, h in heads]
    k = [l2n(ys[b][GDN_HEADS + h][rows_of(g)]) for b, g, h in heads]
    v = [ys[b][2 * GDN_HEADS + h][rows_of(g)] for b, g, h in heads]
    gc, gr, bc = [], [], []
    for b in range(NB):
        for g in range(G):
            gbv = gb_ref[b, rows_of(g), :]
            gcum = _dot_exact_lhs(tri, gbv)
            gpad = jnp.concatenate([gcum, jnp.zeros((LANES - L, LANES), F32)], axis=0)
            gt = gpad.T
            for h in range(GDN_HEADS):
                gc.append(gcum[:, h:h + 1])
                gr.append(gt[h:h + 1, 0:L])
                bc.append(gbv[:, GDN_HEADS + h:GDN_HEADS + h + 1])
    side = lambda x0, x1: jnp.where(left, x0, x1)
    gcp = [side(gc[2 * p], gc[2 * p + 1]) for p in pairs]
    grp = [jnp.concatenate([gr[2 * p], gr[2 * p + 1]], axis=1) for p in pairs]
    bcp = [side(bc[2 * p], bc[2 * p + 1]) for p in pairs]
    dec = _each(lambda c_, r_: jnp.where(incl, jnp.exp(jnp.where(incl, c_ - r_, 0.0)), 0.0), gcp, grp)
    kb = _each(lambda x: x.astype(BF16), k)
    qb = _each(lambda x: x.astype(BF16), q)

    def blockdiag(y0, y1):
        z0 = jnp.zeros(y1.shape, y1.dtype)
        z1 = jnp.zeros(y0.shape, y0.dtype)
        return jnp.concatenate([jnp.concatenate([y0, z0], axis=1), jnp.concatenate([z1, y1], axis=1)], axis=0)

    def mm(x, y):
        yb = y.astype(BF16)
        zero = jnp.zeros_like(yb)
        return _dot(x.astype(BF16), jnp.concatenate([jnp.where(left, yb, zero), jnp.where(left, zero, yb)], axis=0))

    kq = [_dot_nt(jnp.concatenate([jnp.concatenate([kb[2 * p], qb[2 * p]], axis=0),
                                   jnp.concatenate([kb[2 * p + 1], qb[2 * p + 1]], axis=0)], axis=1),
                  blockdiag(kb[2 * p], kb[2 * p + 1])) for p in pairs]
    a = _each(lambda b_, x, d_: jnp.where(strict, b_ * x[:L] * d_, 0.0), bcp, kq, dec)
    t = _unit_lower_inverse(a, eye, bd, mm)
    eg = _each(jnp.exp, gc)
    rhs = _each(lambda b_, v_, e_, k_: jnp.concatenate([b_ * v_, (b_ * e_) * k_], axis=1).astype(BF16),
                bc, v, eg, k)
    sol = [_dot(t[p].astype(BF16), blockdiag(rhs[2 * p], rhs[2 * p + 1])) for p in pairs]
    solk = lambda i: sol[i // 2][:, 2 * (i % 2) * LANES + LANES:2 * (i % 2 + 1) * LANES]
    solv = lambda i: sol[i // 2][:, 2 * (i % 2) * LANES:2 * (i % 2) * LANES + LANES]
    qkd = [(kq[p][L:] * dec[p]).astype(BF16) for p in pairs]
    gl = [c_[L - 1:L, :] for c_ in gc]
    kd = _each(lambda k_, l_, c_: (k_ * jnp.exp(l_ - c_)).astype(BF16), k, gl, gc)

    item = lambda b, g, h: (b * G + g) * GDN_HEADS + h
    half = lambda x, i: x[:, (i % 2) * LANES:(i % 2 + 1) * LANES]
    s = {(b, h): s_ref[b, h] for b in range(NB) for h in range(GDN_HEADS)}
    for g in range(G):
        ids = [item(b, g, h) for b in range(NB) for h in range(GDN_HEADS)]
        sb = {i: s[(heads[i][0], heads[i][2])].astype(BF16) for i in ids}
        ksq = {i: _dot(jnp.concatenate([jnp.concatenate([solk(i).astype(BF16), qb[i]], axis=0),
                                        jnp.concatenate([solk(i + 1).astype(BF16), qb[i + 1]], axis=0)], axis=1),
                       blockdiag(sb[i], sb[i + 1])) for i in ids[::2]}
        ub = {i: (solv(i) - half(ksq[i - i % 2][:L], i)).astype(BF16) for i in ids}
        qku = {i: _dot(qkd[i // 2], blockdiag(ub[i], ub[i + 1])) for i in ids[::2]}
        ktu = {i: _dot_tn(kd[i], ub[i]) for i in ids}
        for i in ids:
            b, _, h = heads[i]
            s[(b, h)] = s[(b, h)] * jnp.exp(gl[i]) + ktu[i]
        for b in range(NB):
            za = za_ref[b, rows_of(g), :]
            outs = [(_rms(half(ksq[i - i % 2][L:], i) * eg[i] + half(qku[i - i % 2], i), og)
                     * za[:, heads[i][2] * LANES:(heads[i][2] + 1) * LANES]).astype(BF16)
                    for i in ids if heads[i][0] == b]
            o_ref[b, rows_of(g), :] = jnp.concatenate(outs, axis=1)
    for (b, h), val in s.items():
        s_ref[b, h] = val


def _gdn(u, gb, za, state, cw, og, L):
    b, s, _ = u.shape
    nc = s // L
    ns = 3 * GDN_HEADS
    nb = GDN_GROUP if b % GDN_GROUP == 0 else 1
    g = GDN_STEP_CHUNKS if nc % GDN_STEP_CHUNKS == 0 else 1
    blk = lambda wd: pl.BlockSpec((nb, g * L, wd), lambda i, j: (i, j, 0))
    has_state = state is not None
    state_specs = [pl.BlockSpec((nb, 8, ns * LANES), lambda i, j: (i, 0, 0)),
                   pl.BlockSpec((nb, GDN_HEADS, GDN_DK, LANES), lambda i, j: (i, 0, 0, 0))]
    return pl.pallas_call(
        functools.partial(_gdn_body, L=L, NB=nb, G=g, has_state=has_state),
        grid=(b // nb, nc // g),
        in_specs=[blk(ns * LANES), blk(LANES), blk(GDN_HEADS * LANES),
                  _const_spec((ns, 8, LANES)), _const_spec((1, LANES))] + (state_specs if has_state else []),
        out_specs=[blk(GDN_HEADS * LANES),
                   pl.BlockSpec((nb, GDN_HEADS, GDN_DK, LANES), lambda i, j: (i, 0, 0, 0))],
        out_shape=[jax.ShapeDtypeStruct((b, s, GDN_HEADS * LANES), BF16),
                   jax.ShapeDtypeStruct((b, GDN_HEADS, GDN_DK, LANES), F32)],
        scratch_shapes=[pltpu.VMEM((nb * ns, 8 + g * L, LANES), F32)],
        compiler_params=_params(("arbitrary", "arbitrary")),
        name="gdn",
    )(u, gb, za, cw, og, *(state if has_state else ()))


def _band_body(*refs, L, W, has_cache):
    if has_cache:
        (qlo_ref, qhi_ref, k_ref, vlo_ref, vhi_ref, zb_ref, bias_ref, ck_ref, cv_ref,
         o_ref, kscr, vlo, vhi, biasm) = refs
    else:
        (qlo_ref, qhi_ref, k_ref, vlo_ref, vhi_ref, zb_ref, bias_ref,
         o_ref, kscr, vlo, vhi, biasm) = refs
    c = pl.program_id(1)
    dm = BAND_HEADS * BAND_HD
    pad_chunks = BAND_PAST // L

    @pl.when(c == 0)
    def _():
        if has_cache:
            lo = (lax.broadcasted_iota(jnp.int32, (1, dm), 1) & (LANES - 1)) < BAND_HD
            kscr[0:BAND_PAST, :] = ck_ref[0].astype(BF16)
            cv = cv_ref[0]
            vlo[0:BAND_PAST, :] = jnp.where(lo, cv, 0.0).astype(BF16)
            vhi[0:BAND_PAST, :] = jnp.where(lo, 0.0, cv).astype(BF16)
        else:
            zero = jnp.zeros((BAND_PAST, dm), BF16)
            kscr[0:BAND_PAST, :] = zero
            vlo[0:BAND_PAST, :] = zero
            vhi[0:BAND_PAST, :] = zero

    new0 = pl.multiple_of(BAND_PAST + c * L, L)
    kscr[pl.ds(new0, L), :] = k_ref[0]
    vlo[pl.ds(new0, L), :] = vlo_ref[0]
    vhi[pl.ds(new0, L), :] = vhi_ref[0]

    if has_cache:
        @pl.when(c == 0)
        def _():
            biasm[...] = bias_ref[...] * LOG2_E
    else:
        @pl.when(c < pad_chunks)
        def _():
            wcol = lax.broadcasted_iota(jnp.int32, (1, W), 1)
            valid = (wcol + c * L) >= BAND_PAST
            for h in range(BAND_HEADS):
                biasm[h] = jnp.where(valid, bias_ref[h] * LOG2_E, -jnp.inf)

        @pl.when(c == pad_chunks)
        def _():
            biasm[...] = bias_ref[...] * LOG2_E
    bias_src = biasm

    w0 = pl.multiple_of(c * L, L)
    zb = zb_ref[0]
    slabs = [slice(s * LANES, (s + 1) * LANES) for s in range(BAND_HEADS // 2)]
    sc = []
    for sl in slabs:
        ks = kscr[pl.ds(w0, W), sl]
        sc.append(_dot_nt(qlo_ref[0, :, sl], ks))
        sc.append(_dot_nt(qhi_ref[0, :, sl], ks))
    ps, rs = [], []
    for h in range(BAND_HEADS):
        x = sc[h] + bias_src[h]
        e = jnp.exp2(x - jnp.max(x, axis=-1, keepdims=True))
        rs.append(1.0 / jnp.sum(e, axis=-1, keepdims=True))
        ps.append(e.astype(BF16))
    pv = []
    for i, sl in enumerate(slabs):
        pv.append(_dot(ps[2 * i], vlo[pl.ds(w0, W), sl]))
        pv.append(_dot(ps[2 * i + 1], vhi[pl.ds(w0, W), sl]))
    outs = [((pv[2 * i] * rs[2 * i] + pv[2 * i + 1] * rs[2 * i + 1]) * zb[:, sl]).astype(BF16)
            for i, sl in enumerate(slabs)]
    o_ref[0] = jnp.concatenate(outs, axis=1)


def _band(qlo, qhi, kn, vlo, vhi, zb, bias, caches, L):
    b, s, dm = qlo.shape
    nc = s // L
    W = BAND_PAST + L
    blk = pl.BlockSpec((1, L, dm), lambda i, j: (i, j, 0))
    cache = pl.BlockSpec((1, BAND_PAST, dm), lambda i, j: (i, 0, 0))
    has_cache = caches is not None
    scratch = [pltpu.VMEM((BAND_PAST + s, dm), BF16)] * 3 + [pltpu.VMEM((BAND_HEADS, L, W), F32)]
    return pl.pallas_call(
        functools.partial(_band_body, L=L, W=W, has_cache=has_cache),
        grid=(b, nc),
        in_specs=[blk] * 6 + [_const_spec((BAND_HEADS, L, W))] + ([cache, cache] if has_cache else []),
        out_specs=blk,
        out_shape=jax.ShapeDtypeStruct((b, s, dm), BF16),
        scratch_shapes=scratch,
        compiler_params=_params(("arbitrary", "arbitrary")),
        name="band",
    )(qlo, qhi, kn, vlo, vhi, zb, bias, *(caches if has_cache else ()))


def _gla_body(*refs, L, NB, has_state):
    if has_state:
        q_ref, k_ref, v_ref, lg_ref, z_ref, og_ref, sinit_ref, o_ref, sout_ref, st_ref, cbs = refs
    else:
        q_ref, k_ref, v_ref, lg_ref, z_ref, og_ref, o_ref, sout_ref, st_ref, cbs = refs
    c = pl.program_id(1)

    @pl.when(c == 0)
    def _():
        for b in range(NB):
            for h in range(GLA_HEADS):
                st_ref[b, h] = sinit_ref[b, h].T if has_state else jnp.zeros((GLA_DV, GLA_DK), F32)

    levels = [L >> (t + 1) for t in range(L.bit_length() - 1)]
    small = [s for s in levels if 2 * s < 16]
    row = lax.broadcasted_iota(jnp.int32, (L, L), 0)
    col = lax.broadcasted_iota(jnp.int32, (L, L), 1)
    anchor = lambda s: lax.shift_left(lax.shift_right_logical(row, s.bit_length()), s.bit_length()) + (s - 1)
    between = lambda s: (col > jnp.minimum(row, anchor(s))) & (col <= jnp.maximum(row, anchor(s)))
    onehot = lambda m: jnp.where(m, 1.0, 0.0).astype(BF16)
    tri = onehot(row >= col)
    stack = jnp.concatenate([onehot(between(s)) for s in small], axis=0)
    ex = jnp.exp2
    og = og_ref[...]
    scale = GLA_DK ** -0.5
    pieces = [_split3(lg_ref[b] * LOG2_E) for b in range(NB)]
    cb_all = [_dot(tri, p[0]) + (_dot(tri, p[1]) + _dot(tri, p[2])) for p in pieces]
    y_small = [_dot(stack, p[0]) + _dot(stack, p[1]) for p in pieces]
    ydec = [dict() for _ in range(NB)]
    for b in range(NB):
        cbs[b] = cb_all[b]
        for t, s in enumerate(small):
            ydec[b][s] = y_small[b][t * L:(t + 1) * L]
        for s in levels:
            if s not in ydec[b]:
                anc = jnp.concatenate(
                    [jnp.broadcast_to(cbs[b, p * 2 * s + s - 1:p * 2 * s + s, :], (2 * s, cbs.shape[2]))
                     for p in range(L // (2 * s))], axis=0)
                ydec[b][s] = -jnp.abs(cb_all[b] - anc)

    heads = [(b, h) for b in range(NB) for h in range(GLA_HEADS)]
    pairs = range(len(heads) // 2)
    ksl = lambda h: slice(h * GLA_DK, (h + 1) * GLA_DK)
    vsl = lambda h: slice(h * GLA_DV, (h + 1) * GLA_DV)
    cb = [cb_all[b][:, ksl(h)] for b, h in heads]
    q = [q_ref[b, :, ksl(h)] * scale for b, h in heads]
    k = [k_ref[b, :, ksl(h)] for b, h in heads]
    vb = [v_ref[b, :, vsl(h)].astype(BF16) for b, h in heads]
    st = [st_ref[b, h] for b, h in heads]
    o_inter = _each(lambda q_, c_, s_: _dot_nt((q_ * ex(c_)).astype(BF16), s_.astype(BF16)), q, cb, st)
    cl = [c_[L - 1:L] for c_ in cb]
    ke = _each(lambda k_, l_, c_: (k_ * ex(l_ - c_)).astype(BF16), k, cl, cb)
    vtk = _each(_dot_tn, vb, ke)

    zk = jnp.zeros((L, GLA_DK), BF16)
    zv = jnp.zeros((L, GLA_DV), BF16)

    def side_by_side(qs, ks):
        lhs = jnp.concatenate(qs, axis=1).astype(BF16)
        rhs = jnp.concatenate([jnp.concatenate([ks[0].astype(BF16), zk], axis=1),
                               jnp.concatenate([zk, ks[1].astype(BF16)], axis=1)], axis=0)
        return _dot_nt(lhs, rhs)

    prow = lax.broadcasted_iota(jnp.int32, (L, 2 * L), 0)
    pcol = lax.broadcasted_iota(jnp.int32, (L, 2 * L), 1) & (L - 1)
    att = [jnp.where(prow == pcol, side_by_side((q[2 * p], q[2 * p + 1]), (k[2 * p], k[2 * p + 1])), 0.0)
           for p in pairs]
    for s in levels:
        sh = s.bit_length() - 1
        same_parent = lax.shift_right_logical(prow, sh + 1) == lax.shift_right_logical(pcol, sh + 1)
        take = same_parent & ((lax.shift_right_logical(prow, sh) & 1) == 1) & \
            ((lax.shift_right_logical(pcol, sh) & 1) == 0)
        f = [ex(ydec[b][s][:, ksl(h)]) for b, h in heads]
        prod = [side_by_side((q[2 * p] * f[2 * p], q[2 * p + 1] * f[2 * p + 1]),
                             (k[2 * p] * f[2 * p], k[2 * p + 1] * f[2 * p + 1])) for p in pairs]
        att = [jnp.where(take, prod[p], att[p]) for p in pairs]
    vpair = [jnp.concatenate([jnp.concatenate([vb[2 * p], zv], axis=1),
                              jnp.concatenate([zv, vb[2 * p + 1]], axis=1)], axis=0) for p in pairs]
    o_intra = [_dot(att[p].astype(BF16), vpair[p]) for p in pairs]
    new_st = [st[i] * ex(cl[i]) + vtk[i] for i in range(len(heads))]
    for i, (b, h) in enumerate(heads):
        st_ref[b, h] = new_st[i]
    for b in range(NB):
        z = z_ref[b]
        outs = [(_rms(o_inter[i] + o_intra[i // 2][:, (i % 2) * GLA_DV:(i % 2 + 1) * GLA_DV], og)
                 * z[:, vsl(h)]).astype(BF16) for i, (b_, h) in enumerate(heads) if b_ == b]
        o_ref[b] = jnp.concatenate(outs, axis=1)

    @pl.when(c == pl.num_programs(1) - 1)
    def _():
        for i, (b, h) in enumerate(heads):
            sout_ref[b, h] = new_st[i].T


def _gla(q, k, v, lg, z, sinit, og, L):
    b, s, _ = q.shape
    nc = s // L
    nb = GLA_GROUP if b % GLA_GROUP == 0 else 1
    qk = pl.BlockSpec((nb, L, GLA_HEADS * GLA_DK), lambda i, j: (i, j, 0))
    vv = pl.BlockSpec((nb, L, GLA_HEADS * GLA_DV), lambda i, j: (i, j, 0))
    st = pl.BlockSpec((nb, GLA_HEADS, GLA_DK, GLA_DV), lambda i, j: (i, 0, 0, 0))
    has_state = sinit is not None
    return pl.pallas_call(
        functools.partial(_gla_body, L=L, NB=nb, has_state=has_state),
        grid=(b // nb, nc),
        in_specs=[qk, qk, vv, qk, vv, _const_spec((1, GLA_DV))] + ([st] if has_state else []),
        out_specs=[vv, st],
        out_shape=[jax.ShapeDtypeStruct((b, s, GLA_HEADS * GLA_DV), BF16),
                   jax.ShapeDtypeStruct((b, GLA_HEADS, GLA_DK, GLA_DV), F32)],
        scratch_shapes=[pltpu.VMEM((nb, GLA_HEADS, GLA_DV, GLA_DK), F32),
                        pltpu.VMEM((nb, L, GLA_HEADS * GLA_DK), F32)],
        compiler_params=_params(("arbitrary", "arbitrary")),
        name="gla",
    )(q, k, v, lg, z, og, *((sinit,) if has_state else ()))


def _post_body(*refs, n_o):
    x_ref = refs[0]
    o_refs = refs[1:1 + n_o]
    wout_ref, mg_ref, wmq_ref, mqg_ref, mk_ref, mv_ref, wmo_ref, y_ref = refs[1 + n_o:]
    tm = x_ref.shape[1]
    ts = tm // POST_SPLIT if tm % (SUB_TILE_MIN * POST_SPLIT) == 0 else tm
    rows = [slice(t * ts, (t + 1) * ts) for t in range(tm // ts)]
    acc = [x_ref[0, r, :] for r in rows]
    off = 0
    for o_ref in o_refs:
        kd = o_ref.shape[-1]
        w = wout_ref[off:off + kd, :]
        acc = [a + _dot(o_ref[0, r, :], w) for a, r in zip(acc, rows)]
        off += kd
    hm = [_rms(a, mg_ref[...]).astype(BF16) for a in acc]
    qz = [_dot(h_, wmq_ref[...]) for h_ in hm]
    hw = MEM_HEADS * MEM_HD
    sls = [slice(h * MEM_HD, (h + 1) * MEM_HD) for h in range(MEM_HEADS)]
    mkb = [mk_ref[0, :, sl].astype(BF16) for sl in sls]
    mvb = [mv_ref[0, :, sl].astype(BF16) for sl in sls]
    qscale = MEM_HD ** -0.5 * LOG2_E
    qn = [[(_rms(z[:, sl], mqg_ref[...]) * qscale).astype(BF16) for sl in sls] for z in qz]
    sc = [[_dot_nt(qh, kh) for qh, kh in zip(qt, mkb)] for qt in qn]
    p = [[_softmax2_rows(x).astype(BF16) for x in st] for st in sc]
    oh = [[_dot(ph, vh) for ph, vh in zip(pt, mvb)] for pt in p]
    for t, r in enumerate(rows):
        outs = [oh[t][h] * _silu(qz[t][:, hw + h * MEM_HD:hw + (h + 1) * MEM_HD]) for h in range(MEM_HEADS)]
        oh[t] = jnp.concatenate(outs, axis=1).astype(BF16)
    ym = [_dot(om, wmo_ref[...]) for om in oh]
    for t, r in enumerate(rows):
        y_ref[0, r, :] = acc[t] + ym[t]


def _post(x, os_, wout, mg, wmq, mqg, mk, mv, wmo):
    b, s, d = x.shape
    tm = _row_tile(s, 256 * POST_SPLIT)
    hw = MEM_HEADS * MEM_HD
    nm = mk.shape[1]
    blk = lambda wd: pl.BlockSpec((1, tm, wd), lambda i, j: (i, j, 0))
    mem = pl.BlockSpec((1, nm, hw), lambda i, j: (i, 0, 0))
    kin = sum(o.shape[-1] for o in os_)
    return pl.pallas_call(
        functools.partial(_post_body, n_o=len(os_)),
        grid=(b, s // tm),
        in_specs=[blk(d)] + [blk(o.shape[-1]) for o in os_]
        + [_const_spec((kin, d)), _const_spec((1, d)), _const_spec((d, 2 * hw)), _const_spec((1, MEM_HD)),
           mem, mem, _const_spec((hw, d))],
        out_specs=blk(d),
        out_shape=jax.ShapeDtypeStruct((b, s, d), F32),
        compiler_params=_params(("arbitrary", "arbitrary")),
        name="post",
    )(x, *os_, wout, mg, wmq, mqg, mk, mv, wmo)


def _pad_cols(w, n):
    return jnp.pad(w, ((0, 0), (0, n - w.shape[1])))


def _prep_l0(w_in, conv_w, a_log, dt_bias, q_g, k_g):
    c0 = 3 * 1024
    ab = w_in[:, c0:c0 + 2 * GDN_HEADS]
    c1 = c0 + 2 * GDN_HEADS
    cols = lambda i: w_in[:, c1 + i * 1024:c1 + (i + 1) * 1024].astype(BF16)
    w = dict(
        wu=w_in[:, :c0].astype(BF16), wab=_pad_cols(ab, LANES).astype(BF16),
        wza=cols(0), wq=cols(1), wk=cols(2), wv=cols(3), wzb=cols(4),
        alog=jnp.pad(a_log, (0, LANES - GDN_HEADS)).reshape(1, LANES).astype(F32),
        dtb=jnp.pad(dt_bias, (0, LANES - GDN_HEADS)).reshape(1, LANES).astype(F32),
    )
    ns = 3 * GDN_HEADS
    cw = jnp.pad(conv_w.astype(F32), ((0, 8 - CONV_TAPS), (0, 0))).reshape(8, ns, LANES).transpose(1, 0, 2)
    dm = BAND_HEADS * BAND_HD
    head_of = np.arange(dm) // BAND_HD
    seg = jnp.asarray(head_of[:, None] == np.arange(LANES)[None, :], BF16)
    qg = jnp.tile(q_g.astype(F32), BAND_HEADS).reshape(1, dm)
    kg = jnp.tile(k_g.astype(F32), BAND_HEADS).reshape(1, dm)
    return w, cw, seg, qg, kg


def _band_bias_table(rel_bias, L, chunk):
    W = BAND_PAST + L
    n = np.arange(W + L - 1)
    idx = np.clip(BAND_PAST + (L - 1) - n, -BAND_MAX_REL, BAND_MAX_REL) + BAND_MAX_REL
    strip = rel_bias.astype(F32)[:, idx]
    bias = jnp.stack([strip[:, L - 1 - a:L - 1 - a + W] for a in range(L)], axis=1)
    back = np.arange(L)[:, None] // chunk - (np.arange(W)[None, :] - BAND_PAST) // chunk
    readable = (back >= 0) & (back <= BAND_PAST // chunk)
    return bias if readable.all() else jnp.where(jnp.asarray(readable)[None], bias, -jnp.inf)


def _prep_l1(w_in, w_gate_up, gate_bias):
    qk = GLA_HEADS * GLA_DK
    vw = GLA_HEADS * GLA_DV
    o = np.cumsum([0, qk, qk, vw, GLA_RANK, vw])
    return dict(
        wq=w_in[:, o[0]:o[1]].astype(BF16), wk=w_in[:, o[1]:o[2]].astype(BF16),
        wv=w_in[:, o[2]:o[3]].astype(BF16), wlr=_pad_cols(w_in[:, o[3]:o[4]], LANES).astype(BF16),
        wz=w_in[:, o[4]:o[5]].astype(BF16),
        wg=jnp.pad(w_gate_up, ((0, LANES - GLA_RANK), (0, 0))).astype(BF16),
        gbias=gate_bias.reshape(1, qk).astype(F32),
    )


def _layer0(x, gdn_states, band_caches, mk, mv, norm_g, pw, cw, seg, qg, kg, bias,
            a_onorm_g, w_out, mnorm_g, w_mq, mq_g, w_mo, L):
    b, s, d = x.shape
    outs = _proj0(x.reshape(b * s, d), norm_g.reshape(1, d), pw, qg, kg, seg, s)
    u, gb, za, qlo, qhi, kn, klast, vlo, vhi, vlast, zb = (t.reshape((b, -1) + t.shape[1:]) for t in outs)
    if gdn_states is not None:
        conv_state, gdn_state = gdn_states
        gdn_states = (jnp.pad(conv_state.astype(F32), ((0, 0), (8 - (CONV_TAPS - 1), 0), (0, 0))),
                      gdn_state.astype(F32))
    o_a, s_new = _gdn(u, gb, za, gdn_states, cw, a_onorm_g.reshape(1, LANES), L)
    o_b = _band(qlo, qhi, kn, vlo, vhi, zb, bias, band_caches, bias.shape[1])
    y = _post(x, (o_a, o_b), w_out, mnorm_g.reshape(1, d), w_mq, mq_g.reshape(1, MEM_HD), mk, mv, w_mo)
    return y, u[:, s - (CONV_TAPS - 1):, :], s_new, klast, vlast


def _layer1(x, gla_state, mk, mv, norm_g, pw, c_onorm_g, w_out, mnorm_g, w_mq, mq_g, w_mo, L):
    b, s, d = x.shape
    q, k, v, lg, z = _proj1(x.reshape(b * s, d), norm_g.reshape(1, d), pw)
    r3 = lambda t: t.reshape(b, s, t.shape[-1])
    q, k, v, lg, z = map(r3, (q, k, v, lg, z))
    o, st = _gla(q, k, v, lg, z, gla_state, c_onorm_g.reshape(1, GLA_DV), L)
    y = _post(x, (o,), w_out, mnorm_g.reshape(1, d), w_mq, mq_g.reshape(1, MEM_HD), mk, mv, w_mo)
    return y, st


def kernel(x_prompt, x_sample, mem_prompt, state_l0_gdn_conv, state_l0_gdn, cache_l0_band_k, cache_l0_band_v, cache_l0_mem_k, cache_l0_mem_v, state_l1_gla, cache_l1_mem_k, cache_l1_mem_v, l0_norm_g, l0_w_in, l0_conv_w, l0_a_log, l0_dt_bias, l0_a_onorm_g, l0_b_q_g, l0_b_k_g, l0_b_rel_bias, l0_w_out, l0_mnorm_g, l0_mem_norm_g, l0_w_mkv, l0_mk_g, l0_w_mq, l0_mq_g, l0_w_mo, l1_norm_g, l1_w_in, l1_w_gate_up, l1_gate_bias, l1_c_onorm_g, l1_w_out, l1_mnorm_g, l1_mem_norm_g, l1_w_mkv, l1_mk_g, l1_w_mq, l1_mq_g, l1_w_mo):
    bp, sp, d = x_prompt.shape
    bs, ss, _ = x_sample.shape
    nm = mem_prompt.shape[1]
    hw = MEM_HEADS * MEM_HD
    dm = BAND_HEADS * BAND_HD
    assert sp % CHUNK_ == 0 and ss % INV_SUB == 0 and ss <= CHUNK_
    assert cache_l0_band_k.shape[1] == BAND_PAST

    pw0, cw, seg, qg, kg = _prep_l0(l0_w_in, l0_conv_w, l0_a_log, l0_dt_bias, l0_b_q_g, l0_b_k_g)
    pw1 = _prep_l1(l1_w_in, l1_w_gate_up, l1_gate_bias)
    band_rows = BAND_STEP_CHUNKS * CHUNK_ if sp % (BAND_STEP_CHUNKS * CHUNK_) == 0 else CHUNK_
    bias_p = _band_bias_table(l0_b_rel_bias, band_rows, CHUNK_)
    bias_s = _band_bias_table(l0_b_rel_bias, ss, ss)
    bf = lambda w: w.astype(BF16)
    mem2 = mem_prompt.reshape(bp * nm, d)

    p_mk0, p_mv0, mk0, mv0 = _memkv(mem2, l0_mem_norm_g.reshape(1, d), bf(l0_w_mkv), l0_mk_g.reshape(1, MEM_HD))
    mk0 = mk0.reshape(bp, nm, hw)
    mv0 = mv0.reshape(bp, nm, hw)
    l0_shared = (l0_norm_g, pw0, cw, seg, qg, kg)
    l0_tail = (l0_a_onorm_g, bf(l0_w_out), l0_mnorm_g, bf(l0_w_mq), l0_mq_g, bf(l0_w_mo))
    yp, p_conv, p_gdn, p_kn, p_v = _layer0(
        x_prompt, None, None, mk0, mv0,
        *l0_shared, bias_p, *l0_tail, CHUNK_)
    ys, s_conv, s_gdn, s_kn, s_v = _layer0(
        x_sample, (state_l0_gdn_conv, state_l0_gdn),
        (cache_l0_band_k.reshape(bs, BAND_PAST, dm), cache_l0_band_v.reshape(bs, BAND_PAST, dm)),
        cache_l0_mem_k.reshape(bs, nm, hw), cache_l0_mem_v.reshape(bs, nm, hw),
        *l0_shared, bias_s, *l0_tail, ss)

    p_mk1, p_mv1, mk1, mv1 = _memkv(mem2, l1_mem_norm_g.reshape(1, d), bf(l1_w_mkv), l1_mk_g.reshape(1, MEM_HD))
    mk1 = mk1.reshape(bp, nm, hw)
    mv1 = mv1.reshape(bp, nm, hw)
    l1_tail = (l1_c_onorm_g, bf(l1_w_out), l1_mnorm_g, bf(l1_w_mq), l1_mq_g, bf(l1_w_mo))
    yp, p_gla = _layer1(yp, None, mk1, mv1, l1_norm_g, pw1, *l1_tail, CHUNK_)
    ys, s_gla = _layer1(ys, state_l1_gla.astype(F32), cache_l1_mem_k.reshape(bs, nm, hw),
                        cache_l1_mem_v.reshape(bs, nm, hw), l1_norm_g, pw1, *l1_tail, ss)

    m4 = lambda t: t.reshape(bp, nm, MEM_HEADS, MEM_HD)
    return (yp, ys, p_conv, p_gdn, p_kn, p_v,
            m4(p_mk0), m4(p_mv0), p_gla, m4(p_mk1), m4(p_mv1),
            s_conv, s_gdn, s_kn, s_v, s_gla)
```

```python
import functools

import jax
import jax.numpy as jnp
import numpy as np
from jax import lax
from jax.experimental import pallas as pl
from jax.experimental.pallas import tpu as pltpu

F32 = jnp.float32
BF16 = jnp.bfloat16
NORM_EPS = 1e-6
LOG2_E = 1.4426950408889634

CHUNK_ = 64
CONV_TAPS = 4
GDN_HEADS = 8
GDN_DK = 128
BAND_HEADS = 16
BAND_HD = 64
BAND_PAST = 512
BAND_MAX_REL = 128
GLA_HEADS = 8
GLA_DK = 128
GLA_DV = 256
GLA_RANK = 16
GLA_TAU = 16.0
MEM_HEADS = 4
MEM_HD = 128
INV_SUB = 16
GDN_GROUP = 2
GDN_STEP_CHUNKS = 1
GLA_GROUP = 2
POST_SPLIT = 4
PROJ_SPLIT = 2
SUB_TILE_MIN = 128
BAND_STEP_CHUNKS = 2
LANES = 128
VMEM_LIMIT = 56 * 1024 * 1024


def _dot(a, b):
    return jnp.dot(a, b, preferred_element_type=F32)


def _dot_nt(a, b):
    return lax.dot_general(a, b, (((1,), (1,)), ((), ())), preferred_element_type=F32)


def _dot_tn(a, b):
    return lax.dot_general(a, b, (((0,), (0,)), ((), ())), preferred_element_type=F32)


def _split3(x):
    hi = x.astype(BF16)
    r = x - hi.astype(F32)
    mid = r.astype(BF16)
    lo = (r - mid.astype(F32)).astype(BF16)
    return hi, mid, lo


def _dot_exact_lhs(a_bf, b):
    h, m, l = _split3(b)
    return _dot(a_bf, h) + (_dot(a_bf, m) + _dot(a_bf, l))


def _rms(x, g):
    ms = jnp.mean(x * x, axis=-1, keepdims=True)
    return x * lax.rsqrt(ms + NORM_EPS) * g


def _silu(x):
    return x * jax.nn.sigmoid(x)


def _softplus(x):
    return jnp.maximum(x, 0.0) + jnp.log1p(jnp.exp(-jnp.abs(x)))


def _softmax2_rows(s):
    m = jnp.max(s, axis=-1, keepdims=True)
    e = jnp.exp2(s - m)
    return e * (1.0 / jnp.sum(e, axis=-1, keepdims=True))


def _const_spec(shape):
    nd = len(shape)
    return pl.BlockSpec(shape, lambda *_: (0,) * nd, pipeline_mode=pl.Buffered(1))


def _params(sem):
    return pltpu.CompilerParams(dimension_semantics=sem, vmem_limit_bytes=VMEM_LIMIT)


def _row_tile(n, want):
    t = min(n, want)
    assert n % t == 0
    return t


def _sub_tiles(tm):
    ts = tm // PROJ_SPLIT if tm % (SUB_TILE_MIN * PROJ_SPLIT) == 0 else tm
    return [slice(t * ts, (t + 1) * ts) for t in range(tm // ts)]


def _proj0_body(x_ref, g_ref, wu_ref, wab_ref, wza_ref, wq_ref, wk_ref, wv_ref, wzb_ref,
                alog_ref, dtb_ref, qg_ref, kg_ref, seg_ref,
                u_ref, gb_ref, za_ref, qlo_ref, qhi_ref, kn_ref, klast_ref, vlo_ref, vhi_ref, vlast_ref, zb_ref,
                *, kept_tiles):
    h = _rms(x_ref[...], g_ref[...]).astype(BF16)
    dm = BAND_HEADS * BAND_HD
    lo = (lax.broadcasted_iota(jnp.int32, (1, dm), 1) & (LANES - 1)) < BAND_HD
    seg = seg_ref[...]
    q = _dot(h, wq_ref[...])
    k = _dot(h, wk_ref[...])
    ab = _dot(h, wab_ref[...])
    za = _dot(h, wza_ref[...])

    def head_rsqrt(x):
        ss = _dot((x * x).astype(BF16), seg)
        return lax.rsqrt(ss * (1.0 / BAND_HD) + NORM_EPS)

    rq = head_rsqrt(q)
    rk = head_rsqrt(k)
    zb = _dot(h, wzb_ref[...])
    v = _dot(h, wv_ref[...])
    u_ref[...] = _dot(h, wu_ref[...])
    za_ref[...] = _silu(za)
    zb_ref[...] = _silu(zb)
    lo1 = lo[:, :LANES]

    def spread(r):
        return jnp.concatenate([jnp.where(lo1, r[:, 2 * s:2 * s + 1], r[:, 2 * s + 1:2 * s + 2])
                                for s in range(BAND_HEADS // 2)], axis=1)

    qn = q * spread(rq) * (qg_ref[...] * (BAND_HD ** -0.5 * LOG2_E))
    kn = k * spread(rk) * kg_ref[...]
    qlo_ref[...] = jnp.where(lo, qn, 0.0).astype(BF16)
    qhi_ref[...] = jnp.where(lo, 0.0, qn).astype(BF16)
    kn_ref[...] = kn.astype(BF16)
    vlo_ref[...] = jnp.where(lo, v, 0.0).astype(BF16)
    vhi_ref[...] = jnp.where(lo, 0.0, v).astype(BF16)

    def hand_on():
        klast_ref[...] = kn.reshape(kn.shape[0], BAND_HEADS, BAND_HD)
        vlast_ref[...] = v.reshape(v.shape[0], BAND_HEADS, BAND_HD)

    if kept_tiles is None:
        hand_on()
    else:
        tpb, kt = kept_tiles
        pl.when(pl.program_id(0) % tpb >= tpb - kt)(hand_on)
    lane = lax.broadcasted_iota(jnp.int32, ab.shape, 1)
    gval = -jnp.exp(alog_ref[...]) * _softplus(ab + dtb_ref[...])
    gb_ref[...] = jnp.where(lane < GDN_HEADS, gval, jax.nn.sigmoid(ab))


def _proj0(x2, g, w, qg, kg, seg, rows_per_batch):
    n, d = x2.shape
    tm = _row_tile(n, 256)
    dm = BAND_HEADS * BAND_HD
    widths = (3 * 1024, LANES, 1024, dm, dm, dm, dm)
    row = lambda wd: pl.BlockSpec((tm, wd), lambda i: (i, 0))
    keep = min(BAND_PAST, rows_per_batch)
    if rows_per_batch > keep:
        assert rows_per_batch % tm == 0 and keep % tm == 0
        tpb, kt = rows_per_batch // tm, keep // tm
        last = pl.BlockSpec((tm, BAND_HEADS, BAND_HD),
                            lambda i: ((i // tpb) * kt + jnp.maximum(i % tpb - (tpb - kt), 0), 0, 0))
        n_last = (n // rows_per_batch) * keep
        kept_tiles = (tpb, kt)
    else:
        last, n_last = pl.BlockSpec((tm, BAND_HEADS, BAND_HD), lambda i: (i, 0, 0)), n
        kept_tiles = None
    last_shape = jax.ShapeDtypeStruct((n_last, BAND_HEADS, BAND_HD), F32)
    f32 = lambda rows, wd: jax.ShapeDtypeStruct((rows, wd), F32)
    bf16 = lambda wd: jax.ShapeDtypeStruct((n, wd), BF16)
    return pl.pallas_call(
        functools.partial(_proj0_body, kept_tiles=kept_tiles),
        grid=(n // tm,),
        in_specs=[row(d), _const_spec((1, d))]
        + [_const_spec((d, wd)) for wd in widths]
        + [_const_spec((1, LANES)), _const_spec((1, LANES)), _const_spec((1, dm)), _const_spec((1, dm)),
           _const_spec((dm, LANES))],
        out_specs=[row(3 * 1024), row(LANES), row(1024), row(dm), row(dm), row(dm), last, row(dm), row(dm), last,
                   row(dm)],
        out_shape=[f32(n, 3 * 1024), f32(n, LANES), f32(n, 1024), bf16(dm), bf16(dm), bf16(dm), last_shape,
                   bf16(dm), bf16(dm), last_shape, f32(n, dm)],
        compiler_params=_params(("arbitrary",)),
        name="proj0",
    )(x2, g, w["wu"], w["wab"], w["wza"], w["wq"], w["wk"], w["wv"], w["wzb"], w["alog"], w["dtb"],
      qg, kg, seg)


def _proj1_body(x_ref, g_ref, wq_ref, wk_ref, wv_ref, wlr_ref, wz_ref, wg_ref, gbias_ref,
                q_ref, k_ref, v_ref, lg_ref, z_ref):
    rows = _sub_tiles(x_ref.shape[0])
    h = [_rms(x_ref[r, :], g_ref[...]).astype(BF16) for r in rows]
    lr = [_dot(h_, wlr_ref[...]) for h_ in h]
    for r, h_ in zip(rows, h):
        q_ref[r, :] = _dot(h_, wq_ref[...])
    pre = [_dot(x.astype(BF16), wg_ref[...]) + gbias_ref[...] for x in lr]
    zs = [_dot(h_, wz_ref[...]) for h_ in h]
    for r, p_ in zip(rows, pre):
        lg_ref[r, :] = -_softplus(-p_) * (1.0 / GLA_TAU)
    for r, h_ in zip(rows, h):
        k_ref[r, :] = _dot(h_, wk_ref[...])
    for r, z_ in zip(rows, zs):
        z_ref[r, :] = _silu(z_)
    for r, h_ in zip(rows, h):
        v_ref[r, :] = _dot(h_, wv_ref[...])


def _proj1(x2, g, w):
    n, d = x2.shape
    tm = _row_tile(n, 512)
    row = lambda wd: pl.BlockSpec((tm, wd), lambda i: (i, 0))
    outw = (1024, 1024, 2048, 1024, 2048)
    return pl.pallas_call(
        _proj1_body,
        grid=(n // tm,),
        in_specs=[row(d), _const_spec((1, d)), _const_spec((d, 1024)), _const_spec((d, 1024)),
                  _const_spec((d, 2048)), _const_spec((d, LANES)), _const_spec((d, 2048)),
                  _const_spec((LANES, 1024)), _const_spec((1, 1024))],
        out_specs=[row(wd) for wd in outw],
        out_shape=[jax.ShapeDtypeStruct((n, wd), F32) for wd in outw],
        compiler_params=_params(("arbitrary",)),
        name="proj1",
    )(x2, g, w["wq"], w["wk"], w["wv"], w["wlr"], w["wz"], w["wg"], w["gbias"])


def _memkv_body(m_ref, g_ref, w_ref, kg_ref, k4_ref, v4_ref, kb_ref, vb_ref):
    h = _rms(m_ref[...], g_ref[...]).astype(BF16)
    kv = _dot(h, w_ref[...])
    hw = MEM_HEADS * MEM_HD
    k = jnp.concatenate([_rms(kv[:, hh * MEM_HD:(hh + 1) * MEM_HD], kg_ref[...]) for hh in range(MEM_HEADS)],
                        axis=1)
    v = kv[:, hw:]
    k4_ref[...] = k.reshape(k.shape[0], MEM_HEADS, MEM_HD)
    v4_ref[...] = v.reshape(v.shape[0], MEM_HEADS, MEM_HD)
    kb_ref[...] = k.astype(BF16)
    vb_ref[...] = v.astype(BF16)


def _memkv(m2, g, w_bf, kg):
    n, d = m2.shape
    tm = _row_tile(n, 256)
    hw = MEM_HEADS * MEM_HD
    row = lambda wd: pl.BlockSpec((tm, wd), lambda i: (i, 0))
    row4 = pl.BlockSpec((tm, MEM_HEADS, MEM_HD), lambda i: (i, 0, 0))
    return pl.pallas_call(
        _memkv_body,
        grid=(n // tm,),
        in_specs=[row(d), _const_spec((1, d)), _const_spec((d, 2 * hw)), _const_spec((1, MEM_HD))],
        out_specs=[row4, row4, row(hw), row(hw)],
        out_shape=[jax.ShapeDtypeStruct((n, MEM_HEADS, MEM_HD), F32)] * 2
        + [jax.ShapeDtypeStruct((n, hw), BF16)] * 2,
        compiler_params=_params(("arbitrary",)),
        name="memkv",
    )(m2, g, w_bf, kg)


def _each(fn, *lists):
    return [fn(*xs) for xs in zip(*lists)]


def _unit_lower_inverse(a, eye, bd, mm):
    d = _each(lambda x: jnp.where(bd, x, 0.0), a)
    nl = _each(lambda x, y: x - y, a, d)
    d2 = _each(mm, d, d)
    d4 = _each(mm, d2, d2)
    td = _each(lambda x, y: mm(eye - x, eye + y), d, d2)
    d8 = _each(mm, d4, d4)
    td = _each(lambda x, y: mm(x, eye + y), td, d4)
    td = _each(lambda x, y: mm(x, eye + y), td, d8)
    m = _each(mm, td, nl)
    m2 = _each(mm, m, m)
    mt = _each(mm, m, td)
    return _each(lambda x, y, z: mm(eye + x, y - z), m2, td, mt)


def _gdn_body(*refs, L, NB, G, has_state):
    if has_state:
        u_ref, gb_ref, za_ref, cw_ref, og_ref, cinit_ref, sinit_ref, o_ref, s_ref, ubuf = refs
    else:
        u_ref, gb_ref, za_ref, cw_ref, og_ref, o_ref, s_ref, ubuf = refs
    c = pl.program_id(1)
    ns = 3 * GDN_HEADS
    hist = 8
    R = G * L

    @pl.when(c == 0)
    def _():
        if has_state:
            for b in range(NB):
                for j in range(ns):
                    ubuf[b * ns + j, 0:hist, :] = cinit_ref[b, :, j * LANES:(j + 1) * LANES]
            s_ref[...] = sinit_ref[...]
        else:
            ubuf[:, 0:hist, :] = jnp.zeros((NB * ns, hist, LANES), F32)
            s_ref[...] = jnp.zeros(s_ref.shape, F32)

    for b in range(NB):
        for j in range(ns):
            ubuf[b * ns + j, hist:hist + R, :] = u_ref[b, :, j * LANES:(j + 1) * LANES]
    base = hist - (CONV_TAPS - 1)
    ys = []
    for b in range(NB):
        bsl = slice(b * ns, (b + 1) * ns)
        yb = ubuf[bsl, base:base + R, :] * cw_ref[:, 0:1, :]
        for i in range(1, CONV_TAPS):
            yb = yb + ubuf[bsl, base + i:base + i + R, :] * cw_ref[:, i:i + 1, :]
        ys.append(_silu(yb))
    ubuf[:, base:hist, :] = ubuf[:, base + R:hist + R, :]

    row = lax.broadcasted_iota(jnp.int32, (L, 2 * L), 0)
    lane = lax.broadcasted_iota(jnp.int32, (L, 2 * L), 1)
    col = lane & (L - 1)
    left = lane < L
    incl = row >= col
    strict = row > col
    sub_shift = INV_SUB.bit_length() - 1
    bd = lax.shift_right_logical(row, sub_shift) == lax.shift_right_logical(col, sub_shift)
    eye = jnp.where(row == col, 1.0, 0.0).astype(F32)
    trow = lax.broadcasted_iota(jnp.int32, (L, L), 0)
    tcol = lax.broadcasted_iota(jnp.int32, (L, L), 1)
    tri = jnp.where(trow >= tcol, 1.0, 0.0).astype(BF16)
    og = og_ref[...]
    heads = [(b, g, h) for b in range(NB) for g in range(G) for h in range(GDN_HEADS)]
    pairs = range(len(heads) // 2)
    rows_of = lambda g: slice(g * L, (g + 1) * L)
    l2n = lambda x: x * lax.rsqrt(jnp.sum(x * x, axis=-1, keepdims=True) + NORM_EPS)
    q = [l2n(ys[b][h][rows_of(g)]) * (GDN_DK ** -0.5) for b, g, h in heads]
    k = [l2n(ys[b][GDN_HEADS + h][rows_of(g)]) for b, g, h in heads]
    v = [ys[b][2 * GDN_HEADS + h][rows_of(g)] for b, g, h in heads]
    gc, gr, bc = [], [], []
    for b in range(NB):
        for g in range(G):
            gbv = gb_ref[b, rows_of(g), :]
            gcum = _dot_exact_lhs(tri, gbv)
            gpad = jnp.concatenate([gcum, jnp.zeros((LANES - L, LANES), F32)], axis=0)
            gt = gpad.T
            for h in range(GDN_HEADS):
                gc.append(gcum[:, h:h + 1])
                gr.append(gt[h:h + 1, 0:L])
                bc.append(gbv[:, GDN_HEADS + h:GDN_HEADS + h + 1])
    side = lambda x0, x1: jnp.where(left, x0, x1)
    gcp = [side(gc[2 * p], gc[2 * p + 1]) for p in pairs]
    grp = [jnp.concatenate([gr[2 * p], gr[2 * p + 1]], axis=1) for p in pairs]
    bcp = [side(bc[2 * p], bc[2 * p + 1]) for p in pairs]
    dec = _each(lambda c_, r_: jnp.where(incl, jnp.exp(jnp.where(incl, c_ - r_, 0.0)), 0.0), gcp, grp)
    kb = _each(lambda x: x.astype(BF16), k)
    qb = _each(lambda x: x.astype(BF16), q)

    def blockdiag(y0, y1):
        z0 = jnp.zeros(y1.shape, y1.dtype)
        z1 = jnp.zeros(y0.shape, y0.dtype)
        return jnp.concatenate([jnp.concatenate([y0, z0], axis=1), jnp.concatenate([z1, y1], axis=1)], axis=0)

    def mm(x, y):
        yb = y.astype(BF16)
        zero = jnp.zeros_like(yb)
        return _dot(x.astype(BF16), jnp.concatenate([jnp.where(left, yb, zero), jnp.where(left, zero, yb)], axis=0))

    kq = [_dot_nt(jnp.concatenate([jnp.concatenate([kb[2 * p], qb[2 * p]], axis=0),
                                   jnp.concatenate([kb[2 * p + 1], qb[2 * p + 1]], axis=0)], axis=1),
                  blockdiag(kb[2 * p], kb[2 * p + 1])) for p in pairs]
    a = _each(lambda b_, x, d_: jnp.where(strict, b_ * x[:L] * d_, 0.0), bcp, kq, dec)
    t = _unit_lower_inverse(a, eye, bd, mm)
    eg = _each(jnp.exp, gc)
    rhs = _each(lambda b_, v_, e_, k_: jnp.concatenate([b_ * v_, (b_ * e_) * k_], axis=1).astype(BF16),
                bc, v, eg, k)
    sol = [_dot(t[p].astype(BF16), blockdiag(rhs[2 * p], rhs[2 * p + 1])) for p in pairs]
    solk = lambda i: sol[i // 2][:, 2 * (i % 2) * LANES + LANES:2 * (i % 2 + 1) * LANES]
    solv = lambda i: sol[i // 2][:, 2 * (i % 2) * LANES:2 * (i % 2) * LANES + LANES]
    qkd = [(kq[p][L:] * dec[p]).astype(BF16) for p in pairs]
    gl = [c_[L - 1:L, :] for c_ in gc]
    kd = _each(lambda k_, l_, c_: (k_ * jnp.exp(l_ - c_)).astype(BF16), k, gl, gc)

    item = lambda b, g, h: (b * G + g) * GDN_HEADS + h
    half = lambda x, i: x[:, (i % 2) * LANES:(i % 2 + 1) * LANES]
    s = {(b, h): s_ref[b, h] for b in range(NB) for h in range(GDN_HEADS)}
    for g in range(G):
        ids = [item(b, g, h) for b in range(NB) for h in range(GDN_HEADS)]
        sb = {i: s[(heads[i][0], heads[i][2])].astype(BF16) for i in ids}
        ksq = {i: _dot(jnp.concatenate([jnp.concatenate([solk(i).astype(BF16), qb[i]], axis=0),
                                        jnp.concatenate([solk(i + 1).astype(BF16), qb[i + 1]], axis=0)], axis=1),
                       blockdiag(sb[i], sb[i + 1])) for i in ids[::2]}
        ub = {i: (solv(i) - half(ksq[i - i % 2][:L], i)).astype(BF16) for i in ids}
        qku = {i: _dot(qkd[i // 2], blockdiag(ub[i], ub[i + 1])) for i in ids[::2]}
        ktu = {i: _dot_tn(kd[i], ub[i]) for i in ids}
        for i in ids:
            b, _, h = heads[i]
            s[(b, h)] = s[(b, h)] * jnp.exp(gl[i]) + ktu[i]
        for b in range(NB):
            za = za_ref[b, rows_of(g), :]
            outs = [(_rms(half(ksq[i - i % 2][L:], i) * eg[i] + half(qku[i - i % 2], i), og)
                     * za[:, heads[i][2] * LANES:(heads[i][2] + 1) * LANES]).astype(BF16)
                    for i in ids if heads[i][0] == b]
            o_ref[b, rows_of(g), :] = jnp.concatenate(outs, axis=1)
    for (b, h), val in s.items():
        s_ref[b, h] = val


def _gdn(u, gb, za, state, cw, og, L):
    b, s, _ = u.shape
    nc = s // L
    ns = 3 * GDN_HEADS
    nb = GDN_GROUP if b % GDN_GROUP == 0 else 1
    g = GDN_STEP_CHUNKS if nc % GDN_STEP_CHUNKS == 0 else 1
    blk = lambda wd: pl.BlockSpec((nb, g * L, wd), lambda i, j: (i, j, 0))
    has_state = state is not None
    state_specs = [pl.BlockSpec((nb, 8, ns * LANES), lambda i, j: (i, 0, 0)),
                   pl.BlockSpec((nb, GDN_HEADS, GDN_DK, LANES), lambda i, j: (i, 0, 0, 0))]
    return pl.pallas_call(
        functools.partial(_gdn_body, L=L, NB=nb, G=g, has_state=has_state),
        grid=(b // nb, nc // g),
        in_specs=[blk(ns * LANES), blk(LANES), blk(GDN_HEADS * LANES),
                  _const_spec((ns, 8, LANES)), _const_spec((1, LANES))] + (state_specs if has_state else []),
        out_specs=[blk(GDN_HEADS * LANES),
                   pl.BlockSpec((nb, GDN_HEADS, GDN_DK, LANES), lambda i, j: (i, 0, 0, 0))],
        out_shape=[jax.ShapeDtypeStruct((b, s, GDN_HEADS * LANES), BF16),
                   jax.ShapeDtypeStruct((b, GDN_HEADS, GDN_DK, LANES), F32)],
        scratch_shapes=[pltpu.VMEM((nb * ns, 8 + g * L, LANES), F32)],
        compiler_params=_params(("arbitrary", "arbitrary")),
        name="gdn",
    )(u, gb, za, cw, og, *(state if has_state else ()))


def _band_body(*refs, L, W, has_cache):
    if has_cache:
        (qlo_ref, qhi_ref, k_ref, vlo_ref, vhi_ref, zb_ref, bias_ref, ck_ref, cv_ref,
         o_ref, kscr, vlo, vhi, biasm) = refs
    else:
        (qlo_ref, qhi_ref, k_ref, vlo_ref, vhi_ref, zb_ref, bias_ref,
         o_ref, kscr, vlo, vhi, biasm) = refs
    c = pl.program_id(1)
    dm = BAND_HEADS * BAND_HD
    pad_chunks = BAND_PAST // L

    @pl.when(c == 0)
    def _():
        if has_cache:
            lo = (lax.broadcasted_iota(jnp.int32, (1, dm), 1) & (LANES - 1)) < BAND_HD
            kscr[0:BAND_PAST, :] = ck_ref[0].reshape(BAND_PAST, dm).astype(BF16)
            cv = cv_ref[0].reshape(BAND_PAST, dm)
            vlo[0:BAND_PAST, :] = jnp.where(lo, cv, 0.0).astype(BF16)
            vhi[0:BAND_PAST, :] = jnp.where(lo, 0.0, cv).astype(BF16)
        else:
            zero = jnp.zeros((BAND_PAST, dm), BF16)
            kscr[0:BAND_PAST, :] = zero
            vlo[0:BAND_PAST, :] = zero
            vhi[0:BAND_PAST, :] = zero

    new0 = pl.multiple_of(BAND_PAST + c * L, L)
    kscr[pl.ds(new0, L), :] = k_ref[0]
    vlo[pl.ds(new0, L), :] = vlo_ref[0]
    vhi[pl.ds(new0, L), :] = vhi_ref[0]

    if has_cache:
        @pl.when(c == 0)
        def _():
            biasm[...] = bias_ref[...] * LOG2_E
    else:
        @pl.when(c < pad_chunks)
        def _():
            wcol = lax.broadcasted_iota(jnp.int32, (1, W), 1)
            valid = (wcol + c * L) >= BAND_PAST
            for h in range(BAND_HEADS):
                biasm[h] = jnp.where(valid, bias_ref[h] * LOG2_E, -jnp.inf)

        @pl.when(c == pad_chunks)
        def _():
            biasm[...] = bias_ref[...] * LOG2_E
    bias_src = biasm

    w0 = pl.multiple_of(c * L, L)
    zb = zb_ref[0]
    slabs = [slice(s * LANES, (s + 1) * LANES) for s in range(BAND_HEADS // 2)]
    sc = []
    for sl in slabs:
        ks = kscr[pl.ds(w0, W), sl]
        sc.append(_dot_nt(qlo_ref[0, :, sl], ks))
        sc.append(_dot_nt(qhi_ref[0, :, sl], ks))
    ps, rs = [], []
    for h in range(BAND_HEADS):
        x = sc[h] + bias_src[h]
        e = jnp.exp2(x - jnp.max(x, axis=-1, keepdims=True))
        rs.append(1.0 / jnp.sum(e, axis=-1, keepdims=True))
        ps.append(e.astype(BF16))
    pv = []
    for i, sl in enumerate(slabs):
        pv.append(_dot(ps[2 * i], vlo[pl.ds(w0, W), sl]))
        pv.append(_dot(ps[2 * i + 1], vhi[pl.ds(w0, W), sl]))
    outs = [((pv[2 * i] * rs[2 * i] + pv[2 * i + 1] * rs[2 * i + 1]) * zb[:, sl]).astype(BF16)
            for i, sl in enumerate(slabs)]
    o_ref[0] = jnp.concatenate(outs, axis=1)


def _band(qlo, qhi, kn, vlo, vhi, zb, bias, caches, L):
    b, s, dm = qlo.shape
    nc = s // L
    W = BAND_PAST + L
    blk = pl.BlockSpec((1, L, dm), lambda i, j: (i, j, 0))
    cache = pl.BlockSpec((1, BAND_PAST, BAND_HEADS, BAND_HD), lambda i, j: (i, 0, 0, 0))
    has_cache = caches is not None
    scratch = [pltpu.VMEM((BAND_PAST + s, dm), BF16)] * 3 + [pltpu.VMEM((BAND_HEADS, L, W), F32)]
    return pl.pallas_call(
        functools.partial(_band_body, L=L, W=W, has_cache=has_cache),
        grid=(b, nc),
        in_specs=[blk] * 6 + [_const_spec((BAND_HEADS, L, W))] + ([cache, cache] if has_cache else []),
        out_specs=blk,
        out_shape=jax.ShapeDtypeStruct((b, s, dm), BF16),
        scratch_shapes=scratch,
        compiler_params=_params(("arbitrary", "arbitrary")),
        name="band",
    )(qlo, qhi, kn, vlo, vhi, zb, bias, *(caches if has_cache else ()))


def _gla_body(*refs, L, NB, has_state):
    if has_state:
        q_ref, k_ref, v_ref, lg_ref, z_ref, og_ref, sinit_ref, o_ref, sout_ref, st_ref, cbs = refs
    else:
        q_ref, k_ref, v_ref, lg_ref, z_ref, og_ref, o_ref, sout_ref, st_ref, cbs = refs
    c = pl.program_id(1)

    @pl.when(c == 0)
    def _():
        for b in range(NB):
            for h in range(GLA_HEADS):
                st_ref[b, h] = sinit_ref[b, h].T if has_state else jnp.zeros((GLA_DV, GLA_DK), F32)

    levels = [L >> (t + 1) for t in range(L.bit_length() - 1)]
    small = [s for s in levels if 2 < 2 * s < 16]
    row = lax.broadcasted_iota(jnp.int32, (L, L), 0)
    col = lax.broadcasted_iota(jnp.int32, (L, L), 1)
    anchor = lambda s: lax.shift_left(lax.shift_right_logical(row, s.bit_length()), s.bit_length()) + (s - 1)
    between = lambda s: (col > jnp.minimum(row, anchor(s))) & (col <= jnp.maximum(row, anchor(s)))
    onehot = lambda m: jnp.where(m, 1.0, 0.0).astype(BF16)
    tri = onehot(row >= col)
    stack = jnp.concatenate([onehot(between(s)) for s in small], axis=0)
    ex = jnp.exp2
    og = og_ref[...]
    scale = GLA_DK ** -0.5
    lg2 = [lg_ref[b] * LOG2_E for b in range(NB)]
    pieces = [_split3(x) for x in lg2]
    cb_all = [_dot(tri, p[0]) + (_dot(tri, p[1]) + _dot(tri, p[2])) for p in pieces]
    y_small = [_dot(stack, p[0]) + _dot(stack, p[1]) for p in pieces]
    odd_row = (lax.broadcasted_iota(jnp.int32, (L, 1), 0) & 1) == 1
    ydec = [dict() for _ in range(NB)]
    for b in range(NB):
        cbs[b] = cb_all[b]
        ydec[b][1] = jnp.where(odd_row, lg2[b], 0.0)
        for t, s in enumerate(small):
            ydec[b][s] = y_small[b][t * L:(t + 1) * L]
        for s in levels:
            if s not in ydec[b]:
                anc = jnp.concatenate(
                    [jnp.broadcast_to(cbs[b, p * 2 * s + s - 1:p * 2 * s + s, :], (2 * s, cbs.shape[2]))
                     for p in range(L // (2 * s))], axis=0)
                ydec[b][s] = -jnp.abs(cb_all[b] - anc)

    heads = [(b, h) for b in range(NB) for h in range(GLA_HEADS)]
    pairs = range(len(heads) // 2)
    ksl = lambda h: slice(h * GLA_DK, (h + 1) * GLA_DK)
    vsl = lambda h: slice(h * GLA_DV, (h + 1) * GLA_DV)
    cb = [cb_all[b][:, ksl(h)] for b, h in heads]
    q = [q_ref[b, :, ksl(h)] * scale for b, h in heads]
    k = [k_ref[b, :, ksl(h)] for b, h in heads]
    vb = [v_ref[b, :, vsl(h)].astype(BF16) for b, h in heads]
    st = [st_ref[b, h] for b, h in heads]
    o_inter = _each(lambda q_, c_, s_: _dot_nt((q_ * ex(c_)).astype(BF16), s_.astype(BF16)), q, cb, st)
    cl = [c_[L - 1:L] for c_ in cb]
    ke = _each(lambda k_, l_, c_: (k_ * ex(l_ - c_)).astype(BF16), k, cl, cb)
    vtk = _each(_dot_tn, vb, ke)

    zk = jnp.zeros((L, GLA_DK), BF16)
    zv = jnp.zeros((L, GLA_DV), BF16)

    def side_by_side(qs, ks):
        lhs = jnp.concatenate(qs, axis=1).astype(BF16)
        rhs = jnp.concatenate([jnp.concatenate([ks[0].astype(BF16), zk], axis=1),
                               jnp.concatenate([zk, ks[1].astype(BF16)], axis=1)], axis=0)
        return _dot_nt(lhs, rhs)

    prow = lax.broadcasted_iota(jnp.int32, (L, 2 * L), 0)
    pcol = lax.broadcasted_iota(jnp.int32, (L, 2 * L), 1) & (L - 1)
    att = [jnp.where(prow == pcol, side_by_side((q[2 * p], q[2 * p + 1]), (k[2 * p], k[2 * p + 1])), 0.0)
           for p in pairs]
    for s in levels:
        sh = s.bit_length() - 1
        same_parent = lax.shift_right_logical(prow, sh + 1) == lax.shift_right_logical(pcol, sh + 1)
        take = same_parent & ((lax.shift_right_logical(prow, sh) & 1) == 1) & \
            ((lax.shift_right_logical(pcol, sh) & 1) == 0)
        f = [ex(ydec[b][s][:, ksl(h)]) for b, h in heads]
        prod = [side_by_side((q[2 * p] * f[2 * p], q[2 * p + 1] * f[2 * p + 1]),
                             (k[2 * p] * f[2 * p], k[2 * p + 1] * f[2 * p + 1])) for p in pairs]
        att = [jnp.where(take, prod[p], att[p]) for p in pairs]
    vpair = [jnp.concatenate([jnp.concatenate([vb[2 * p], zv], axis=1),
                              jnp.concatenate([zv, vb[2 * p + 1]], axis=1)], axis=0) for p in pairs]
    o_intra = [_dot(att[p].astype(BF16), vpair[p]) for p in pairs]
    new_st = [st[i] * ex(cl[i]) + vtk[i] for i in range(len(heads))]
    for i, (b, h) in enumerate(heads):
        st_ref[b, h] = new_st[i]
    for b in range(NB):
        z = z_ref[b]
        outs = [(_rms(o_inter[i] + o_intra[i // 2][:, (i % 2) * GLA_DV:(i % 2 + 1) * GLA_DV], og)
                 * z[:, vsl(h)]).astype(BF16) for i, (b_, h) in enumerate(heads) if b_ == b]
        o_ref[b] = jnp.concatenate(outs, axis=1)

    @pl.when(c == pl.num_programs(1) - 1)
    def _():
        for i, (b, h) in enumerate(heads):
            sout_ref[b, h] = new_st[i].T


def _gla(q, k, v, lg, z, sinit, og, L):
    b, s, _ = q.shape
    nc = s // L
    nb = GLA_GROUP if b % GLA_GROUP == 0 else 1
    qk = pl.BlockSpec((nb, L, GLA_HEADS * GLA_DK), lambda i, j: (i, j, 0))
    vv = pl.BlockSpec((nb, L, GLA_HEADS * GLA_DV), lambda i, j: (i, j, 0))
    st = pl.BlockSpec((nb, GLA_HEADS, GLA_DK, GLA_DV), lambda i, j: (i, 0, 0, 0))
    has_state = sinit is not None
    return pl.pallas_call(
        functools.partial(_gla_body, L=L, NB=nb, has_state=has_state),
        grid=(b // nb, nc),
        in_specs=[qk, qk, vv, qk, vv, _const_spec((1, GLA_DV))] + ([st] if has_state else []),
        out_specs=[vv, st],
        out_shape=[jax.ShapeDtypeStruct((b, s, GLA_HEADS * GLA_DV), BF16),
                   jax.ShapeDtypeStruct((b, GLA_HEADS, GLA_DK, GLA_DV), F32)],
        scratch_shapes=[pltpu.VMEM((nb, GLA_HEADS, GLA_DV, GLA_DK), F32),
                        pltpu.VMEM((nb, L, GLA_HEADS * GLA_DK), F32)],
        compiler_params=_params(("arbitrary", "arbitrary")),
        name="gla",
    )(q, k, v, lg, z, og, *((sinit,) if has_state else ()))


def _post_body(*refs, n_o):
    x_ref = refs[0]
    o_refs = refs[1:1 + n_o]
    wout_ref, mg_ref, wmq_ref, mqg_ref, mk_ref, mv_ref, wmo_ref, y_ref = refs[1 + n_o:]
    tm = x_ref.shape[1]
    ts = tm // POST_SPLIT if tm % (SUB_TILE_MIN * POST_SPLIT) == 0 else tm
    rows = [slice(t * ts, (t + 1) * ts) for t in range(tm // ts)]
    acc = [x_ref[0, r, :] for r in rows]
    off = 0
    for o_ref in o_refs:
        kd = o_ref.shape[-1]
        w = wout_ref[off:off + kd, :]
        acc = [a + _dot(o_ref[0, r, :], w) for a, r in zip(acc, rows)]
        off += kd
    hm = [_rms(a, mg_ref[...]).astype(BF16) for a in acc]
    qz = [_dot(h_, wmq_ref[...]) for h_ in hm]
    hw = MEM_HEADS * MEM_HD
    sls = [slice(h * MEM_HD, (h + 1) * MEM_HD) for h in range(MEM_HEADS)]
    mkb = [mk_ref[0, :, sl].astype(BF16) for sl in sls]
    mvb = [mv_ref[0, :, sl].astype(BF16) for sl in sls]
    qscale = MEM_HD ** -0.5 * LOG2_E
    qn = [[(_rms(z[:, sl], mqg_ref[...]) * qscale).astype(BF16) for sl in sls] for z in qz]
    sc = [[_dot_nt(qh, kh) for qh, kh in zip(qt, mkb)] for qt in qn]
    p = [[_softmax2_rows(x).astype(BF16) for x in st] for st in sc]
    oh = [[_dot(ph, vh) for ph, vh in zip(pt, mvb)] for pt in p]
    for t, r in enumerate(rows):
        outs = [oh[t][h] * _silu(qz[t][:, hw + h * MEM_HD:hw + (h + 1) * MEM_HD]) for h in range(MEM_HEADS)]
        oh[t] = jnp.concatenate(outs, axis=1).astype(BF16)
    ym = [_dot(om, wmo_ref[...]) for om in oh]
    for t, r in enumerate(rows):
        y_ref[0, r, :] = acc[t] + ym[t]


def _post(x, os_, wout, mg, wmq, mqg, mk, mv, wmo):
    b, s, d = x.shape
    tm = _row_tile(s, 256 * POST_SPLIT)
    hw = MEM_HEADS * MEM_HD
    nm = mk.shape[1]
    blk = lambda wd: pl.BlockSpec((1, tm, wd), lambda i, j: (i, j, 0))
    mem = pl.BlockSpec((1, nm, hw), lambda i, j: (i, 0, 0))
    kin = sum(o.shape[-1] for o in os_)
    return pl.pallas_call(
        functools.partial(_post_body, n_o=len(os_)),
        grid=(b, s // tm),
        in_specs=[blk(d)] + [blk(o.shape[-1]) for o in os_]
        + [_const_spec((kin, d)), _const_spec((1, d)), _const_spec((d, 2 * hw)), _const_spec((1, MEM_HD)),
           mem, mem, _const_spec((hw, d))],
        out_specs=blk(d),
        out_shape=jax.ShapeDtypeStruct((b, s, d), F32),
        compiler_params=_params(("arbitrary", "arbitrary")),
        name="post",
    )(x, *os_, wout, mg, wmq, mqg, mk, mv, wmo)


def _pad_cols(w, n):
    return jnp.pad(w, ((0, 0), (0, n - w.shape[1])))


def _prep_l0(w_in, conv_w, a_log, dt_bias, q_g, k_g):
    c0 = 3 * 1024
    ab = w_in[:, c0:c0 + 2 * GDN_HEADS]
    c1 = c0 + 2 * GDN_HEADS
    cols = lambda i: w_in[:, c1 + i * 1024:c1 + (i + 1) * 1024].astype(BF16)
    w = dict(
        wu=w_in[:, :c0].astype(BF16), wab=_pad_cols(ab, LANES).astype(BF16),
        wza=cols(0), wq=cols(1), wk=cols(2), wv=cols(3), wzb=cols(4),
        alog=jnp.pad(a_log, (0, LANES - GDN_HEADS)).reshape(1, LANES).astype(F32),
        dtb=jnp.pad(dt_bias, (0, LANES - GDN_HEADS)).reshape(1, LANES).astype(F32),
    )
    ns = 3 * GDN_HEADS
    cw = jnp.pad(conv_w.astype(F32), ((0, 8 - CONV_TAPS), (0, 0))).reshape(8, ns, LANES).transpose(1, 0, 2)
    dm = BAND_HEADS * BAND_HD
    head_of = np.arange(dm) // BAND_HD
    seg = jnp.asarray(head_of[:, None] == np.arange(LANES)[None, :], BF16)
    qg = jnp.tile(q_g.astype(F32), BAND_HEADS).reshape(1, dm)
    kg = jnp.tile(k_g.astype(F32), BAND_HEADS).reshape(1, dm)
    return w, cw, seg, qg, kg


def _band_bias_table(rel_bias, L, chunk):
    W = BAND_PAST + L
    n = np.arange(W + L - 1)
    idx = np.clip(BAND_PAST + (L - 1) - n, -BAND_MAX_REL, BAND_MAX_REL) + BAND_MAX_REL
    strip = rel_bias.astype(F32)[:, idx]
    bias = jnp.stack([strip[:, L - 1 - a:L - 1 - a + W] for a in range(L)], axis=1)
    back = np.arange(L)[:, None] // chunk - (np.arange(W)[None, :] - BAND_PAST) // chunk
    readable = (back >= 0) & (back <= BAND_PAST // chunk)
    return bias if readable.all() else jnp.where(jnp.asarray(readable)[None], bias, -jnp.inf)


def _prep_l1(w_in, w_gate_up, gate_bias):
    qk = GLA_HEADS * GLA_DK
    vw = GLA_HEADS * GLA_DV
    o = np.cumsum([0, qk, qk, vw, GLA_RANK, vw])
    return dict(
        wq=w_in[:, o[0]:o[1]].astype(BF16), wk=w_in[:, o[1]:o[2]].astype(BF16),
        wv=w_in[:, o[2]:o[3]].astype(BF16), wlr=_pad_cols(w_in[:, o[3]:o[4]], LANES).astype(BF16),
        wz=w_in[:, o[4]:o[5]].astype(BF16),
        wg=jnp.pad(w_gate_up, ((0, LANES - GLA_RANK), (0, 0))).astype(BF16),
        gbias=gate_bias.reshape(1, qk).astype(F32),
    )


def _layer0(x, gdn_states, band_caches, mk, mv, norm_g, pw, cw, seg, qg, kg, bias,
            a_onorm_g, w_out, mnorm_g, w_mq, mq_g, w_mo, L):
    b, s, d = x.shape
    outs = _proj0(x.reshape(b * s, d), norm_g.reshape(1, d), pw, qg, kg, seg, s)
    u, gb, za, qlo, qhi, kn, klast, vlo, vhi, vlast, zb = (t.reshape((b, -1) + t.shape[1:]) for t in outs)
    if gdn_states is not None:
        conv_state, gdn_state = gdn_states
        gdn_states = (jnp.pad(conv_state.astype(F32), ((0, 0), (8 - (CONV_TAPS - 1), 0), (0, 0))),
                      gdn_state.astype(F32))
    o_a, s_new = _gdn(u, gb, za, gdn_states, cw, a_onorm_g.reshape(1, LANES), L)
    o_b = _band(qlo, qhi, kn, vlo, vhi, zb, bias, band_caches, bias.shape[1])
    y = _post(x, (o_a, o_b), w_out, mnorm_g.reshape(1, d), w_mq, mq_g.reshape(1, MEM_HD), mk, mv, w_mo)
    return y, u[:, s - (CONV_TAPS - 1):, :], s_new, klast, vlast


def _layer1(x, gla_state, mk, mv, norm_g, pw, c_onorm_g, w_out, mnorm_g, w_mq, mq_g, w_mo, L):
    b, s, d = x.shape
    q, k, v, lg, z = _proj1(x.reshape(b * s, d), norm_g.reshape(1, d), pw)
    r3 = lambda t: t.reshape(b, s, t.shape[-1])
    q, k, v, lg, z = map(r3, (q, k, v, lg, z))
    o, st = _gla(q, k, v, lg, z, gla_state, c_onorm_g.reshape(1, GLA_DV), L)
    y = _post(x, (o,), w_out, mnorm_g.reshape(1, d), w_mq, mq_g.reshape(1, MEM_HD), mk, mv, w_mo)
    return y, st


def kernel(x_prompt, x_sample, mem_prompt, state_l0_gdn_conv, state_l0_gdn, cache_l0_band_k, cache_l0_band_v, cache_l0_mem_k, cache_l0_mem_v, state_l1_gla, cache_l1_mem_k, cache_l1_mem_v, l0_norm_g, l0_w_in, l0_conv_w, l0_a_log, l0_dt_bias, l0_a_onorm_g, l0_b_q_g, l0_b_k_g, l0_b_rel_bias, l0_w_out, l0_mnorm_g, l0_mem_norm_g, l0_w_mkv, l0_mk_g, l0_w_mq, l0_mq_g, l0_w_mo, l1_norm_g, l1_w_in, l1_w_gate_up, l1_gate_bias, l1_c_onorm_g, l1_w_out, l1_mnorm_g, l1_mem_norm_g, l1_w_mkv, l1_mk_g, l1_w_mq, l1_mq_g, l1_w_mo):
    bp, sp, d = x_prompt.shape
    bs, ss, _ = x_sample.shape
    nm = mem_prompt.shape[1]
    hw = MEM_HEADS * MEM_HD
    dm = BAND_HEADS * BAND_HD
    assert sp % CHUNK_ == 0 and ss % INV_SUB == 0 and ss <= CHUNK_
    assert cache_l0_band_k.shape[1] == BAND_PAST

    pw0, cw, seg, qg, kg = _prep_l0(l0_w_in, l0_conv_w, l0_a_log, l0_dt_bias, l0_b_q_g, l0_b_k_g)
    pw1 = _prep_l1(l1_w_in, l1_w_gate_up, l1_gate_bias)
    band_rows = BAND_STEP_CHUNKS * CHUNK_ if sp % (BAND_STEP_CHUNKS * CHUNK_) == 0 else CHUNK_
    bias_p = _band_bias_table(l0_b_rel_bias, band_rows, CHUNK_)
    bias_s = _band_bias_table(l0_b_rel_bias, ss, ss)
    bf = lambda w: w.astype(BF16)
    mem2 = mem_prompt.reshape(bp * nm, d)

    p_mk0, p_mv0, mk0, mv0 = _memkv(mem2, l0_mem_norm_g.reshape(1, d), bf(l0_w_mkv), l0_mk_g.reshape(1, MEM_HD))
    mk0 = mk0.reshape(bp, nm, hw)
    mv0 = mv0.reshape(bp, nm, hw)
    l0_shared = (l0_norm_g, pw0, cw, seg, qg, kg)
    l0_tail = (l0_a_onorm_g, bf(l0_w_out), l0_mnorm_g, bf(l0_w_mq), l0_mq_g, bf(l0_w_mo))
    yp, p_conv, p_gdn, p_kn, p_v = _layer0(
        x_prompt, None, None, mk0, mv0,
        *l0_shared, bias_p, *l0_tail, CHUNK_)
    ys, s_conv, s_gdn, s_kn, s_v = _layer0(
        x_sample, (state_l0_gdn_conv, state_l0_gdn),
        (cache_l0_band_k, cache_l0_band_v),
        cache_l0_mem_k.reshape(bs, nm, hw), cache_l0_mem_v.reshape(bs, nm, hw),
        *l0_shared, bias_s, *l0_tail, ss)

    p_mk1, p_mv1, mk1, mv1 = _memkv(mem2, l1_mem_norm_g.reshape(1, d), bf(l1_w_mkv), l1_mk_g.reshape(1, MEM_HD))
    mk1 = mk1.reshape(bp, nm, hw)
    mv1 = mv1.reshape(bp, nm, hw)
    l1_tail = (l1_c_onorm_g, bf(l1_w_out), l1_mnorm_g, bf(l1_w_mq), l1_mq_g, bf(l1_w_mo))
    yp, p_gla = _layer1(yp, None, mk1, mv1, l1_norm_g, pw1, *l1_tail, CHUNK_)
    ys, s_gla = _layer1(ys, state_l1_gla.astype(F32), cache_l1_mem_k.reshape(bs, nm, hw),
                        cache_l1_mem_v.reshape(bs, nm, hw), l1_norm_g, pw1, *l1_tail, ss)

    m4 = lambda t: t.reshape(bp, nm, MEM_HEADS, MEM_HD)
    return (yp, ys, p_conv, p_gdn, p_kn, p_v,
            m4(p_mk0), m4(p_mv0), p_gla, m4(p_mk1), m4(p_mv1),
            s_conv, s_gdn, s_kn, s_v, s_gla)
```

```python
import functools

import jax
import jax.numpy as jnp
import numpy as np
from jax import lax
from jax.experimental import pallas as pl
from jax.experimental.pallas import tpu as pltpu

F32 = jnp.float32
BF16 = jnp.bfloat16
NORM_EPS = 1e-6
LOG2_E = 1.4426950408889634

CHUNK_ = 64
CONV_TAPS = 4
GDN_HEADS = 8
GDN_DK = 128
BAND_HEADS = 16
BAND_HD = 64
BAND_PAST = 512
BAND_MAX_REL = 128
GLA_HEADS = 8
GLA_DK = 128
GLA_DV = 256
GLA_RANK = 16
GLA_TAU = 16.0
MEM_HEADS = 4
MEM_HD = 128
INV_SUB = 16
GDN_GROUP = 2
GDN_STEP_CHUNKS = 1
GLA_GROUP = 4
POST_SPLIT = 4
PROJ_SPLIT = 2
SUB_TILE_MIN = 128
BAND_STEP_CHUNKS = 2
LANES = 128
VMEM_LIMIT = 56 * 1024 * 1024


def _dot(a, b):
    return jnp.dot(a, b, preferred_element_type=F32)


def _dot_nt(a, b):
    return lax.dot_general(a, b, (((1,), (1,)), ((), ())), preferred_element_type=F32)


def _dot_tn(a, b):
    return lax.dot_general(a, b, (((0,), (0,)), ((), ())), preferred_element_type=F32)


def _split3(x):
    hi = x.astype(BF16)
    r = x - hi.astype(F32)
    mid = r.astype(BF16)
    lo = (r - mid.astype(F32)).astype(BF16)
    return hi, mid, lo


def _dot_exact_lhs(a_bf, b):
    h, m, l = _split3(b)
    return _dot(a_bf, h) + (_dot(a_bf, m) + _dot(a_bf, l))


def _rms(x, g):
    ms = jnp.mean(x * x, axis=-1, keepdims=True)
    return x * lax.rsqrt(ms + NORM_EPS) * g


def _silu(x):
    return x * jax.nn.sigmoid(x)


def _softplus(x):
    return jnp.maximum(x, 0.0) + jnp.log1p(jnp.exp(-jnp.abs(x)))


def _softmax2_rows(s):
    m = jnp.max(s, axis=-1, keepdims=True)
    e = jnp.exp2(s - m)
    return e * (1.0 / jnp.sum(e, axis=-1, keepdims=True))


def _const_spec(shape):
    nd = len(shape)
    return pl.BlockSpec(shape, lambda *_: (0,) * nd, pipeline_mode=pl.Buffered(1))


def _params(sem):
    return pltpu.CompilerParams(dimension_semantics=sem, vmem_limit_bytes=VMEM_LIMIT)


def _row_tile(n, want):
    t = min(n, want)
    assert n % t == 0
    return t


def _sub_tiles(tm):
    ts = tm // PROJ_SPLIT if tm % (SUB_TILE_MIN * PROJ_SPLIT) == 0 else tm
    return [slice(t * ts, (t + 1) * ts) for t in range(tm // ts)]


def _proj0_body(x_ref, g_ref, wu_ref, wab_ref, wza_ref, wq_ref, wk_ref, wv_ref, wzb_ref,
                alog_ref, dtb_ref, qg_ref, kg_ref, seg_ref,
                u_ref, gb_ref, za_ref, qlo_ref, qhi_ref, kn_ref, klast_ref, vlo_ref, vhi_ref, vlast_ref, zb_ref,
                *, kept_tiles):
    h = _rms(x_ref[...], g_ref[...]).astype(BF16)
    dm = BAND_HEADS * BAND_HD
    lo = (lax.broadcasted_iota(jnp.int32, (1, dm), 1) & (LANES - 1)) < BAND_HD
    seg = seg_ref[...]
    q = _dot(h, wq_ref[...])
    k = _dot(h, wk_ref[...])
    ab = _dot(h, wab_ref[...])
    za = _dot(h, wza_ref[...])

    def head_rsqrt(x):
        ss = _dot((x * x).astype(BF16), seg)
        return lax.rsqrt(ss * (1.0 / BAND_HD) + NORM_EPS)

    rq = head_rsqrt(q)
    rk = head_rsqrt(k)
    zb = _dot(h, wzb_ref[...])
    v = _dot(h, wv_ref[...])
    u_ref[...] = _dot(h, wu_ref[...])
    za_ref[...] = _silu(za)
    zb_ref[...] = _silu(zb)
    lo1 = lo[:, :LANES]

    def spread(r):
        return jnp.concatenate([jnp.where(lo1, r[:, 2 * s:2 * s + 1], r[:, 2 * s + 1:2 * s + 2])
                                for s in range(BAND_HEADS // 2)], axis=1)

    qn = q * spread(rq) * (qg_ref[...] * (BAND_HD ** -0.5 * LOG2_E))
    kn = k * spread(rk) * kg_ref[...]
    qlo_ref[...] = jnp.where(lo, qn, 0.0).astype(BF16)
    qhi_ref[...] = jnp.where(lo, 0.0, qn).astype(BF16)
    kn_ref[...] = kn.astype(BF16)
    vlo_ref[...] = jnp.where(lo, v, 0.0).astype(BF16)
    vhi_ref[...] = jnp.where(lo, 0.0, v).astype(BF16)

    def hand_on():
        klast_ref[...] = kn.reshape(kn.shape[0], BAND_HEADS, BAND_HD)
        vlast_ref[...] = v.reshape(v.shape[0], BAND_HEADS, BAND_HD)

    if kept_tiles is None:
        hand_on()
    else:
        tpb, kt = kept_tiles
        pl.when(pl.program_id(0) % tpb >= tpb - kt)(hand_on)
    lane = lax.broadcasted_iota(jnp.int32, ab.shape, 1)
    gval = -jnp.exp(alog_ref[...]) * _softplus(ab + dtb_ref[...])
    gb_ref[...] = jnp.where(lane < GDN_HEADS, gval, jax.nn.sigmoid(ab))


def _proj0(x2, g, w, qg, kg, seg, rows_per_batch):
    n, d = x2.shape
    tm = _row_tile(n, 256)
    dm = BAND_HEADS * BAND_HD
    widths = (3 * 1024, LANES, 1024, dm, dm, dm, dm)
    row = lambda wd: pl.BlockSpec((tm, wd), lambda i: (i, 0))
    keep = min(BAND_PAST, rows_per_batch)
    if rows_per_batch > keep:
        assert rows_per_batch % tm == 0 and keep % tm == 0
        tpb, kt = rows_per_batch // tm, keep // tm
        last = pl.BlockSpec((tm, BAND_HEADS, BAND_HD),
                            lambda i: ((i // tpb) * kt + jnp.maximum(i % tpb - (tpb - kt), 0), 0, 0))
        n_last = (n // rows_per_batch) * keep
        kept_tiles = (tpb, kt)
    else:
        last, n_last = pl.BlockSpec((tm, BAND_HEADS, BAND_HD), lambda i: (i, 0, 0)), n
        kept_tiles = None
    last_shape = jax.ShapeDtypeStruct((n_last, BAND_HEADS, BAND_HD), F32)
    f32 = lambda rows, wd: jax.ShapeDtypeStruct((rows, wd), F32)
    bf16 = lambda wd: jax.ShapeDtypeStruct((n, wd), BF16)
    return pl.pallas_call(
        functools.partial(_proj0_body, kept_tiles=kept_tiles),
        grid=(n // tm,),
        in_specs=[row(d), _const_spec((1, d))]
        + [_const_spec((d, wd)) for wd in widths]
        + [_const_spec((1, LANES)), _const_spec((1, LANES)), _const_spec((1, dm)), _const_spec((1, dm)),
           _const_spec((dm, LANES))],
        out_specs=[row(3 * 1024), row(LANES), row(1024), row(dm), row(dm), row(dm), last, row(dm), row(dm), last,
                   row(dm)],
        out_shape=[f32(n, 3 * 1024), f32(n, LANES), f32(n, 1024), bf16(dm), bf16(dm), bf16(dm), last_shape,
                   bf16(dm), bf16(dm), last_shape, f32(n, dm)],
        compiler_params=_params(("arbitrary",)),
        name="proj0",
    )(x2, g, w["wu"], w["wab"], w["wza"], w["wq"], w["wk"], w["wv"], w["wzb"], w["alog"], w["dtb"],
      qg, kg, seg)


def _proj1_body(x_ref, g_ref, wq_ref, wk_ref, wv_ref, wlr_ref, wz_ref, wg_ref, gbias_ref,
                q_ref, k_ref, v_ref, lg_ref, z_ref):
    rows = _sub_tiles(x_ref.shape[0])
    h = [_rms(x_ref[r, :], g_ref[...]).astype(BF16) for r in rows]
    lr = [_dot(h_, wlr_ref[...]) for h_ in h]
    for r, h_ in zip(rows, h):
        q_ref[r, :] = _dot(h_, wq_ref[...])
    pre = [_dot(x.astype(BF16), wg_ref[...]) + gbias_ref[...] for x in lr]
    zs = [_dot(h_, wz_ref[...]) for h_ in h]
    for r, p_ in zip(rows, pre):
        lg_ref[r, :] = -_softplus(-p_) * (1.0 / GLA_TAU)
    for r, h_ in zip(rows, h):
        k_ref[r, :] = _dot(h_, wk_ref[...])
    for r, z_ in zip(rows, zs):
        z_ref[r, :] = _silu(z_)
    for r, h_ in zip(rows, h):
        v_ref[r, :] = _dot(h_, wv_ref[...])


def _proj1(x2, g, w):
    n, d = x2.shape
    tm = _row_tile(n, 512)
    row = lambda wd: pl.BlockSpec((tm, wd), lambda i: (i, 0))
    outw = (1024, 1024, 2048, 1024, 2048)
    return pl.pallas_call(
        _proj1_body,
        grid=(n // tm,),
        in_specs=[row(d), _const_spec((1, d)), _const_spec((d, 1024)), _const_spec((d, 1024)),
                  _const_spec((d, 2048)), _const_spec((d, LANES)), _const_spec((d, 2048)),
                  _const_spec((LANES, 1024)), _const_spec((1, 1024))],
        out_specs=[row(wd) for wd in outw],
        out_shape=[jax.ShapeDtypeStruct((n, wd), F32) for wd in outw],
        compiler_params=_params(("arbitrary",)),
        name="proj1",
    )(x2, g, w["wq"], w["wk"], w["wv"], w["wlr"], w["wz"], w["wg"], w["gbias"])


def _memkv_body(m_ref, g_ref, w_ref, kg_ref, k4_ref, v4_ref, kb_ref, vb_ref):
    h = _rms(m_ref[...], g_ref[...]).astype(BF16)
    kv = _dot(h, w_ref[...])
    hw = MEM_HEADS * MEM_HD
    k = jnp.concatenate([_rms(kv[:, hh * MEM_HD:(hh + 1) * MEM_HD], kg_ref[...]) for hh in range(MEM_HEADS)],
                        axis=1)
    v = kv[:, hw:]
    k4_ref[...] = k.reshape(k.shape[0], MEM_HEADS, MEM_HD)
    v4_ref[...] = v.reshape(v.shape[0], MEM_HEADS, MEM_HD)
    kb_ref[...] = k.astype(BF16)
    vb_ref[...] = v.astype(BF16)


def _memkv(m2, g, w_bf, kg):
    n, d = m2.shape
    tm = _row_tile(n, 256)
    hw = MEM_HEADS * MEM_HD
    row = lambda wd: pl.BlockSpec((tm, wd), lambda i: (i, 0))
    row4 = pl.BlockSpec((tm, MEM_HEADS, MEM_HD), lambda i: (i, 0, 0))
    return pl.pallas_call(
        _memkv_body,
        grid=(n // tm,),
        in_specs=[row(d), _const_spec((1, d)), _const_spec((d, 2 * hw)), _const_spec((1, MEM_HD))],
        out_specs=[row4, row4, row(hw), row(hw)],
        out_shape=[jax.ShapeDtypeStruct((n, MEM_HEADS, MEM_HD), F32)] * 2
        + [jax.ShapeDtypeStruct((n, hw), BF16)] * 2,
        compiler_params=_params(("arbitrary",)),
        name="memkv",
    )(m2, g, w_bf, kg)


def _each(fn, *lists):
    return [fn(*xs) for xs in zip(*lists)]


def _unit_lower_inverse(a, eye, bd, mm):
    d = _each(lambda x: jnp.where(bd, x, 0.0), a)
    nl = _each(lambda x, y: x - y, a, d)
    d2 = _each(mm, d, d)
    d4 = _each(mm, d2, d2)
    td = _each(lambda x, y: mm(eye - x, eye + y), d, d2)
    d8 = _each(mm, d4, d4)
    td = _each(lambda x, y: mm(x, eye + y), td, d4)
    td = _each(lambda x, y: mm(x, eye + y), td, d8)
    m = _each(mm, td, nl)
    m2 = _each(mm, m, m)
    mt = _each(mm, m, td)
    return _each(lambda x, y, z: mm(eye + x, y - z), m2, td, mt)


def _gdn_body(*refs, L, NB, G, has_state):
    if has_state:
        u_ref, gb_ref, za_ref, cw_ref, og_ref, cinit_ref, sinit_ref, o_ref, s_ref, ubuf = refs
    else:
        u_ref, gb_ref, za_ref, cw_ref, og_ref, o_ref, s_ref, ubuf = refs
    c = pl.program_id(1)
    ns = 3 * GDN_HEADS
    hist = 8
    R = G * L

    @pl.when(c == 0)
    def _():
        if has_state:
            for b in range(NB):
                for j in range(ns):
                    ubuf[b * ns + j, 0:hist, :] = cinit_ref[b, :, j * LANES:(j + 1) * LANES]
            s_ref[...] = sinit_ref[...]
        else:
            ubuf[:, 0:hist, :] = jnp.zeros((NB * ns, hist, LANES), F32)
            s_ref[...] = jnp.zeros(s_ref.shape, F32)

    for b in range(NB):
        for j in range(ns):
            ubuf[b * ns + j, hist:hist + R, :] = u_ref[b, :, j * LANES:(j + 1) * LANES]
    base = hist - (CONV_TAPS - 1)
    ys = []
    for b in range(NB):
        bsl = slice(b * ns, (b + 1) * ns)
        yb = ubuf[bsl, base:base + R, :] * cw_ref[:, 0:1, :]
        for i in range(1, CONV_TAPS):
            yb = yb + ubuf[bsl, base + i:base + i + R, :] * cw_ref[:, i:i + 1, :]
        ys.append(_silu(yb))
    ubuf[:, base:hist, :] = ubuf[:, base + R:hist + R, :]

    row = lax.broadcasted_iota(jnp.int32, (L, 2 * L), 0)
    lane = lax.broadcasted_iota(jnp.int32, (L, 2 * L), 1)
    col = lane & (L - 1)
    left = lane < L
    incl = row >= col
    strict = row > col
    sub_shift = INV_SUB.bit_length() - 1
    bd = lax.shift_right_logical(row, sub_shift) == lax.shift_right_logical(col, sub_shift)
    eye = jnp.where(row == col, 1.0, 0.0).astype(F32)
    trow = lax.broadcasted_iota(jnp.int32, (L, L), 0)
    tcol = lax.broadcasted_iota(jnp.int32, (L, L), 1)
    tri = jnp.where(trow >= tcol, 1.0, 0.0).astype(BF16)
    og = og_ref[...]
    heads = [(b, g, h) for b in range(NB) for g in range(G) for h in range(GDN_HEADS)]
    pairs = range(len(heads) // 2)
    rows_of = lambda g: slice(g * L, (g + 1) * L)
    l2n = lambda x: x * lax.rsqrt(jnp.sum(x * x, axis=-1, keepdims=True) + NORM_EPS)
    q = [l2n(ys[b][h][rows_of(g)]) * (GDN_DK ** -0.5) for b, g, h in heads]
    k = [l2n(ys[b][GDN_HEADS + h][rows_of(g)]) for b, g, h in heads]
    v = [ys[b][2 * GDN_HEADS + h][rows_of(g)] for b, g, h in heads]
    gc, gr, bc = [], [], []
    for b in range(NB):
        for g in range(G):
            gbv = gb_ref[b, rows_of(g), :]
            gcum = _dot_exact_lhs(tri, gbv)
            gpad = jnp.concatenate([gcum, jnp.zeros((LANES - L, LANES), F32)], axis=0)
            gt = gpad.T
            for h in range(GDN_HEADS):
                gc.append(gcum[:, h:h + 1])
                gr.append(gt[h:h + 1, 0:L])
                bc.append(gbv[:, GDN_HEADS + h:GDN_HEADS + h + 1])
    side = lambda x0, x1: jnp.where(left, x0, x1)
    gcp = [side(gc[2 * p], gc[2 * p + 1]) for p in pairs]
    grp = [jnp.concatenate([gr[2 * p], gr[2 * p + 1]], axis=1) for p in pairs]
    bcp = [side(bc[2 * p], bc[2 * p + 1]) for p in pairs]
    dec = _each(lambda c_, r_: jnp.where(incl, jnp.exp(jnp.where(incl, c_ - r_, 0.0)), 0.0), gcp, grp)
    kb = _each(lambda x: x.astype(BF16), k)
    qb = _each(lambda x: x.astype(BF16), q)

    def blockdiag(y0, y1):
        z0 = jnp.zeros(y1.shape, y1.dtype)
        z1 = jnp.zeros(y0.shape, y0.dtype)
        return jnp.concatenate([jnp.concatenate([y0, z0], axis=1), jnp.concatenate([z1, y1], axis=1)], axis=0)

    def mm(x, y):
        yb = y.astype(BF16)
        zero = jnp.zeros_like(yb)
        return _dot(x.astype(BF16), jnp.concatenate([jnp.where(left, yb, zero), jnp.where(left, zero, yb)], axis=0))

    kq = [_dot_nt(jnp.concatenate([jnp.concatenate([kb[2 * p], qb[2 * p]], axis=0),
                                   jnp.concatenate([kb[2 * p + 1], qb[2 * p + 1]], axis=0)], axis=1),
                  blockdiag(kb[2 * p], kb[2 * p + 1])) for p in pairs]
    a = _each(lambda b_, x, d_: jnp.where(strict, b_ * x[:L] * d_, 0.0), bcp, kq, dec)
    t = _unit_lower_inverse(a, eye, bd, mm)
    eg = _each(jnp.exp, gc)
    rhs = _each(lambda b_, v_, e_, k_: jnp.concatenate([b_ * v_, (b_ * e_) * k_], axis=1).astype(BF16),
                bc, v, eg, k)
    sol = [_dot(t[p].astype(BF16), blockdiag(rhs[2 * p], rhs[2 * p + 1])) for p in pairs]
    solk = lambda i: sol[i // 2][:, 2 * (i % 2) * LANES + LANES:2 * (i % 2 + 1) * LANES]
    solv = lambda i: sol[i // 2][:, 2 * (i % 2) * LANES:2 * (i % 2) * LANES + LANES]
    qkd = [(kq[p][L:] * dec[p]).astype(BF16) for p in pairs]
    gl = [c_[L - 1:L, :] for c_ in gc]
    kd = _each(lambda k_, l_, c_: (k_ * jnp.exp(l_ - c_)).astype(BF16), k, gl, gc)

    item = lambda b, g, h: (b * G + g) * GDN_HEADS + h
    half = lambda x, i: x[:, (i % 2) * LANES:(i % 2 + 1) * LANES]
    s = {(b, h): s_ref[b, h] for b in range(NB) for h in range(GDN_HEADS)}
    for g in range(G):
        ids = [item(b, g, h) for b in range(NB) for h in range(GDN_HEADS)]
        sb = {i: s[(heads[i][0], heads[i][2])].astype(BF16) for i in ids}
        ksq = {i: _dot(jnp.concatenate([jnp.concatenate([solk(i).astype(BF16), qb[i]], axis=0),
                                        jnp.concatenate([solk(i + 1).astype(BF16), qb[i + 1]], axis=0)], axis=1),
                       blockdiag(sb[i], sb[i + 1])) for i in ids[::2]}
        ub = {i: (solv(i) - half(ksq[i - i % 2][:L], i)).astype(BF16) for i in ids}
        qku = {i: _dot(qkd[i // 2], blockdiag(ub[i], ub[i + 1])) for i in ids[::2]}
        ktu = {i: _dot_tn(kd[i], ub[i]) for i in ids}
        for i in ids:
            b, _, h = heads[i]
            s[(b, h)] = s[(b, h)] * jnp.exp(gl[i]) + ktu[i]
        for b in range(NB):
            za = za_ref[b, rows_of(g), :]
            outs = [(_rms(half(ksq[i - i % 2][L:], i) * eg[i] + half(qku[i - i % 2], i), og)
                     * za[:, heads[i][2] * LANES:(heads[i][2] + 1) * LANES]).astype(BF16)
                    for i in ids if heads[i][0] == b]
            o_ref[b, rows_of(g), :] = jnp.concatenate(outs, axis=1)
    for (b, h), val in s.items():
        s_ref[b, h] = val


def _gdn(u, gb, za, state, cw, og, L):
    b, s, _ = u.shape
    nc = s // L
    ns = 3 * GDN_HEADS
    nb = GDN_GROUP if b % GDN_GROUP == 0 else 1
    g = GDN_STEP_CHUNKS if nc % GDN_STEP_CHUNKS == 0 else 1
    blk = lambda wd: pl.BlockSpec((nb, g * L, wd), lambda i, j: (i, j, 0))
    has_state = state is not None
    state_specs = [pl.BlockSpec((nb, 8, ns * LANES), lambda i, j: (i, 0, 0)),
                   pl.BlockSpec((nb, GDN_HEADS, GDN_DK, LANES), lambda i, j: (i, 0, 0, 0))]
    return pl.pallas_call(
        functools.partial(_gdn_body, L=L, NB=nb, G=g, has_state=has_state),
        grid=(b // nb, nc // g),
        in_specs=[blk(ns * LANES), blk(LANES), blk(GDN_HEADS * LANES),
                  _const_spec((ns, 8, LANES)), _const_spec((1, LANES))] + (state_specs if has_state else []),
        out_specs=[blk(GDN_HEADS * LANES),
                   pl.BlockSpec((nb, GDN_HEADS, GDN_DK, LANES), lambda i, j: (i, 0, 0, 0))],
        out_shape=[jax.ShapeDtypeStruct((b, s, GDN_HEADS * LANES), BF16),
                   jax.ShapeDtypeStruct((b, GDN_HEADS, GDN_DK, LANES), F32)],
        scratch_shapes=[pltpu.VMEM((nb * ns, 8 + g * L, LANES), F32)],
        compiler_params=_params(("arbitrary", "arbitrary")),
        name="gdn",
    )(u, gb, za, cw, og, *(state if has_state else ()))


def _band_body(*refs, L, W, has_cache):
    if has_cache:
        (qlo_ref, qhi_ref, k_ref, vlo_ref, vhi_ref, zb_ref, bias_ref, ck_ref, cv_ref,
         o_ref, kscr, vlo, vhi, biasm) = refs
    else:
        (qlo_ref, qhi_ref, k_ref, vlo_ref, vhi_ref, zb_ref, bias_ref,
         o_ref, kscr, vlo, vhi, biasm) = refs
    c = pl.program_id(1)
    dm = BAND_HEADS * BAND_HD
    pad_chunks = BAND_PAST // L

    @pl.when(c == 0)
    def _():
        if has_cache:
            lo = (lax.broadcasted_iota(jnp.int32, (1, dm), 1) & (LANES - 1)) < BAND_HD
            kscr[0:BAND_PAST, :] = ck_ref[0].reshape(BAND_PAST, dm).astype(BF16)
            cv = cv_ref[0].reshape(BAND_PAST, dm)
            vlo[0:BAND_PAST, :] = jnp.where(lo, cv, 0.0).astype(BF16)
            vhi[0:BAND_PAST, :] = jnp.where(lo, 0.0, cv).astype(BF16)
        else:
            zero = jnp.zeros((BAND_PAST, dm), BF16)
            kscr[0:BAND_PAST, :] = zero
            vlo[0:BAND_PAST, :] = zero
            vhi[0:BAND_PAST, :] = zero

    new0 = pl.multiple_of(BAND_PAST + c * L, L)
    kscr[pl.ds(new0, L), :] = k_ref[0]
    vlo[pl.ds(new0, L), :] = vlo_ref[0]
    vhi[pl.ds(new0, L), :] = vhi_ref[0]

    if has_cache:
        @pl.when(c == 0)
        def _():
            biasm[...] = bias_ref[...] * LOG2_E
    else:
        @pl.when(c < pad_chunks)
        def _():
            wcol = lax.broadcasted_iota(jnp.int32, (1, W), 1)
            valid = (wcol + c * L) >= BAND_PAST
            for h in range(BAND_HEADS):
                biasm[h] = jnp.where(valid, bias_ref[h] * LOG2_E, -jnp.inf)

        @pl.when(c == pad_chunks)
        def _():
            biasm[...] = bias_ref[...] * LOG2_E
    bias_src = biasm

    w0 = pl.multiple_of(c * L, L)
    zb = zb_ref[0]
    slabs = [slice(s * LANES, (s + 1) * LANES) for s in range(BAND_HEADS // 2)]
    sc = []
    for sl in slabs:
        ks = kscr[pl.ds(w0, W), sl]
        sc.append(_dot_nt(qlo_ref[0, :, sl], ks))
        sc.append(_dot_nt(qhi_ref[0, :, sl], ks))
    ps, rs = [], []
    for h in range(BAND_HEADS):
        x = sc[h] + bias_src[h]
        e = jnp.exp2(x - jnp.max(x, axis=-1, keepdims=True))
        rs.append(1.0 / jnp.sum(e, axis=-1, keepdims=True))
        ps.append(e.astype(BF16))
    pv = []
    for i, sl in enumerate(slabs):
        pv.append(_dot(ps[2 * i], vlo[pl.ds(w0, W), sl]))
        pv.append(_dot(ps[2 * i + 1], vhi[pl.ds(w0, W), sl]))
    outs = [((pv[2 * i] * rs[2 * i] + pv[2 * i + 1] * rs[2 * i + 1]) * zb[:, sl]).astype(BF16)
            for i, sl in enumerate(slabs)]
    o_ref[0] = jnp.concatenate(outs, axis=1)


def _band(qlo, qhi, kn, vlo, vhi, zb, bias, caches, L):
    b, s, dm = qlo.shape
    nc = s // L
    W = BAND_PAST + L
    blk = pl.BlockSpec((1, L, dm), lambda i, j: (i, j, 0))
    cache = pl.BlockSpec((1, BAND_PAST, BAND_HEADS, BAND_HD), lambda i, j: (i, 0, 0, 0))
    has_cache = caches is not None
    scratch = [pltpu.VMEM((BAND_PAST + s, dm), BF16)] * 3 + [pltpu.VMEM((BAND_HEADS, L, W), F32)]
    return pl.pallas_call(
        functools.partial(_band_body, L=L, W=W, has_cache=has_cache),
        grid=(b, nc),
        in_specs=[blk] * 6 + [_const_spec((BAND_HEADS, L, W))] + ([cache, cache] if has_cache else []),
        out_specs=blk,
        out_shape=jax.ShapeDtypeStruct((b, s, dm), BF16),
        scratch_shapes=scratch,
        compiler_params=_params(("arbitrary", "arbitrary")),
        name="band",
    )(qlo, qhi, kn, vlo, vhi, zb, bias, *(caches if has_cache else ()))


def _gla_body(*refs, L, NB, has_state):
    if has_state:
        q_ref, k_ref, v_ref, lg_ref, z_ref, og_ref, sinit_ref, o_ref, sout_ref, st_ref, cbs = refs
    else:
        q_ref, k_ref, v_ref, lg_ref, z_ref, og_ref, o_ref, sout_ref, st_ref, cbs = refs
    c = pl.program_id(1)

    @pl.when(c == 0)
    def _():
        for b in range(NB):
            for h in range(GLA_HEADS):
                st_ref[b, h] = sinit_ref[b, h].T if has_state else jnp.zeros((GLA_DV, GLA_DK), F32)

    levels = [L >> (t + 1) for t in range(L.bit_length() - 1)]
    small = [s for s in levels if 2 < 2 * s < 16]
    row = lax.broadcasted_iota(jnp.int32, (L, L), 0)
    col = lax.broadcasted_iota(jnp.int32, (L, L), 1)
    anchor = lambda s: lax.shift_left(lax.shift_right_logical(row, s.bit_length()), s.bit_length()) + (s - 1)
    between = lambda s: (col > jnp.minimum(row, anchor(s))) & (col <= jnp.maximum(row, anchor(s)))
    onehot = lambda m: jnp.where(m, 1.0, 0.0).astype(BF16)
    tri = onehot(row >= col)
    stack = jnp.concatenate([onehot(between(s)) for s in small], axis=0)
    ex = jnp.exp2
    og = og_ref[...]
    scale = GLA_DK ** -0.5
    lg2 = [lg_ref[b] * LOG2_E for b in range(NB)]
    pieces = [_split3(x) for x in lg2]
    cb_all = [_dot(tri, p[0]) + (_dot(tri, p[1]) + _dot(tri, p[2])) for p in pieces]
    y_small = [_dot(stack, p[0]) + _dot(stack, p[1]) for p in pieces]
    odd_row = (lax.broadcasted_iota(jnp.int32, (L, 1), 0) & 1) == 1
    ydec = [dict() for _ in range(NB)]
    for b in range(NB):
        cbs[b] = cb_all[b]
        ydec[b][1] = jnp.where(odd_row, lg2[b], 0.0)
        for t, s in enumerate(small):
            ydec[b][s] = y_small[b][t * L:(t + 1) * L]
        for s in levels:
            if s not in ydec[b]:
                anc = jnp.concatenate(
                    [jnp.broadcast_to(cbs[b, p * 2 * s + s - 1:p * 2 * s + s, :], (2 * s, cbs.shape[2]))
                     for p in range(L // (2 * s))], axis=0)
                ydec[b][s] = -jnp.abs(cb_all[b] - anc)

    heads = [(b, h) for b in range(NB) for h in range(GLA_HEADS)]
    pairs = range(len(heads) // 2)
    ksl = lambda h: slice(h * GLA_DK, (h + 1) * GLA_DK)
    vsl = lambda h: slice(h * GLA_DV, (h + 1) * GLA_DV)
    cb = [cb_all[b][:, ksl(h)] for b, h in heads]
    q = [q_ref[b, :, ksl(h)] * scale for b, h in heads]
    k = [k_ref[b, :, ksl(h)] for b, h in heads]
    vb = [v_ref[b, :, vsl(h)].astype(BF16) for b, h in heads]
    st = [st_ref[b, h] for b, h in heads]
    o_inter = _each(lambda q_, c_, s_: _dot_nt((q_ * ex(c_)).astype(BF16), s_.astype(BF16)), q, cb, st)
    cl = [c_[L - 1:L] for c_ in cb]
    ke = _each(lambda k_, l_, c_: (k_ * ex(l_ - c_)).astype(BF16), k, cl, cb)
    vtk = _each(_dot_tn, vb, ke)

    zk = jnp.zeros((L, GLA_DK), BF16)
    zv = jnp.zeros((L, GLA_DV), BF16)

    def side_by_side(qs, ks):
        lhs = jnp.concatenate(qs, axis=1).astype(BF16)
        rhs = jnp.concatenate([jnp.concatenate([ks[0].astype(BF16), zk], axis=1),
                               jnp.concatenate([zk, ks[1].astype(BF16)], axis=1)], axis=0)
        return _dot_nt(lhs, rhs)

    prow = lax.broadcasted_iota(jnp.int32, (L, 2 * L), 0)
    pcol = lax.broadcasted_iota(jnp.int32, (L, 2 * L), 1) & (L - 1)
    att = [jnp.where(prow == pcol, side_by_side((q[2 * p], q[2 * p + 1]), (k[2 * p], k[2 * p + 1])), 0.0)
           for p in pairs]
    for s in levels:
        sh = s.bit_length() - 1
        same_parent = lax.shift_right_logical(prow, sh + 1) == lax.shift_right_logical(pcol, sh + 1)
        take = same_parent & ((lax.shift_right_logical(prow, sh) & 1) == 1) & \
            ((lax.shift_right_logical(pcol, sh) & 1) == 0)
        f = [ex(ydec[b][s][:, ksl(h)]) for b, h in heads]
        prod = [side_by_side((q[2 * p] * f[2 * p], q[2 * p + 1] * f[2 * p + 1]),
                             (k[2 * p] * f[2 * p], k[2 * p + 1] * f[2 * p + 1])) for p in pairs]
        att = [jnp.where(take, prod[p], att[p]) for p in pairs]
    vpair = [jnp.concatenate([jnp.concatenate([vb[2 * p], zv], axis=1),
                              jnp.concatenate([zv, vb[2 * p + 1]], axis=1)], axis=0) for p in pairs]
    o_intra = [_dot(att[p].astype(BF16), vpair[p]) for p in pairs]
    new_st = [st[i] * ex(cl[i]) + vtk[i] for i in range(len(heads))]
    for i, (b, h) in enumerate(heads):
        st_ref[b, h] = new_st[i]
    for b in range(NB):
        z = z_ref[b]
        outs = [(_rms(o_inter[i] + o_intra[i // 2][:, (i % 2) * GLA_DV:(i % 2 + 1) * GLA_DV], og)
                 * z[:, vsl(h)]).astype(BF16) for i, (b_, h) in enumerate(heads) if b_ == b]
        o_ref[b] = jnp.concatenate(outs, axis=1)

    @pl.when(c == pl.num_programs(1) - 1)
    def _():
        for i, (b, h) in enumerate(heads):
            sout_ref[b, h] = new_st[i].T


def _gla(q, k, v, lg, z, sinit, og, L):
    b, s, _ = q.shape
    nc = s // L
    nb = GLA_GROUP if b % GLA_GROUP == 0 else 1
    qk = pl.BlockSpec((nb, L, GLA_HEADS * GLA_DK), lambda i, j: (i, j, 0))
    vv = pl.BlockSpec((nb, L, GLA_HEADS * GLA_DV), lambda i, j: (i, j, 0))
    st = pl.BlockSpec((nb, GLA_HEADS, GLA_DK, GLA_DV), lambda i, j: (i, 0, 0, 0))
    has_state = sinit is not None
    return pl.pallas_call(
        functools.partial(_gla_body, L=L, NB=nb, has_state=has_state),
        grid=(b // nb, nc),
        in_specs=[qk, qk, vv, qk, vv, _const_spec((1, GLA_DV))] + ([st] if has_state else []),
        out_specs=[vv, st],
        out_shape=[jax.ShapeDtypeStruct((b, s, GLA_HEADS * GLA_DV), BF16),
                   jax.ShapeDtypeStruct((b, GLA_HEADS, GLA_DK, GLA_DV), F32)],
        scratch_shapes=[pltpu.VMEM((nb, GLA_HEADS, GLA_DV, GLA_DK), F32),
                        pltpu.VMEM((nb, L, GLA_HEADS * GLA_DK), F32)],
        compiler_params=_params(("arbitrary", "arbitrary")),
        name="gla",
    )(q, k, v, lg, z, og, *((sinit,) if has_state else ()))


def _post_body(*refs, n_o):
    x_ref = refs[0]
    o_refs = refs[1:1 + n_o]
    wout_ref, mg_ref, wmq_ref, mqg_ref, mk_ref, mv_ref, wmo_ref, y_ref = refs[1 + n_o:]
    tm = x_ref.shape[1]
    ts = tm // POST_SPLIT if tm % (SUB_TILE_MIN * POST_SPLIT) == 0 else tm
    rows = [slice(t * ts, (t + 1) * ts) for t in range(tm // ts)]
    acc = [x_ref[0, r, :] for r in rows]
    off = 0
    for o_ref in o_refs:
        kd = o_ref.shape[-1]
        w = wout_ref[off:off + kd, :]
        acc = [a + _dot(o_ref[0, r, :], w) for a, r in zip(acc, rows)]
        off += kd
    hm = [_rms(a, mg_ref[...]).astype(BF16) for a in acc]
    qz = [_dot(h_, wmq_ref[...]) for h_ in hm]
    hw = MEM_HEADS * MEM_HD
    sls = [slice(h * MEM_HD, (h + 1) * MEM_HD) for h in range(MEM_HEADS)]
    mkb = [mk_ref[0, :, sl].astype(BF16) for sl in sls]
    mvb = [mv_ref[0, :, sl].astype(BF16) for sl in sls]
    qscale = MEM_HD ** -0.5 * LOG2_E
    qn = [[(_rms(z[:, sl], mqg_ref[...]) * qscale).astype(BF16) for sl in sls] for z in qz]
    sc = [[_dot_nt(qh, kh) for qh, kh in zip(qt, mkb)] for qt in qn]
    p = [[_softmax2_rows(x).astype(BF16) for x in st] for st in sc]
    oh = [[_dot(ph, vh) for ph, vh in zip(pt, mvb)] for pt in p]
    for t, r in enumerate(rows):
        outs = [oh[t][h] * _silu(qz[t][:, hw + h * MEM_HD:hw + (h + 1) * MEM_HD]) for h in range(MEM_HEADS)]
        oh[t] = jnp.concatenate(outs, axis=1).astype(BF16)
    ym = [_dot(om, wmo_ref[...]) for om in oh]
    for t, r in enumerate(rows):
        y_ref[0, r, :] = acc[t] + ym[t]


def _post(x, os_, wout, mg, wmq, mqg, mk, mv, wmo):
    b, s, d = x.shape
    tm = _row_tile(s, 256 * POST_SPLIT)
    hw = MEM_HEADS * MEM_HD
    nm = mk.shape[1]
    blk = lambda wd: pl.BlockSpec((1, tm, wd), lambda i, j: (i, j, 0))
    mem = pl.BlockSpec((1, nm, hw), lambda i, j: (i, 0, 0))
    kin = sum(o.shape[-1] for o in os_)
    return pl.pallas_call(
        functools.partial(_post_body, n_o=len(os_)),
        grid=(b, s // tm),
        in_specs=[blk(d)] + [blk(o.shape[-1]) for o in os_]
        + [_const_spec((kin, d)), _const_spec((1, d)), _const_spec((d, 2 * hw)), _const_spec((1, MEM_HD)),
           mem, mem, _const_spec((hw, d))],
        out_specs=blk(d),
        out_shape=jax.ShapeDtypeStruct((b, s, d), F32),
        compiler_params=_params(("arbitrary", "arbitrary")),
        name="post",
    )(x, *os_, wout, mg, wmq, mqg, mk, mv, wmo)


def _pad_cols(w, n):
    return jnp.pad(w, ((0, 0), (0, n - w.shape[1])))


def _prep_l0(w_in, conv_w, a_log, dt_bias, q_g, k_g):
    c0 = 3 * 1024
    ab = w_in[:, c0:c0 + 2 * GDN_HEADS]
    c1 = c0 + 2 * GDN_HEADS
    cols = lambda i: w_in[:, c1 + i * 1024:c1 + (i + 1) * 1024].astype(BF16)
    w = dict(
        wu=w_in[:, :c0].astype(BF16), wab=_pad_cols(ab, LANES).astype(BF16),
        wza=cols(0), wq=cols(1), wk=cols(2), wv=cols(3), wzb=cols(4),
        alog=jnp.pad(a_log, (0, LANES - GDN_HEADS)).reshape(1, LANES).astype(F32),
        dtb=jnp.pad(dt_bias, (0, LANES - GDN_HEADS)).reshape(1, LANES).astype(F32),
    )
    ns = 3 * GDN_HEADS
    cw = jnp.pad(conv_w.astype(F32), ((0, 8 - CONV_TAPS), (0, 0))).reshape(8, ns, LANES).transpose(1, 0, 2)
    dm = BAND_HEADS * BAND_HD
    head_of = np.arange(dm) // BAND_HD
    seg = jnp.asarray(head_of[:, None] == np.arange(LANES)[None, :], BF16)
    qg = jnp.tile(q_g.astype(F32), BAND_HEADS).reshape(1, dm)
    kg = jnp.tile(k_g.astype(F32), BAND_HEADS).reshape(1, dm)
    return w, cw, seg, qg, kg


def _band_bias_table(rel_bias, L, chunk):
    W = BAND_PAST + L
    n = np.arange(W + L - 1)
    idx = np.clip(BAND_PAST + (L - 1) - n, -BAND_MAX_REL, BAND_MAX_REL) + BAND_MAX_REL
    strip = rel_bias.astype(F32)[:, idx]
    bias = jnp.stack([strip[:, L - 1 - a:L - 1 - a + W] for a in range(L)], axis=1)
    back = np.arange(L)[:, None] // chunk - (np.arange(W)[None, :] - BAND_PAST) // chunk
    readable = (back >= 0) & (back <= BAND_PAST // chunk)
    return bias if readable.all() else jnp.where(jnp.asarray(readable)[None], bias, -jnp.inf)


def _prep_l1(w_in, w_gate_up, gate_bias):
    qk = GLA_HEADS * GLA_DK
    vw = GLA_HEADS * GLA_DV
    o = np.cumsum([0, qk, qk, vw, GLA_RANK, vw])
    return dict(
        wq=w_in[:, o[0]:o[1]].astype(BF16), wk=w_in[:, o[1]:o[2]].astype(BF16),
        wv=w_in[:, o[2]:o[3]].astype(BF16), wlr=_pad_cols(w_in[:, o[3]:o[4]], LANES).astype(BF16),
        wz=w_in[:, o[4]:o[5]].astype(BF16),
        wg=jnp.pad(w_gate_up, ((0, LANES - GLA_RANK), (0, 0))).astype(BF16),
        gbias=gate_bias.reshape(1, qk).astype(F32),
    )


def _layer0(x, gdn_states, band_caches, mk, mv, norm_g, pw, cw, seg, qg, kg, bias,
            a_onorm_g, w_out, mnorm_g, w_mq, mq_g, w_mo, L):
    b, s, d = x.shape
    outs = _proj0(x.reshape(b * s, d), norm_g.reshape(1, d), pw, qg, kg, seg, s)
    u, gb, za, qlo, qhi, kn, klast, vlo, vhi, vlast, zb = (t.reshape((b, -1) + t.shape[1:]) for t in outs)
    if gdn_states is not None:
        conv_state, gdn_state = gdn_states
        gdn_states = (jnp.pad(conv_state.astype(F32), ((0, 0), (8 - (CONV_TAPS - 1), 0), (0, 0))),
                      gdn_state.astype(F32))
    o_a, s_new = _gdn(u, gb, za, gdn_states, cw, a_onorm_g.reshape(1, LANES), L)
    o_b = _band(qlo, qhi, kn, vlo, vhi, zb, bias, band_caches, bias.shape[1])
    y = _post(x, (o_a, o_b), w_out, mnorm_g.reshape(1, d), w_mq, mq_g.reshape(1, MEM_HD), mk, mv, w_mo)
    return y, u[:, s - (CONV_TAPS - 1):, :], s_new, klast, vlast


def _layer1(x, gla_state, mk, mv, norm_g, pw, c_onorm_g, w_out, mnorm_g, w_mq, mq_g, w_mo, L):
    b, s, d = x.shape
    q, k, v, lg, z = _proj1(x.reshape(b * s, d), norm_g.reshape(1, d), pw)
    r3 = lambda t: t.reshape(b, s, t.shape[-1])
    q, k, v, lg, z = map(r3, (q, k, v, lg, z))
    o, st = _gla(q, k, v, lg, z, gla_state, c_onorm_g.reshape(1, GLA_DV), L)
    y = _post(x, (o,), w_out, mnorm_g.reshape(1, d), w_mq, mq_g.reshape(1, MEM_HD), mk, mv, w_mo)
    return y, st


def kernel(x_prompt, x_sample, mem_prompt, state_l0_gdn_conv, state_l0_gdn, cache_l0_band_k, cache_l0_band_v, cache_l0_mem_k, cache_l0_mem_v, state_l1_gla, cache_l1_mem_k, cache_l1_mem_v, l0_norm_g, l0_w_in, l0_conv_w, l0_a_log, l0_dt_bias, l0_a_onorm_g, l0_b_q_g, l0_b_k_g, l0_b_rel_bias, l0_w_out, l0_mnorm_g, l0_mem_norm_g, l0_w_mkv, l0_mk_g, l0_w_mq, l0_mq_g, l0_w_mo, l1_norm_g, l1_w_in, l1_w_gate_up, l1_gate_bias, l1_c_onorm_g, l1_w_out, l1_mnorm_g, l1_mem_norm_g, l1_w_mkv, l1_mk_g, l1_w_mq, l1_mq_g, l1_w_mo):
    bp, sp, d = x_prompt.shape
    bs, ss, _ = x_sample.shape
    nm = mem_prompt.shape[1]
    hw = MEM_HEADS * MEM_HD
    dm = BAND_HEADS * BAND_HD
    assert sp % CHUNK_ == 0 and ss % INV_SUB == 0 and ss <= CHUNK_
    assert cache_l0_band_k.shape[1] == BAND_PAST

    pw0, cw, seg, qg, kg = _prep_l0(l0_w_in, l0_conv_w, l0_a_log, l0_dt_bias, l0_b_q_g, l0_b_k_g)
    pw1 = _prep_l1(l1_w_in, l1_w_gate_up, l1_gate_bias)
    band_rows = BAND_STEP_CHUNKS * CHUNK_ if sp % (BAND_STEP_CHUNKS * CHUNK_) == 0 else CHUNK_
    bias_p = _band_bias_table(l0_b_rel_bias, band_rows, CHUNK_)
    bias_s = _band_bias_table(l0_b_rel_bias, ss, ss)
    bf = lambda w: w.astype(BF16)
    mem2 = mem_prompt.reshape(bp * nm, d)

    p_mk0, p_mv0, mk0, mv0 = _memkv(mem2, l0_mem_norm_g.reshape(1, d), bf(l0_w_mkv), l0_mk_g.reshape(1, MEM_HD))
    mk0 = mk0.reshape(bp, nm, hw)
    mv0 = mv0.reshape(bp, nm, hw)
    l0_shared = (l0_norm_g, pw0, cw, seg, qg, kg)
    l0_tail = (l0_a_onorm_g, bf(l0_w_out), l0_mnorm_g, bf(l0_w_mq), l0_mq_g, bf(l0_w_mo))
    yp, p_conv, p_gdn, p_kn, p_v = _layer0(
        x_prompt, None, None, mk0, mv0,
        *l0_shared, bias_p, *l0_tail, CHUNK_)
    ys, s_conv, s_gdn, s_kn, s_v = _layer0(
        x_sample, (state_l0_gdn_conv, state_l0_gdn),
        (cache_l0_band_k, cache_l0_band_v),
        cache_l0_mem_k.reshape(bs, nm, hw), cache_l0_mem_v.reshape(bs, nm, hw),
        *l0_shared, bias_s, *l0_tail, ss)

    p_mk1, p_mv1, mk1, mv1 = _memkv(mem2, l1_mem_norm_g.reshape(1, d), bf(l1_w_mkv), l1_mk_g.reshape(1, MEM_HD))
    mk1 = mk1.reshape(bp, nm, hw)
    mv1 = mv1.reshape(bp, nm, hw)
    l1_tail = (l1_c_onorm_g, bf(l1_w_out), l1_mnorm_g, bf(l1_w_mq), l1_mq_g, bf(l1_w_mo))
    yp, p_gla = _layer1(yp, None, mk1, mv1, l1_norm_g, pw1, *l1_tail, CHUNK_)
    ys, s_gla = _layer1(ys, state_l1_gla.astype(F32), cache_l1_mem_k.reshape(bs, nm, hw),
                        cache_l1_mem_v.reshape(bs, nm, hw), l1_norm_g, pw1, *l1_tail, ss)

    m4 = lambda t: t.reshape(bp, nm, MEM_HEADS, MEM_HD)
    return (yp, ys, p_conv, p_gdn, p_kn, p_v,
            m4(p_mk0), m4(p_mv0), p_gla, m4(p_mk1), m4(p_mv1),
            s_conv, s_gdn, s_kn, s_v, s_gla)
```

```python
import functools

import jax
import jax.numpy as jnp
import numpy as np
from jax import lax
from jax.experimental import pallas as pl
from jax.experimental.pallas import tpu as pltpu

F32 = jnp.float32
BF16 = jnp.bfloat16
NORM_EPS = 1e-6
LOG2_E = 1.4426950408889634

CHUNK_ = 64
CONV_TAPS = 4
GDN_HEADS = 8
GDN_DK = 128
BAND_HEADS = 16
BAND_HD = 64
BAND_PAST = 512
BAND_MAX_REL = 128
GLA_HEADS = 8
GLA_DK = 128
GLA_DV = 256
GLA_RANK = 16
GLA_TAU = 16.0
MEM_HEADS = 4
MEM_HD = 128
INV_SUB = 16
GDN_GROUP = 2
GDN_STEP_CHUNKS = 1
GLA_GROUP = 4
POST_SPLIT = 4
PROJ_SPLIT = 2
SUB_TILE_MIN = 128
BAND_STEP_CHUNKS = 2
LANES = 128
VMEM_LIMIT = 56 * 1024 * 1024


def _dot(a, b):
    return jnp.dot(a, b, preferred_element_type=F32)


def _dot_nt(a, b):
    return lax.dot_general(a, b, (((1,), (1,)), ((), ())), preferred_element_type=F32)


def _dot_tn(a, b):
    return lax.dot_general(a, b, (((0,), (0,)), ((), ())), preferred_element_type=F32)


def _split3(x):
    hi = x.astype(BF16)
    r = x - hi.astype(F32)
    mid = r.astype(BF16)
    lo = (r - mid.astype(F32)).astype(BF16)
    return hi, mid, lo


def _dot_exact_lhs(a_bf, b):
    h, m, l = _split3(b)
    return _dot(a_bf, h) + (_dot(a_bf, m) + _dot(a_bf, l))


def _rms(x, g):
    ms = jnp.mean(x * x, axis=-1, keepdims=True)
    return x * lax.rsqrt(ms + NORM_EPS) * g


def _silu(x):
    return x * jax.nn.sigmoid(x)


def _softplus(x):
    return jnp.maximum(x, 0.0) + jnp.log1p(jnp.exp(-jnp.abs(x)))


def _softmax2_rows(s):
    m = jnp.max(s, axis=-1, keepdims=True)
    e = jnp.exp2(s - m)
    return e * (1.0 / jnp.sum(e, axis=-1, keepdims=True))


def _const_spec(shape):
    nd = len(shape)
    return pl.BlockSpec(shape, lambda *_: (0,) * nd, pipeline_mode=pl.Buffered(1))


def _params(sem):
    return pltpu.CompilerParams(dimension_semantics=sem, vmem_limit_bytes=VMEM_LIMIT)


def _row_tile(n, want):
    t = min(n, want)
    assert n % t == 0
    return t


def _sub_tiles(tm):
    ts = tm // PROJ_SPLIT if tm % (SUB_TILE_MIN * PROJ_SPLIT) == 0 else tm
    return [slice(t * ts, (t + 1) * ts) for t in range(tm // ts)]


def _proj0_body(x_ref, g_ref, wu_ref, wab_ref, wza_ref, wq_ref, wk_ref, wv_ref, wzb_ref,
                alog_ref, dtb_ref, qg_ref, kg_ref, seg_ref,
                u_ref, gb_ref, za_ref, qlo_ref, qhi_ref, kn_ref, klast_ref, vlo_ref, vhi_ref, vlast_ref, zb_ref,
                *, kept_tiles):
    h = _rms(x_ref[...], g_ref[...]).astype(BF16)
    dm = BAND_HEADS * BAND_HD
    lo = (lax.broadcasted_iota(jnp.int32, (1, dm), 1) & (LANES - 1)) < BAND_HD
    seg = seg_ref[...]
    q = _dot(h, wq_ref[...])
    k = _dot(h, wk_ref[...])
    ab = _dot(h, wab_ref[...])
    za = _dot(h, wza_ref[...])

    def head_rsqrt(x):
        ss = _dot((x * x).astype(BF16), seg)
        return lax.rsqrt(ss * (1.0 / BAND_HD) + NORM_EPS)

    rq = head_rsqrt(q)
    rk = head_rsqrt(k)
    zb = _dot(h, wzb_ref[...])
    v = _dot(h, wv_ref[...])
    u_ref[...] = _dot(h, wu_ref[...])
    za_ref[...] = _silu(za)
    zb_ref[...] = _silu(zb)
    lo1 = lo[:, :LANES]

    def spread(r):
        return jnp.concatenate([jnp.where(lo1, r[:, 2 * s:2 * s + 1], r[:, 2 * s + 1:2 * s + 2])
                                for s in range(BAND_HEADS // 2)], axis=1)

    qn = q * spread(rq) * (qg_ref[...] * (BAND_HD ** -0.5 * LOG2_E))
    kn = k * spread(rk) * kg_ref[...]
    qlo_ref[...] = jnp.where(lo, qn, 0.0).astype(BF16)
    qhi_ref[...] = jnp.where(lo, 0.0, qn).astype(BF16)
    kn_ref[...] = kn.astype(BF16)
    vlo_ref[...] = jnp.where(lo, v, 0.0).astype(BF16)
    vhi_ref[...] = jnp.where(lo, 0.0, v).astype(BF16)

    def hand_on():
        klast_ref[...] = kn.reshape(kn.shape[0], BAND_HEADS, BAND_HD)
        vlast_ref[...] = v.reshape(v.shape[0], BAND_HEADS, BAND_HD)

    if kept_tiles is None:
        hand_on()
    else:
        tpb, kt = kept_tiles
        pl.when(pl.program_id(0) % tpb >= tpb - kt)(hand_on)
    lane = lax.broadcasted_iota(jnp.int32, ab.shape, 1)
    gval = -jnp.exp(alog_ref[...]) * _softplus(ab + dtb_ref[...])
    gb_ref[...] = jnp.where(lane < GDN_HEADS, gval, jax.nn.sigmoid(ab))


def _proj0(x2, g, w, qg, kg, seg, rows_per_batch):
    n, d = x2.shape
    tm = _row_tile(n, 256)
    dm = BAND_HEADS * BAND_HD
    widths = (3 * 1024, LANES, 1024, dm, dm, dm, dm)
    row = lambda wd: pl.BlockSpec((tm, wd), lambda i: (i, 0))
    keep = min(BAND_PAST, rows_per_batch)
    if rows_per_batch > keep:
        assert rows_per_batch % tm == 0 and keep % tm == 0
        tpb, kt = rows_per_batch // tm, keep // tm
        last = pl.BlockSpec((tm, BAND_HEADS, BAND_HD),
                            lambda i: ((i // tpb) * kt + jnp.maximum(i % tpb - (tpb - kt), 0), 0, 0))
        n_last = (n // rows_per_batch) * keep
        kept_tiles = (tpb, kt)
    else:
        last, n_last = pl.BlockSpec((tm, BAND_HEADS, BAND_HD), lambda i: (i, 0, 0)), n
        kept_tiles = None
    last_shape = jax.ShapeDtypeStruct((n_last, BAND_HEADS, BAND_HD), F32)
    f32 = lambda rows, wd: jax.ShapeDtypeStruct((rows, wd), F32)
    bf16 = lambda wd: jax.ShapeDtypeStruct((n, wd), BF16)
    return pl.pallas_call(
        functools.partial(_proj0_body, kept_tiles=kept_tiles),
        grid=(n // tm,),
        in_specs=[row(d), _const_spec((1, d))]
        + [_const_spec((d, wd)) for wd in widths]
        + [_const_spec((1, LANES)), _const_spec((1, LANES)), _const_spec((1, dm)), _const_spec((1, dm)),
           _const_spec((dm, LANES))],
        out_specs=[row(3 * 1024), row(LANES), row(1024), row(dm), row(dm), row(dm), last, row(dm), row(dm), last,
                   row(dm)],
        out_shape=[f32(n, 3 * 1024), f32(n, LANES), f32(n, 1024), bf16(dm), bf16(dm), bf16(dm), last_shape,
                   bf16(dm), bf16(dm), last_shape, f32(n, dm)],
        compiler_params=_params(("arbitrary",)),
        name="proj0",
    )(x2, g, w["wu"], w["wab"], w["wza"], w["wq"], w["wk"], w["wv"], w["wzb"], w["alog"], w["dtb"],
      qg, kg, seg)


def _proj1_body(x_ref, g_ref, wq_ref, wk_ref, wv_ref, wlr_ref, wz_ref, wg_ref, gbias_ref,
                q_ref, k_ref, v_ref, lg_ref, z_ref):
    rows = _sub_tiles(x_ref.shape[0])
    h = [_rms(x_ref[r, :], g_ref[...]).astype(BF16) for r in rows]
    lr = [_dot(h_, wlr_ref[...]) for h_ in h]
    for r, h_ in zip(rows, h):
        q_ref[r, :] = _dot(h_, wq_ref[...])
    pre = [_dot(x.astype(BF16), wg_ref[...]) + gbias_ref[...] for x in lr]
    zs = [_dot(h_, wz_ref[...]) for h_ in h]
    for r, p_ in zip(rows, pre):
        lg_ref[r, :] = -_softplus(-p_) * (1.0 / GLA_TAU)
    for r, h_ in zip(rows, h):
        k_ref[r, :] = _dot(h_, wk_ref[...])
    for r, z_ in zip(rows, zs):
        z_ref[r, :] = _silu(z_)
    for r, h_ in zip(rows, h):
        v_ref[r, :] = _dot(h_, wv_ref[...])


def _proj1(x2, g, w):
    n, d = x2.shape
    tm = _row_tile(n, 512)
    row = lambda wd: pl.BlockSpec((tm, wd), lambda i: (i, 0))
    outw = (1024, 1024, 2048, 1024, 2048)
    return pl.pallas_call(
        _proj1_body,
        grid=(n // tm,),
        in_specs=[row(d), _const_spec((1, d)), _const_spec((d, 1024)), _const_spec((d, 1024)),
                  _const_spec((d, 2048)), _const_spec((d, LANES)), _const_spec((d, 2048)),
                  _const_spec((LANES, 1024)), _const_spec((1, 1024))],
        out_specs=[row(wd) for wd in outw],
        out_shape=[jax.ShapeDtypeStruct((n, wd), F32) for wd in outw],
        compiler_params=_params(("arbitrary",)),
        name="proj1",
    )(x2, g, w["wq"], w["wk"], w["wv"], w["wlr"], w["wz"], w["wg"], w["gbias"])


def _memkv_body(m_ref, g_ref, w_ref, kg_ref, k4_ref, v4_ref, kb_ref, vb_ref):
    h = _rms(m_ref[...], g_ref[...]).astype(BF16)
    kv = _dot(h, w_ref[...])
    hw = MEM_HEADS * MEM_HD
    k = jnp.concatenate([_rms(kv[:, hh * MEM_HD:(hh + 1) * MEM_HD], kg_ref[...]) for hh in range(MEM_HEADS)],
                        axis=1)
    v = kv[:, hw:]
    k4_ref[...] = k.reshape(k.shape[0], MEM_HEADS, MEM_HD)
    v4_ref[...] = v.reshape(v.shape[0], MEM_HEADS, MEM_HD)
    kb_ref[...] = k.astype(BF16)
    vb_ref[...] = v.astype(BF16)


def _memkv(m2, g, w_bf, kg):
    n, d = m2.shape
    tm = _row_tile(n, 256)
    hw = MEM_HEADS * MEM_HD
    row = lambda wd: pl.BlockSpec((tm, wd), lambda i: (i, 0))
    row4 = pl.BlockSpec((tm, MEM_HEADS, MEM_HD), lambda i: (i, 0, 0))
    return pl.pallas_call(
        _memkv_body,
        grid=(n // tm,),
        in_specs=[row(d), _const_spec((1, d)), _const_spec((d, 2 * hw)), _const_spec((1, MEM_HD))],
        out_specs=[row4, row4, row(hw), row(hw)],
        out_shape=[jax.ShapeDtypeStruct((n, MEM_HEADS, MEM_HD), F32)] * 2
        + [jax.ShapeDtypeStruct((n, hw), BF16)] * 2,
        compiler_params=_params(("arbitrary",)),
        name="memkv",
    )(m2, g, w_bf, kg)


def _each(fn, *lists):
    return [fn(*xs) for xs in zip(*lists)]


def _unit_lower_inverse(a, eye, bd, mm):
    d = _each(lambda x: jnp.where(bd, x, 0.0), a)
    nl = _each(lambda x, y: x - y, a, d)
    d2 = _each(mm, d, d)
    d4 = _each(mm, d2, d2)
    td = _each(lambda x, y: mm(eye - x, eye + y), d, d2)
    d8 = _each(mm, d4, d4)
    td = _each(lambda x, y: mm(x, eye + y), td, d4)
    td = _each(lambda x, y: mm(x, eye + y), td, d8)
    m = _each(mm, td, nl)
    m2 = _each(mm, m, m)
    mt = _each(mm, m, td)
    return _each(lambda x, y, z: mm(eye + x, y - z), m2, td, mt)


def _gdn_body(*refs, L, NB, G, has_state):
    if has_state:
        u_ref, gb_ref, za_ref, cw_ref, og_ref, cinit_ref, sinit_ref, o_ref, s_ref, ubuf = refs
    else:
        u_ref, gb_ref, za_ref, cw_ref, og_ref, o_ref, s_ref, ubuf = refs
    c = pl.program_id(1)
    ns = 3 * GDN_HEADS
    hist = 8
    R = G * L

    @pl.when(c == 0)
    def _():
        if has_state:
            for b in range(NB):
                for j in range(ns):
                    ubuf[b * ns + j, 0:hist, :] = cinit_ref[b, :, j * LANES:(j + 1) * LANES]
            s_ref[...] = sinit_ref[...]
        else:
            ubuf[:, 0:hist, :] = jnp.zeros((NB * ns, hist, LANES), F32)
            s_ref[...] = jnp.zeros(s_ref.shape, F32)

    for b in range(NB):
        for j in range(ns):
            ubuf[b * ns + j, hist:hist + R, :] = u_ref[b, :, j * LANES:(j + 1) * LANES]
    base = hist - (CONV_TAPS - 1)
    ys = []
    for b in range(NB):
        bsl = slice(b * ns, (b + 1) * ns)
        yb = ubuf[bsl, base:base + R, :] * cw_ref[:, 0:1, :]
        for i in range(1, CONV_TAPS):
            yb = yb + ubuf[bsl, base + i:base + i + R, :] * cw_ref[:, i:i + 1, :]
        ys.append(_silu(yb))
    ubuf[:, base:hist, :] = ubuf[:, base + R:hist + R, :]

    row = lax.broadcasted_iota(jnp.int32, (L, 2 * L), 0)
    lane = lax.broadcasted_iota(jnp.int32, (L, 2 * L), 1)
    col = lane & (L - 1)
    left = lane < L
    incl = row >= col
    strict = row > col
    sub_shift = INV_SUB.bit_length() - 1
    bd = lax.shift_right_logical(row, sub_shift) == lax.shift_right_logical(col, sub_shift)
    eye = jnp.where(row == col, 1.0, 0.0).astype(F32)
    trow = lax.broadcasted_iota(jnp.int32, (L, L), 0)
    tcol = lax.broadcasted_iota(jnp.int32, (L, L), 1)
    tri = jnp.where(trow >= tcol, 1.0, 0.0).astype(BF16)
    og = og_ref[...]
    heads = [(b, g, h) for b in range(NB) for g in range(G) for h in range(GDN_HEADS)]
    pairs = range(len(heads) // 2)
    rows_of = lambda g: slice(g * L, (g + 1) * L)
    l2n = lambda x: x * lax.rsqrt(jnp.sum(x * x, axis=-1, keepdims=True) + NORM_EPS)
    q = [l2n(ys[b][h][rows_of(g)]) * (GDN_DK ** -0.5) for b, g, h in heads]
    k = [l2n(ys[b][GDN_HEADS + h][rows_of(g)]) for b, g, h in heads]
    v = [ys[b][2 * GDN_HEADS + h][rows_of(g)] for b, g, h in heads]
    gc, gr, bc = [], [], []
    for b in range(NB):
        for g in range(G):
            gbv = gb_ref[b, rows_of(g), :]
            gcum = _dot_exact_lhs(tri, gbv)
            gpad = jnp.concatenate([gcum, jnp.zeros((LANES - L, LANES), F32)], axis=0)
            gt = gpad.T
            for h in range(GDN_HEADS):
                gc.append(gcum[:, h:h + 1])
                gr.append(gt[h:h + 1, 0:L])
                bc.append(gbv[:, GDN_HEADS + h:GDN_HEADS + h + 1])
    side = lambda x0, x1: jnp.where(left, x0, x1)
    gcp = [side(gc[2 * p], gc[2 * p + 1]) for p in pairs]
    grp = [jnp.concatenate([gr[2 * p], gr[2 * p + 1]], axis=1) for p in pairs]
    bcp = [side(bc[2 * p], bc[2 * p + 1]) for p in pairs]
    dec = _each(lambda c_, r_: jnp.where(incl, jnp.exp(jnp.where(incl, c_ - r_, 0.0)), 0.0), gcp, grp)
    kb = _each(lambda x: x.astype(BF16), k)
    qb = _each(lambda x: x.astype(BF16), q)

    def blockdiag(y0, y1):
        z0 = jnp.zeros(y1.shape, y1.dtype)
        z1 = jnp.zeros(y0.shape, y0.dtype)
        return jnp.concatenate([jnp.concatenate([y0, z0], axis=1), jnp.concatenate([z1, y1], axis=1)], axis=0)

    def mm(x, y):
        yb = y.astype(BF16)
        zero = jnp.zeros_like(yb)
        return _dot(x.astype(BF16), jnp.concatenate([jnp.where(left, yb, zero), jnp.where(left, zero, yb)], axis=0))

    kq = [_dot_nt(jnp.concatenate([jnp.concatenate([kb[2 * p], qb[2 * p]], axis=0),
                                   jnp.concatenate([kb[2 * p + 1], qb[2 * p + 1]], axis=0)], axis=1),
                  blockdiag(kb[2 * p], kb[2 * p + 1])) for p in pairs]
    a = _each(lambda b_, x, d_: jnp.where(strict, b_ * x[:L] * d_, 0.0), bcp, kq, dec)
    t = _unit_lower_inverse(a, eye, bd, mm)
    eg = _each(jnp.exp, gc)
    rhs = _each(lambda b_, v_, e_, k_: jnp.concatenate([b_ * v_, (b_ * e_) * k_], axis=1).astype(BF16),
                bc, v, eg, k)
    sol = [_dot(t[p].astype(BF16), blockdiag(rhs[2 * p], rhs[2 * p + 1])) for p in pairs]
    solk = lambda i: sol[i // 2][:, 2 * (i % 2) * LANES + LANES:2 * (i % 2 + 1) * LANES]
    solv = lambda i: sol[i // 2][:, 2 * (i % 2) * LANES:2 * (i % 2) * LANES + LANES]
    qkd = [(kq[p][L:] * dec[p]).astype(BF16) for p in pairs]
    gl = [c_[L - 1:L, :] for c_ in gc]
    kd = _each(lambda k_, l_, c_: (k_ * jnp.exp(l_ - c_)).astype(BF16), k, gl, gc)

    item = lambda b, g, h: (b * G + g) * GDN_HEADS + h
    half = lambda x, i: x[:, (i % 2) * LANES:(i % 2 + 1) * LANES]
    s = {(b, h): s_ref[b, h] for b in range(NB) for h in range(GDN_HEADS)}
    for g in range(G):
        ids = [item(b, g, h) for b in range(NB) for h in range(GDN_HEADS)]
        sb = {i: s[(heads[i][0], heads[i][2])].astype(BF16) for i in ids}
        ksq = {i: _dot(jnp.concatenate([jnp.concatenate([solk(i).astype(BF16), qb[i]], axis=0),
                                        jnp.concatenate([solk(i + 1).astype(BF16), qb[i + 1]], axis=0)], axis=1),
                       blockdiag(sb[i], sb[i + 1])) for i in ids[::2]}
        ub = {i: (solv(i) - half(ksq[i - i % 2][:L], i)).astype(BF16) for i in ids}
        qku = {i: _dot(qkd[i // 2], blockdiag(ub[i], ub[i + 1])) for i in ids[::2]}
        ktu = {i: _dot_tn(kd[i], ub[i]) for i in ids}
        for i in ids:
            b, _, h = heads[i]
            s[(b, h)] = s[(b, h)] * jnp.exp(gl[i]) + ktu[i]
        for b in range(NB):
            za = za_ref[b, rows_of(g), :]
            outs = [(_rms(half(ksq[i - i % 2][L:], i) * eg[i] + half(qku[i - i % 2], i), og)
                     * za[:, heads[i][2] * LANES:(heads[i][2] + 1) * LANES]).astype(BF16)
                    for i in ids if heads[i][0] == b]
            o_ref[b, rows_of(g), :] = jnp.concatenate(outs, axis=1)
    for (b, h), val in s.items():
        s_ref[b, h] = val


def _gdn(u, gb, za, state, cw, og, L):
    b, s, _ = u.shape
    nc = s // L
    ns = 3 * GDN_HEADS
    nb = GDN_GROUP if b % GDN_GROUP == 0 else 1
    g = GDN_STEP_CHUNKS if nc % GDN_STEP_CHUNKS == 0 else 1
    blk = lambda wd: pl.BlockSpec((nb, g * L, wd), lambda i, j: (i, j, 0))
    has_state = state is not None
    state_specs = [pl.BlockSpec((nb, 8, ns * LANES), lambda i, j: (i, 0, 0)),
                   pl.BlockSpec((nb, GDN_HEADS, GDN_DK, LANES), lambda i, j: (i, 0, 0, 0))]
    return pl.pallas_call(
        functools.partial(_gdn_body, L=L, NB=nb, G=g, has_state=has_state),
        grid=(b // nb, nc // g),
        in_specs=[blk(ns * LANES), blk(LANES), blk(GDN_HEADS * LANES),
                  _const_spec((ns, 8, LANES)), _const_spec((1, LANES))] + (state_specs if has_state else []),
        out_specs=[blk(GDN_HEADS * LANES),
                   pl.BlockSpec((nb, GDN_HEADS, GDN_DK, LANES), lambda i, j: (i, 0, 0, 0))],
        out_shape=[jax.ShapeDtypeStruct((b, s, GDN_HEADS * LANES), BF16),
                   jax.ShapeDtypeStruct((b, GDN_HEADS, GDN_DK, LANES), F32)],
        scratch_shapes=[pltpu.VMEM((nb * ns, 8 + g * L, LANES), F32)],
        compiler_params=_params(("arbitrary", "arbitrary")),
        name="gdn",
    )(u, gb, za, cw, og, *(state if has_state else ()))


def _band_body(*refs, L, W, has_cache):
    if has_cache:
        (qlo_ref, qhi_ref, k_ref, vlo_ref, vhi_ref, zb_ref, bias_ref, ck_ref, cv_ref,
         o_ref, kscr, vlo, vhi, biasm) = refs
    else:
        (qlo_ref, qhi_ref, k_ref, vlo_ref, vhi_ref, zb_ref, bias_ref,
         o_ref, kscr, vlo, vhi, biasm) = refs
    c = pl.program_id(1)
    dm = BAND_HEADS * BAND_HD
    pad_chunks = BAND_PAST // L

    @pl.when(c == 0)
    def _():
        if has_cache:
            lo = (lax.broadcasted_iota(jnp.int32, (1, dm), 1) & (LANES - 1)) < BAND_HD
            kscr[0:BAND_PAST, :] = ck_ref[0].reshape(BAND_PAST, dm).astype(BF16)
            cv = cv_ref[0].reshape(BAND_PAST, dm)
            vlo[0:BAND_PAST, :] = jnp.where(lo, cv, 0.0).astype(BF16)
            vhi[0:BAND_PAST, :] = jnp.where(lo, 0.0, cv).astype(BF16)
        else:
            zero = jnp.zeros((BAND_PAST, dm), BF16)
            kscr[0:BAND_PAST, :] = zero
            vlo[0:BAND_PAST, :] = zero
            vhi[0:BAND_PAST, :] = zero

    new0 = pl.multiple_of(BAND_PAST + c * L, L)
    kscr[pl.ds(new0, L), :] = k_ref[0]
    vlo[pl.ds(new0, L), :] = vlo_ref[0]
    vhi[pl.ds(new0, L), :] = vhi_ref[0]

    if has_cache:
        @pl.when(c == 0)
        def _():
            biasm[...] = bias_ref[...] * LOG2_E
    else:
        @pl.when(c < pad_chunks)
        def _():
            wcol = lax.broadcasted_iota(jnp.int32, (1, W), 1)
            valid = (wcol + c * L) >= BAND_PAST
            for h in range(BAND_HEADS):
                biasm[h] = jnp.where(valid, bias_ref[h] * LOG2_E, -jnp.inf)

        @pl.when(c == pad_chunks)
        def _():
            biasm[...] = bias_ref[...] * LOG2_E
    bias_src = biasm

    w0 = pl.multiple_of(c * L, L)
    zb = zb_ref[0]
    slabs = [slice(s * LANES, (s + 1) * LANES) for s in range(BAND_HEADS // 2)]
    sc = []
    for sl in slabs:
        ks = kscr[pl.ds(w0, W), sl]
        sc.append(_dot_nt(qlo_ref[0, :, sl], ks))
        sc.append(_dot_nt(qhi_ref[0, :, sl], ks))
    ps, rs = [], []
    for h in range(BAND_HEADS):
        x = sc[h] + bias_src[h]
        e = jnp.exp2(x - jnp.max(x, axis=-1, keepdims=True))
        rs.append(1.0 / jnp.sum(e, axis=-1, keepdims=True))
        ps.append(e.astype(BF16))
    pv = []
    for i, sl in enumerate(slabs):
        pv.append(_dot(ps[2 * i], vlo[pl.ds(w0, W), sl]))
        pv.append(_dot(ps[2 * i + 1], vhi[pl.ds(w0, W), sl]))
    outs = [((pv[2 * i] * rs[2 * i] + pv[2 * i + 1] * rs[2 * i + 1]) * zb[:, sl]).astype(BF16)
            for i, sl in enumerate(slabs)]
    o_ref[0] = jnp.concatenate(outs, axis=1)


def _band(qlo, qhi, kn, vlo, vhi, zb, bias, caches, L):
    b, s, dm = qlo.shape
    nc = s // L
    W = BAND_PAST + L
    blk = pl.BlockSpec((1, L, dm), lambda i, j: (i, j, 0))
    cache = pl.BlockSpec((1, BAND_PAST, BAND_HEADS, BAND_HD), lambda i, j: (i, 0, 0, 0))
    has_cache = caches is not None
    scratch = [pltpu.VMEM((BAND_PAST + s, dm), BF16)] * 3 + [pltpu.VMEM((BAND_HEADS, L, W), F32)]
    return pl.pallas_call(
        functools.partial(_band_body, L=L, W=W, has_cache=has_cache),
        grid=(b, nc),
        in_specs=[blk] * 6 + [_const_spec((BAND_HEADS, L, W))] + ([cache, cache] if has_cache else []),
        out_specs=blk,
        out_shape=jax.ShapeDtypeStruct((b, s, dm), BF16),
        scratch_shapes=scratch,
        compiler_params=_params(("arbitrary", "arbitrary")),
        name="band",
    )(qlo, qhi, kn, vlo, vhi, zb, bias, *(caches if has_cache else ()))


def _gla_body(*refs, L, NB, has_state):
    if has_state:
        q_ref, k_ref, v_ref, lg_ref, z_ref, og_ref, sinit_ref, o_ref, st_ref, cbs = refs
    else:
        q_ref, k_ref, v_ref, lg_ref, z_ref, og_ref, o_ref, st_ref, cbs = refs
    c = pl.program_id(1)

    @pl.when(c == 0)
    def _():
        st_ref[...] = sinit_ref[...] if has_state else jnp.zeros(st_ref.shape, F32)

    levels = [L >> (t + 1) for t in range(L.bit_length() - 1)]
    small = [s for s in levels if 2 < 2 * s < 16]
    row = lax.broadcasted_iota(jnp.int32, (L, L), 0)
    col = lax.broadcasted_iota(jnp.int32, (L, L), 1)
    anchor = lambda s: lax.shift_left(lax.shift_right_logical(row, s.bit_length()), s.bit_length()) + (s - 1)
    between = lambda s: (col > jnp.minimum(row, anchor(s))) & (col <= jnp.maximum(row, anchor(s)))
    onehot = lambda m: jnp.where(m, 1.0, 0.0).astype(BF16)
    tri = onehot(row >= col)
    stack = jnp.concatenate([onehot(between(s)) for s in small], axis=0)
    ex = jnp.exp2
    og = og_ref[...]
    scale = GLA_DK ** -0.5
    lg2 = [lg_ref[b] * LOG2_E for b in range(NB)]
    pieces = [_split3(x) for x in lg2]
    cb_all = [_dot(tri, p[0]) + (_dot(tri, p[1]) + _dot(tri, p[2])) for p in pieces]
    y_small = [_dot(stack, p[0]) + _dot(stack, p[1]) for p in pieces]
    odd_row = (lax.broadcasted_iota(jnp.int32, (L, 1), 0) & 1) == 1
    ydec = [dict() for _ in range(NB)]
    for b in range(NB):
        cbs[b] = cb_all[b]
        ydec[b][1] = jnp.where(odd_row, lg2[b], 0.0)
        for t, s in enumerate(small):
            ydec[b][s] = y_small[b][t * L:(t + 1) * L]
        for s in levels:
            if s not in ydec[b]:
                anc = jnp.concatenate(
                    [jnp.broadcast_to(cbs[b, p * 2 * s + s - 1:p * 2 * s + s, :], (2 * s, cbs.shape[2]))
                     for p in range(L // (2 * s))], axis=0)
                ydec[b][s] = -jnp.abs(cb_all[b] - anc)

    heads = [(b, h) for b in range(NB) for h in range(GLA_HEADS)]
    pairs = range(len(heads) // 2)
    ksl = lambda h: slice(h * GLA_DK, (h + 1) * GLA_DK)
    vsl = lambda h: slice(h * GLA_DV, (h + 1) * GLA_DV)
    cb = [cb_all[b][:, ksl(h)] for b, h in heads]
    q = [q_ref[b, :, ksl(h)] * scale for b, h in heads]
    k = [k_ref[b, :, ksl(h)] for b, h in heads]
    vb = [v_ref[b, :, vsl(h)].astype(BF16) for b, h in heads]
    st = [st_ref[b, h] for b, h in heads]
    o_inter = _each(lambda q_, c_, s_: _dot((q_ * ex(c_)).astype(BF16), s_.astype(BF16)), q, cb, st)
    cl = [c_[L - 1:L] for c_ in cb]
    ke = _each(lambda k_, l_, c_: (k_ * ex(l_ - c_)).astype(BF16), k, cl, cb)
    ktv = _each(_dot_tn, ke, vb)
    dcol = [jnp.broadcast_to(ex(l_), (GLA_DK, GLA_DK)).T for l_ in cl]

    zk = jnp.zeros((L, GLA_DK), BF16)
    zv = jnp.zeros((L, GLA_DV), BF16)

    def side_by_side(qs, ks):
        lhs = jnp.concatenate(qs, axis=1).astype(BF16)
        rhs = jnp.concatenate([jnp.concatenate([ks[0].astype(BF16), zk], axis=1),
                               jnp.concatenate([zk, ks[1].astype(BF16)], axis=1)], axis=0)
        return _dot_nt(lhs, rhs)

    prow = lax.broadcasted_iota(jnp.int32, (L, 2 * L), 0)
    pcol = lax.broadcasted_iota(jnp.int32, (L, 2 * L), 1) & (L - 1)
    att = [jnp.where(prow == pcol, side_by_side((q[2 * p], q[2 * p + 1]), (k[2 * p], k[2 * p + 1])), 0.0)
           for p in pairs]
    for s in levels:
        sh = s.bit_length() - 1
        same_parent = lax.shift_right_logical(prow, sh + 1) == lax.shift_right_logical(pcol, sh + 1)
        take = same_parent & ((lax.shift_right_logical(prow, sh) & 1) == 1) & \
            ((lax.shift_right_logical(pcol, sh) & 1) == 0)
        f = [ex(ydec[b][s][:, ksl(h)]) for b, h in heads]
        prod = [side_by_side((q[2 * p] * f[2 * p], q[2 * p + 1] * f[2 * p + 1]),
                             (k[2 * p] * f[2 * p], k[2 * p + 1] * f[2 * p + 1])) for p in pairs]
        att = [jnp.where(take, prod[p], att[p]) for p in pairs]
    vpair = [jnp.concatenate([jnp.concatenate([vb[2 * p], zv], axis=1),
                              jnp.concatenate([zv, vb[2 * p + 1]], axis=1)], axis=0) for p in pairs]
    o_intra = [_dot(att[p].astype(BF16), vpair[p]) for p in pairs]
    for i, (b, h) in enumerate(heads):
        st_ref[b, h] = st[i] * jnp.concatenate([dcol[i]] * (GLA_DV // GLA_DK), axis=1) + ktv[i]
    for b in range(NB):
        z = z_ref[b]
        outs = [(_rms(o_inter[i] + o_intra[i // 2][:, (i % 2) * GLA_DV:(i % 2 + 1) * GLA_DV], og)
                 * z[:, vsl(h)]).astype(BF16) for i, (b_, h) in enumerate(heads) if b_ == b]
        o_ref[b] = jnp.concatenate(outs, axis=1)


def _gla(q, k, v, lg, z, sinit, og, L):
    b, s, _ = q.shape
    nc = s // L
    nb = GLA_GROUP if b % GLA_GROUP == 0 else 1
    qk = pl.BlockSpec((nb, L, GLA_HEADS * GLA_DK), lambda i, j: (i, j, 0))
    vv = pl.BlockSpec((nb, L, GLA_HEADS * GLA_DV), lambda i, j: (i, j, 0))
    st = pl.BlockSpec((nb, GLA_HEADS, GLA_DK, GLA_DV), lambda i, j: (i, 0, 0, 0))
    has_state = sinit is not None
    return pl.pallas_call(
        functools.partial(_gla_body, L=L, NB=nb, has_state=has_state),
        grid=(b // nb, nc),
        in_specs=[qk, qk, vv, qk, vv, _const_spec((1, GLA_DV))] + ([st] if has_state else []),
        out_specs=[vv, st],
        out_shape=[jax.ShapeDtypeStruct((b, s, GLA_HEADS * GLA_DV), BF16),
                   jax.ShapeDtypeStruct((b, GLA_HEADS, GLA_DK, GLA_DV), F32)],
        scratch_shapes=[pltpu.VMEM((nb, L, GLA_HEADS * GLA_DK), F32)],
        compiler_params=_params(("arbitrary", "arbitrary")),
        name="gla",
    )(q, k, v, lg, z, og, *((sinit,) if has_state else ()))


def _post_body(*refs, n_o):
    x_ref = refs[0]
    o_refs = refs[1:1 + n_o]
    wout_ref, mg_ref, wmq_ref, mqg_ref, mk_ref, mv_ref, wmo_ref, y_ref = refs[1 + n_o:]
    tm = x_ref.shape[1]
    ts = tm // POST_SPLIT if tm % (SUB_TILE_MIN * POST_SPLIT) == 0 else tm
    rows = [slice(t * ts, (t + 1) * ts) for t in range(tm // ts)]
    acc = [x_ref[0, r, :] for r in rows]
    off = 0
    for o_ref in o_refs:
        kd = o_ref.shape[-1]
        w = wout_ref[off:off + kd, :]
        acc = [a + _dot(o_ref[0, r, :], w) for a, r in zip(acc, rows)]
        off += kd
    hm = [_rms(a, mg_ref[...]).astype(BF16) for a in acc]
    qz = [_dot(h_, wmq_ref[...]) for h_ in hm]
    hw = MEM_HEADS * MEM_HD
    sls = [slice(h * MEM_HD, (h + 1) * MEM_HD) for h in range(MEM_HEADS)]
    mkb = [mk_ref[0, :, sl].astype(BF16) for sl in sls]
    mvb = [mv_ref[0, :, sl].astype(BF16) for sl in sls]
    qscale = MEM_HD ** -0.5 * LOG2_E
    qn = [[(_rms(z[:, sl], mqg_ref[...]) * qscale).astype(BF16) for sl in sls] for z in qz]
    sc = [[_dot_nt(qh, kh) for qh, kh in zip(qt, mkb)] for qt in qn]
    p = [[_softmax2_rows(x).astype(BF16) for x in st] for st in sc]
    oh = [[_dot(ph, vh) for ph, vh in zip(pt, mvb)] for pt in p]
    for t, r in enumerate(rows):
        outs = [oh[t][h] * _silu(qz[t][:, hw + h * MEM_HD:hw + (h + 1) * MEM_HD]) for h in range(MEM_HEADS)]
        oh[t] = jnp.concatenate(outs, axis=1).astype(BF16)
    ym = [_dot(om, wmo_ref[...]) for om in oh]
    for t, r in enumerate(rows):
        y_ref[0, r, :] = acc[t] + ym[t]


def _post(x, os_, wout, mg, wmq, mqg, mk, mv, wmo):
    b, s, d = x.shape
    tm = _row_tile(s, 256 * POST_SPLIT)
    hw = MEM_HEADS * MEM_HD
    nm = mk.shape[1]
    blk = lambda wd: pl.BlockSpec((1, tm, wd), lambda i, j: (i, j, 0))
    mem = pl.BlockSpec((1, nm, hw), lambda i, j: (i, 0, 0))
    kin = sum(o.shape[-1] for o in os_)
    return pl.pallas_call(
        functools.partial(_post_body, n_o=len(os_)),
        grid=(b, s // tm),
        in_specs=[blk(d)] + [blk(o.shape[-1]) for o in os_]
        + [_const_spec((kin, d)), _const_spec((1, d)), _const_spec((d, 2 * hw)), _const_spec((1, MEM_HD)),
           mem, mem, _const_spec((hw, d))],
        out_specs=blk(d),
        out_shape=jax.ShapeDtypeStruct((b, s, d), F32),
        compiler_params=_params(("arbitrary", "arbitrary")),
        name="post",
    )(x, *os_, wout, mg, wmq, mqg, mk, mv, wmo)


def _pad_cols(w, n):
    return jnp.pad(w, ((0, 0), (0, n - w.shape[1])))


def _prep_l0(w_in, conv_w, a_log, dt_bias, q_g, k_g):
    c0 = 3 * 1024
    ab = w_in[:, c0:c0 + 2 * GDN_HEADS]
    c1 = c0 + 2 * GDN_HEADS
    cols = lambda i: w_in[:, c1 + i * 1024:c1 + (i + 1) * 1024].astype(BF16)
    w = dict(
        wu=w_in[:, :c0].astype(BF16), wab=_pad_cols(ab, LANES).astype(BF16),
        wza=cols(0), wq=cols(1), wk=cols(2), wv=cols(3), wzb=cols(4),
        alog=jnp.pad(a_log, (0, LANES - GDN_HEADS)).reshape(1, LANES).astype(F32),
        dtb=jnp.pad(dt_bias, (0, LANES - GDN_HEADS)).reshape(1, LANES).astype(F32),
    )
    ns = 3 * GDN_HEADS
    cw = jnp.pad(conv_w.astype(F32), ((0, 8 - CONV_TAPS), (0, 0))).reshape(8, ns, LANES).transpose(1, 0, 2)
    dm = BAND_HEADS * BAND_HD
    head_of = np.arange(dm) // BAND_HD
    seg = jnp.asarray(head_of[:, None] == np.arange(LANES)[None, :], BF16)
    qg = jnp.tile(q_g.astype(F32), BAND_HEADS).reshape(1, dm)
    kg = jnp.tile(k_g.astype(F32), BAND_HEADS).reshape(1, dm)
    return w, cw, seg, qg, kg


def _band_bias_table(rel_bias, L, chunk):
    W = BAND_PAST + L
    n = np.arange(W + L - 1)
    idx = np.clip(BAND_PAST + (L - 1) - n, -BAND_MAX_REL, BAND_MAX_REL) + BAND_MAX_REL
    strip = rel_bias.astype(F32)[:, idx]
    bias = jnp.stack([strip[:, L - 1 - a:L - 1 - a + W] for a in range(L)], axis=1)
    back = np.arange(L)[:, None] // chunk - (np.arange(W)[None, :] - BAND_PAST) // chunk
    readable = (back >= 0) & (back <= BAND_PAST // chunk)
    return bias if readable.all() else jnp.where(jnp.asarray(readable)[None], bias, -jnp.inf)


def _prep_l1(w_in, w_gate_up, gate_bias):
    qk = GLA_HEADS * GLA_DK
    vw = GLA_HEADS * GLA_DV
    o = np.cumsum([0, qk, qk, vw, GLA_RANK, vw])
    return dict(
        wq=w_in[:, o[0]:o[1]].astype(BF16), wk=w_in[:, o[1]:o[2]].astype(BF16),
        wv=w_in[:, o[2]:o[3]].astype(BF16), wlr=_pad_cols(w_in[:, o[3]:o[4]], LANES).astype(BF16),
        wz=w_in[:, o[4]:o[5]].astype(BF16),
        wg=jnp.pad(w_gate_up, ((0, LANES - GLA_RANK), (0, 0))).astype(BF16),
        gbias=gate_bias.reshape(1, qk).astype(F32),
    )


def _layer0(x, gdn_states, band_caches, mk, mv, norm_g, pw, cw, seg, qg, kg, bias,
            a_onorm_g, w_out, mnorm_g, w_mq, mq_g, w_mo, L):
    b, s, d = x.shape
    outs = _proj0(x.reshape(b * s, d), norm_g.reshape(1, d), pw, qg, kg, seg, s)
    u, gb, za, qlo, qhi, kn, klast, vlo, vhi, vlast, zb = (t.reshape((b, -1) + t.shape[1:]) for t in outs)
    if gdn_states is not None:
        conv_state, gdn_state = gdn_states
        gdn_states = (jnp.pad(conv_state.astype(F32), ((0, 0), (8 - (CONV_TAPS - 1), 0), (0, 0))),
                      gdn_state.astype(F32))
    o_a, s_new = _gdn(u, gb, za, gdn_states, cw, a_onorm_g.reshape(1, LANES), L)
    o_b = _band(qlo, qhi, kn, vlo, vhi, zb, bias, band_caches, bias.shape[1])
    y = _post(x, (o_a, o_b), w_out, mnorm_g.reshape(1, d), w_mq, mq_g.reshape(1, MEM_HD), mk, mv, w_mo)
    return y, u[:, s - (CONV_TAPS - 1):, :], s_new, klast, vlast


def _layer1(x, gla_state, mk, mv, norm_g, pw, c_onorm_g, w_out, mnorm_g, w_mq, mq_g, w_mo, L):
    b, s, d = x.shape
    q, k, v, lg, z = _proj1(x.reshape(b * s, d), norm_g.reshape(1, d), pw)
    r3 = lambda t: t.reshape(b, s, t.shape[-1])
    q, k, v, lg, z = map(r3, (q, k, v, lg, z))
    o, st = _gla(q, k, v, lg, z, gla_state, c_onorm_g.reshape(1, GLA_DV), L)
    y = _post(x, (o,), w_out, mnorm_g.reshape(1, d), w_mq, mq_g.reshape(1, MEM_HD), mk, mv, w_mo)
    return y, st


def kernel(x_prompt, x_sample, mem_prompt, state_l0_gdn_conv, state_l0_gdn, cache_l0_band_k, cache_l0_band_v, cache_l0_mem_k, cache_l0_mem_v, state_l1_gla, cache_l1_mem_k, cache_l1_mem_v, l0_norm_g, l0_w_in, l0_conv_w, l0_a_log, l0_dt_bias, l0_a_onorm_g, l0_b_q_g, l0_b_k_g, l0_b_rel_bias, l0_w_out, l0_mnorm_g, l0_mem_norm_g, l0_w_mkv, l0_mk_g, l0_w_mq, l0_mq_g, l0_w_mo, l1_norm_g, l1_w_in, l1_w_gate_up, l1_gate_bias, l1_c_onorm_g, l1_w_out, l1_mnorm_g, l1_mem_norm_g, l1_w_mkv, l1_mk_g, l1_w_mq, l1_mq_g, l1_w_mo):
    bp, sp, d = x_prompt.shape
    bs, ss, _ = x_sample.shape
    nm = mem_prompt.shape[1]
    hw = MEM_HEADS * MEM_HD
    dm = BAND_HEADS * BAND_HD
    assert sp % CHUNK_ == 0 and ss % INV_SUB == 0 and ss <= CHUNK_
    assert cache_l0_band_k.shape[1] == BAND_PAST

    pw0, cw, seg, qg, kg = _prep_l0(l0_w_in, l0_conv_w, l0_a_log, l0_dt_bias, l0_b_q_g, l0_b_k_g)
    pw1 = _prep_l1(l1_w_in, l1_w_gate_up, l1_gate_bias)
    band_rows = BAND_STEP_CHUNKS * CHUNK_ if sp % (BAND_STEP_CHUNKS * CHUNK_) == 0 else CHUNK_
    bias_p = _band_bias_table(l0_b_rel_bias, band_rows, CHUNK_)
    bias_s = _band_bias_table(l0_b_rel_bias, ss, ss)
    bf = lambda w: w.astype(BF16)
    mem2 = mem_prompt.reshape(bp * nm, d)

    p_mk0, p_mv0, mk0, mv0 = _memkv(mem2, l0_mem_norm_g.reshape(1, d), bf(l0_w_mkv), l0_mk_g.reshape(1, MEM_HD))
    mk0 = mk0.reshape(bp, nm, hw)
    mv0 = mv0.reshape(bp, nm, hw)
    l0_shared = (l0_norm_g, pw0, cw, seg, qg, kg)
    l0_tail = (l0_a_onorm_g, bf(l0_w_out), l0_mnorm_g, bf(l0_w_mq), l0_mq_g, bf(l0_w_mo))
    yp, p_conv, p_gdn, p_kn, p_v = _layer0(
        x_prompt, None, None, mk0, mv0,
        *l0_shared, bias_p, *l0_tail, CHUNK_)
    ys, s_conv, s_gdn, s_kn, s_v = _layer0(
        x_sample, (state_l0_gdn_conv, state_l0_gdn),
        (cache_l0_band_k, cache_l0_band_v),
        cache_l0_mem_k.reshape(bs, nm, hw), cache_l0_mem_v.reshape(bs, nm, hw),
        *l0_shared, bias_s, *l0_tail, ss)

    p_mk1, p_mv1, mk1, mv1 = _memkv(mem2, l1_mem_norm_g.reshape(1, d), bf(l1_w_mkv), l1_mk_g.reshape(1, MEM_HD))
    mk1 = mk1.reshape(bp, nm, hw)
    mv1 = mv1.reshape(bp, nm, hw)
    l1_tail = (l1_c_onorm_g, bf(l1_w_out), l1_mnorm_g, bf(l1_w_mq), l1_mq_g, bf(l1_w_mo))
    yp, p_gla = _layer1(yp, None, mk1, mv1, l1_norm_g, pw1, *l1_tail, CHUNK_)
    ys, s_gla = _layer1(ys, state_l1_gla.astype(F32), cache_l1_mem_k.reshape(bs, nm, hw),
                        cache_l1_mem_v.reshape(bs, nm, hw), l1_norm_g, pw1, *l1_tail, ss)

    m4 = lambda t: t.reshape(bp, nm, MEM_HEADS, MEM_HD)
    return (yp, ys, p_conv, p_gdn, p_kn, p_v,
            m4(p_mk0), m4(p_mv0), p_gla, m4(p_mk1), m4(p_mv1),
            s_conv, s_gdn, s_kn, s_v, s_gla)
```

```python
import functools

import jax
import jax.numpy as jnp
import numpy as np
from jax import lax
from jax.experimental import pallas as pl
from jax.experimental.pallas import tpu as pltpu

F32 = jnp.float32
BF16 = jnp.bfloat16
NORM_EPS = 1e-6
LOG2_E = 1.4426950408889634

CHUNK_ = 64
CONV_TAPS = 4
GDN_HEADS = 8
GDN_DK = 128
BAND_HEADS = 16
BAND_HD = 64
BAND_PAST = 512
BAND_MAX_REL = 128
GLA_HEADS = 8
GLA_DK = 128
GLA_DV = 256
GLA_RANK = 16
GLA_TAU = 16.0
MEM_HEADS = 4
MEM_HD = 128
INV_SUB = 16
GDN_GROUP = 2
GDN_STEP_CHUNKS = 1
GLA_GROUP = 4
POST_SPLIT = 4
PROJ_SPLIT = 2
SUB_TILE_MIN = 128
BAND_STEP_CHUNKS = 2
LANES = 128
VMEM_LIMIT = 56 * 1024 * 1024


def _dot(a, b):
    return jnp.dot(a, b, preferred_element_type=F32)


def _dot_nt(a, b):
    return lax.dot_general(a, b, (((1,), (1,)), ((), ())), preferred_element_type=F32)


def _dot_tn(a, b):
    return lax.dot_general(a, b, (((0,), (0,)), ((), ())), preferred_element_type=F32)


def _split3(x):
    hi = x.astype(BF16)
    r = x - hi.astype(F32)
    mid = r.astype(BF16)
    lo = (r - mid.astype(F32)).astype(BF16)
    return hi, mid, lo


def _dot_exact_lhs(a_bf, b):
    h, m, l = _split3(b)
    return _dot(a_bf, h) + (_dot(a_bf, m) + _dot(a_bf, l))


def _rms(x, g):
    ms = jnp.mean(x * x, axis=-1, keepdims=True)
    return x * lax.rsqrt(ms + NORM_EPS) * g


def _silu(x):
    return x * jax.nn.sigmoid(x)


def _softplus(x):
    return jnp.maximum(x, 0.0) + jnp.log1p(jnp.exp(-jnp.abs(x)))


def _softmax2_rows(s):
    m = jnp.max(s, axis=-1, keepdims=True)
    e = jnp.exp2(s - m)
    return e * (1.0 / jnp.sum(e, axis=-1, keepdims=True))


def _const_spec(shape):
    nd = len(shape)
    return pl.BlockSpec(shape, lambda *_: (0,) * nd, pipeline_mode=pl.Buffered(1))


def _params(sem):
    return pltpu.CompilerParams(dimension_semantics=sem, vmem_limit_bytes=VMEM_LIMIT)


def _row_tile(n, want):
    t = min(n, want)
    assert n % t == 0
    return t


def _sub_tiles(tm):
    ts = tm // PROJ_SPLIT if tm % (SUB_TILE_MIN * PROJ_SPLIT) == 0 else tm
    return [slice(t * ts, (t + 1) * ts) for t in range(tm // ts)]


def _proj0_body(x_ref, g_ref, wu_ref, wab_ref, wza_ref, wq_ref, wk_ref, wv_ref, wzb_ref,
                alog_ref, dtb_ref, qg_ref, kg_ref,
                u_ref, gb_ref, za_ref, qlo_ref, qhi_ref, kn_ref, klast_ref, vlo_ref, vhi_ref, vlast_ref, zb_ref,
                *, kept_tiles):
    h = _rms(x_ref[...], g_ref[...]).astype(BF16)
    dm = BAND_HEADS * BAND_HD
    lo = (lax.broadcasted_iota(jnp.int32, (1, dm), 1) & (LANES - 1)) < BAND_HD
    q = _dot(h, wq_ref[...])
    k = _dot(h, wk_ref[...])
    ab = _dot(h, wab_ref[...])
    za = _dot(h, wza_ref[...])

    zb = _dot(h, wzb_ref[...])
    v = _dot(h, wv_ref[...])
    u_ref[...] = _dot(h, wu_ref[...])
    za_ref[...] = _silu(za)
    zb_ref[...] = _silu(zb)
    lo1 = lo[:, :LANES]

    def head_rsqrt(x):
        out = []
        for s in range(BAND_HEADS // 2):
            x2 = x[:, s * LANES:(s + 1) * LANES]
            x2 = x2 * x2
            both = jnp.sum(x2, axis=-1, keepdims=True)
            first = jnp.sum(jnp.where(lo1, x2, 0.0), axis=-1, keepdims=True)
            r = lambda t: lax.rsqrt(t * (1.0 / BAND_HD) + NORM_EPS)
            out.append(jnp.where(lo1, r(first), r(both - first)))
        return jnp.concatenate(out, axis=1)

    qn = q * head_rsqrt(q) * (qg_ref[...] * (BAND_HD ** -0.5 * LOG2_E))
    kn = k * head_rsqrt(k) * kg_ref[...]
    qlo_ref[...] = jnp.where(lo, qn, 0.0).astype(BF16)
    qhi_ref[...] = jnp.where(lo, 0.0, qn).astype(BF16)
    kn_ref[...] = kn.astype(BF16)
    vlo_ref[...] = jnp.where(lo, v, 0.0).astype(BF16)
    vhi_ref[...] = jnp.where(lo, 0.0, v).astype(BF16)

    def hand_on():
        klast_ref[...] = kn.reshape(kn.shape[0], BAND_HEADS, BAND_HD)
        vlast_ref[...] = v.reshape(v.shape[0], BAND_HEADS, BAND_HD)

    if kept_tiles is None:
        hand_on()
    else:
        tpb, kt = kept_tiles
        pl.when(pl.program_id(0) % tpb >= tpb - kt)(hand_on)
    lane = lax.broadcasted_iota(jnp.int32, ab.shape, 1)
    gval = -jnp.exp(alog_ref[...]) * _softplus(ab + dtb_ref[...])
    gb_ref[...] = jnp.where(lane < GDN_HEADS, gval, jax.nn.sigmoid(ab))


def _proj0(x2, g, w, qg, kg, rows_per_batch):
    n, d = x2.shape
    tm = _row_tile(n, 256)
    dm = BAND_HEADS * BAND_HD
    widths = (3 * 1024, LANES, 1024, dm, dm, dm, dm)
    row = lambda wd: pl.BlockSpec((tm, wd), lambda i: (i, 0))
    keep = min(BAND_PAST, rows_per_batch)
    if rows_per_batch > keep:
        assert rows_per_batch % tm == 0 and keep % tm == 0
        tpb, kt = rows_per_batch // tm, keep // tm
        last = pl.BlockSpec((tm, BAND_HEADS, BAND_HD),
                            lambda i: ((i // tpb) * kt + jnp.maximum(i % tpb - (tpb - kt), 0), 0, 0))
        n_last = (n // rows_per_batch) * keep
        kept_tiles = (tpb, kt)
    else:
        last, n_last = pl.BlockSpec((tm, BAND_HEADS, BAND_HD), lambda i: (i, 0, 0)), n
        kept_tiles = None
    last_shape = jax.ShapeDtypeStruct((n_last, BAND_HEADS, BAND_HD), F32)
    f32 = lambda rows, wd: jax.ShapeDtypeStruct((rows, wd), F32)
    bf16 = lambda wd: jax.ShapeDtypeStruct((n, wd), BF16)
    return pl.pallas_call(
        functools.partial(_proj0_body, kept_tiles=kept_tiles),
        grid=(n // tm,),
        in_specs=[row(d), _const_spec((1, d))]
        + [_const_spec((d, wd)) for wd in widths]
        + [_const_spec((1, LANES)), _const_spec((1, LANES)), _const_spec((1, dm)), _const_spec((1, dm))],
        out_specs=[row(3 * 1024), row(LANES), row(1024), row(dm), row(dm), row(dm), last, row(dm), row(dm), last,
                   row(dm)],
        out_shape=[f32(n, 3 * 1024), f32(n, LANES), f32(n, 1024), bf16(dm), bf16(dm), bf16(dm), last_shape,
                   bf16(dm), bf16(dm), last_shape, f32(n, dm)],
        compiler_params=_params(("arbitrary",)),
        name="proj0",
    )(x2, g, w["wu"], w["wab"], w["wza"], w["wq"], w["wk"], w["wv"], w["wzb"], w["alog"], w["dtb"],
      qg, kg)


def _proj1_body(x_ref, g_ref, wq_ref, wk_ref, wv_ref, wlr_ref, wz_ref, wg_ref, gbias_ref,
                q_ref, k_ref, v_ref, lg_ref, z_ref):
    rows = _sub_tiles(x_ref.shape[0])
    h = [_rms(x_ref[r, :], g_ref[...]).astype(BF16) for r in rows]
    lr = [_dot(h_, wlr_ref[...]) for h_ in h]
    for r, h_ in zip(rows, h):
        q_ref[r, :] = _dot(h_, wq_ref[...])
    pre = [_dot(x.astype(BF16), wg_ref[...]) + gbias_ref[...] for x in lr]
    zs = [_dot(h_, wz_ref[...]) for h_ in h]
    for r, p_ in zip(rows, pre):
        lg_ref[r, :] = -_softplus(-p_) * (1.0 / GLA_TAU)
    for r, h_ in zip(rows, h):
        k_ref[r, :] = _dot(h_, wk_ref[...])
    for r, z_ in zip(rows, zs):
        z_ref[r, :] = _silu(z_)
    for r, h_ in zip(rows, h):
        v_ref[r, :] = _dot(h_, wv_ref[...])


def _proj1(x2, g, w):
    n, d = x2.shape
    tm = _row_tile(n, 512)
    row = lambda wd: pl.BlockSpec((tm, wd), lambda i: (i, 0))
    outw = (1024, 1024, 2048, 1024, 2048)
    return pl.pallas_call(
        _proj1_body,
        grid=(n // tm,),
        in_specs=[row(d), _const_spec((1, d)), _const_spec((d, 1024)), _const_spec((d, 1024)),
                  _const_spec((d, 2048)), _const_spec((d, LANES)), _const_spec((d, 2048)),
                  _const_spec((LANES, 1024)), _const_spec((1, 1024))],
        out_specs=[row(wd) for wd in outw],
        out_shape=[jax.ShapeDtypeStruct((n, wd), F32) for wd in outw],
        compiler_params=_params(("arbitrary",)),
        name="proj1",
    )(x2, g, w["wq"], w["wk"], w["wv"], w["wlr"], w["wz"], w["wg"], w["gbias"])


def _memkv_body(m_ref, g_ref, w_ref, kg_ref, k4_ref, v4_ref, kb_ref, vb_ref):
    h = _rms(m_ref[...], g_ref[...]).astype(BF16)
    kv = _dot(h, w_ref[...])
    hw = MEM_HEADS * MEM_HD
    k = jnp.concatenate([_rms(kv[:, hh * MEM_HD:(hh + 1) * MEM_HD], kg_ref[...]) for hh in range(MEM_HEADS)],
                        axis=1)
    v = kv[:, hw:]
    k4_ref[...] = k.reshape(k.shape[0], MEM_HEADS, MEM_HD)
    v4_ref[...] = v.reshape(v.shape[0], MEM_HEADS, MEM_HD)
    kb_ref[...] = k.astype(BF16)
    vb_ref[...] = v.astype(BF16)


def _memkv(m2, g, w_bf, kg):
    n, d = m2.shape
    tm = _row_tile(n, 256)
    hw = MEM_HEADS * MEM_HD
    row = lambda wd: pl.BlockSpec((tm, wd), lambda i: (i, 0))
    row4 = pl.BlockSpec((tm, MEM_HEADS, MEM_HD), lambda i: (i, 0, 0))
    return pl.pallas_call(
        _memkv_body,
        grid=(n // tm,),
        in_specs=[row(d), _const_spec((1, d)), _const_spec((d, 2 * hw)), _const_spec((1, MEM_HD))],
        out_specs=[row4, row4, row(hw), row(hw)],
        out_shape=[jax.ShapeDtypeStruct((n, MEM_HEADS, MEM_HD), F32)] * 2
        + [jax.ShapeDtypeStruct((n, hw), BF16)] * 2,
        compiler_params=_params(("arbitrary",)),
        name="memkv",
    )(m2, g, w_bf, kg)


def _each(fn, *lists):
    return [fn(*xs) for xs in zip(*lists)]


def _unit_lower_inverse(a, eye, bd, mm):
    d = _each(lambda x: jnp.where(bd, x, 0.0), a)
    nl = _each(lambda x, y: x - y, a, d)
    d2 = _each(mm, d, d)
    d4 = _each(mm, d2, d2)
    td = _each(lambda x, y: mm(eye - x, eye + y), d, d2)
    d8 = _each(mm, d4, d4)
    td = _each(lambda x, y: mm(x, eye + y), td, d4)
    td = _each(lambda x, y: mm(x, eye + y), td, d8)
    m = _each(mm, td, nl)
    m2 = _each(mm, m, m)
    mt = _each(mm, m, td)
    return _each(lambda x, y, z: mm(eye + x, y - z), m2, td, mt)


def _gdn_body(*refs, L, NB, G, has_state):
    if has_state:
        u_ref, gb_ref, za_ref, cw_ref, og_ref, cinit_ref, sinit_ref, o_ref, s_ref, ubuf = refs
    else:
        u_ref, gb_ref, za_ref, cw_ref, og_ref, o_ref, s_ref, ubuf = refs
    c = pl.program_id(1)
    ns = 3 * GDN_HEADS
    hist = 8
    R = G * L

    @pl.when(c == 0)
    def _():
        if has_state:
            for b in range(NB):
                for j in range(ns):
                    ubuf[b * ns + j, 0:hist, :] = cinit_ref[b, :, j * LANES:(j + 1) * LANES]
            s_ref[...] = sinit_ref[...]
        else:
            ubuf[:, 0:hist, :] = jnp.zeros((NB * ns, hist, LANES), F32)
            s_ref[...] = jnp.zeros(s_ref.shape, F32)

    for b in range(NB):
        for j in range(ns):
            ubuf[b * ns + j, hist:hist + R, :] = u_ref[b, :, j * LANES:(j + 1) * LANES]
    base = hist - (CONV_TAPS - 1)
    ys = []
    for b in range(NB):
        bsl = slice(b * ns, (b + 1) * ns)
        yb = ubuf[bsl, base:base + R, :] * cw_ref[:, 0:1, :]
        for i in range(1, CONV_TAPS):
            yb = yb + ubuf[bsl, base + i:base + i + R, :] * cw_ref[:, i:i + 1, :]
        ys.append(_silu(yb))
    ubuf[:, base:hist, :] = ubuf[:, base + R:hist + R, :]

    row = lax.broadcasted_iota(jnp.int32, (L, 2 * L), 0)
    lane = lax.broadcasted_iota(jnp.int32, (L, 2 * L), 1)
    col = lane & (L - 1)
    left = lane < L
    incl = row >= col
    strict = row > col
    sub_shift = INV_SUB.bit_length() - 1
    bd = lax.shift_right_logical(row, sub_shift) == lax.shift_right_logical(col, sub_shift)
    eye = jnp.where(row == col, 1.0, 0.0).astype(F32)
    trow = lax.broadcasted_iota(jnp.int32, (L, L), 0)
    tcol = lax.broadcasted_iota(jnp.int32, (L, L), 1)
    tri = jnp.where(trow >= tcol, 1.0, 0.0).astype(BF16)
    og = og_ref[...]
    heads = [(b, g, h) for b in range(NB) for g in range(G) for h in range(GDN_HEADS)]
    pairs = range(len(heads) // 2)
    rows_of = lambda g: slice(g * L, (g + 1) * L)
    l2n = lambda x: x * lax.rsqrt(jnp.sum(x * x, axis=-1, keepdims=True) + NORM_EPS)
    q = [l2n(ys[b][h][rows_of(g)]) * (GDN_DK ** -0.5) for b, g, h in heads]
    k = [l2n(ys[b][GDN_HEADS + h][rows_of(g)]) for b, g, h in heads]
    v = [ys[b][2 * GDN_HEADS + h][rows_of(g)] for b, g, h in heads]
    gc, gr, bc = [], [], []
    for b in range(NB):
        for g in range(G):
            gbv = gb_ref[b, rows_of(g), :]
            gcum = _dot_exact_lhs(tri, gbv)
            gpad = jnp.concatenate([gcum, jnp.zeros((LANES - L, LANES), F32)], axis=0)
            gt = gpad.T
            for h in range(GDN_HEADS):
                gc.append(gcum[:, h:h + 1])
                gr.append(gt[h:h + 1, 0:L])
                bc.append(gbv[:, GDN_HEADS + h:GDN_HEADS + h + 1])
    side = lambda x0, x1: jnp.where(left, x0, x1)
    gcp = [side(gc[2 * p], gc[2 * p + 1]) for p in pairs]
    grp = [jnp.concatenate([gr[2 * p], gr[2 * p + 1]], axis=1) for p in pairs]
    bcp = [side(bc[2 * p], bc[2 * p + 1]) for p in pairs]
    dec = _each(lambda c_, r_: jnp.where(incl, jnp.exp(jnp.where(incl, c_ - r_, 0.0)), 0.0), gcp, grp)
    kb = _each(lambda x: x.astype(BF16), k)
    qb = _each(lambda x: x.astype(BF16), q)

    def blockdiag(y0, y1):
        z0 = jnp.zeros(y1.shape, y1.dtype)
        z1 = jnp.zeros(y0.shape, y0.dtype)
        return jnp.concatenate([jnp.concatenate([y0, z0], axis=1), jnp.concatenate([z1, y1], axis=1)], axis=0)

    def mm(x, y):
        yb = y.astype(BF16)
        zero = jnp.zeros_like(yb)
        return _dot(x.astype(BF16), jnp.concatenate([jnp.where(left, yb, zero), jnp.where(left, zero, yb)], axis=0))

    kq = [_dot_nt(jnp.concatenate([jnp.concatenate([kb[2 * p], qb[2 * p]], axis=0),
                                   jnp.concatenate([kb[2 * p + 1], qb[2 * p + 1]], axis=0)], axis=1),
                  blockdiag(kb[2 * p], kb[2 * p + 1])) for p in pairs]
    a = _each(lambda b_, x, d_: jnp.where(strict, b_ * x[:L] * d_, 0.0), bcp, kq, dec)
    t = _unit_lower_inverse(a, eye, bd, mm)
    eg = _each(jnp.exp, gc)
    rhs = _each(lambda b_, v_, e_, k_: jnp.concatenate([b_ * v_, (b_ * e_) * k_], axis=1).astype(BF16),
                bc, v, eg, k)
    sol = [_dot(t[p].astype(BF16), blockdiag(rhs[2 * p], rhs[2 * p + 1])) for p in pairs]
    solk = lambda i: sol[i // 2][:, 2 * (i % 2) * LANES + LANES:2 * (i % 2 + 1) * LANES]
    solv = lambda i: sol[i // 2][:, 2 * (i % 2) * LANES:2 * (i % 2) * LANES + LANES]
    qkd = [(kq[p][L:] * dec[p]).astype(BF16) for p in pairs]
    gl = [c_[L - 1:L, :] for c_ in gc]
    kd = _each(lambda k_, l_, c_: (k_ * jnp.exp(l_ - c_)).astype(BF16), k, gl, gc)

    item = lambda b, g, h: (b * G + g) * GDN_HEADS + h
    half = lambda x, i: x[:, (i % 2) * LANES:(i % 2 + 1) * LANES]
    s = {(b, h): s_ref[b, h] for b in range(NB) for h in range(GDN_HEADS)}
    for g in range(G):
        ids = [item(b, g, h) for b in range(NB) for h in range(GDN_HEADS)]
        sb = {i: s[(heads[i][0], heads[i][2])].astype(BF16) for i in ids}
        ksq = {i: _dot(jnp.concatenate([jnp.concatenate([solk(i).astype(BF16), qb[i]], axis=0),
                                        jnp.concatenate([solk(i + 1).astype(BF16), qb[i + 1]], axis=0)], axis=1),
                       blockdiag(sb[i], sb[i + 1])) for i in ids[::2]}
        ub = {i: (solv(i) - half(ksq[i - i % 2][:L], i)).astype(BF16) for i in ids}
        qku = {i: _dot(qkd[i // 2], blockdiag(ub[i], ub[i + 1])) for i in ids[::2]}
        ktu = {i: _dot_tn(kd[i], ub[i]) for i in ids}
        for i in ids:
            b, _, h = heads[i]
            s[(b, h)] = s[(b, h)] * jnp.exp(gl[i]) + ktu[i]
        for b in range(NB):
            za = za_ref[b, rows_of(g), :]
            outs = [(_rms(half(ksq[i - i % 2][L:], i) * eg[i] + half(qku[i - i % 2], i), og)
                     * za[:, heads[i][2] * LANES:(heads[i][2] + 1) * LANES]).astype(BF16)
                    for i in ids if heads[i][0] == b]
            o_ref[b, rows_of(g), :] = jnp.concatenate(outs, axis=1)
    for (b, h), val in s.items():
        s_ref[b, h] = val


def _gdn(u, gb, za, state, cw, og, L):
    b, s, _ = u.shape
    nc = s // L
    ns = 3 * GDN_HEADS
    nb = GDN_GROUP if b % GDN_GROUP == 0 else 1
    g = GDN_STEP_CHUNKS if nc % GDN_STEP_CHUNKS == 0 else 1
    blk = lambda wd: pl.BlockSpec((nb, g * L, wd), lambda i, j: (i, j, 0))
    has_state = state is not None
    state_specs = [pl.BlockSpec((nb, 8, ns * LANES), lambda i, j: (i, 0, 0)),
                   pl.BlockSpec((nb, GDN_HEADS, GDN_DK, LANES), lambda i, j: (i, 0, 0, 0))]
    return pl.pallas_call(
        functools.partial(_gdn_body, L=L, NB=nb, G=g, has_state=has_state),
        grid=(b // nb, nc // g),
        in_specs=[blk(ns * LANES), blk(LANES), blk(GDN_HEADS * LANES),
                  _const_spec((ns, 8, LANES)), _const_spec((1, LANES))] + (state_specs if has_state else []),
        out_specs=[blk(GDN_HEADS * LANES),
                   pl.BlockSpec((nb, GDN_HEADS, GDN_DK, LANES), lambda i, j: (i, 0, 0, 0))],
        out_shape=[jax.ShapeDtypeStruct((b, s, GDN_HEADS * LANES), BF16),
                   jax.ShapeDtypeStruct((b, GDN_HEADS, GDN_DK, LANES), F32)],
        scratch_shapes=[pltpu.VMEM((nb * ns, 8 + g * L, LANES), F32)],
        compiler_params=_params(("arbitrary", "arbitrary")),
        name="gdn",
    )(u, gb, za, cw, og, *(state if has_state else ()))


def _band_body(*refs, L, W, has_cache):
    if has_cache:
        (qlo_ref, qhi_ref, k_ref, vlo_ref, vhi_ref, zb_ref, bias_ref, ck_ref, cv_ref,
         o_ref, kscr, vlo, vhi, biasm) = refs
    else:
        (qlo_ref, qhi_ref, k_ref, vlo_ref, vhi_ref, zb_ref, bias_ref,
         o_ref, kscr, vlo, vhi, biasm) = refs
    c = pl.program_id(1)
    dm = BAND_HEADS * BAND_HD
    pad_chunks = BAND_PAST // L

    @pl.when(c == 0)
    def _():
        if has_cache:
            lo = (lax.broadcasted_iota(jnp.int32, (1, dm), 1) & (LANES - 1)) < BAND_HD
            kscr[0:BAND_PAST, :] = ck_ref[0].reshape(BAND_PAST, dm).astype(BF16)
            cv = cv_ref[0].reshape(BAND_PAST, dm)
            vlo[0:BAND_PAST, :] = jnp.where(lo, cv, 0.0).astype(BF16)
            vhi[0:BAND_PAST, :] = jnp.where(lo, 0.0, cv).astype(BF16)
        else:
            zero = jnp.zeros((BAND_PAST, dm), BF16)
            kscr[0:BAND_PAST, :] = zero
            vlo[0:BAND_PAST, :] = zero
            vhi[0:BAND_PAST, :] = zero

    new0 = pl.multiple_of(BAND_PAST + c * L, L)
    kscr[pl.ds(new0, L), :] = k_ref[0]
    vlo[pl.ds(new0, L), :] = vlo_ref[0]
    vhi[pl.ds(new0, L), :] = vhi_ref[0]

    if has_cache:
        @pl.when(c == 0)
        def _():
            biasm[...] = bias_ref[...] * LOG2_E
    else:
        @pl.when(c < pad_chunks)
        def _():
            wcol = lax.broadcasted_iota(jnp.int32, (1, W), 1)
            valid = (wcol + c * L) >= BAND_PAST
            for h in range(BAND_HEADS):
                biasm[h] = jnp.where(valid, bias_ref[h] * LOG2_E, -jnp.inf)

        @pl.when(c == pad_chunks)
        def _():
            biasm[...] = bias_ref[...] * LOG2_E
    bias_src = biasm

    w0 = pl.multiple_of(c * L, L)
    zb = zb_ref[0]
    slabs = [slice(s * LANES, (s + 1) * LANES) for s in range(BAND_HEADS // 2)]
    sc = []
    for sl in slabs:
        ks = kscr[pl.ds(w0, W), sl]
        sc.append(_dot_nt(qlo_ref[0, :, sl], ks))
        sc.append(_dot_nt(qhi_ref[0, :, sl], ks))
    ps, rs = [], []
    for h in range(BAND_HEADS):
        x = sc[h] + bias_src[h]
        e = jnp.exp2(x - jnp.max(x, axis=-1, keepdims=True))
        rs.append(1.0 / jnp.sum(e, axis=-1, keepdims=True))
        ps.append(e.astype(BF16))
    pv = []
    for i, sl in enumerate(slabs):
        pv.append(_dot(ps[2 * i], vlo[pl.ds(w0, W), sl]))
        pv.append(_dot(ps[2 * i + 1], vhi[pl.ds(w0, W), sl]))
    outs = [((pv[2 * i] * rs[2 * i] + pv[2 * i + 1] * rs[2 * i + 1]) * zb[:, sl]).astype(BF16)
            for i, sl in enumerate(slabs)]
    o_ref[0] = jnp.concatenate(outs, axis=1)


def _band(qlo, qhi, kn, vlo, vhi, zb, bias, caches, L):
    b, s, dm = qlo.shape
    nc = s // L
    W = BAND_PAST + L
    blk = pl.BlockSpec((1, L, dm), lambda i, j: (i, j, 0))
    cache = pl.BlockSpec((1, BAND_PAST, BAND_HEADS, BAND_HD), lambda i, j: (i, 0, 0, 0))
    has_cache = caches is not None
    scratch = [pltpu.VMEM((BAND_PAST + s, dm), BF16)] * 3 + [pltpu.VMEM((BAND_HEADS, L, W), F32)]
    return pl.pallas_call(
        functools.partial(_band_body, L=L, W=W, has_cache=has_cache),
        grid=(b, nc),
        in_specs=[blk] * 6 + [_const_spec((BAND_HEADS, L, W))] + ([cache, cache] if has_cache else []),
        out_specs=blk,
        out_shape=jax.ShapeDtypeStruct((b, s, dm), BF16),
        scratch_shapes=scratch,
        compiler_params=_params(("arbitrary", "arbitrary")),
        name="band",
    )(qlo, qhi, kn, vlo, vhi, zb, bias, *(caches if has_cache else ()))


def _gla_body(*refs, L, NB, has_state):
    if has_state:
        q_ref, k_ref, v_ref, lg_ref, z_ref, og_ref, sinit_ref, o_ref, st_ref, cbs = refs
    else:
        q_ref, k_ref, v_ref, lg_ref, z_ref, og_ref, o_ref, st_ref, cbs = refs
    c = pl.program_id(1)

    @pl.when(c == 0)
    def _():
        st_ref[...] = sinit_ref[...] if has_state else jnp.zeros(st_ref.shape, F32)

    levels = [L >> (t + 1) for t in range(L.bit_length() - 1)]
    small = [s for s in levels if 2 < 2 * s < 16]
    row = lax.broadcasted_iota(jnp.int32, (L, L), 0)
    col = lax.broadcasted_iota(jnp.int32, (L, L), 1)
    anchor = lambda s: lax.shift_left(lax.shift_right_logical(row, s.bit_length()), s.bit_length()) + (s - 1)
    between = lambda s: (col > jnp.minimum(row, anchor(s))) & (col <= jnp.maximum(row, anchor(s)))
    onehot = lambda m: jnp.where(m, 1.0, 0.0).astype(BF16)
    tri = onehot(row >= col)
    stack = jnp.concatenate([onehot(between(s)) for s in small], axis=0)
    ex = jnp.exp2
    og = og_ref[...]
    scale = GLA_DK ** -0.5
    lg2 = [lg_ref[b] * LOG2_E for b in range(NB)]
    pieces = [_split3(x) for x in lg2]
    cb_all = [_dot(tri, p[0]) + (_dot(tri, p[1]) + _dot(tri, p[2])) for p in pieces]
    y_small = [_dot(stack, p[0]) + _dot(stack, p[1]) for p in pieces]
    odd_row = (lax.broadcasted_iota(jnp.int32, (L, 1), 0) & 1) == 1
    ydec = [dict() for _ in range(NB)]
    for b in range(NB):
        cbs[b] = cb_all[b]
        ydec[b][1] = jnp.where(odd_row, lg2[b], 0.0)
        for t, s in enumerate(small):
            ydec[b][s] = y_small[b][t * L:(t + 1) * L]
        for s in levels:
            if s not in ydec[b]:
                anc = jnp.concatenate(
                    [jnp.broadcast_to(cbs[b, p * 2 * s + s - 1:p * 2 * s + s, :], (2 * s, cbs.shape[2]))
                     for p in range(L // (2 * s))], axis=0)
                ydec[b][s] = -jnp.abs(cb_all[b] - anc)

    heads = [(b, h) for b in range(NB) for h in range(GLA_HEADS)]
    pairs = range(len(heads) // 2)
    ksl = lambda h: slice(h * GLA_DK, (h + 1) * GLA_DK)
    vsl = lambda h: slice(h * GLA_DV, (h + 1) * GLA_DV)
    cb = [cb_all[b][:, ksl(h)] for b, h in heads]
    q = [q_ref[b, :, ksl(h)] * scale for b, h in heads]
    k = [k_ref[b, :, ksl(h)] for b, h in heads]
    vb = [v_ref[b, :, vsl(h)].astype(BF16) for b, h in heads]
    st = [st_ref[b, h] for b, h in heads]
    o_inter = _each(lambda q_, c_, s_: _dot((q_ * ex(c_)).astype(BF16), s_.astype(BF16)), q, cb, st)
    cl = [c_[L - 1:L] for c_ in cb]
    ke = _each(lambda k_, l_, c_: (k_ * ex(l_ - c_)).astype(BF16), k, cl, cb)
    ktv = _each(_dot_tn, ke, vb)
    dcol = [jnp.broadcast_to(ex(l_), (GLA_DK, GLA_DK)).T for l_ in cl]

    zk = jnp.zeros((L, GLA_DK), BF16)
    zv = jnp.zeros((L, GLA_DV), BF16)

    def side_by_side(qs, ks):
        lhs = jnp.concatenate(qs, axis=1).astype(BF16)
        rhs = jnp.concatenate([jnp.concatenate([ks[0].astype(BF16), zk], axis=1),
                               jnp.concatenate([zk, ks[1].astype(BF16)], axis=1)], axis=0)
        return _dot_nt(lhs, rhs)

    prow = lax.broadcasted_iota(jnp.int32, (L, 2 * L), 0)
    pcol = lax.broadcasted_iota(jnp.int32, (L, 2 * L), 1) & (L - 1)
    att = [jnp.where(prow == pcol, side_by_side((q[2 * p], q[2 * p + 1]), (k[2 * p], k[2 * p + 1])), 0.0)
           for p in pairs]
    for s in levels:
        sh = s.bit_length() - 1
        same_parent = lax.shift_right_logical(prow, sh + 1) == lax.shift_right_logical(pcol, sh + 1)
        take = same_parent & ((lax.shift_right_logical(prow, sh) & 1) == 1) & \
            ((lax.shift_right_logical(pcol, sh) & 1) == 0)
        f = [ex(ydec[b][s][:, ksl(h)]) for b, h in heads]
        prod = [side_by_side((q[2 * p] * f[2 * p], q[2 * p + 1] * f[2 * p + 1]),
                             (k[2 * p] * f[2 * p], k[2 * p + 1] * f[2 * p + 1])) for p in pairs]
        att = [jnp.where(take, prod[p], att[p]) for p in pairs]
    vpair = [jnp.concatenate([jnp.concatenate([vb[2 * p], zv], axis=1),
                              jnp.concatenate([zv, vb[2 * p + 1]], axis=1)], axis=0) for p in pairs]
    o_intra = [_dot(att[p].astype(BF16), vpair[p]) for p in pairs]
    for i, (b, h) in enumerate(heads):
        st_ref[b, h] = st[i] * jnp.concatenate([dcol[i]] * (GLA_DV // GLA_DK), axis=1) + ktv[i]
    for b in range(NB):
        z = z_ref[b]
        outs = [(_rms(o_inter[i] + o_intra[i // 2][:, (i % 2) * GLA_DV:(i % 2 + 1) * GLA_DV], og)
                 * z[:, vsl(h)]).astype(BF16) for i, (b_, h) in enumerate(heads) if b_ == b]
        o_ref[b] = jnp.concatenate(outs, axis=1)


def _gla(q, k, v, lg, z, sinit, og, L):
    b, s, _ = q.shape
    nc = s // L
    nb = GLA_GROUP if b % GLA_GROUP == 0 else 1
    qk = pl.BlockSpec((nb, L, GLA_HEADS * GLA_DK), lambda i, j: (i, j, 0))
    vv = pl.BlockSpec((nb, L, GLA_HEADS * GLA_DV), lambda i, j: (i, j, 0))
    st = pl.BlockSpec((nb, GLA_HEADS, GLA_DK, GLA_DV), lambda i, j: (i, 0, 0, 0))
    has_state = sinit is not None
    return pl.pallas_call(
        functools.partial(_gla_body, L=L, NB=nb, has_state=has_state),
        grid=(b // nb, nc),
        in_specs=[qk, qk, vv, qk, vv, _const_spec((1, GLA_DV))] + ([st] if has_state else []),
        out_specs=[vv, st],
        out_shape=[jax.ShapeDtypeStruct((b, s, GLA_HEADS * GLA_DV), BF16),
                   jax.ShapeDtypeStruct((b, GLA_HEADS, GLA_DK, GLA_DV), F32)],
        scratch_shapes=[pltpu.VMEM((nb, L, GLA_HEADS * GLA_DK), F32)],
        compiler_params=_params(("arbitrary", "arbitrary")),
        name="gla",
    )(q, k, v, lg, z, og, *((sinit,) if has_state else ()))


def _post_body(*refs, n_o):
    x_ref = refs[0]
    o_refs = refs[1:1 + n_o]
    wout_ref, mg_ref, wmq_ref, mqg_ref, mk_ref, mv_ref, wmo_ref, y_ref = refs[1 + n_o:]
    tm = x_ref.shape[1]
    ts = tm // POST_SPLIT if tm % (SUB_TILE_MIN * POST_SPLIT) == 0 else tm
    rows = [slice(t * ts, (t + 1) * ts) for t in range(tm // ts)]
    acc = [x_ref[0, r, :] for r in rows]
    off = 0
    for o_ref in o_refs:
        kd = o_ref.shape[-1]
        w = wout_ref[off:off + kd, :]
        acc = [a + _dot(o_ref[0, r, :], w) for a, r in zip(acc, rows)]
        off += kd
    hm = [_rms(a, mg_ref[...]).astype(BF16) for a in acc]
    qz = [_dot(h_, wmq_ref[...]) for h_ in hm]
    hw = MEM_HEADS * MEM_HD
    sls = [slice(h * MEM_HD, (h + 1) * MEM_HD) for h in range(MEM_HEADS)]
    mkb = [mk_ref[0, :, sl].astype(BF16) for sl in sls]
    mvb = [mv_ref[0, :, sl].astype(BF16) for sl in sls]
    qscale = MEM_HD ** -0.5 * LOG2_E
    qn = [[(_rms(z[:, sl], mqg_ref[...]) * qscale).astype(BF16) for sl in sls] for z in qz]
    sc = [[_dot_nt(qh, kh) for qh, kh in zip(qt, mkb)] for qt in qn]
    p = [[_softmax2_rows(x).astype(BF16) for x in st] for st in sc]
    oh = [[_dot(ph, vh) for ph, vh in zip(pt, mvb)] for pt in p]
    for t, r in enumerate(rows):
        outs = [oh[t][h] * _silu(qz[t][:, hw + h * MEM_HD:hw + (h + 1) * MEM_HD]) for h in range(MEM_HEADS)]
        oh[t] = jnp.concatenate(outs, axis=1).astype(BF16)
    ym = [_dot(om, wmo_ref[...]) for om in oh]
    for t, r in enumerate(rows):
        y_ref[0, r, :] = acc[t] + ym[t]


def _post(x, os_, wout, mg, wmq, mqg, mk, mv, wmo):
    b, s, d = x.shape
    tm = _row_tile(s, 256 * POST_SPLIT)
    hw = MEM_HEADS * MEM_HD
    nm = mk.shape[1]
    blk = lambda wd: pl.BlockSpec((1, tm, wd), lambda i, j: (i, j, 0))
    mem = pl.BlockSpec((1, nm, hw), lambda i, j: (i, 0, 0))
    kin = sum(o.shape[-1] for o in os_)
    return pl.pallas_call(
        functools.partial(_post_body, n_o=len(os_)),
        grid=(b, s // tm),
        in_specs=[blk(d)] + [blk(o.shape[-1]) for o in os_]
        + [_const_spec((kin, d)), _const_spec((1, d)), _const_spec((d, 2 * hw)), _const_spec((1, MEM_HD)),
           mem, mem, _const_spec((hw, d))],
        out_specs=blk(d),
        out_shape=jax.ShapeDtypeStruct((b, s, d), F32),
        compiler_params=_params(("arbitrary", "arbitrary")),
        name="post",
    )(x, *os_, wout, mg, wmq, mqg, mk, mv, wmo)


def _pad_cols(w, n):
    return jnp.pad(w, ((0, 0), (0, n - w.shape[1])))


def _prep_l0(w_in, conv_w, a_log, dt_bias, q_g, k_g):
    c0 = 3 * 1024
    ab = w_in[:, c0:c0 + 2 * GDN_HEADS]
    c1 = c0 + 2 * GDN_HEADS
    cols = lambda i: w_in[:, c1 + i * 1024:c1 + (i + 1) * 1024].astype(BF16)
    w = dict(
        wu=w_in[:, :c0].astype(BF16), wab=_pad_cols(ab, LANES).astype(BF16),
        wza=cols(0), wq=cols(1), wk=cols(2), wv=cols(3), wzb=cols(4),
        alog=jnp.pad(a_log, (0, LANES - GDN_HEADS)).reshape(1, LANES).astype(F32),
        dtb=jnp.pad(dt_bias, (0, LANES - GDN_HEADS)).reshape(1, LANES).astype(F32),
    )
    ns = 3 * GDN_HEADS
    cw = jnp.pad(conv_w.astype(F32), ((0, 8 - CONV_TAPS), (0, 0))).reshape(8, ns, LANES).transpose(1, 0, 2)
    dm = BAND_HEADS * BAND_HD
    qg = jnp.tile(q_g.astype(F32), BAND_HEADS).reshape(1, dm)
    kg = jnp.tile(k_g.astype(F32), BAND_HEADS).reshape(1, dm)
    return w, cw, qg, kg


def _band_bias_table(rel_bias, L, chunk):
    W = BAND_PAST + L
    n = np.arange(W + L - 1)
    idx = np.clip(BAND_PAST + (L - 1) - n, -BAND_MAX_REL, BAND_MAX_REL) + BAND_MAX_REL
    strip = rel_bias.astype(F32)[:, idx]
    bias = jnp.stack([strip[:, L - 1 - a:L - 1 - a + W] for a in range(L)], axis=1)
    back = np.arange(L)[:, None] // chunk - (np.arange(W)[None, :] - BAND_PAST) // chunk
    readable = (back >= 0) & (back <= BAND_PAST // chunk)
    return bias if readable.all() else jnp.where(jnp.asarray(readable)[None], bias, -jnp.inf)


def _prep_l1(w_in, w_gate_up, gate_bias):
    qk = GLA_HEADS * GLA_DK
    vw = GLA_HEADS * GLA_DV
    o = np.cumsum([0, qk, qk, vw, GLA_RANK, vw])
    return dict(
        wq=w_in[:, o[0]:o[1]].astype(BF16), wk=w_in[:, o[1]:o[2]].astype(BF16),
        wv=w_in[:, o[2]:o[3]].astype(BF16), wlr=_pad_cols(w_in[:, o[3]:o[4]], LANES).astype(BF16),
        wz=w_in[:, o[4]:o[5]].astype(BF16),
        wg=jnp.pad(w_gate_up, ((0, LANES - GLA_RANK), (0, 0))).astype(BF16),
        gbias=gate_bias.reshape(1, qk).astype(F32),
    )


def _layer0(x, gdn_states, band_caches, mk, mv, norm_g, pw, cw, qg, kg, bias,
            a_onorm_g, w_out, mnorm_g, w_mq, mq_g, w_mo, L):
    b, s, d = x.shape
    outs = _proj0(x.reshape(b * s, d), norm_g.reshape(1, d), pw, qg, kg, s)
    u, gb, za, qlo, qhi, kn, klast, vlo, vhi, vlast, zb = (t.reshape((b, -1) + t.shape[1:]) for t in outs)
    if gdn_states is not None:
        conv_state, gdn_state = gdn_states
        gdn_states = (jnp.pad(conv_state.astype(F32), ((0, 0), (8 - (CONV_TAPS - 1), 0), (0, 0))),
                      gdn_state.astype(F32))
    o_a, s_new = _gdn(u, gb, za, gdn_states, cw, a_onorm_g.reshape(1, LANES), L)
    o_b = _band(qlo, qhi, kn, vlo, vhi, zb, bias, band_caches, bias.shape[1])
    y = _post(x, (o_a, o_b), w_out, mnorm_g.reshape(1, d), w_mq, mq_g.reshape(1, MEM_HD), mk, mv, w_mo)
    return y, u[:, s - (CONV_TAPS - 1):, :], s_new, klast, vlast


def _layer1(x, gla_state, mk, mv, norm_g, pw, c_onorm_g, w_out, mnorm_g, w_mq, mq_g, w_mo, L):
    b, s, d = x.shape
    q, k, v, lg, z = _proj1(x.reshape(b * s, d), norm_g.reshape(1, d), pw)
    r3 = lambda t: t.reshape(b, s, t.shape[-1])
    q, k, v, lg, z = map(r3, (q, k, v, lg, z))
    o, st = _gla(q, k, v, lg, z, gla_state, c_onorm_g.reshape(1, GLA_DV), L)
    y = _post(x, (o,), w_out, mnorm_g.reshape(1, d), w_mq, mq_g.reshape(1, MEM_HD), mk, mv, w_mo)
    return y, st


def kernel(x_prompt, x_sample, mem_prompt, state_l0_gdn_conv, state_l0_gdn, cache_l0_band_k, cache_l0_band_v, cache_l0_mem_k, cache_l0_mem_v, state_l1_gla, cache_l1_mem_k, cache_l1_mem_v, l0_norm_g, l0_w_in, l0_conv_w, l0_a_log, l0_dt_bias, l0_a_onorm_g, l0_b_q_g, l0_b_k_g, l0_b_rel_bias, l0_w_out, l0_mnorm_g, l0_mem_norm_g, l0_w_mkv, l0_mk_g, l0_w_mq, l0_mq_g, l0_w_mo, l1_norm_g, l1_w_in, l1_w_gate_up, l1_gate_bias, l1_c_onorm_g, l1_w_out, l1_mnorm_g, l1_mem_norm_g, l1_w_mkv, l1_mk_g, l1_w_mq, l1_mq_g, l1_w_mo):
    bp, sp, d = x_prompt.shape
    bs, ss, _ = x_sample.shape
    nm = mem_prompt.shape[1]
    hw = MEM_HEADS * MEM_HD
    dm = BAND_HEADS * BAND_HD
    assert sp % CHUNK_ == 0 and ss % INV_SUB == 0 and ss <= CHUNK_
    assert cache_l0_band_k.shape[1] == BAND_PAST

    pw0, cw, qg, kg = _prep_l0(l0_w_in, l0_conv_w, l0_a_log, l0_dt_bias, l0_b_q_g, l0_b_k_g)
    pw1 = _prep_l1(l1_w_in, l1_w_gate_up, l1_gate_bias)
    band_rows = BAND_STEP_CHUNKS * CHUNK_ if sp % (BAND_STEP_CHUNKS * CHUNK_) == 0 else CHUNK_
    bias_p = _band_bias_table(l0_b_rel_bias, band_rows, CHUNK_)
    bias_s = _band_bias_table(l0_b_rel_bias, ss, ss)
    bf = lambda w: w.astype(BF16)
    mem2 = mem_prompt.reshape(bp * nm, d)

    p_mk0, p_mv0, mk0, mv0 = _memkv(mem2, l0_mem_norm_g.reshape(1, d), bf(l0_w_mkv), l0_mk_g.reshape(1, MEM_HD))
    mk0 = mk0.reshape(bp, nm, hw)
    mv0 = mv0.reshape(bp, nm, hw)
    l0_shared = (l0_norm_g, pw0, cw, qg, kg)
    l0_tail = (l0_a_onorm_g, bf(l0_w_out), l0_mnorm_g, bf(l0_w_mq), l0_mq_g, bf(l0_w_mo))
    yp, p_conv, p_gdn, p_kn, p_v = _layer0(
        x_prompt, None, None, mk0, mv0,
        *l0_shared, bias_p, *l0_tail, CHUNK_)
    ys, s_conv, s_gdn, s_kn, s_v = _layer0(
        x_sample, (state_l0_gdn_conv, state_l0_gdn),
        (cache_l0_band_k, cache_l0_band_v),
        cache_l0_mem_k.reshape(bs, nm, hw), cache_l0_mem_v.reshape(bs, nm, hw),
        *l0_shared, bias_s, *l0_tail, ss)

    p_mk1, p_mv1, mk1, mv1 = _memkv(mem2, l1_mem_norm_g.reshape(1, d), bf(l1_w_mkv), l1_mk_g.reshape(1, MEM_HD))
    mk1 = mk1.reshape(bp, nm, hw)
    mv1 = mv1.reshape(bp, nm, hw)
    l1_tail = (l1_c_onorm_g, bf(l1_w_out), l1_mnorm_g, bf(l1_w_mq), l1_mq_g, bf(l1_w_mo))
    yp, p_gla = _layer1(yp, None, mk1, mv1, l1_norm_g, pw1, *l1_tail, CHUNK_)
    ys, s_gla = _layer1(ys, state_l1_gla.astype(F32), cache_l1_mem_k.reshape(bs, nm, hw),
                        cache_l1_mem_v.reshape(bs, nm, hw), l1_norm_g, pw1, *l1_tail, ss)

    m4 = lambda t: t.reshape(bp, nm, MEM_HEADS, MEM_HD)
    return (yp, ys, p_conv, p_gdn, p_kn, p_v,
            m4(p_mk0), m4(p_mv0), p_gla, m4(p_mk1), m4(p_mv1),
            s_conv, s_gdn, s_kn, s_v, s_gla)
```

```python
import functools

import jax
import jax.numpy as jnp
import numpy as np
from jax import lax
from jax.experimental import pallas as pl
from jax.experimental.pallas import tpu as pltpu

F32 = jnp.float32
BF16 = jnp.bfloat16
NORM_EPS = 1e-6
LOG2_E = 1.4426950408889634

CHUNK_ = 64
CONV_TAPS = 4
GDN_HEADS = 8
GDN_DK = 128
BAND_HEADS = 16
BAND_HD = 64
BAND_PAST = 512
BAND_MAX_REL = 128
GLA_HEADS = 8
GLA_DK = 128
GLA_DV = 256
GLA_RANK = 16
GLA_TAU = 16.0
MEM_HEADS = 4
MEM_HD = 128
INV_SUB = 16
GDN_GROUP = 2
GDN_STEP_CHUNKS = 1
GLA_GROUP = 4
POST_SPLIT = 4
PROJ_SPLIT = 2
SUB_TILE_MIN = 128
BAND_STEP_CHUNKS = 2
LANES = 128
VMEM_LIMIT = 56 * 1024 * 1024


def _dot(a, b):
    return jnp.dot(a, b, preferred_element_type=F32)


def _dot_nt(a, b):
    return lax.dot_general(a, b, (((1,), (1,)), ((), ())), preferred_element_type=F32)


def _dot_tn(a, b):
    return lax.dot_general(a, b, (((0,), (0,)), ((), ())), preferred_element_type=F32)


def _split3(x):
    hi = x.astype(BF16)
    r = x - hi.astype(F32)
    mid = r.astype(BF16)
    lo = (r - mid.astype(F32)).astype(BF16)
    return hi, mid, lo


def _dot_exact_lhs(a_bf, b):
    h, m, l = _split3(b)
    return _dot(a_bf, h) + (_dot(a_bf, m) + _dot(a_bf, l))


def _rms(x, g):
    ms = jnp.mean(x * x, axis=-1, keepdims=True)
    return x * lax.rsqrt(ms + NORM_EPS) * g


def _silu(x):
    return x * jax.nn.sigmoid(x)


def _softplus(x):
    return jnp.maximum(x, 0.0) + jnp.log1p(jnp.exp(-jnp.abs(x)))


def _log_sigmoid_scaled(x, scale):
    e = jnp.exp2(jnp.abs(x) * -LOG2_E)
    return (jnp.minimum(x, 0.0) - jnp.log(1.0 + e)) * scale


def _softmax2_rows(s):
    m = jnp.max(s, axis=-1, keepdims=True)
    e = jnp.exp2(s - m)
    return e * (1.0 / jnp.sum(e, axis=-1, keepdims=True))


def _const_spec(shape):
    nd = len(shape)
    return pl.BlockSpec(shape, lambda *_: (0,) * nd, pipeline_mode=pl.Buffered(1))


def _params(sem):
    return pltpu.CompilerParams(dimension_semantics=sem, vmem_limit_bytes=VMEM_LIMIT)


def _row_tile(n, want):
    t = min(n, want)
    assert n % t == 0
    return t


def _sub_tiles(tm):
    ts = tm // PROJ_SPLIT if tm % (SUB_TILE_MIN * PROJ_SPLIT) == 0 else tm
    return [slice(t * ts, (t + 1) * ts) for t in range(tm // ts)]


def _proj0_body(x_ref, g_ref, wu_ref, wab_ref, wza_ref, wq_ref, wk_ref, wv_ref, wzb_ref,
                alog_ref, dtb_ref, qg_ref, kg_ref,
                u_ref, gb_ref, za_ref, qlo_ref, qhi_ref, kn_ref, klast_ref, vlo_ref, vhi_ref, vlast_ref, zb_ref,
                *, kept_tiles):
    h = _rms(x_ref[...], g_ref[...]).astype(BF16)
    dm = BAND_HEADS * BAND_HD
    lo = (lax.broadcasted_iota(jnp.int32, (1, dm), 1) & (LANES - 1)) < BAND_HD
    q = _dot(h, wq_ref[...])
    k = _dot(h, wk_ref[...])
    ab = _dot(h, wab_ref[...])
    za = _dot(h, wza_ref[...])

    zb = _dot(h, wzb_ref[...])
    v = _dot(h, wv_ref[...])
    u_ref[...] = _dot(h, wu_ref[...])
    za_ref[...] = _silu(za)
    zb_ref[...] = _silu(zb)
    lo1 = lo[:, :LANES]

    def head_rsqrt(x):
        out = []
        for s in range(BAND_HEADS // 2):
            x2 = x[:, s * LANES:(s + 1) * LANES]
            x2 = x2 * x2
            first = jnp.sum(jnp.where(lo1, x2, 0.0), axis=-1, keepdims=True)
            second = jnp.sum(jnp.where(lo1, 0.0, x2), axis=-1, keepdims=True)
            r = lambda t: lax.rsqrt(t * (1.0 / BAND_HD) + NORM_EPS)
            out.append(jnp.where(lo1, r(first), r(second)))
        return jnp.concatenate(out, axis=1)

    qn = q * head_rsqrt(q) * (qg_ref[...] * (BAND_HD ** -0.5 * LOG2_E))
    kn = k * head_rsqrt(k) * kg_ref[...]
    qlo_ref[...] = jnp.where(lo, qn, 0.0).astype(BF16)
    qhi_ref[...] = jnp.where(lo, 0.0, qn).astype(BF16)
    kn_ref[...] = kn.astype(BF16)
    vlo_ref[...] = jnp.where(lo, v, 0.0).astype(BF16)
    vhi_ref[...] = jnp.where(lo, 0.0, v).astype(BF16)

    def hand_on():
        klast_ref[...] = kn.reshape(kn.shape[0], BAND_HEADS, BAND_HD)
        vlast_ref[...] = v.reshape(v.shape[0], BAND_HEADS, BAND_HD)

    if kept_tiles is None:
        hand_on()
    else:
        tpb, kt = kept_tiles
        pl.when(pl.program_id(0) % tpb >= tpb - kt)(hand_on)
    lane = lax.broadcasted_iota(jnp.int32, ab.shape, 1)
    gval = -jnp.exp(alog_ref[...]) * _softplus(ab + dtb_ref[...])
    gb_ref[...] = jnp.where(lane < GDN_HEADS, gval, jax.nn.sigmoid(ab))


def _proj0(x2, g, w, qg, kg, rows_per_batch):
    n, d = x2.shape
    tm = _row_tile(n, 256)
    dm = BAND_HEADS * BAND_HD
    widths = (3 * 1024, LANES, 1024, dm, dm, dm, dm)
    row = lambda wd: pl.BlockSpec((tm, wd), lambda i: (i, 0))
    keep = min(BAND_PAST, rows_per_batch)
    if rows_per_batch > keep:
        assert rows_per_batch % tm == 0 and keep % tm == 0
        tpb, kt = rows_per_batch // tm, keep // tm
        last = pl.BlockSpec((tm, BAND_HEADS, BAND_HD),
                            lambda i: ((i // tpb) * kt + jnp.maximum(i % tpb - (tpb - kt), 0), 0, 0))
        n_last = (n // rows_per_batch) * keep
        kept_tiles = (tpb, kt)
    else:
        last, n_last = pl.BlockSpec((tm, BAND_HEADS, BAND_HD), lambda i: (i, 0, 0)), n
        kept_tiles = None
    last_shape = jax.ShapeDtypeStruct((n_last, BAND_HEADS, BAND_HD), F32)
    f32 = lambda rows, wd: jax.ShapeDtypeStruct((rows, wd), F32)
    bf16 = lambda wd: jax.ShapeDtypeStruct((n, wd), BF16)
    return pl.pallas_call(
        functools.partial(_proj0_body, kept_tiles=kept_tiles),
        grid=(n // tm,),
        in_specs=[row(d), _const_spec((1, d))]
        + [_const_spec((d, wd)) for wd in widths]
        + [_const_spec((1, LANES)), _const_spec((1, LANES)), _const_spec((1, dm)), _const_spec((1, dm))],
        out_specs=[row(3 * 1024), row(LANES), row(1024), row(dm), row(dm), row(dm), last, row(dm), row(dm), last,
                   row(dm)],
        out_shape=[f32(n, 3 * 1024), f32(n, LANES), f32(n, 1024), bf16(dm), bf16(dm), bf16(dm), last_shape,
                   bf16(dm), bf16(dm), last_shape, f32(n, dm)],
        compiler_params=_params(("arbitrary",)),
        name="proj0",
    )(x2, g, w["wu"], w["wab"], w["wza"], w["wq"], w["wk"], w["wv"], w["wzb"], w["alog"], w["dtb"],
      qg, kg)


def _proj1_body(x_ref, g_ref, wq_ref, wk_ref, wv_ref, wlr_ref, wz_ref, wg_ref, gbias_ref,
                q_ref, k_ref, v_ref, lg_ref, z_ref):
    rows = _sub_tiles(x_ref.shape[0])
    h = [_rms(x_ref[r, :], g_ref[...]).astype(BF16) for r in rows]
    lr = [_dot(h_, wlr_ref[...]) for h_ in h]
    for r, h_ in zip(rows, h):
        q_ref[r, :] = _dot(h_, wq_ref[...])
    for r, x in zip(rows, lr):
        lg_ref[r, :] = _dot(x.astype(BF16), wg_ref[...])
    for r, h_ in zip(rows, h):
        z_ref[r, :] = _dot(h_, wz_ref[...])
    for r, h_ in zip(rows, h):
        k_ref[r, :] = _dot(h_, wk_ref[...])
    for r in rows:
        lg_ref[r, :] = _log_sigmoid_scaled(lg_ref[r, :] + gbias_ref[...], 1.0 / GLA_TAU)
    for r, h_ in zip(rows, h):
        v_ref[r, :] = _dot(h_, wv_ref[...])
    for r in rows:
        z_ref[r, :] = _silu(z_ref[r, :])


def _proj1(x2, g, w):
    n, d = x2.shape
    tm = _row_tile(n, 512)
    row = lambda wd: pl.BlockSpec((tm, wd), lambda i: (i, 0))
    outw = (1024, 1024, 2048, 1024, 2048)
    return pl.pallas_call(
        _proj1_body,
        grid=(n // tm,),
        in_specs=[row(d), _const_spec((1, d)), _const_spec((d, 1024)), _const_spec((d, 1024)),
                  _const_spec((d, 2048)), _const_spec((d, LANES)), _const_spec((d, 2048)),
                  _const_spec((LANES, 1024)), _const_spec((1, 1024))],
        out_specs=[row(wd) for wd in outw],
        out_shape=[jax.ShapeDtypeStruct((n, wd), F32) for wd in outw],
        compiler_params=_params(("arbitrary",)),
        name="proj1",
    )(x2, g, w["wq"], w["wk"], w["wv"], w["wlr"], w["wz"], w["wg"], w["gbias"])


def _memkv_body(m_ref, g_ref, w_ref, kg_ref, k4_ref, v4_ref, kb_ref, vb_ref):
    h = _rms(m_ref[...], g_ref[...]).astype(BF16)
    kv = _dot(h, w_ref[...])
    hw = MEM_HEADS * MEM_HD
    k = jnp.concatenate([_rms(kv[:, hh * MEM_HD:(hh + 1) * MEM_HD], kg_ref[...]) for hh in range(MEM_HEADS)],
                        axis=1)
    v = kv[:, hw:]
    k4_ref[...] = k.reshape(k.shape[0], MEM_HEADS, MEM_HD)
    v4_ref[...] = v.reshape(v.shape[0], MEM_HEADS, MEM_HD)
    kb_ref[...] = k.astype(BF16)
    vb_ref[...] = v.astype(BF16)


def _memkv(m2, g, w_bf, kg):
    n, d = m2.shape
    tm = _row_tile(n, 256)
    hw = MEM_HEADS * MEM_HD
    row = lambda wd: pl.BlockSpec((tm, wd), lambda i: (i, 0))
    row4 = pl.BlockSpec((tm, MEM_HEADS, MEM_HD), lambda i: (i, 0, 0))
    return pl.pallas_call(
        _memkv_body,
        grid=(n // tm,),
        in_specs=[row(d), _const_spec((1, d)), _const_spec((d, 2 * hw)), _const_spec((1, MEM_HD))],
        out_specs=[row4, row4, row(hw), row(hw)],
        out_shape=[jax.ShapeDtypeStruct((n, MEM_HEADS, MEM_HD), F32)] * 2
        + [jax.ShapeDtypeStruct((n, hw), BF16)] * 2,
        compiler_params=_params(("arbitrary",)),
        name="memkv",
    )(m2, g, w_bf, kg)


def _each(fn, *lists):
    return [fn(*xs) for xs in zip(*lists)]


def _unit_lower_inverse(a, eye, bd, mm):
    d = _each(lambda x: jnp.where(bd, x, 0.0), a)
    nl = _each(lambda x, y: x - y, a, d)
    d2 = _each(mm, d, d)
    d4 = _each(mm, d2, d2)
    td = _each(lambda x, y: mm(eye - x, eye + y), d, d2)
    d8 = _each(mm, d4, d4)
    td = _each(lambda x, y: mm(x, eye + y), td, d4)
    td = _each(lambda x, y: mm(x, eye + y), td, d8)
    m = _each(mm, td, nl)
    m2 = _each(mm, m, m)
    mt = _each(mm, m, td)
    return _each(lambda x, y, z: mm(eye + x, y - z), m2, td, mt)


def _gdn_body(*refs, L, NB, G, has_state):
    if has_state:
        u_ref, gb_ref, za_ref, cw_ref, og_ref, cinit_ref, sinit_ref, o_ref, s_ref, ubuf = refs
    else:
        u_ref, gb_ref, za_ref, cw_ref, og_ref, o_ref, s_ref, ubuf = refs
    c = pl.program_id(1)
    ns = 3 * GDN_HEADS
    hist = 8
    R = G * L

    @pl.when(c == 0)
    def _():
        if has_state:
            for b in range(NB):
                for j in range(ns):
                    ubuf[b * ns + j, 0:hist, :] = cinit_ref[b, :, j * LANES:(j + 1) * LANES]
            s_ref[...] = sinit_ref[...]
        else:
            ubuf[:, 0:hist, :] = jnp.zeros((NB * ns, hist, LANES), F32)
            s_ref[...] = jnp.zeros(s_ref.shape, F32)

    for b in range(NB):
        for j in range(ns):
            ubuf[b * ns + j, hist:hist + R, :] = u_ref[b, :, j * LANES:(j + 1) * LANES]
    base = hist - (CONV_TAPS - 1)
    ys = []
    for b in range(NB):
        bsl = slice(b * ns, (b + 1) * ns)
        yb = ubuf[bsl, base:base + R, :] * cw_ref[:, 0:1, :]
        for i in range(1, CONV_TAPS):
            yb = yb + ubuf[bsl, base + i:base + i + R, :] * cw_ref[:, i:i + 1, :]
        ys.append(_silu(yb))
    ubuf[:, base:hist, :] = ubuf[:, base + R:hist + R, :]

    row = lax.broadcasted_iota(jnp.int32, (L, 2 * L), 0)
    lane = lax.broadcasted_iota(jnp.int32, (L, 2 * L), 1)
    col = lane & (L - 1)
    left = lane < L
    incl = row >= col
    strict = row > col
    sub_shift = INV_SUB.bit_length() - 1
    bd = lax.shift_right_logical(row, sub_shift) == lax.shift_right_logical(col, sub_shift)
    eye = jnp.where(row == col, 1.0, 0.0).astype(F32)
    trow = lax.broadcasted_iota(jnp.int32, (L, L), 0)
    tcol = lax.broadcasted_iota(jnp.int32, (L, L), 1)
    tri = jnp.where(trow >= tcol, 1.0, 0.0).astype(BF16)
    og = og_ref[...]
    heads = [(b, g, h) for b in range(NB) for g in range(G) for h in range(GDN_HEADS)]
    pairs = range(len(heads) // 2)
    rows_of = lambda g: slice(g * L, (g + 1) * L)
    l2n = lambda x: x * lax.rsqrt(jnp.sum(x * x, axis=-1, keepdims=True) + NORM_EPS)
    q = [l2n(ys[b][h][rows_of(g)]) * (GDN_DK ** -0.5) for b, g, h in heads]
    k = [l2n(ys[b][GDN_HEADS + h][rows_of(g)]) for b, g, h in heads]
    v = [ys[b][2 * GDN_HEADS + h][rows_of(g)] for b, g, h in heads]
    gc, gr, bc = [], [], []
    for b in range(NB):
        for g in range(G):
            gbv = gb_ref[b, rows_of(g), :]
            gcum = _dot_exact_lhs(tri, gbv)
            gpad = jnp.concatenate([gcum, jnp.zeros((LANES - L, LANES), F32)], axis=0)
            gt = gpad.T
            for h in range(GDN_HEADS):
                gc.append(gcum[:, h:h + 1])
                gr.append(gt[h:h + 1, 0:L])
                bc.append(gbv[:, GDN_HEADS + h:GDN_HEADS + h + 1])
    side = lambda x0, x1: jnp.where(left, x0, x1)
    gcp = [side(gc[2 * p], gc[2 * p + 1]) for p in pairs]
    grp = [jnp.concatenate([gr[2 * p], gr[2 * p + 1]], axis=1) for p in pairs]
    bcp = [side(bc[2 * p], bc[2 * p + 1]) for p in pairs]
    dec = _each(lambda c_, r_: jnp.where(incl, jnp.exp(jnp.where(incl, c_ - r_, 0.0)), 0.0), gcp, grp)
    kb = _each(lambda x: x.astype(BF16), k)
    qb = _each(lambda x: x.astype(BF16), q)

    def blockdiag(y0, y1):
        z0 = jnp.zeros(y1.shape, y1.dtype)
        z1 = jnp.zeros(y0.shape, y0.dtype)
        return jnp.concatenate([jnp.concatenate([y0, z0], axis=1), jnp.concatenate([z1, y1], axis=1)], axis=0)

    def mm(x, y):
        yb = y.astype(BF16)
        zero = jnp.zeros_like(yb)
        return _dot(x.astype(BF16), jnp.concatenate([jnp.where(left, yb, zero), jnp.where(left, zero, yb)], axis=0))

    kq = [_dot_nt(jnp.concatenate([jnp.concatenate([kb[2 * p], qb[2 * p]], axis=0),
                                   jnp.concatenate([kb[2 * p + 1], qb[2 * p + 1]], axis=0)], axis=1),
                  blockdiag(kb[2 * p], kb[2 * p + 1])) for p in pairs]
    a = _each(lambda b_, x, d_: jnp.where(strict, b_ * x[:L] * d_, 0.0), bcp, kq, dec)
    t = _unit_lower_inverse(a, eye, bd, mm)
    eg = _each(jnp.exp, gc)
    rhs = _each(lambda b_, v_, e_, k_: jnp.concatenate([b_ * v_, (b_ * e_) * k_], axis=1).astype(BF16),
                bc, v, eg, k)
    sol = [_dot(t[p].astype(BF16), blockdiag(rhs[2 * p], rhs[2 * p + 1])) for p in pairs]
    solk = lambda i: sol[i // 2][:, 2 * (i % 2) * LANES + LANES:2 * (i % 2 + 1) * LANES]
    solv = lambda i: sol[i // 2][:, 2 * (i % 2) * LANES:2 * (i % 2) * LANES + LANES]
    qkd = [(kq[p][L:] * dec[p]).astype(BF16) for p in pairs]
    gl = [c_[L - 1:L, :] for c_ in gc]
    kd = _each(lambda k_, l_, c_: (k_ * jnp.exp(l_ - c_)).astype(BF16), k, gl, gc)

    item = lambda b, g, h: (b * G + g) * GDN_HEADS + h
    half = lambda x, i: x[:, (i % 2) * LANES:(i % 2 + 1) * LANES]
    s = {(b, h): s_ref[b, h] for b in range(NB) for h in range(GDN_HEADS)}
    for g in range(G):
        ids = [item(b, g, h) for b in range(NB) for h in range(GDN_HEADS)]
        sb = {i: s[(heads[i][0], heads[i][2])].astype(BF16) for i in ids}
        ksq = {i: _dot(jnp.concatenate([jnp.concatenate([solk(i).astype(BF16), qb[i]], axis=0),
                                        jnp.concatenate([solk(i + 1).astype(BF16), qb[i + 1]], axis=0)], axis=1),
                       blockdiag(sb[i], sb[i + 1])) for i in ids[::2]}
        ub = {i: (solv(i) - half(ksq[i - i % 2][:L], i)).astype(BF16) for i in ids}
        qku = {i: _dot(qkd[i // 2], blockdiag(ub[i], ub[i + 1])) for i in ids[::2]}
        ktu = {i: _dot_tn(kd[i], ub[i]) for i in ids}
        for i in ids:
            b, _, h = heads[i]
            s[(b, h)] = s[(b, h)] * jnp.exp(gl[i]) + ktu[i]
        for b in range(NB):
            za = za_ref[b, rows_of(g), :]
            outs = [(_rms(half(ksq[i - i % 2][L:], i) * eg[i] + half(qku[i - i % 2], i), og)
                     * za[:, heads[i][2] * LANES:(heads[i][2] + 1) * LANES]).astype(BF16)
                    for i in ids if heads[i][0] == b]
            o_ref[b, rows_of(g), :] = jnp.concatenate(outs, axis=1)
    for (b, h), val in s.items():
        s_ref[b, h] = val


def _gdn(u, gb, za, state, cw, og, L):
    b, s, _ = u.shape
    nc = s // L
    ns = 3 * GDN_HEADS
    nb = GDN_GROUP if b % GDN_GROUP == 0 else 1
    g = GDN_STEP_CHUNKS if nc % GDN_STEP_CHUNKS == 0 else 1
    blk = lambda wd: pl.BlockSpec((nb, g * L, wd), lambda i, j: (i, j, 0))
    has_state = state is not None
    state_specs = [pl.BlockSpec((nb, 8, ns * LANES), lambda i, j: (i, 0, 0)),
                   pl.BlockSpec((nb, GDN_HEADS, GDN_DK, LANES), lambda i, j: (i, 0, 0, 0))]
    return pl.pallas_call(
        functools.partial(_gdn_body, L=L, NB=nb, G=g, has_state=has_state),
        grid=(b // nb, nc // g),
        in_specs=[blk(ns * LANES), blk(LANES), blk(GDN_HEADS * LANES),
                  _const_spec((ns, 8, LANES)), _const_spec((1, LANES))] + (state_specs if has_state else []),
        out_specs=[blk(GDN_HEADS * LANES),
                   pl.BlockSpec((nb, GDN_HEADS, GDN_DK, LANES), lambda i, j: (i, 0, 0, 0))],
        out_shape=[jax.ShapeDtypeStruct((b, s, GDN_HEADS * LANES), BF16),
                   jax.ShapeDtypeStruct((b, GDN_HEADS, GDN_DK, LANES), F32)],
        scratch_shapes=[pltpu.VMEM((nb * ns, 8 + g * L, LANES), F32)],
        compiler_params=_params(("arbitrary", "arbitrary")),
        name="gdn",
    )(u, gb, za, cw, og, *(state if has_state else ()))


def _band_body(*refs, L, W, has_cache):
    if has_cache:
        (qlo_ref, qhi_ref, k_ref, vlo_ref, vhi_ref, zb_ref, bias_ref, ck_ref, cv_ref,
         o_ref, kscr, vlo, vhi, biasm) = refs
    else:
        (qlo_ref, qhi_ref, k_ref, vlo_ref, vhi_ref, zb_ref, bias_ref,
         o_ref, kscr, vlo, vhi, biasm) = refs
    c = pl.program_id(1)
    dm = BAND_HEADS * BAND_HD
    pad_chunks = BAND_PAST // L

    @pl.when(c == 0)
    def _():
        if has_cache:
            lo = (lax.broadcasted_iota(jnp.int32, (1, dm), 1) & (LANES - 1)) < BAND_HD
            kscr[0:BAND_PAST, :] = ck_ref[0].reshape(BAND_PAST, dm).astype(BF16)
            cv = cv_ref[0].reshape(BAND_PAST, dm)
            vlo[0:BAND_PAST, :] = jnp.where(lo, cv, 0.0).astype(BF16)
            vhi[0:BAND_PAST, :] = jnp.where(lo, 0.0, cv).astype(BF16)
        else:
            zero = jnp.zeros((BAND_PAST, dm), BF16)
            kscr[0:BAND_PAST, :] = zero
            vlo[0:BAND_PAST, :] = zero
            vhi[0:BAND_PAST, :] = zero

    new0 = pl.multiple_of(BAND_PAST + c * L, L)
    kscr[pl.ds(new0, L), :] = k_ref[0]
    vlo[pl.ds(new0, L), :] = vlo_ref[0]
    vhi[pl.ds(new0, L), :] = vhi_ref[0]

    if has_cache:
        @pl.when(c == 0)
        def _():
            biasm[...] = bias_ref[...] * LOG2_E
    else:
        @pl.when(c < pad_chunks)
        def _():
            wcol = lax.broadcasted_iota(jnp.int32, (1, W), 1)
            valid = (wcol + c * L) >= BAND_PAST
            for h in range(BAND_HEADS):
                biasm[h] = jnp.where(valid, bias_ref[h] * LOG2_E, -jnp.inf)

        @pl.when(c == pad_chunks)
        def _():
            biasm[...] = bias_ref[...] * LOG2_E
    bias_src = biasm

    w0 = pl.multiple_of(c * L, L)
    zb = zb_ref[0]
    slabs = [slice(s * LANES, (s + 1) * LANES) for s in range(BAND_HEADS // 2)]
    sc = []
    for sl in slabs:
        ks = kscr[pl.ds(w0, W), sl]
        sc.append(_dot_nt(qlo_ref[0, :, sl], ks))
        sc.append(_dot_nt(qhi_ref[0, :, sl], ks))
    ps, rs = [], []
    for h in range(BAND_HEADS):
        x = sc[h] + bias_src[h]
        e = jnp.exp2(x - jnp.max(x, axis=-1, keepdims=True))
        rs.append(1.0 / jnp.sum(e, axis=-1, keepdims=True))
        ps.append(e.astype(BF16))
    pv = []
    for i, sl in enumerate(slabs):
        pv.append(_dot(ps[2 * i], vlo[pl.ds(w0, W), sl]))
        pv.append(_dot(ps[2 * i + 1], vhi[pl.ds(w0, W), sl]))
    outs = [((pv[2 * i] * rs[2 * i] + pv[2 * i + 1] * rs[2 * i + 1]) * zb[:, sl]).astype(BF16)
            for i, sl in enumerate(slabs)]
    o_ref[0] = jnp.concatenate(outs, axis=1)


def _band(qlo, qhi, kn, vlo, vhi, zb, bias, caches, L):
    b, s, dm = qlo.shape
    nc = s // L
    W = BAND_PAST + L
    blk = pl.BlockSpec((1, L, dm), lambda i, j: (i, j, 0))
    cache = pl.BlockSpec((1, BAND_PAST, BAND_HEADS, BAND_HD), lambda i, j: (i, 0, 0, 0))
    has_cache = caches is not None
    scratch = [pltpu.VMEM((BAND_PAST + s, dm), BF16)] * 3 + [pltpu.VMEM((BAND_HEADS, L, W), F32)]
    return pl.pallas_call(
        functools.partial(_band_body, L=L, W=W, has_cache=has_cache),
        grid=(b, nc),
        in_specs=[blk] * 6 + [_const_spec((BAND_HEADS, L, W))] + ([cache, cache] if has_cache else []),
        out_specs=blk,
        out_shape=jax.ShapeDtypeStruct((b, s, dm), BF16),
        scratch_shapes=scratch,
        compiler_params=_params(("arbitrary", "arbitrary")),
        name="band",
    )(qlo, qhi, kn, vlo, vhi, zb, bias, *(caches if has_cache else ()))


def _gla_body(*refs, L, NB, has_state):
    if has_state:
        q_ref, k_ref, v_ref, lg_ref, z_ref, og_ref, sinit_ref, o_ref, st_ref, cbs = refs
    else:
        q_ref, k_ref, v_ref, lg_ref, z_ref, og_ref, o_ref, st_ref, cbs = refs
    c = pl.program_id(1)

    @pl.when(c == 0)
    def _():
        st_ref[...] = sinit_ref[...] if has_state else jnp.zeros(st_ref.shape, F32)

    levels = [L >> (t + 1) for t in range(L.bit_length() - 1)]
    small = [s for s in levels if 2 < 2 * s < 16]
    row = lax.broadcasted_iota(jnp.int32, (L, L), 0)
    col = lax.broadcasted_iota(jnp.int32, (L, L), 1)
    anchor = lambda s: lax.shift_left(lax.shift_right_logical(row, s.bit_length()), s.bit_length()) + (s - 1)
    between = lambda s: (col > jnp.minimum(row, anchor(s))) & (col <= jnp.maximum(row, anchor(s)))
    onehot = lambda m: jnp.where(m, 1.0, 0.0).astype(BF16)
    tri = onehot(row >= col)
    stack = jnp.concatenate([onehot(between(s)) for s in small], axis=0)
    ex = jnp.exp2
    og = og_ref[...]
    scale = GLA_DK ** -0.5
    lg2 = [lg_ref[b] * LOG2_E for b in range(NB)]
    pieces = [_split3(x) for x in lg2]
    cb_all = [_dot(tri, p[0]) + (_dot(tri, p[1]) + _dot(tri, p[2])) for p in pieces]
    y_small = [_dot(stack, p[0]) + _dot(stack, p[1]) for p in pieces]
    odd_row = (lax.broadcasted_iota(jnp.int32, (L, 1), 0) & 1) == 1
    ydec = [dict() for _ in range(NB)]
    for b in range(NB):
        cbs[b] = cb_all[b]
        ydec[b][1] = jnp.where(odd_row, lg2[b], 0.0)
        for t, s in enumerate(small):
            ydec[b][s] = y_small[b][t * L:(t + 1) * L]
        for s in levels:
            if s not in ydec[b]:
                anc = jnp.concatenate(
                    [jnp.broadcast_to(cbs[b, p * 2 * s + s - 1:p * 2 * s + s, :], (2 * s, cbs.shape[2]))
                     for p in range(L // (2 * s))], axis=0)
                ydec[b][s] = -jnp.abs(cb_all[b] - anc)

    heads = [(b, h) for b in range(NB) for h in range(GLA_HEADS)]
    pairs = range(len(heads) // 2)
    ksl = lambda h: slice(h * GLA_DK, (h + 1) * GLA_DK)
    vsl = lambda h: slice(h * GLA_DV, (h + 1) * GLA_DV)
    cb = [cb_all[b][:, ksl(h)] for b, h in heads]
    q = [q_ref[b, :, ksl(h)] * scale for b, h in heads]
    k = [k_ref[b, :, ksl(h)] for b, h in heads]
    vb = [v_ref[b, :, vsl(h)].astype(BF16) for b, h in heads]
    st = [st_ref[b, h] for b, h in heads]
    o_inter = _each(lambda q_, c_, s_: _dot((q_ * ex(c_)).astype(BF16), s_.astype(BF16)), q, cb, st)
    cl = [c_[L - 1:L] for c_ in cb]
    ke = _each(lambda k_, l_, c_: (k_ * ex(l_ - c_)).astype(BF16), k, cl, cb)
    ktv = _each(_dot_tn, ke, vb)
    dcol = [jnp.broadcast_to(ex(l_), (GLA_DK, GLA_DK)).T for l_ in cl]

    zk = jnp.zeros((L, GLA_DK), BF16)
    zv = jnp.zeros((L, GLA_DV), BF16)

    def side_by_side(qs, ks):
        lhs = jnp.concatenate(qs, axis=1).astype(BF16)
        rhs = jnp.concatenate([jnp.concatenate([ks[0].astype(BF16), zk], axis=1),
                               jnp.concatenate([zk, ks[1].astype(BF16)], axis=1)], axis=0)
        return _dot_nt(lhs, rhs)

    prow = lax.broadcasted_iota(jnp.int32, (L, 2 * L), 0)
    pcol = lax.broadcasted_iota(jnp.int32, (L, 2 * L), 1) & (L - 1)
    att = [jnp.where(prow == pcol, side_by_side((q[2 * p], q[2 * p + 1]), (k[2 * p], k[2 * p + 1])), 0.0)
           for p in pairs]
    for s in levels:
        sh = s.bit_length() - 1
        same_parent = lax.shift_right_logical(prow, sh + 1) == lax.shift_right_logical(pcol, sh + 1)
        take = same_parent & ((lax.shift_right_logical(prow, sh) & 1) == 1) & \
            ((lax.shift_right_logical(pcol, sh) & 1) == 0)
        f = [ex(ydec[b][s][:, ksl(h)]) for b, h in heads]
        prod = [side_by_side((q[2 * p] * f[2 * p], q[2 * p + 1] * f[2 * p + 1]),
                             (k[2 * p] * f[2 * p], k[2 * p + 1] * f[2 * p + 1])) for p in pairs]
        att = [jnp.where(take, prod[p], att[p]) for p in pairs]
    vpair = [jnp.concatenate([jnp.concatenate([vb[2 * p], zv], axis=1),
                              jnp.concatenate([zv, vb[2 * p + 1]], axis=1)], axis=0) for p in pairs]
    o_intra = [_dot(att[p].astype(BF16), vpair[p]) for p in pairs]
    for i, (b, h) in enumerate(heads):
        st_ref[b, h] = st[i] * jnp.concatenate([dcol[i]] * (GLA_DV // GLA_DK), axis=1) + ktv[i]
    for b in range(NB):
        z = z_ref[b]
        outs = [(_rms(o_inter[i] + o_intra[i // 2][:, (i % 2) * GLA_DV:(i % 2 + 1) * GLA_DV], og)
                 * z[:, vsl(h)]).astype(BF16) for i, (b_, h) in enumerate(heads) if b_ == b]
        o_ref[b] = jnp.concatenate(outs, axis=1)


def _gla(q, k, v, lg, z, sinit, og, L):
    b, s, _ = q.shape
    nc = s // L
    nb = GLA_GROUP if b % GLA_GROUP == 0 else 1
    qk = pl.BlockSpec((nb, L, GLA_HEADS * GLA_DK), lambda i, j: (i, j, 0))
    vv = pl.BlockSpec((nb, L, GLA_HEADS * GLA_DV), lambda i, j: (i, j, 0))
    st = pl.BlockSpec((nb, GLA_HEADS, GLA_DK, GLA_DV), lambda i, j: (i, 0, 0, 0))
    has_state = sinit is not None
    return pl.pallas_call(
        functools.partial(_gla_body, L=L, NB=nb, has_state=has_state),
        grid=(b // nb, nc),
        in_specs=[qk, qk, vv, qk, vv, _const_spec((1, GLA_DV))] + ([st] if has_state else []),
        out_specs=[vv, st],
        out_shape=[jax.ShapeDtypeStruct((b, s, GLA_HEADS * GLA_DV), BF16),
                   jax.ShapeDtypeStruct((b, GLA_HEADS, GLA_DK, GLA_DV), F32)],
        scratch_shapes=[pltpu.VMEM((nb, L, GLA_HEADS * GLA_DK), F32)],
        compiler_params=_params(("arbitrary", "arbitrary")),
        name="gla",
    )(q, k, v, lg, z, og, *((sinit,) if has_state else ()))


def _post_body(*refs, n_o):
    x_ref = refs[0]
    o_refs = refs[1:1 + n_o]
    wout_ref, mg_ref, wmq_ref, mqg_ref, mk_ref, mv_ref, wmo_ref, y_ref = refs[1 + n_o:]
    tm = x_ref.shape[1]
    ts = tm // POST_SPLIT if tm % (SUB_TILE_MIN * POST_SPLIT) == 0 else tm
    rows = [slice(t * ts, (t + 1) * ts) for t in range(tm // ts)]
    acc = [x_ref[0, r, :] for r in rows]
    off = 0
    for o_ref in o_refs:
        kd = o_ref.shape[-1]
        w = wout_ref[off:off + kd, :]
        acc = [a + _dot(o_ref[0, r, :], w) for a, r in zip(acc, rows)]
        off += kd
    hm = [_rms(a, mg_ref[...]).astype(BF16) for a in acc]
    qz = [_dot(h_, wmq_ref[...]) for h_ in hm]
    hw = MEM_HEADS * MEM_HD
    sls = [slice(h * MEM_HD, (h + 1) * MEM_HD) for h in range(MEM_HEADS)]
    mkb = [mk_ref[0, :, sl].astype(BF16) for sl in sls]
    mvb = [mv_ref[0, :, sl].astype(BF16) for sl in sls]
    qscale = MEM_HD ** -0.5 * LOG2_E
    qn = [[(_rms(z[:, sl], mqg_ref[...]) * qscale).astype(BF16) for sl in sls] for z in qz]
    sc = [[_dot_nt(qh, kh) for qh, kh in zip(qt, mkb)] for qt in qn]
    p = [[_softmax2_rows(x).astype(BF16) for x in st] for st in sc]
    oh = [[_dot(ph, vh) for ph, vh in zip(pt, mvb)] for pt in p]
    for t, r in enumerate(rows):
        outs = [oh[t][h] * _silu(qz[t][:, hw + h * MEM_HD:hw + (h + 1) * MEM_HD]) for h in range(MEM_HEADS)]
        oh[t] = jnp.concatenate(outs, axis=1).astype(BF16)
    ym = [_dot(om, wmo_ref[...]) for om in oh]
    for t, r in enumerate(rows):
        y_ref[0, r, :] = acc[t] + ym[t]


def _post(x, os_, wout, mg, wmq, mqg, mk, mv, wmo):
    b, s, d = x.shape
    tm = _row_tile(s, 256 * POST_SPLIT)
    hw = MEM_HEADS * MEM_HD
    nm = mk.shape[1]
    blk = lambda wd: pl.BlockSpec((1, tm, wd), lambda i, j: (i, j, 0))
    mem = pl.BlockSpec((1, nm, hw), lambda i, j: (i, 0, 0))
    kin = sum(o.shape[-1] for o in os_)
    return pl.pallas_call(
        functools.partial(_post_body, n_o=len(os_)),
        grid=(b, s // tm),
        in_specs=[blk(d)] + [blk(o.shape[-1]) for o in os_]
        + [_const_spec((kin, d)), _const_spec((1, d)), _const_spec((d, 2 * hw)), _const_spec((1, MEM_HD)),
           mem, mem, _const_spec((hw, d))],
        out_specs=blk(d),
        out_shape=jax.ShapeDtypeStruct((b, s, d), F32),
        compiler_params=_params(("arbitrary", "arbitrary")),
        name="post",
    )(x, *os_, wout, mg, wmq, mqg, mk, mv, wmo)


def _pad_cols(w, n):
    return jnp.pad(w, ((0, 0), (0, n - w.shape[1])))


def _prep_l0(w_in, conv_w, a_log, dt_bias, q_g, k_g):
    c0 = 3 * 1024
    ab = w_in[:, c0:c0 + 2 * GDN_HEADS]
    c1 = c0 + 2 * GDN_HEADS
    cols = lambda i: w_in[:, c1 + i * 1024:c1 + (i + 1) * 1024].astype(BF16)
    w = dict(
        wu=w_in[:, :c0].astype(BF16), wab=_pad_cols(ab, LANES).astype(BF16),
        wza=cols(0), wq=cols(1), wk=cols(2), wv=cols(3), wzb=cols(4),
        alog=jnp.pad(a_log, (0, LANES - GDN_HEADS)).reshape(1, LANES).astype(F32),
        dtb=jnp.pad(dt_bias, (0, LANES - GDN_HEADS)).reshape(1, LANES).astype(F32),
    )
    ns = 3 * GDN_HEADS
    cw = jnp.pad(conv_w.astype(F32), ((0, 8 - CONV_TAPS), (0, 0))).reshape(8, ns, LANES).transpose(1, 0, 2)
    dm = BAND_HEADS * BAND_HD
    qg = jnp.tile(q_g.astype(F32), BAND_HEADS).reshape(1, dm)
    kg = jnp.tile(k_g.astype(F32), BAND_HEADS).reshape(1, dm)
    return w, cw, qg, kg


def _band_bias_table(rel_bias, L, chunk):
    W = BAND_PAST + L
    n = np.arange(W + L - 1)
    idx = np.clip(BAND_PAST + (L - 1) - n, -BAND_MAX_REL, BAND_MAX_REL) + BAND_MAX_REL
    strip = rel_bias.astype(F32)[:, idx]
    bias = jnp.stack([strip[:, L - 1 - a:L - 1 - a + W] for a in range(L)], axis=1)
    back = np.arange(L)[:, None] // chunk - (np.arange(W)[None, :] - BAND_PAST) // chunk
    readable = (back >= 0) & (back <= BAND_PAST // chunk)
    return bias if readable.all() else jnp.where(jnp.asarray(readable)[None], bias, -jnp.inf)


def _prep_l1(w_in, w_gate_up, gate_bias):
    qk = GLA_HEADS * GLA_DK
    vw = GLA_HEADS * GLA_DV
    o = np.cumsum([0, qk, qk, vw, GLA_RANK, vw])
    return dict(
        wq=w_in[:, o[0]:o[1]].astype(BF16), wk=w_in[:, o[1]:o[2]].astype(BF16),
        wv=w_in[:, o[2]:o[3]].astype(BF16), wlr=_pad_cols(w_in[:, o[3]:o[4]], LANES).astype(BF16),
        wz=w_in[:, o[4]:o[5]].astype(BF16),
        wg=jnp.pad(w_gate_up, ((0, LANES - GLA_RANK), (0, 0))).astype(BF16),
        gbias=gate_bias.reshape(1, qk).astype(F32),
    )


def _layer0(x, gdn_states, band_caches, mk, mv, norm_g, pw, cw, qg, kg, bias,
            a_onorm_g, w_out, mnorm_g, w_mq, mq_g, w_mo, L):
    b, s, d = x.shape
    outs = _proj0(x.reshape(b * s, d), norm_g.reshape(1, d), pw, qg, kg, s)
    u, gb, za, qlo, qhi, kn, klast, vlo, vhi, vlast, zb = (t.reshape((b, -1) + t.shape[1:]) for t in outs)
    if gdn_states is not None:
        conv_state, gdn_state = gdn_states
        gdn_states = (jnp.pad(conv_state.astype(F32), ((0, 0), (8 - (CONV_TAPS - 1), 0), (0, 0))),
                      gdn_state.astype(F32))
    o_a, s_new = _gdn(u, gb, za, gdn_states, cw, a_onorm_g.reshape(1, LANES), L)
    o_b = _band(qlo, qhi, kn, vlo, vhi, zb, bias, band_caches, bias.shape[1])
    y = _post(x, (o_a, o_b), w_out, mnorm_g.reshape(1, d), w_mq, mq_g.reshape(1, MEM_HD), mk, mv, w_mo)
    return y, u[:, s - (CONV_TAPS - 1):, :], s_new, klast, vlast


def _layer1(x, gla_state, mk, mv, norm_g, pw, c_onorm_g, w_out, mnorm_g, w_mq, mq_g, w_mo, L):
    b, s, d = x.shape
    q, k, v, lg, z = _proj1(x.reshape(b * s, d), norm_g.reshape(1, d), pw)
    r3 = lambda t: t.reshape(b, s, t.shape[-1])
    q, k, v, lg, z = map(r3, (q, k, v, lg, z))
    o, st = _gla(q, k, v, lg, z, gla_state, c_onorm_g.reshape(1, GLA_DV), L)
    y = _post(x, (o,), w_out, mnorm_g.reshape(1, d), w_mq, mq_g.reshape(1, MEM_HD), mk, mv, w_mo)
    return y, st


def kernel(x_prompt, x_sample, mem_prompt, state_l0_gdn_conv, state_l0_gdn, cache_l0_band_k, cache_l0_band_v, cache_l0_mem_k, cache_l0_mem_v, state_l1_gla, cache_l1_mem_k, cache_l1_mem_v, l0_norm_g, l0_w_in, l0_conv_w, l0_a_log, l0_dt_bias, l0_a_onorm_g, l0_b_q_g, l0_b_k_g, l0_b_rel_bias, l0_w_out, l0_mnorm_g, l0_mem_norm_g, l0_w_mkv, l0_mk_g, l0_w_mq, l0_mq_g, l0_w_mo, l1_norm_g, l1_w_in, l1_w_gate_up, l1_gate_bias, l1_c_onorm_g, l1_w_out, l1_mnorm_g, l1_mem_norm_g, l1_w_mkv, l1_mk_g, l1_w_mq, l1_mq_g, l1_w_mo):
    bp, sp, d = x_prompt.shape
    bs, ss, _ = x_sample.shape
    nm = mem_prompt.shape[1]
    hw = MEM_HEADS * MEM_HD
    dm = BAND_HEADS * BAND_HD
    assert sp % CHUNK_ == 0 and ss % INV_SUB == 0 and ss <= CHUNK_
    assert cache_l0_band_k.shape[1] == BAND_PAST

    pw0, cw, qg, kg = _prep_l0(l0_w_in, l0_conv_w, l0_a_log, l0_dt_bias, l0_b_q_g, l0_b_k_g)
    pw1 = _prep_l1(l1_w_in, l1_w_gate_up, l1_gate_bias)
    band_rows = BAND_STEP_CHUNKS * CHUNK_ if sp % (BAND_STEP_CHUNKS * CHUNK_) == 0 else CHUNK_
    bias_p = _band_bias_table(l0_b_rel_bias, band_rows, CHUNK_)
    bias_s = _band_bias_table(l0_b_rel_bias, ss, ss)
    bf = lambda w: w.astype(BF16)
    mem2 = mem_prompt.reshape(bp * nm, d)

    p_mk0, p_mv0, mk0, mv0 = _memkv(mem2, l0_mem_norm_g.reshape(1, d), bf(l0_w_mkv), l0_mk_g.reshape(1, MEM_HD))
    mk0 = mk0.reshape(bp, nm, hw)
    mv0 = mv0.reshape(bp, nm, hw)
    l0_shared = (l0_norm_g, pw0, cw, qg, kg)
    l0_tail = (l0_a_onorm_g, bf(l0_w_out), l0_mnorm_g, bf(l0_w_mq), l0_mq_g, bf(l0_w_mo))
    yp, p_conv, p_gdn, p_kn, p_v = _layer0(
        x_prompt, None, None, mk0, mv0,
        *l0_shared, bias_p, *l0_tail, CHUNK_)
    ys, s_conv, s_gdn, s_kn, s_v = _layer0(
        x_sample, (state_l0_gdn_conv, state_l0_gdn),
        (cache_l0_band_k, cache_l0_band_v),
        cache_l0_mem_k.reshape(bs, nm, hw), cache_l0_mem_v.reshape(bs, nm, hw),
        *l0_shared, bias_s, *l0_tail, ss)

    p_mk1, p_mv1, mk1, mv1 = _memkv(mem2, l1_mem_norm_g.reshape(1, d), bf(l1_w_mkv), l1_mk_g.reshape(1, MEM_HD))
    mk1 = mk1.reshape(bp, nm, hw)
    mv1 = mv1.reshape(bp, nm, hw)
    l1_tail = (l1_c_onorm_g, bf(l1_w_out), l1_mnorm_g, bf(l1_w_mq), l1_mq_g, bf(l1_w_mo))
    yp, p_gla = _layer1(yp, None, mk1, mv1, l1_norm_g, pw1, *l1_tail, CHUNK_)
    ys, s_gla = _layer1(ys, state_l1_gla.astype(F32), cache_l1_mem_k.reshape(bs, nm, hw),
                        cache_l1_mem_v.reshape(bs, nm, hw), l1_norm_g, pw1, *l1_tail, ss)

    m4 = lambda t: t.reshape(bp, nm, MEM_HEADS, MEM_HD)
    return (yp, ys, p_conv, p_gdn, p_kn, p_v,
            m4(p_mk0), m4(p_mv0), p_gla, m4(p_mk1), m4(p_mv1),
            s_conv, s_gdn, s_kn, s_v, s_gla)
```

```python
import functools

import jax
import jax.numpy as jnp
import numpy as np
from jax import lax
from jax.experimental import pallas as pl
from jax.experimental.pallas import tpu as pltpu

F32 = jnp.float32
BF16 = jnp.bfloat16
NORM_EPS = 1e-6
LOG2_E = 1.4426950408889634

CHUNK_ = 64
CONV_TAPS = 4
GDN_HEADS = 8
GDN_DK = 128
BAND_HEADS = 16
BAND_HD = 64
BAND_PAST = 512
BAND_MAX_REL = 128
GLA_HEADS = 8
GLA_DK = 128
GLA_DV = 256
GLA_RANK = 16
GLA_TAU = 16.0
MEM_HEADS = 4
MEM_HD = 128
INV_SUB = 16
GDN_GROUP = 2
GDN_STEP_CHUNKS = 1
GLA_GROUP = 4
POST_SPLIT = 4
PROJ_SPLIT = 2
SUB_TILE_MIN = 128
BAND_STEP_CHUNKS = 2
LANES = 128
VMEM_LIMIT = 56 * 1024 * 1024


def _dot(a, b):
    return jnp.dot(a, b, preferred_element_type=F32)


def _dot_nt(a, b):
    return lax.dot_general(a, b, (((1,), (1,)), ((), ())), preferred_element_type=F32)


def _dot_tn(a, b):
    return lax.dot_general(a, b, (((0,), (0,)), ((), ())), preferred_element_type=F32)


def _split3(x):
    hi = x.astype(BF16)
    r = x - hi.astype(F32)
    mid = r.astype(BF16)
    lo = (r - mid.astype(F32)).astype(BF16)
    return hi, mid, lo


def _dot_exact_lhs(a_bf, b):
    h, m, l = _split3(b)
    return _dot(a_bf, h) + (_dot(a_bf, m) + _dot(a_bf, l))


def _rms(x, g):
    ms = jnp.mean(x * x, axis=-1, keepdims=True)
    return x * lax.rsqrt(ms + NORM_EPS) * g


def _silu(x):
    return x * jax.nn.sigmoid(x)


def _log1p_exp_neg_abs(x):
    return jnp.log(1.0 + jnp.exp2(jnp.abs(x) * -LOG2_E))


def _softplus(x):
    return jnp.maximum(x, 0.0) + _log1p_exp_neg_abs(x)


def _log_sigmoid(x):
    return jnp.minimum(x, 0.0) - _log1p_exp_neg_abs(x)


def _softmax2_rows(s):
    m = jnp.max(s, axis=-1, keepdims=True)
    e = jnp.exp2(s - m)
    return e * (1.0 / jnp.sum(e, axis=-1, keepdims=True))


def _const_spec(shape):
    nd = len(shape)
    return pl.BlockSpec(shape, lambda *_: (0,) * nd, pipeline_mode=pl.Buffered(1))


def _params(sem):
    return pltpu.CompilerParams(dimension_semantics=sem, vmem_limit_bytes=VMEM_LIMIT)


def _row_tile(n, want):
    t = min(n, want)
    assert n % t == 0
    return t


def _sub_tiles(tm):
    ts = tm // PROJ_SPLIT if tm % (SUB_TILE_MIN * PROJ_SPLIT) == 0 else tm
    return [slice(t * ts, (t + 1) * ts) for t in range(tm // ts)]


def _proj0_body(x_ref, g_ref, wu_ref, wab_ref, wza_ref, wq_ref, wk_ref, wv_ref, wzb_ref,
                alog_ref, dtb_ref, qg_ref, kg_ref,
                u_ref, gb_ref, za_ref, qlo_ref, qhi_ref, kn_ref, klast_ref, vlo_ref, vhi_ref, vlast_ref, zb_ref,
                *, kept_tiles):
    h = _rms(x_ref[...], g_ref[...]).astype(BF16)
    dm = BAND_HEADS * BAND_HD
    lo = (lax.broadcasted_iota(jnp.int32, (1, dm), 1) & (LANES - 1)) < BAND_HD
    q = _dot(h, wq_ref[...])
    k = _dot(h, wk_ref[...])
    ab = _dot(h, wab_ref[...])
    za = _dot(h, wza_ref[...])

    zb = _dot(h, wzb_ref[...])
    v = _dot(h, wv_ref[...])
    u_ref[...] = _dot(h, wu_ref[...])
    za_ref[...] = _silu(za)
    zb_ref[...] = _silu(zb)
    lo1 = lo[:, :LANES]

    def head_rsqrt(x):
        out = []
        for s in range(BAND_HEADS // 2):
            x2 = x[:, s * LANES:(s + 1) * LANES]
            x2 = x2 * x2
            first = jnp.sum(jnp.where(lo1, x2, 0.0), axis=-1, keepdims=True)
            second = jnp.sum(jnp.where(lo1, 0.0, x2), axis=-1, keepdims=True)
            r = lambda t: lax.rsqrt(t * (1.0 / BAND_HD) + NORM_EPS)
            out.append(jnp.where(lo1, r(first), r(second)))
        return jnp.concatenate(out, axis=1)

    qn = q * head_rsqrt(q) * (qg_ref[...] * (BAND_HD ** -0.5 * LOG2_E))
    kn = k * head_rsqrt(k) * kg_ref[...]
    qlo_ref[...] = jnp.where(lo, qn, 0.0).astype(BF16)
    qhi_ref[...] = jnp.where(lo, 0.0, qn).astype(BF16)
    kn_ref[...] = kn.astype(BF16)
    vlo_ref[...] = jnp.where(lo, v, 0.0).astype(BF16)
    vhi_ref[...] = jnp.where(lo, 0.0, v).astype(BF16)

    def hand_on():
        klast_ref[...] = kn.reshape(kn.shape[0], BAND_HEADS, BAND_HD)
        vlast_ref[...] = v.reshape(v.shape[0], BAND_HEADS, BAND_HD)

    if kept_tiles is None:
        hand_on()
    else:
        tpb, kt = kept_tiles
        pl.when(pl.program_id(0) % tpb >= tpb - kt)(hand_on)
    lane = lax.broadcasted_iota(jnp.int32, ab.shape, 1)
    gval = -jnp.exp(alog_ref[...]) * _softplus(ab + dtb_ref[...])
    gb_ref[...] = jnp.where(lane < GDN_HEADS, gval, jax.nn.sigmoid(ab))


def _proj0(x2, g, w, qg, kg, rows_per_batch):
    n, d = x2.shape
    tm = _row_tile(n, 256)
    dm = BAND_HEADS * BAND_HD
    widths = (3 * 1024, LANES, 1024, dm, dm, dm, dm)
    row = lambda wd: pl.BlockSpec((tm, wd), lambda i: (i, 0))
    keep = min(BAND_PAST, rows_per_batch)
    if rows_per_batch > keep:
        assert rows_per_batch % tm == 0 and keep % tm == 0
        tpb, kt = rows_per_batch // tm, keep // tm
        last = pl.BlockSpec((tm, BAND_HEADS, BAND_HD),
                            lambda i: ((i // tpb) * kt + jnp.maximum(i % tpb - (tpb - kt), 0), 0, 0))
        n_last = (n // rows_per_batch) * keep
        kept_tiles = (tpb, kt)
    else:
        last, n_last = pl.BlockSpec((tm, BAND_HEADS, BAND_HD), lambda i: (i, 0, 0)), n
        kept_tiles = None
    last_shape = jax.ShapeDtypeStruct((n_last, BAND_HEADS, BAND_HD), F32)
    f32 = lambda rows, wd: jax.ShapeDtypeStruct((rows, wd), F32)
    bf16 = lambda wd: jax.ShapeDtypeStruct((n, wd), BF16)
    return pl.pallas_call(
        functools.partial(_proj0_body, kept_tiles=kept_tiles),
        grid=(n // tm,),
        in_specs=[row(d), _const_spec((1, d))]
        + [_const_spec((d, wd)) for wd in widths]
        + [_const_spec((1, LANES)), _const_spec((1, LANES)), _const_spec((1, dm)), _const_spec((1, dm))],
        out_specs=[row(3 * 1024), row(LANES), row(1024), row(dm), row(dm), row(dm), last, row(dm), row(dm), last,
                   row(dm)],
        out_shape=[f32(n, 3 * 1024), f32(n, LANES), f32(n, 1024), bf16(dm), bf16(dm), bf16(dm), last_shape,
                   bf16(dm), bf16(dm), last_shape, f32(n, dm)],
        compiler_params=_params(("arbitrary",)),
        name="proj0",
    )(x2, g, w["wu"], w["wab"], w["wza"], w["wq"], w["wk"], w["wv"], w["wzb"], w["alog"], w["dtb"],
      qg, kg)


def _proj1_body(x_ref, g_ref, wq_ref, wk_ref, wv_ref, wlr_ref, wz_ref, wg_ref, gbias_ref,
                q_ref, k_ref, v_ref, lg_ref, z_ref):
    rows = _sub_tiles(x_ref.shape[0])
    h = [_rms(x_ref[r, :], g_ref[...]).astype(BF16) for r in rows]
    lr = [_dot(h_, wlr_ref[...]) for h_ in h]
    for r, h_ in zip(rows, h):
        q_ref[r, :] = _dot(h_, wq_ref[...])
    for r, x in zip(rows, lr):
        lg_ref[r, :] = _dot(x.astype(BF16), wg_ref[...])
    for r, h_ in zip(rows, h):
        z_ref[r, :] = _dot(h_, wz_ref[...])
    for r, h_ in zip(rows, h):
        k_ref[r, :] = _dot(h_, wk_ref[...])
    for r in rows:
        lg_ref[r, :] = _log_sigmoid(lg_ref[r, :] + gbias_ref[...]) * (1.0 / GLA_TAU)
    for r, h_ in zip(rows, h):
        v_ref[r, :] = _dot(h_, wv_ref[...])
    for r in rows:
        z_ref[r, :] = _silu(z_ref[r, :])


def _proj1(x2, g, w):
    n, d = x2.shape
    tm = _row_tile(n, 512)
    row = lambda wd: pl.BlockSpec((tm, wd), lambda i: (i, 0))
    outw = (1024, 1024, 2048, 1024, 2048)
    return pl.pallas_call(
        _proj1_body,
        grid=(n // tm,),
        in_specs=[row(d), _const_spec((1, d)), _const_spec((d, 1024)), _const_spec((d, 1024)),
                  _const_spec((d, 2048)), _const_spec((d, LANES)), _const_spec((d, 2048)),
                  _const_spec((LANES, 1024)), _const_spec((1, 1024))],
        out_specs=[row(wd) for wd in outw],
        out_shape=[jax.ShapeDtypeStruct((n, wd), F32) for wd in outw],
        compiler_params=_params(("arbitrary",)),
        name="proj1",
    )(x2, g, w["wq"], w["wk"], w["wv"], w["wlr"], w["wz"], w["wg"], w["gbias"])


def _memkv_body(m_ref, g_ref, w_ref, kg_ref, k4_ref, v4_ref, kb_ref, vb_ref):
    h = _rms(m_ref[...], g_ref[...]).astype(BF16)
    kv = _dot(h, w_ref[...])
    hw = MEM_HEADS * MEM_HD
    k = jnp.concatenate([_rms(kv[:, hh * MEM_HD:(hh + 1) * MEM_HD], kg_ref[...]) for hh in range(MEM_HEADS)],
                        axis=1)
    v = kv[:, hw:]
    k4_ref[...] = k.reshape(k.shape[0], MEM_HEADS, MEM_HD)
    v4_ref[...] = v.reshape(v.shape[0], MEM_HEADS, MEM_HD)
    kb_ref[...] = k.astype(BF16)
    vb_ref[...] = v.astype(BF16)


def _memkv(m2, g, w_bf, kg):
    n, d = m2.shape
    tm = _row_tile(n, 256)
    hw = MEM_HEADS * MEM_HD
    row = lambda wd: pl.BlockSpec((tm, wd), lambda i: (i, 0))
    row4 = pl.BlockSpec((tm, MEM_HEADS, MEM_HD), lambda i: (i, 0, 0))
    return pl.pallas_call(
        _memkv_body,
        grid=(n // tm,),
        in_specs=[row(d), _const_spec((1, d)), _const_spec((d, 2 * hw)), _const_spec((1, MEM_HD))],
        out_specs=[row4, row4, row(hw), row(hw)],
        out_shape=[jax.ShapeDtypeStruct((n, MEM_HEADS, MEM_HD), F32)] * 2
        + [jax.ShapeDtypeStruct((n, hw), BF16)] * 2,
        compiler_params=_params(("arbitrary",)),
        name="memkv",
    )(m2, g, w_bf, kg)


def _each(fn, *lists):
    return [fn(*xs) for xs in zip(*lists)]


def _unit_lower_inverse(a, eye, bd, mm):
    d = _each(lambda x: jnp.where(bd, x, 0.0), a)
    nl = _each(lambda x, y: x - y, a, d)
    d2 = _each(mm, d, d)
    d4 = _each(mm, d2, d2)
    td = _each(lambda x, y: mm(eye - x, eye + y), d, d2)
    d8 = _each(mm, d4, d4)
    td = _each(lambda x, y: mm(x, eye + y), td, d4)
    td = _each(lambda x, y: mm(x, eye + y), td, d8)
    m = _each(mm, td, nl)
    m2 = _each(mm, m, m)
    mt = _each(mm, m, td)
    return _each(lambda x, y, z: mm(eye + x, y - z), m2, td, mt)


def _gdn_body(*refs, L, NB, G, has_state):
    if has_state:
        u_ref, gb_ref, za_ref, cw_ref, og_ref, cinit_ref, sinit_ref, o_ref, s_ref, ubuf = refs
    else:
        u_ref, gb_ref, za_ref, cw_ref, og_ref, o_ref, s_ref, ubuf = refs
    c = pl.program_id(1)
    ns = 3 * GDN_HEADS
    hist = 8
    R = G * L

    @pl.when(c == 0)
    def _():
        if has_state:
            for b in range(NB):
                for j in range(ns):
                    ubuf[b * ns + j, 0:hist, :] = cinit_ref[b, :, j * LANES:(j + 1) * LANES]
            s_ref[...] = sinit_ref[...]
        else:
            ubuf[:, 0:hist, :] = jnp.zeros((NB * ns, hist, LANES), F32)
            s_ref[...] = jnp.zeros(s_ref.shape, F32)

    for b in range(NB):
        for j in range(ns):
            ubuf[b * ns + j, hist:hist + R, :] = u_ref[b, :, j * LANES:(j + 1) * LANES]
    base = hist - (CONV_TAPS - 1)
    ys = []
    for b in range(NB):
        bsl = slice(b * ns, (b + 1) * ns)
        yb = ubuf[bsl, base:base + R, :] * cw_ref[:, 0:1, :]
        for i in range(1, CONV_TAPS):
            yb = yb + ubuf[bsl, base + i:base + i + R, :] * cw_ref[:, i:i + 1, :]
        ys.append(_silu(yb))
    ubuf[:, base:hist, :] = ubuf[:, base + R:hist + R, :]

    row = lax.broadcasted_iota(jnp.int32, (L, 2 * L), 0)
    lane = lax.broadcasted_iota(jnp.int32, (L, 2 * L), 1)
    col = lane & (L - 1)
    left = lane < L
    incl = row >= col
    strict = row > col
    sub_shift = INV_SUB.bit_length() - 1
    bd = lax.shift_right_logical(row, sub_shift) == lax.shift_right_logical(col, sub_shift)
    eye = jnp.where(row == col, 1.0, 0.0).astype(F32)
    trow = lax.broadcasted_iota(jnp.int32, (L, L), 0)
    tcol = lax.broadcasted_iota(jnp.int32, (L, L), 1)
    tri = jnp.where(trow >= tcol, 1.0, 0.0).astype(BF16)
    og = og_ref[...]
    heads = [(b, g, h) for b in range(NB) for g in range(G) for h in range(GDN_HEADS)]
    pairs = range(len(heads) // 2)
    rows_of = lambda g: slice(g * L, (g + 1) * L)
    l2n = lambda x: x * lax.rsqrt(jnp.sum(x * x, axis=-1, keepdims=True) + NORM_EPS)
    q = [l2n(ys[b][h][rows_of(g)]) * (GDN_DK ** -0.5) for b, g, h in heads]
    k = [l2n(ys[b][GDN_HEADS + h][rows_of(g)]) for b, g, h in heads]
    v = [ys[b][2 * GDN_HEADS + h][rows_of(g)] for b, g, h in heads]
    gc, gr, bc = [], [], []
    for b in range(NB):
        for g in range(G):
            gbv = gb_ref[b, rows_of(g), :]
            gcum = _dot_exact_lhs(tri, gbv)
            gpad = jnp.concatenate([gcum, jnp.zeros((LANES - L, LANES), F32)], axis=0)
            gt = gpad.T
            for h in range(GDN_HEADS):
                gc.append(gcum[:, h:h + 1])
                gr.append(gt[h:h + 1, 0:L])
                bc.append(gbv[:, GDN_HEADS + h:GDN_HEADS + h + 1])
    side = lambda x0, x1: jnp.where(left, x0, x1)
    gcp = [side(gc[2 * p], gc[2 * p + 1]) for p in pairs]
    grp = [jnp.concatenate([gr[2 * p], gr[2 * p + 1]], axis=1) for p in pairs]
    bcp = [side(bc[2 * p], bc[2 * p + 1]) for p in pairs]
    dec = _each(lambda c_, r_: jnp.where(incl, jnp.exp(jnp.where(incl, c_ - r_, 0.0)), 0.0), gcp, grp)
    kb = _each(lambda x: x.astype(BF16), k)
    qb = _each(lambda x: x.astype(BF16), q)

    def blockdiag(y0, y1):
        z0 = jnp.zeros(y1.shape, y1.dtype)
        z1 = jnp.zeros(y0.shape, y0.dtype)
        return jnp.concatenate([jnp.concatenate([y0, z0], axis=1), jnp.concatenate([z1, y1], axis=1)], axis=0)

    def mm(x, y):
        yb = y.astype(BF16)
        zero = jnp.zeros_like(yb)
        return _dot(x.astype(BF16), jnp.concatenate([jnp.where(left, yb, zero), jnp.where(left, zero, yb)], axis=0))

    kq = [_dot_nt(jnp.concatenate([jnp.concatenate([kb[2 * p], qb[2 * p]], axis=0),
                                   jnp.concatenate([kb[2 * p + 1], qb[2 * p + 1]], axis=0)], axis=1),
                  blockdiag(kb[2 * p], kb[2 * p + 1])) for p in pairs]
    a = _each(lambda b_, x, d_: jnp.where(strict, b_ * x[:L] * d_, 0.0), bcp, kq, dec)
    t = _unit_lower_inverse(a, eye, bd, mm)
    eg = _each(jnp.exp, gc)
    rhs = _each(lambda b_, v_, e_, k_: jnp.concatenate([b_ * v_, (b_ * e_) * k_], axis=1).astype(BF16),
                bc, v, eg, k)
    sol = [_dot(t[p].astype(BF16), blockdiag(rhs[2 * p], rhs[2 * p + 1])) for p in pairs]
    solk = lambda i: sol[i // 2][:, 2 * (i % 2) * LANES + LANES:2 * (i % 2 + 1) * LANES]
    solv = lambda i: sol[i // 2][:, 2 * (i % 2) * LANES:2 * (i % 2) * LANES + LANES]
    qkd = [(kq[p][L:] * dec[p]).astype(BF16) for p in pairs]
    gl = [c_[L - 1:L, :] for c_ in gc]
    kd = _each(lambda k_, l_, c_: (k_ * jnp.exp(l_ - c_)).astype(BF16), k, gl, gc)

    item = lambda b, g, h: (b * G + g) * GDN_HEADS + h
    half = lambda x, i: x[:, (i % 2) * LANES:(i % 2 + 1) * LANES]
    s = {(b, h): s_ref[b, h] for b in range(NB) for h in range(GDN_HEADS)}
    for g in range(G):
        ids = [item(b, g, h) for b in range(NB) for h in range(GDN_HEADS)]
        sb = {i: s[(heads[i][0], heads[i][2])].astype(BF16) for i in ids}
        ksq = {i: _dot(jnp.concatenate([jnp.concatenate([solk(i).astype(BF16), qb[i]], axis=0),
                                        jnp.concatenate([solk(i + 1).astype(BF16), qb[i + 1]], axis=0)], axis=1),
                       blockdiag(sb[i], sb[i + 1])) for i in ids[::2]}
        ub = {i: (solv(i) - half(ksq[i - i % 2][:L], i)).astype(BF16) for i in ids}
        qku = {i: _dot(qkd[i // 2], blockdiag(ub[i], ub[i + 1])) for i in ids[::2]}
        ktu = {i: _dot_tn(kd[i], ub[i]) for i in ids}
        for i in ids:
            b, _, h = heads[i]
            s[(b, h)] = s[(b, h)] * jnp.exp(gl[i]) + ktu[i]
        for b in range(NB):
            za = za_ref[b, rows_of(g), :]
            outs = [(_rms(half(ksq[i - i % 2][L:], i) * eg[i] + half(qku[i - i % 2], i), og)
                     * za[:, heads[i][2] * LANES:(heads[i][2] + 1) * LANES]).astype(BF16)
                    for i in ids if heads[i][0] == b]
            o_ref[b, rows_of(g), :] = jnp.concatenate(outs, axis=1)
    for (b, h), val in s.items():
        s_ref[b, h] = val


def _gdn(u, gb, za, state, cw, og, L):
    b, s, _ = u.shape
    nc = s // L
    ns = 3 * GDN_HEADS
    nb = GDN_GROUP if b % GDN_GROUP == 0 else 1
    g = GDN_STEP_CHUNKS if nc % GDN_STEP_CHUNKS == 0 else 1
    blk = lambda wd: pl.BlockSpec((nb, g * L, wd), lambda i, j: (i, j, 0))
    has_state = state is not None
    state_specs = [pl.BlockSpec((nb, 8, ns * LANES), lambda i, j: (i, 0, 0)),
                   pl.BlockSpec((nb, GDN_HEADS, GDN_DK, LANES), lambda i, j: (i, 0, 0, 0))]
    return pl.pallas_call(
        functools.partial(_gdn_body, L=L, NB=nb, G=g, has_state=has_state),
        grid=(b // nb, nc // g),
        in_specs=[blk(ns * LANES), blk(LANES), blk(GDN_HEADS * LANES),
                  _const_spec((ns, 8, LANES)), _const_spec((1, LANES))] + (state_specs if has_state else []),
        out_specs=[blk(GDN_HEADS * LANES),
                   pl.BlockSpec((nb, GDN_HEADS, GDN_DK, LANES), lambda i, j: (i, 0, 0, 0))],
        out_shape=[jax.ShapeDtypeStruct((b, s, GDN_HEADS * LANES), BF16),
                   jax.ShapeDtypeStruct((b, GDN_HEADS, GDN_DK, LANES), F32)],
        scratch_shapes=[pltpu.VMEM((nb * ns, 8 + g * L, LANES), F32)],
        compiler_params=_params(("arbitrary", "arbitrary")),
        name="gdn",
    )(u, gb, za, cw, og, *(state if has_state else ()))


def _band_body(*refs, L, W, has_cache):
    if has_cache:
        (qlo_ref, qhi_ref, k_ref, vlo_ref, vhi_ref, zb_ref, bias_ref, ck_ref, cv_ref,
         o_ref, kscr, vlo, vhi, biasm) = refs
    else:
        (qlo_ref, qhi_ref, k_ref, vlo_ref, vhi_ref, zb_ref, bias_ref,
         o_ref, kscr, vlo, vhi, biasm) = refs
    c = pl.program_id(1)
    dm = BAND_HEADS * BAND_HD
    pad_chunks = BAND_PAST // L

    @pl.when(c == 0)
    def _():
        if has_cache:
            lo = (lax.broadcasted_iota(jnp.int32, (1, dm), 1) & (LANES - 1)) < BAND_HD
            kscr[0:BAND_PAST, :] = ck_ref[0].reshape(BAND_PAST, dm).astype(BF16)
            cv = cv_ref[0].reshape(BAND_PAST, dm)
            vlo[0:BAND_PAST, :] = jnp.where(lo, cv, 0.0).astype(BF16)
            vhi[0:BAND_PAST, :] = jnp.where(lo, 0.0, cv).astype(BF16)
        else:
            zero = jnp.zeros((BAND_PAST, dm), BF16)
            kscr[0:BAND_PAST, :] = zero
            vlo[0:BAND_PAST, :] = zero
            vhi[0:BAND_PAST, :] = zero

    new0 = pl.multiple_of(BAND_PAST + c * L, L)
    kscr[pl.ds(new0, L), :] = k_ref[0]
    vlo[pl.ds(new0, L), :] = vlo_ref[0]
    vhi[pl.ds(new0, L), :] = vhi_ref[0]

    if has_cache:
        @pl.when(c == 0)
        def _():
            biasm[...] = bias_ref[...] * LOG2_E
    else:
        @pl.when(c < pad_chunks)
        def _():
            wcol = lax.broadcasted_iota(jnp.int32, (1, W), 1)
            valid = (wcol + c * L) >= BAND_PAST
            for h in range(BAND_HEADS):
                biasm[h] = jnp.where(valid, bias_ref[h] * LOG2_E, -jnp.inf)

        @pl.when(c == pad_chunks)
        def _():
            biasm[...] = bias_ref[...] * LOG2_E
    bias_src = biasm

    w0 = pl.multiple_of(c * L, L)
    zb = zb_ref[0]
    slabs = [slice(s * LANES, (s + 1) * LANES) for s in range(BAND_HEADS // 2)]
    sc = []
    for sl in slabs:
        ks = kscr[pl.ds(w0, W), sl]
        sc.append(_dot_nt(qlo_ref[0, :, sl], ks))
        sc.append(_dot_nt(qhi_ref[0, :, sl], ks))
    ps, rs = [], []
    for h in range(BAND_HEADS):
        x = sc[h] + bias_src[h]
        e = jnp.exp2(x - jnp.max(x, axis=-1, keepdims=True))
        rs.append(1.0 / jnp.sum(e, axis=-1, keepdims=True))
        ps.append(e.astype(BF16))
    pv = []
    for i, sl in enumerate(slabs):
        pv.append(_dot(ps[2 * i], vlo[pl.ds(w0, W), sl]))
        pv.append(_dot(ps[2 * i + 1], vhi[pl.ds(w0, W), sl]))
    outs = [((pv[2 * i] * rs[2 * i] + pv[2 * i + 1] * rs[2 * i + 1]) * zb[:, sl]).astype(BF16)
            for i, sl in enumerate(slabs)]
    o_ref[0] = jnp.concatenate(outs, axis=1)


def _band(qlo, qhi, kn, vlo, vhi, zb, bias, caches, L):
    b, s, dm = qlo.shape
    nc = s // L
    W = BAND_PAST + L
    blk = pl.BlockSpec((1, L, dm), lambda i, j: (i, j, 0))
    cache = pl.BlockSpec((1, BAND_PAST, BAND_HEADS, BAND_HD), lambda i, j: (i, 0, 0, 0))
    has_cache = caches is not None
    scratch = [pltpu.VMEM((BAND_PAST + s, dm), BF16)] * 3 + [pltpu.VMEM((BAND_HEADS, L, W), F32)]
    return pl.pallas_call(
        functools.partial(_band_body, L=L, W=W, has_cache=has_cache),
        grid=(b, nc),
        in_specs=[blk] * 6 + [_const_spec((BAND_HEADS, L, W))] + ([cache, cache] if has_cache else []),
        out_specs=blk,
        out_shape=jax.ShapeDtypeStruct((b, s, dm), BF16),
        scratch_shapes=scratch,
        compiler_params=_params(("arbitrary", "arbitrary")),
        name="band",
    )(qlo, qhi, kn, vlo, vhi, zb, bias, *(caches if has_cache else ()))


def _gla_body(*refs, L, NB, has_state):
    if has_state:
        q_ref, k_ref, v_ref, lg_ref, z_ref, og_ref, sinit_ref, o_ref, st_ref, cbs = refs
    else:
        q_ref, k_ref, v_ref, lg_ref, z_ref, og_ref, o_ref, st_ref, cbs = refs
    c = pl.program_id(1)

    @pl.when(c == 0)
    def _():
        st_ref[...] = sinit_ref[...] if has_state else jnp.zeros(st_ref.shape, F32)

    levels = [L >> (t + 1) for t in range(L.bit_length() - 1)]
    small = [s for s in levels if 2 < 2 * s < 16]
    row = lax.broadcasted_iota(jnp.int32, (L, L), 0)
    col = lax.broadcasted_iota(jnp.int32, (L, L), 1)
    anchor = lambda s: lax.shift_left(lax.shift_right_logical(row, s.bit_length()), s.bit_length()) + (s - 1)
    between = lambda s: (col > jnp.minimum(row, anchor(s))) & (col <= jnp.maximum(row, anchor(s)))
    onehot = lambda m: jnp.where(m, 1.0, 0.0).astype(BF16)
    tri = onehot(row >= col)
    stack = jnp.concatenate([onehot(between(s)) for s in small], axis=0)
    ex = jnp.exp2
    og = og_ref[...]
    scale = GLA_DK ** -0.5
    lg2 = [lg_ref[b] * LOG2_E for b in range(NB)]
    pieces = [_split3(x) for x in lg2]
    cb_all = [_dot(tri, p[0]) + (_dot(tri, p[1]) + _dot(tri, p[2])) for p in pieces]
    y_small = [_dot(stack, p[0]) + _dot(stack, p[1]) for p in pieces]
    odd_row = (lax.broadcasted_iota(jnp.int32, (L, 1), 0) & 1) == 1
    ydec = [dict() for _ in range(NB)]
    for b in range(NB):
        cbs[b] = cb_all[b]
        ydec[b][1] = jnp.where(odd_row, lg2[b], 0.0)
        for t, s in enumerate(small):
            ydec[b][s] = y_small[b][t * L:(t + 1) * L]
        for s in levels:
            if s not in ydec[b]:
                anc = jnp.concatenate(
                    [jnp.broadcast_to(cbs[b, p * 2 * s + s - 1:p * 2 * s + s, :], (2 * s, cbs.shape[2]))
                     for p in range(L // (2 * s))], axis=0)
                ydec[b][s] = -jnp.abs(cb_all[b] - anc)

    heads = [(b, h) for b in range(NB) for h in range(GLA_HEADS)]
    pairs = range(len(heads) // 2)
    ksl = lambda h: slice(h * GLA_DK, (h + 1) * GLA_DK)
    vsl = lambda h: slice(h * GLA_DV, (h + 1) * GLA_DV)
    cb = [cb_all[b][:, ksl(h)] for b, h in heads]
    q = [q_ref[b, :, ksl(h)] * scale for b, h in heads]
    k = [k_ref[b, :, ksl(h)] for b, h in heads]
    vb = [v_ref[b, :, vsl(h)].astype(BF16) for b, h in heads]
    st = [st_ref[b, h] for b, h in heads]
    o_inter = _each(lambda q_, c_, s_: _dot((q_ * ex(c_)).astype(BF16), s_.astype(BF16)), q, cb, st)
    cl = [c_[L - 1:L] for c_ in cb]
    ke = _each(lambda k_, l_, c_: (k_ * ex(l_ - c_)).astype(BF16), k, cl, cb)
    ktv = _each(_dot_tn, ke, vb)
    dcol = [jnp.broadcast_to(ex(l_), (GLA_DK, GLA_DK)).T for l_ in cl]

    zk = jnp.zeros((L, GLA_DK), BF16)
    zv = jnp.zeros((L, GLA_DV), BF16)

    def side_by_side(qs, ks):
        lhs = jnp.concatenate(qs, axis=1).astype(BF16)
        rhs = jnp.concatenate([jnp.concatenate([ks[0].astype(BF16), zk], axis=1),
                               jnp.concatenate([zk, ks[1].astype(BF16)], axis=1)], axis=0)
        return _dot_nt(lhs, rhs)

    prow = lax.broadcasted_iota(jnp.int32, (L, 2 * L), 0)
    pcol = lax.broadcasted_iota(jnp.int32, (L, 2 * L), 1) & (L - 1)
    att = [jnp.where(prow == pcol, side_by_side((q[2 * p], q[2 * p + 1]), (k[2 * p], k[2 * p + 1])), 0.0)
           for p in pairs]
    for s in levels:
        sh = s.bit_length() - 1
        same_parent = lax.shift_right_logical(prow, sh + 1) == lax.shift_right_logical(pcol, sh + 1)
        take = same_parent & ((lax.shift_right_logical(prow, sh) & 1) == 1) & \
            ((lax.shift_right_logical(pcol, sh) & 1) == 0)
        f = [ex(ydec[b][s][:, ksl(h)]) for b, h in heads]
        prod = [side_by_side((q[2 * p] * f[2 * p], q[2 * p + 1] * f[2 * p + 1]),
                             (k[2 * p] * f[2 * p], k[2 * p + 1] * f[2 * p + 1])) for p in pairs]
        att = [jnp.where(take, prod[p], att[p]) for p in pairs]
    vpair = [jnp.concatenate([jnp.concatenate([vb[2 * p], zv], axis=1),
                              jnp.concatenate([zv, vb[2 * p + 1]], axis=1)], axis=0) for p in pairs]
    o_intra = [_dot(att[p].astype(BF16), vpair[p]) for p in pairs]
    for i, (b, h) in enumerate(heads):
        st_ref[b, h] = st[i] * jnp.concatenate([dcol[i]] * (GLA_DV // GLA_DK), axis=1) + ktv[i]
    for b in range(NB):
        z = z_ref[b]
        outs = [(_rms(o_inter[i] + o_intra[i // 2][:, (i % 2) * GLA_DV:(i % 2 + 1) * GLA_DV], og)
                 * z[:, vsl(h)]).astype(BF16) for i, (b_, h) in enumerate(heads) if b_ == b]
        o_ref[b] = jnp.concatenate(outs, axis=1)


def _gla(q, k, v, lg, z, sinit, og, L):
    b, s, _ = q.shape
    nc = s // L
    nb = GLA_GROUP if b % GLA_GROUP == 0 else 1
    qk = pl.BlockSpec((nb, L, GLA_HEADS * GLA_DK), lambda i, j: (i, j, 0))
    vv = pl.BlockSpec((nb, L, GLA_HEADS * GLA_DV), lambda i, j: (i, j, 0))
    st = pl.BlockSpec((nb, GLA_HEADS, GLA_DK, GLA_DV), lambda i, j: (i, 0, 0, 0))
    has_state = sinit is not None
    return pl.pallas_call(
        functools.partial(_gla_body, L=L, NB=nb, has_state=has_state),
        grid=(b // nb, nc),
        in_specs=[qk, qk, vv, qk, vv, _const_spec((1, GLA_DV))] + ([st] if has_state else []),
        out_specs=[vv, st],
        out_shape=[jax.ShapeDtypeStruct((b, s, GLA_HEADS * GLA_DV), BF16),
                   jax.ShapeDtypeStruct((b, GLA_HEADS, GLA_DK, GLA_DV), F32)],
        scratch_shapes=[pltpu.VMEM((nb, L, GLA_HEADS * GLA_DK), F32)],
        compiler_params=_params(("arbitrary", "arbitrary")),
        name="gla",
    )(q, k, v, lg, z, og, *((sinit,) if has_state else ()))


def _post_body(*refs, n_o):
    x_ref = refs[0]
    o_refs = refs[1:1 + n_o]
    wout_ref, mg_ref, wmq_ref, mqg_ref, mk_ref, mv_ref, wmo_ref, y_ref = refs[1 + n_o:]
    tm = x_ref.shape[1]
    ts = tm // POST_SPLIT if tm % (SUB_TILE_MIN * POST_SPLIT) == 0 else tm
    rows = [slice(t * ts, (t + 1) * ts) for t in range(tm // ts)]
    acc = [x_ref[0, r, :] for r in rows]
    off = 0
    for o_ref in o_refs:
        kd = o_ref.shape[-1]
        w = wout_ref[off:off + kd, :]
        acc = [a + _dot(o_ref[0, r, :], w) for a, r in zip(acc, rows)]
        off += kd
    hm = [_rms(a, mg_ref[...]).astype(BF16) for a in acc]
    qz = [_dot(h_, wmq_ref[...]) for h_ in hm]
    hw = MEM_HEADS * MEM_HD
    sls = [slice(h * MEM_HD, (h + 1) * MEM_HD) for h in range(MEM_HEADS)]
    mkb = [mk_ref[0, :, sl].astype(BF16) for sl in sls]
    mvb = [mv_ref[0, :, sl].astype(BF16) for sl in sls]
    qscale = MEM_HD ** -0.5 * LOG2_E
    qn = [[(_rms(z[:, sl], mqg_ref[...]) * qscale).astype(BF16) for sl in sls] for z in qz]
    sc = [[_dot_nt(qh, kh) for qh, kh in zip(qt, mkb)] for qt in qn]
    p = [[_softmax2_rows(x).astype(BF16) for x in st] for st in sc]
    oh = [[_dot(ph, vh) for ph, vh in zip(pt, mvb)] for pt in p]
    for t, r in enumerate(rows):
        outs = [oh[t][h] * _silu(qz[t][:, hw + h * MEM_HD:hw + (h + 1) * MEM_HD]) for h in range(MEM_HEADS)]
        oh[t] = jnp.concatenate(outs, axis=1).astype(BF16)
    ym = [_dot(om, wmo_ref[...]) for om in oh]
    for t, r in enumerate(rows):
        y_ref[0, r, :] = acc[t] + ym[t]


def _post(x, os_, wout, mg, wmq, mqg, mk, mv, wmo):
    b, s, d = x.shape
    tm = _row_tile(s, 256 * POST_SPLIT)
    hw = MEM_HEADS * MEM_HD
    nm = mk.shape[1]
    blk = lambda wd: pl.BlockSpec((1, tm, wd), lambda i, j: (i, j, 0))
    mem = pl.BlockSpec((1, nm, hw), lambda i, j: (i, 0, 0))
    kin = sum(o.shape[-1] for o in os_)
    return pl.pallas_call(
        functools.partial(_post_body, n_o=len(os_)),
        grid=(b, s // tm),
        in_specs=[blk(d)] + [blk(o.shape[-1]) for o in os_]
        + [_const_spec((kin, d)), _const_spec((1, d)), _const_spec((d, 2 * hw)), _const_spec((1, MEM_HD)),
           mem, mem, _const_spec((hw, d))],
        out_specs=blk(d),
        out_shape=jax.ShapeDtypeStruct((b, s, d), F32),
        compiler_params=_params(("arbitrary", "arbitrary")),
        name="post",
    )(x, *os_, wout, mg, wmq, mqg, mk, mv, wmo)


def _pad_cols(w, n):
    return jnp.pad(w, ((0, 0), (0, n - w.shape[1])))


def _prep_l0(w_in, conv_w, a_log, dt_bias, q_g, k_g):
    c0 = 3 * 1024
    ab = w_in[:, c0:c0 + 2 * GDN_HEADS]
    c1 = c0 + 2 * GDN_HEADS
    cols = lambda i: w_in[:, c1 + i * 1024:c1 + (i + 1) * 1024].astype(BF16)
    w = dict(
        wu=w_in[:, :c0].astype(BF16), wab=_pad_cols(ab, LANES).astype(BF16),
        wza=cols(0), wq=cols(1), wk=cols(2), wv=cols(3), wzb=cols(4),
        alog=jnp.pad(a_log, (0, LANES - GDN_HEADS)).reshape(1, LANES).astype(F32),
        dtb=jnp.pad(dt_bias, (0, LANES - GDN_HEADS)).reshape(1, LANES).astype(F32),
    )
    ns = 3 * GDN_HEADS
    cw = jnp.pad(conv_w.astype(F32), ((0, 8 - CONV_TAPS), (0, 0))).reshape(8, ns, LANES).transpose(1, 0, 2)
    dm = BAND_HEADS * BAND_HD
    qg = jnp.tile(q_g.astype(F32), BAND_HEADS).reshape(1, dm)
    kg = jnp.tile(k_g.astype(F32), BAND_HEADS).reshape(1, dm)
    return w, cw, qg, kg


def _band_bias_table(rel_bias, L, chunk):
    W = BAND_PAST + L
    n = np.arange(W + L - 1)
    idx = np.clip(BAND_PAST + (L - 1) - n, -BAND_MAX_REL, BAND_MAX_REL) + BAND_MAX_REL
    strip = rel_bias.astype(F32)[:, idx]
    bias = jnp.stack([strip[:, L - 1 - a:L - 1 - a + W] for a in range(L)], axis=1)
    back = np.arange(L)[:, None] // chunk - (np.arange(W)[None, :] - BAND_PAST) // chunk
    readable = (back >= 0) & (back <= BAND_PAST // chunk)
    return bias if readable.all() else jnp.where(jnp.asarray(readable)[None], bias, -jnp.inf)


def _prep_l1(w_in, w_gate_up, gate_bias):
    qk = GLA_HEADS * GLA_DK
    vw = GLA_HEADS * GLA_DV
    o = np.cumsum([0, qk, qk, vw, GLA_RANK, vw])
    return dict(
        wq=w_in[:, o[0]:o[1]].astype(BF16), wk=w_in[:, o[1]:o[2]].astype(BF16),
        wv=w_in[:, o[2]:o[3]].astype(BF16), wlr=_pad_cols(w_in[:, o[3]:o[4]], LANES).astype(BF16),
        wz=w_in[:, o[4]:o[5]].astype(BF16),
        wg=jnp.pad(w_gate_up, ((0, LANES - GLA_RANK), (0, 0))).astype(BF16),
        gbias=gate_bias.reshape(1, qk).astype(F32),
    )


def _layer0(x, gdn_states, band_caches, mk, mv, norm_g, pw, cw, qg, kg, bias,
            a_onorm_g, w_out, mnorm_g, w_mq, mq_g, w_mo, L):
    b, s, d = x.shape
    outs = _proj0(x.reshape(b * s, d), norm_g.reshape(1, d), pw, qg, kg, s)
    u, gb, za, qlo, qhi, kn, klast, vlo, vhi, vlast, zb = (t.reshape((b, -1) + t.shape[1:]) for t in outs)
    if gdn_states is not None:
        conv_state, gdn_state = gdn_states
        gdn_states = (jnp.pad(conv_state.astype(F32), ((0, 0), (8 - (CONV_TAPS - 1), 0), (0, 0))),
                      gdn_state.astype(F32))
    o_a, s_new = _gdn(u, gb, za, gdn_states, cw, a_onorm_g.reshape(1, LANES), L)
    o_b = _band(qlo, qhi, kn, vlo, vhi, zb, bias, band_caches, bias.shape[1])
    y = _post(x, (o_a, o_b), w_out, mnorm_g.reshape(1, d), w_mq, mq_g.reshape(1, MEM_HD), mk, mv, w_mo)
    return y, u[:, s - (CONV_TAPS - 1):, :], s_new, klast, vlast


def _layer1(x, gla_state, mk, mv, norm_g, pw, c_onorm_g, w_out, mnorm_g, w_mq, mq_g, w_mo, L):
    b, s, d = x.shape
    q, k, v, lg, z = _proj1(x.reshape(b * s, d), norm_g.reshape(1, d), pw)
    r3 = lambda t: t.reshape(b, s, t.shape[-1])
    q, k, v, lg, z = map(r3, (q, k, v, lg, z))
    o, st = _gla(q, k, v, lg, z, gla_state, c_onorm_g.reshape(1, GLA_DV), L)
    y = _post(x, (o,), w_out, mnorm_g.reshape(1, d), w_mq, mq_g.reshape(1, MEM_HD), mk, mv, w_mo)
    return y, st


def kernel(x_prompt, x_sample, mem_prompt, state_l0_gdn_conv, state_l0_gdn, cache_l0_band_k, cache_l0_band_v, cache_l0_mem_k, cache_l0_mem_v, state_l1_gla, cache_l1_mem_k, cache_l1_mem_v, l0_norm_g, l0_w_in, l0_conv_w, l0_a_log, l0_dt_bias, l0_a_onorm_g, l0_b_q_g, l0_b_k_g, l0_b_rel_bias, l0_w_out, l0_mnorm_g, l0_mem_norm_g, l0_w_mkv, l0_mk_g, l0_w_mq, l0_mq_g, l0_w_mo, l1_norm_g, l1_w_in, l1_w_gate_up, l1_gate_bias, l1_c_onorm_g, l1_w_out, l1_mnorm_g, l1_mem_norm_g, l1_w_mkv, l1_mk_g, l1_w_mq, l1_mq_g, l1_w_mo):
    bp, sp, d = x_prompt.shape
    bs, ss, _ = x_sample.shape
    nm = mem_prompt.shape[1]
    hw = MEM_HEADS * MEM_HD
    dm = BAND_HEADS * BAND_HD
    assert sp % CHUNK_ == 0 and ss % INV_SUB == 0 and ss <= CHUNK_
    assert cache_l0_band_k.shape[1] == BAND_PAST

    pw0, cw, qg, kg = _prep_l0(l0_w_in, l0_conv_w, l0_a_log, l0_dt_bias, l0_b_q_g, l0_b_k_g)
    pw1 = _prep_l1(l1_w_in, l1_w_gate_up, l1_gate_bias)
    band_rows = BAND_STEP_CHUNKS * CHUNK_ if sp % (BAND_STEP_CHUNKS * CHUNK_) == 0 else CHUNK_
    bias_p = _band_bias_table(l0_b_rel_bias, band_rows, CHUNK_)
    bias_s = _band_bias_table(l0_b_rel_bias, ss, ss)
    bf = lambda w: w.astype(BF16)
    mem2 = mem_prompt.reshape(bp * nm, d)

    p_mk0, p_mv0, mk0, mv0 = _memkv(mem2, l0_mem_norm_g.reshape(1, d), bf(l0_w_mkv), l0_mk_g.reshape(1, MEM_HD))
    mk0 = mk0.reshape(bp, nm, hw)
    mv0 = mv0.reshape(bp, nm, hw)
    l0_shared = (l0_norm_g, pw0, cw, qg, kg)
    l0_tail = (l0_a_onorm_g, bf(l0_w_out), l0_mnorm_g, bf(l0_w_mq), l0_mq_g, bf(l0_w_mo))
    yp, p_conv, p_gdn, p_kn, p_v = _layer0(
        x_prompt, None, None, mk0, mv0,
        *l0_shared, bias_p, *l0_tail, CHUNK_)
    ys, s_conv, s_gdn, s_kn, s_v = _layer0(
        x_sample, (state_l0_gdn_conv, state_l0_gdn),
        (cache_l0_band_k, cache_l0_band_v),
        cache_l0_mem_k.reshape(bs, nm, hw), cache_l0_mem_v.reshape(bs, nm, hw),
        *l0_shared, bias_s, *l0_tail, ss)

    p_mk1, p_mv1, mk1, mv1 = _memkv(mem2, l1_mem_norm_g.reshape(1, d), bf(l1_w_mkv), l1_mk_g.reshape(1, MEM_HD))
    mk1 = mk1.reshape(bp, nm, hw)
    mv1 = mv1.reshape(bp, nm, hw)
    l1_tail = (l1_c_onorm_g, bf(l1_w_out), l1_mnorm_g, bf(l1_w_mq), l1_mq_g, bf(l1_w_mo))
    yp, p_gla = _layer1(yp, None, mk1, mv1, l1_norm_g, pw1, *l1_tail, CHUNK_)
    ys, s_gla = _layer1(ys, state_l1_gla.astype(F32), cache_l1_mem_k.reshape(bs, nm, hw),
                        cache_l1_mem_v.reshape(bs, nm, hw), l1_norm_g, pw1, *l1_tail, ss)

    m4 = lambda t: t.reshape(bp, nm, MEM_HEADS, MEM_HD)
    return (yp, ys, p_conv, p_gdn, p_kn, p_v,
            m4(p_mk0), m4(p_mv0), p_gla, m4(p_mk1), m4(p_mv1),
            s_conv, s_gdn, s_kn, s_v, s_gla)
```

```python
import functools

import jax
import jax.numpy as jnp
import numpy as np
from jax import lax
from jax.experimental import pallas as pl
from jax.experimental.pallas import tpu as pltpu

F32 = jnp.float32
BF16 = jnp.bfloat16
NORM_EPS = 1e-6
LOG2_E = 1.4426950408889634

CHUNK_ = 64
CONV_TAPS = 4
GDN_HEADS = 8
GDN_DK = 128
BAND_HEADS = 16
BAND_HD = 64
BAND_PAST = 512
BAND_MAX_REL = 128
GLA_HEADS = 8
GLA_DK = 128
GLA_DV = 256
GLA_RANK = 16
GLA_TAU = 16.0
MEM_HEADS = 4
MEM_HD = 128
INV_SUB = 16
GDN_GROUP = 2
GDN_STEP_CHUNKS = 1
GLA_GROUP = 4
POST_SPLIT = 4
PROJ_SPLIT = 2
SUB_TILE_MIN = 128
BAND_STEP_CHUNKS = 2
BAND_SLAB_GROUPS = 4
LANES = 128
VMEM_LIMIT = 56 * 1024 * 1024


def _dot(a, b):
    return jnp.dot(a, b, preferred_element_type=F32)


def _dot_nt(a, b):
    return lax.dot_general(a, b, (((1,), (1,)), ((), ())), preferred_element_type=F32)


def _dot_tn(a, b):
    return lax.dot_general(a, b, (((0,), (0,)), ((), ())), preferred_element_type=F32)


def _split3(x):
    hi = x.astype(BF16)
    r = x - hi.astype(F32)
    mid = r.astype(BF16)
    lo = (r - mid.astype(F32)).astype(BF16)
    return hi, mid, lo


def _dot_exact_lhs(a_bf, b):
    h, m, l = _split3(b)
    return _dot(a_bf, h) + (_dot(a_bf, m) + _dot(a_bf, l))


def _rms(x, g):
    ms = jnp.mean(x * x, axis=-1, keepdims=True)
    return x * lax.rsqrt(ms + NORM_EPS) * g


def _silu(x):
    return x * jax.nn.sigmoid(x)


def _log1p_exp_neg_abs(x):
    return jnp.log(1.0 + jnp.exp2(jnp.abs(x) * -LOG2_E))


def _softplus(x):
    return jnp.maximum(x, 0.0) + _log1p_exp_neg_abs(x)


def _log_sigmoid(x):
    return jnp.minimum(x, 0.0) - _log1p_exp_neg_abs(x)


def _softmax2_rows(s):
    m = jnp.max(s, axis=-1, keepdims=True)
    e = jnp.exp2(s - m)
    return e * (1.0 / jnp.sum(e, axis=-1, keepdims=True))


def _const_spec(shape):
    nd = len(shape)
    return pl.BlockSpec(shape, lambda *_: (0,) * nd, pipeline_mode=pl.Buffered(1))


def _params(sem):
    return pltpu.CompilerParams(dimension_semantics=sem, vmem_limit_bytes=VMEM_LIMIT)


def _row_tile(n, want):
    t = min(n, want)
    assert n % t == 0
    return t


def _sub_tiles(tm):
    ts = tm // PROJ_SPLIT if tm % (SUB_TILE_MIN * PROJ_SPLIT) == 0 else tm
    return [slice(t * ts, (t + 1) * ts) for t in range(tm // ts)]


def _proj0_body(x_ref, g_ref, wu_ref, wab_ref, wza_ref, wq_ref, wk_ref, wv_ref, wzb_ref,
                alog_ref, dtb_ref, qg_ref, kg_ref,
                u_ref, gb_ref, za_ref, qlo_ref, qhi_ref, kn_ref, klast_ref, vlo_ref, vhi_ref, vlast_ref, zb_ref,
                *, kept_tiles):
    h = _rms(x_ref[...], g_ref[...]).astype(BF16)
    dm = BAND_HEADS * BAND_HD
    lo = (lax.broadcasted_iota(jnp.int32, (1, dm), 1) & (LANES - 1)) < BAND_HD
    q = _dot(h, wq_ref[...])
    k = _dot(h, wk_ref[...])
    ab = _dot(h, wab_ref[...])
    za = _dot(h, wza_ref[...])

    zb = _dot(h, wzb_ref[...])
    v = _dot(h, wv_ref[...])
    u_ref[...] = _dot(h, wu_ref[...])
    za_ref[...] = _silu(za)
    zb_ref[...] = _silu(zb)
    lo1 = lo[:, :LANES]

    def head_rsqrt(x):
        out = []
        for s in range(BAND_HEADS // 2):
            x2 = x[:, s * LANES:(s + 1) * LANES]
            x2 = x2 * x2
            first = jnp.sum(jnp.where(lo1, x2, 0.0), axis=-1, keepdims=True)
            second = jnp.sum(jnp.where(lo1, 0.0, x2), axis=-1, keepdims=True)
            r = lambda t: lax.rsqrt(t * (1.0 / BAND_HD) + NORM_EPS)
            out.append(jnp.where(lo1, r(first), r(second)))
        return jnp.concatenate(out, axis=1)

    qn = q * head_rsqrt(q) * (qg_ref[...] * (BAND_HD ** -0.5 * LOG2_E))
    kn = k * head_rsqrt(k) * kg_ref[...]
    qlo_ref[...] = jnp.where(lo, qn, 0.0).astype(BF16)
    qhi_ref[...] = jnp.where(lo, 0.0, qn).astype(BF16)
    kn_ref[...] = kn.astype(BF16)
    vlo_ref[...] = jnp.where(lo, v, 0.0).astype(BF16)
    vhi_ref[...] = jnp.where(lo, 0.0, v).astype(BF16)

    def hand_on():
        klast_ref[...] = kn.reshape(kn.shape[0], BAND_HEADS, BAND_HD)
        vlast_ref[...] = v.reshape(v.shape[0], BAND_HEADS, BAND_HD)

    if kept_tiles is None:
        hand_on()
    else:
        tpb, kt = kept_tiles
        pl.when(pl.program_id(0) % tpb >= tpb - kt)(hand_on)
    lane = lax.broadcasted_iota(jnp.int32, ab.shape, 1)
    gval = -jnp.exp(alog_ref[...]) * _softplus(ab + dtb_ref[...])
    gb_ref[...] = jnp.where(lane < GDN_HEADS, gval, jax.nn.sigmoid(ab))


def _proj0(x2, g, w, qg, kg, rows_per_batch):
    n, d = x2.shape
    tm = _row_tile(n, 256)
    dm = BAND_HEADS * BAND_HD
    widths = (3 * 1024, LANES, 1024, dm, dm, dm, dm)
    row = lambda wd: pl.BlockSpec((tm, wd), lambda i: (i, 0))
    keep = min(BAND_PAST, rows_per_batch)
    if rows_per_batch > keep:
        assert rows_per_batch % tm == 0 and keep % tm == 0
        tpb, kt = rows_per_batch // tm, keep // tm
        last = pl.BlockSpec((tm, BAND_HEADS, BAND_HD),
                            lambda i: ((i // tpb) * kt + jnp.maximum(i % tpb - (tpb - kt), 0), 0, 0))
        n_last = (n // rows_per_batch) * keep
        kept_tiles = (tpb, kt)
    else:
        last, n_last = pl.BlockSpec((tm, BAND_HEADS, BAND_HD), lambda i: (i, 0, 0)), n
        kept_tiles = None
    last_shape = jax.ShapeDtypeStruct((n_last, BAND_HEADS, BAND_HD), F32)
    f32 = lambda rows, wd: jax.ShapeDtypeStruct((rows, wd), F32)
    bf16 = lambda wd: jax.ShapeDtypeStruct((n, wd), BF16)
    return pl.pallas_call(
        functools.partial(_proj0_body, kept_tiles=kept_tiles),
        grid=(n // tm,),
        in_specs=[row(d), _const_spec((1, d))]
        + [_const_spec((d, wd)) for wd in widths]
        + [_const_spec((1, LANES)), _const_spec((1, LANES)), _const_spec((1, dm)), _const_spec((1, dm))],
        out_specs=[row(3 * 1024), row(LANES), row(1024), row(dm), row(dm), row(dm), last, row(dm), row(dm), last,
                   row(dm)],
        out_shape=[f32(n, 3 * 1024), f32(n, LANES), f32(n, 1024), bf16(dm), bf16(dm), bf16(dm), last_shape,
                   bf16(dm), bf16(dm), last_shape, f32(n, dm)],
        compiler_params=_params(("arbitrary",)),
        name="proj0",
    )(x2, g, w["wu"], w["wab"], w["wza"], w["wq"], w["wk"], w["wv"], w["wzb"], w["alog"], w["dtb"],
      qg, kg)


def _proj1_body(x_ref, g_ref, wq_ref, wk_ref, wv_ref, wlr_ref, wz_ref, wg_ref, gbias_ref,
                q_ref, k_ref, v_ref, lg_ref, z_ref):
    rows = _sub_tiles(x_ref.shape[0])
    h = [_rms(x_ref[r, :], g_ref[...]).astype(BF16) for r in rows]
    lr = [_dot(h_, wlr_ref[...]) for h_ in h]
    for r, h_ in zip(rows, h):
        q_ref[r, :] = _dot(h_, wq_ref[...])
    for r, x in zip(rows, lr):
        lg_ref[r, :] = _dot(x.astype(BF16), wg_ref[...])
    for r, h_ in zip(rows, h):
        z_ref[r, :] = _dot(h_, wz_ref[...])
    for r, h_ in zip(rows, h):
        k_ref[r, :] = _dot(h_, wk_ref[...])
    for r in rows:
        lg_ref[r, :] = _log_sigmoid(lg_ref[r, :] + gbias_ref[...]) * (1.0 / GLA_TAU)
    for r, h_ in zip(rows, h):
        v_ref[r, :] = _dot(h_, wv_ref[...])
    for r in rows:
        z_ref[r, :] = _silu(z_ref[r, :])


def _proj1(x2, g, w):
    n, d = x2.shape
    tm = _row_tile(n, 512)
    row = lambda wd: pl.BlockSpec((tm, wd), lambda i: (i, 0))
    outw = (1024, 1024, 2048, 1024, 2048)
    return pl.pallas_call(
        _proj1_body,
        grid=(n // tm,),
        in_specs=[row(d), _const_spec((1, d)), _const_spec((d, 1024)), _const_spec((d, 1024)),
                  _const_spec((d, 2048)), _const_spec((d, LANES)), _const_spec((d, 2048)),
                  _const_spec((LANES, 1024)), _const_spec((1, 1024))],
        out_specs=[row(wd) for wd in outw],
        out_shape=[jax.ShapeDtypeStruct((n, wd), F32) for wd in outw],
        compiler_params=_params(("arbitrary",)),
        name="proj1",
    )(x2, g, w["wq"], w["wk"], w["wv"], w["wlr"], w["wz"], w["wg"], w["gbias"])


def _memkv_body(m_ref, g_ref, w_ref, kg_ref, k4_ref, v4_ref, kb_ref, vb_ref):
    h = _rms(m_ref[...], g_ref[...]).astype(BF16)
    kv = _dot(h, w_ref[...])
    hw = MEM_HEADS * MEM_HD
    k = jnp.concatenate([_rms(kv[:, hh * MEM_HD:(hh + 1) * MEM_HD], kg_ref[...]) for hh in range(MEM_HEADS)],
                        axis=1)
    v = kv[:, hw:]
    k4_ref[...] = k.reshape(k.shape[0], MEM_HEADS, MEM_HD)
    v4_ref[...] = v.reshape(v.shape[0], MEM_HEADS, MEM_HD)
    kb_ref[...] = k.astype(BF16)
    vb_ref[...] = v.astype(BF16)


def _memkv(m2, g, w_bf, kg):
    n, d = m2.shape
    tm = _row_tile(n, 256)
    hw = MEM_HEADS * MEM_HD
    row = lambda wd: pl.BlockSpec((tm, wd), lambda i: (i, 0))
    row4 = pl.BlockSpec((tm, MEM_HEADS, MEM_HD), lambda i: (i, 0, 0))
    return pl.pallas_call(
        _memkv_body,
        grid=(n // tm,),
        in_specs=[row(d), _const_spec((1, d)), _const_spec((d, 2 * hw)), _const_spec((1, MEM_HD))],
        out_specs=[row4, row4, row(hw), row(hw)],
        out_shape=[jax.ShapeDtypeStruct((n, MEM_HEADS, MEM_HD), F32)] * 2
        + [jax.ShapeDtypeStruct((n, hw), BF16)] * 2,
        compiler_params=_params(("arbitrary",)),
        name="memkv",
    )(m2, g, w_bf, kg)


def _each(fn, *lists):
    return [fn(*xs) for xs in zip(*lists)]


def _unit_lower_inverse(a, eye, bd, mm):
    d = _each(lambda x: jnp.where(bd, x, 0.0), a)
    nl = _each(lambda x, y: x - y, a, d)
    d2 = _each(mm, d, d)
    d4 = _each(mm, d2, d2)
    td = _each(lambda x, y: mm(eye - x, eye + y), d, d2)
    d8 = _each(mm, d4, d4)
    td = _each(lambda x, y: mm(x, eye + y), td, d4)
    td = _each(lambda x, y: mm(x, eye + y), td, d8)
    m = _each(mm, td, nl)
    m2 = _each(mm, m, m)
    mt = _each(mm, m, td)
    return _each(lambda x, y, z: mm(eye + x, y - z), m2, td, mt)


def _gdn_body(*refs, L, NB, G, has_state):
    if has_state:
        u_ref, gb_ref, za_ref, cw_ref, og_ref, cinit_ref, sinit_ref, o_ref, s_ref, ubuf = refs
    else:
        u_ref, gb_ref, za_ref, cw_ref, og_ref, o_ref, s_ref, ubuf = refs
    c = pl.program_id(1)
    ns = 3 * GDN_HEADS
    hist = 8
    R = G * L

    @pl.when(c == 0)
    def _():
        if has_state:
            for b in range(NB):
                for j in range(ns):
                    ubuf[b * ns + j, 0:hist, :] = cinit_ref[b, :, j * LANES:(j + 1) * LANES]
            s_ref[...] = sinit_ref[...]
        else:
            ubuf[:, 0:hist, :] = jnp.zeros((NB * ns, hist, LANES), F32)
            s_ref[...] = jnp.zeros(s_ref.shape, F32)

    for b in range(NB):
        for j in range(ns):
            ubuf[b * ns + j, hist:hist + R, :] = u_ref[b, :, j * LANES:(j + 1) * LANES]
    base = hist - (CONV_TAPS - 1)
    ys = []
    for b in range(NB):
        bsl = slice(b * ns, (b + 1) * ns)
        yb = ubuf[bsl, base:base + R, :] * cw_ref[:, 0:1, :]
        for i in range(1, CONV_TAPS):
            yb = yb + ubuf[bsl, base + i:base + i + R, :] * cw_ref[:, i:i + 1, :]
        ys.append(_silu(yb))
    ubuf[:, base:hist, :] = ubuf[:, base + R:hist + R, :]

    row = lax.broadcasted_iota(jnp.int32, (L, 2 * L), 0)
    lane = lax.broadcasted_iota(jnp.int32, (L, 2 * L), 1)
    col = lane & (L - 1)
    left = lane < L
    incl = row >= col
    strict = row > col
    sub_shift = INV_SUB.bit_length() - 1
    bd = lax.shift_right_logical(row, sub_shift) == lax.shift_right_logical(col, sub_shift)
    eye = jnp.where(row == col, 1.0, 0.0).astype(F32)
    trow = lax.broadcasted_iota(jnp.int32, (L, L), 0)
    tcol = lax.broadcasted_iota(jnp.int32, (L, L), 1)
    tri = jnp.where(trow >= tcol, 1.0, 0.0).astype(BF16)
    og = og_ref[...]
    heads = [(b, g, h) for b in range(NB) for g in range(G) for h in range(GDN_HEADS)]
    pairs = range(len(heads) // 2)
    rows_of = lambda g: slice(g * L, (g + 1) * L)
    l2n = lambda x: x * lax.rsqrt(jnp.sum(x * x, axis=-1, keepdims=True) + NORM_EPS)
    q = [l2n(ys[b][h][rows_of(g)]) * (GDN_DK ** -0.5) for b, g, h in heads]
    k = [l2n(ys[b][GDN_HEADS + h][rows_of(g)]) for b, g, h in heads]
    v = [ys[b][2 * GDN_HEADS + h][rows_of(g)] for b, g, h in heads]
    gc, gr, bc = [], [], []
    for b in range(NB):
        for g in range(G):
            gbv = gb_ref[b, rows_of(g), :]
            gcum = _dot_exact_lhs(tri, gbv)
            gpad = jnp.concatenate([gcum, jnp.zeros((LANES - L, LANES), F32)], axis=0)
            gt = gpad.T
            for h in range(GDN_HEADS):
                gc.append(gcum[:, h:h + 1])
                gr.append(gt[h:h + 1, 0:L])
                bc.append(gbv[:, GDN_HEADS + h:GDN_HEADS + h + 1])
    side = lambda x0, x1: jnp.where(left, x0, x1)
    gcp = [side(gc[2 * p], gc[2 * p + 1]) for p in pairs]
    grp = [jnp.concatenate([gr[2 * p], gr[2 * p + 1]], axis=1) for p in pairs]
    bcp = [side(bc[2 * p], bc[2 * p + 1]) for p in pairs]
    dec = _each(lambda c_, r_: jnp.where(incl, jnp.exp(jnp.where(incl, c_ - r_, 0.0)), 0.0), gcp, grp)
    kb = _each(lambda x: x.astype(BF16), k)
    qb = _each(lambda x: x.astype(BF16), q)

    def blockdiag(y0, y1):
        z0 = jnp.zeros(y1.shape, y1.dtype)
        z1 = jnp.zeros(y0.shape, y0.dtype)
        return jnp.concatenate([jnp.concatenate([y0, z0], axis=1), jnp.concatenate([z1, y1], axis=1)], axis=0)

    def mm(x, y):
        yb = y.astype(BF16)
        zero = jnp.zeros_like(yb)
        return _dot(x.astype(BF16), jnp.concatenate([jnp.where(left, yb, zero), jnp.where(left, zero, yb)], axis=0))

    kq = [_dot_nt(jnp.concatenate([jnp.concatenate([kb[2 * p], qb[2 * p]], axis=0),
                                   jnp.concatenate([kb[2 * p + 1], qb[2 * p + 1]], axis=0)], axis=1),
                  blockdiag(kb[2 * p], kb[2 * p + 1])) for p in pairs]
    a = _each(lambda b_, x, d_: jnp.where(strict, b_ * x[:L] * d_, 0.0), bcp, kq, dec)
    t = _unit_lower_inverse(a, eye, bd, mm)
    eg = _each(jnp.exp, gc)
    rhs = _each(lambda b_, v_, e_, k_: jnp.concatenate([b_ * v_, (b_ * e_) * k_], axis=1).astype(BF16),
                bc, v, eg, k)
    sol = [_dot(t[p].astype(BF16), blockdiag(rhs[2 * p], rhs[2 * p + 1])) for p in pairs]
    solk = lambda i: sol[i // 2][:, 2 * (i % 2) * LANES + LANES:2 * (i % 2 + 1) * LANES]
    solv = lambda i: sol[i // 2][:, 2 * (i % 2) * LANES:2 * (i % 2) * LANES + LANES]
    qkd = [(kq[p][L:] * dec[p]).astype(BF16) for p in pairs]
    gl = [c_[L - 1:L, :] for c_ in gc]
    kd = _each(lambda k_, l_, c_: (k_ * jnp.exp(l_ - c_)).astype(BF16), k, gl, gc)

    item = lambda b, g, h: (b * G + g) * GDN_HEADS + h
    half = lambda x, i: x[:, (i % 2) * LANES:(i % 2 + 1) * LANES]
    s = {(b, h): s_ref[b, h] for b in range(NB) for h in range(GDN_HEADS)}
    for g in range(G):
        ids = [item(b, g, h) for b in range(NB) for h in range(GDN_HEADS)]
        sb = {i: s[(heads[i][0], heads[i][2])].astype(BF16) for i in ids}
        ksq = {i: _dot(jnp.concatenate([jnp.concatenate([solk(i).astype(BF16), qb[i]], axis=0),
                                        jnp.concatenate([solk(i + 1).astype(BF16), qb[i + 1]], axis=0)], axis=1),
                       blockdiag(sb[i], sb[i + 1])) for i in ids[::2]}
        ub = {i: (solv(i) - half(ksq[i - i % 2][:L], i)).astype(BF16) for i in ids}
        qku = {i: _dot(qkd[i // 2], blockdiag(ub[i], ub[i + 1])) for i in ids[::2]}
        ktu = {i: _dot_tn(kd[i], ub[i]) for i in ids}
        for i in ids:
            b, _, h = heads[i]
            s[(b, h)] = s[(b, h)] * jnp.exp(gl[i]) + ktu[i]
        for b in range(NB):
            za = za_ref[b, rows_of(g), :]
            outs = [(_rms(half(ksq[i - i % 2][L:], i) * eg[i] + half(qku[i - i % 2], i), og)
                     * za[:, heads[i][2] * LANES:(heads[i][2] + 1) * LANES]).astype(BF16)
                    for i in ids if heads[i][0] == b]
            o_ref[b, rows_of(g), :] = jnp.concatenate(outs, axis=1)
    for (b, h), val in s.items():
        s_ref[b, h] = val


def _gdn(u, gb, za, state, cw, og, L):
    b, s, _ = u.shape
    nc = s // L
    ns = 3 * GDN_HEADS
    nb = GDN_GROUP if b % GDN_GROUP == 0 else 1
    g = GDN_STEP_CHUNKS if nc % GDN_STEP_CHUNKS == 0 else 1
    blk = lambda wd: pl.BlockSpec((nb, g * L, wd), lambda i, j: (i, j, 0))
    has_state = state is not None
    state_specs = [pl.BlockSpec((nb, 8, ns * LANES), lambda i, j: (i, 0, 0)),
                   pl.BlockSpec((nb, GDN_HEADS, GDN_DK, LANES), lambda i, j: (i, 0, 0, 0))]
    return pl.pallas_call(
        functools.partial(_gdn_body, L=L, NB=nb, G=g, has_state=has_state),
        grid=(b // nb, nc // g),
        in_specs=[blk(ns * LANES), blk(LANES), blk(GDN_HEADS * LANES),
                  _const_spec((ns, 8, LANES)), _const_spec((1, LANES))] + (state_specs if has_state else []),
        out_specs=[blk(GDN_HEADS * LANES),
                   pl.BlockSpec((nb, GDN_HEADS, GDN_DK, LANES), lambda i, j: (i, 0, 0, 0))],
        out_shape=[jax.ShapeDtypeStruct((b, s, GDN_HEADS * LANES), BF16),
                   jax.ShapeDtypeStruct((b, GDN_HEADS, GDN_DK, LANES), F32)],
        scratch_shapes=[pltpu.VMEM((nb * ns, 8 + g * L, LANES), F32)],
        compiler_params=_params(("arbitrary", "arbitrary")),
        name="gdn",
    )(u, gb, za, cw, og, *(state if has_state else ()))


def _band_body(*refs, L, W, has_cache):
    if has_cache:
        (qlo_ref, qhi_ref, k_ref, vlo_ref, vhi_ref, zb_ref, bias_ref, ck_ref, cv_ref,
         o_ref, kscr, vlo, vhi, biasm) = refs
    else:
        (qlo_ref, qhi_ref, k_ref, vlo_ref, vhi_ref, zb_ref, bias_ref,
         o_ref, kscr, vlo, vhi, biasm) = refs
    c = pl.program_id(1)
    dm = BAND_HEADS * BAND_HD
    pad_chunks = BAND_PAST // L

    @pl.when(c == 0)
    def _():
        if has_cache:
            lo = (lax.broadcasted_iota(jnp.int32, (1, dm), 1) & (LANES - 1)) < BAND_HD
            kscr[0:BAND_PAST, :] = ck_ref[0].reshape(BAND_PAST, dm).astype(BF16)
            cv = cv_ref[0].reshape(BAND_PAST, dm)
            vlo[0:BAND_PAST, :] = jnp.where(lo, cv, 0.0).astype(BF16)
            vhi[0:BAND_PAST, :] = jnp.where(lo, 0.0, cv).astype(BF16)
        else:
            zero = jnp.zeros((BAND_PAST, dm), BF16)
            kscr[0:BAND_PAST, :] = zero
            vlo[0:BAND_PAST, :] = zero
            vhi[0:BAND_PAST, :] = zero

    new0 = pl.multiple_of(BAND_PAST + c * L, L)
    kscr[pl.ds(new0, L), :] = k_ref[0]
    vlo[pl.ds(new0, L), :] = vlo_ref[0]
    vhi[pl.ds(new0, L), :] = vhi_ref[0]

    if has_cache:
        @pl.when(c == 0)
        def _():
            biasm[...] = bias_ref[...] * LOG2_E
    else:
        @pl.when(c < pad_chunks)
        def _():
            wcol = lax.broadcasted_iota(jnp.int32, (1, W), 1)
            valid = (wcol + c * L) >= BAND_PAST
            for h in range(BAND_HEADS):
                biasm[h] = jnp.where(valid, bias_ref[h] * LOG2_E, -jnp.inf)

        @pl.when(c == pad_chunks)
        def _():
            biasm[...] = bias_ref[...] * LOG2_E
    bias_src = biasm

    w0 = pl.multiple_of(c * L, L)
    zb = zb_ref[0]
    slabs = [slice(s * LANES, (s + 1) * LANES) for s in range(BAND_HEADS // 2)]
    per = len(slabs) // BAND_SLAB_GROUPS
    groups = [range(g * per, (g + 1) * per) for g in range(BAND_SLAB_GROUPS)]

    def scores(i):
        ks = kscr[pl.ds(w0, W), slabs[i]]
        return _dot_nt(qlo_ref[0, :, slabs[i]], ks), _dot_nt(qhi_ref[0, :, slabs[i]], ks)

    def weights(x, h):
        x = x + bias_src[h]
        e = jnp.exp2(x - jnp.max(x, axis=-1, keepdims=True))
        return e.astype(BF16), 1.0 / jnp.sum(e, axis=-1, keepdims=True)

    sc = [[scores(i) for i in grp] for grp in groups]
    for grp, sc_g in zip(groups, sc):
        pr = [(weights(lo_, 2 * i), weights(hi_, 2 * i + 1)) for i, (lo_, hi_) in zip(grp, sc_g)]
        pv = [(_dot(p[0][0], vlo[pl.ds(w0, W), slabs[i]]), _dot(p[1][0], vhi[pl.ds(w0, W), slabs[i]]))
              for i, p in zip(grp, pr)]
        for i, p, o in zip(grp, pr, pv):
            o_ref[0, :, slabs[i]] = ((o[0] * p[0][1] + o[1] * p[1][1]) * zb[:, slabs[i]]).astype(BF16)


def _band(qlo, qhi, kn, vlo, vhi, zb, bias, caches, L):
    b, s, dm = qlo.shape
    nc = s // L
    W = BAND_PAST + L
    blk = pl.BlockSpec((1, L, dm), lambda i, j: (i, j, 0))
    cache = pl.BlockSpec((1, BAND_PAST, BAND_HEADS, BAND_HD), lambda i, j: (i, 0, 0, 0))
    has_cache = caches is not None
    scratch = [pltpu.VMEM((BAND_PAST + s, dm), BF16)] * 3 + [pltpu.VMEM((BAND_HEADS, L, W), F32)]
    return pl.pallas_call(
        functools.partial(_band_body, L=L, W=W, has_cache=has_cache),
        grid=(b, nc),
        in_specs=[blk] * 6 + [_const_spec((BAND_HEADS, L, W))] + ([cache, cache] if has_cache else []),
        out_specs=blk,
        out_shape=jax.ShapeDtypeStruct((b, s, dm), BF16),
        scratch_shapes=scratch,
        compiler_params=_params(("arbitrary", "arbitrary")),
        name="band",
    )(qlo, qhi, kn, vlo, vhi, zb, bias, *(caches if has_cache else ()))


def _gla_body(*refs, L, NB, has_state):
    if has_state:
        q_ref, k_ref, v_ref, lg_ref, z_ref, og_ref, sinit_ref, o_ref, st_ref, cbs = refs
    else:
        q_ref, k_ref, v_ref, lg_ref, z_ref, og_ref, o_ref, st_ref, cbs = refs
    c = pl.program_id(1)

    @pl.when(c == 0)
    def _():
        st_ref[...] = sinit_ref[...] if has_state else jnp.zeros(st_ref.shape, F32)

    levels = [L >> (t + 1) for t in range(L.bit_length() - 1)]
    small = [s for s in levels if 2 < 2 * s < 16]
    row = lax.broadcasted_iota(jnp.int32, (L, L), 0)
    col = lax.broadcasted_iota(jnp.int32, (L, L), 1)
    anchor = lambda s: lax.shift_left(lax.shift_right_logical(row, s.bit_length()), s.bit_length()) + (s - 1)
    between = lambda s: (col > jnp.minimum(row, anchor(s))) & (col <= jnp.maximum(row, anchor(s)))
    onehot = lambda m: jnp.where(m, 1.0, 0.0).astype(BF16)
    tri = onehot(row >= col)
    stack = jnp.concatenate([onehot(between(s)) for s in small], axis=0)
    ex = jnp.exp2
    og = og_ref[...]
    scale = GLA_DK ** -0.5
    lg2 = [lg_ref[b] * LOG2_E for b in range(NB)]
    pieces = [_split3(x) for x in lg2]
    cb_all = [_dot(tri, p[0]) + (_dot(tri, p[1]) + _dot(tri, p[2])) for p in pieces]
    y_small = [_dot(stack, p[0]) + _dot(stack, p[1]) for p in pieces]
    odd_row = (lax.broadcasted_iota(jnp.int32, (L, 1), 0) & 1) == 1
    ydec = [dict() for _ in range(NB)]
    for b in range(NB):
        cbs[b] = cb_all[b]
        ydec[b][1] = jnp.where(odd_row, lg2[b], 0.0)
        for t, s in enumerate(small):
            ydec[b][s] = y_small[b][t * L:(t + 1) * L]
        for s in levels:
            if s not in ydec[b]:
                anc = jnp.concatenate(
                    [jnp.broadcast_to(cbs[b, p * 2 * s + s - 1:p * 2 * s + s, :], (2 * s, cbs.shape[2]))
                     for p in range(L // (2 * s))], axis=0)
                ydec[b][s] = -jnp.abs(cb_all[b] - anc)

    heads = [(b, h) for b in range(NB) for h in range(GLA_HEADS)]
    pairs = range(len(heads) // 2)
    ksl = lambda h: slice(h * GLA_DK, (h + 1) * GLA_DK)
    vsl = lambda h: slice(h * GLA_DV, (h + 1) * GLA_DV)
    cb = [cb_all[b][:, ksl(h)] for b, h in heads]
    q = [q_ref[b, :, ksl(h)] * scale for b, h in heads]
    k = [k_ref[b, :, ksl(h)] for b, h in heads]
    vb = [v_ref[b, :, vsl(h)].astype(BF16) for b, h in heads]
    st = [st_ref[b, h] for b, h in heads]
    o_inter = _each(lambda q_, c_, s_: _dot((q_ * ex(c_)).astype(BF16), s_.astype(BF16)), q, cb, st)
    cl = [c_[L - 1:L] for c_ in cb]
    ke = _each(lambda k_, l_, c_: (k_ * ex(l_ - c_)).astype(BF16), k, cl, cb)
    ktv = _each(_dot_tn, ke, vb)
    dcol = [jnp.broadcast_to(ex(l_), (GLA_DK, GLA_DK)).T for l_ in cl]

    zk = jnp.zeros((L, GLA_DK), BF16)
    zv = jnp.zeros((L, GLA_DV), BF16)

    def side_by_side(qs, ks):
        lhs = jnp.concatenate(qs, axis=1).astype(BF16)
        rhs = jnp.concatenate([jnp.concatenate([ks[0].astype(BF16), zk], axis=1),
                               jnp.concatenate([zk, ks[1].astype(BF16)], axis=1)], axis=0)
        return _dot_nt(lhs, rhs)

    prow = lax.broadcasted_iota(jnp.int32, (L, 2 * L), 0)
    pcol = lax.broadcasted_iota(jnp.int32, (L, 2 * L), 1) & (L - 1)
    att = [jnp.where(prow == pcol, side_by_side((q[2 * p], q[2 * p + 1]), (k[2 * p], k[2 * p + 1])), 0.0)
           for p in pairs]
    for s in levels:
        sh = s.bit_length() - 1
        same_parent = lax.shift_right_logical(prow, sh + 1) == lax.shift_right_logical(pcol, sh + 1)
        take = same_parent & ((lax.shift_right_logical(prow, sh) & 1) == 1) & \
            ((lax.shift_right_logical(pcol, sh) & 1) == 0)
        f = [ex(ydec[b][s][:, ksl(h)]) for b, h in heads]
        prod = [side_by_side((q[2 * p] * f[2 * p], q[2 * p + 1] * f[2 * p + 1]),
                             (k[2 * p] * f[2 * p], k[2 * p + 1] * f[2 * p + 1])) for p in pairs]
        att = [jnp.where(take, prod[p], att[p]) for p in pairs]
    vpair = [jnp.concatenate([jnp.concatenate([vb[2 * p], zv], axis=1),
                              jnp.concatenate([zv, vb[2 * p + 1]], axis=1)], axis=0) for p in pairs]
    o_intra = [_dot(att[p].astype(BF16), vpair[p]) for p in pairs]
    for i, (b, h) in enumerate(heads):
        st_ref[b, h] = st[i] * jnp.concatenate([dcol[i]] * (GLA_DV // GLA_DK), axis=1) + ktv[i]
    for b in range(NB):
        z = z_ref[b]
        outs = [(_rms(o_inter[i] + o_intra[i // 2][:, (i % 2) * GLA_DV:(i % 2 + 1) * GLA_DV], og)
                 * z[:, vsl(h)]).astype(BF16) for i, (b_, h) in enumerate(heads) if b_ == b]
        o_ref[b] = jnp.concatenate(outs, axis=1)


def _gla(q, k, v, lg, z, sinit, og, L):
    b, s, _ = q.shape
    nc = s // L
    nb = GLA_GROUP if b % GLA_GROUP == 0 else 1
    qk = pl.BlockSpec((nb, L, GLA_HEADS * GLA_DK), lambda i, j: (i, j, 0))
    vv = pl.BlockSpec((nb, L, GLA_HEADS * GLA_DV), lambda i, j: (i, j, 0))
    st = pl.BlockSpec((nb, GLA_HEADS, GLA_DK, GLA_DV), lambda i, j: (i, 0, 0, 0))
    has_state = sinit is not None
    return pl.pallas_call(
        functools.partial(_gla_body, L=L, NB=nb, has_state=has_state),
        grid=(b // nb, nc),
        in_specs=[qk, qk, vv, qk, vv, _const_spec((1, GLA_DV))] + ([st] if has_state else []),
        out_specs=[vv, st],
        out_shape=[jax.ShapeDtypeStruct((b, s, GLA_HEADS * GLA_DV), BF16),
                   jax.ShapeDtypeStruct((b, GLA_HEADS, GLA_DK, GLA_DV), F32)],
        scratch_shapes=[pltpu.VMEM((nb, L, GLA_HEADS * GLA_DK), F32)],
        compiler_params=_params(("arbitrary", "arbitrary")),
        name="gla",
    )(q, k, v, lg, z, og, *((sinit,) if has_state else ()))


def _post_body(*refs, n_o):
    x_ref = refs[0]
    o_refs = refs[1:1 + n_o]
    wout_ref, mg_ref, wmq_ref, mqg_ref, mk_ref, mv_ref, wmo_ref, y_ref = refs[1 + n_o:]
    tm = x_ref.shape[1]
    ts = tm // POST_SPLIT if tm % (SUB_TILE_MIN * POST_SPLIT) == 0 else tm
    rows = [slice(t * ts, (t + 1) * ts) for t in range(tm // ts)]
    acc = [x_ref[0, r, :] for r in rows]
    off = 0
    for o_ref in o_refs:
        kd = o_ref.shape[-1]
        w = wout_ref[off:off + kd, :]
        acc = [a + _dot(o_ref[0, r, :], w) for a, r in zip(acc, rows)]
        off += kd
    hm = [_rms(a, mg_ref[...]).astype(BF16) for a in acc]
    qz = [_dot(h_, wmq_ref[...]) for h_ in hm]
    hw = MEM_HEADS * MEM_HD
    sls = [slice(h * MEM_HD, (h + 1) * MEM_HD) for h in range(MEM_HEADS)]
    mkb = [mk_ref[0, :, sl].astype(BF16) for sl in sls]
    mvb = [mv_ref[0, :, sl].astype(BF16) for sl in sls]
    qscale = MEM_HD ** -0.5 * LOG2_E
    qn = [[(_rms(z[:, sl], mqg_ref[...]) * qscale).astype(BF16) for sl in sls] for z in qz]
    sc = [[_dot_nt(qh, kh) for qh, kh in zip(qt, mkb)] for qt in qn]
    p = [[_softmax2_rows(x).astype(BF16) for x in st] for st in sc]
    oh = [[_dot(ph, vh) for ph, vh in zip(pt, mvb)] for pt in p]
    for t, r in enumerate(rows):
        outs = [oh[t][h] * _silu(qz[t][:, hw + h * MEM_HD:hw + (h + 1) * MEM_HD]) for h in range(MEM_HEADS)]
        oh[t] = jnp.concatenate(outs, axis=1).astype(BF16)
    ym = [_dot(om, wmo_ref[...]) for om in oh]
    for t, r in enumerate(rows):
        y_ref[0, r, :] = acc[t] + ym[t]


def _post(x, os_, wout, mg, wmq, mqg, mk, mv, wmo):
    b, s, d = x.shape
    tm = _row_tile(s, 256 * POST_SPLIT)
    hw = MEM_HEADS * MEM_HD
    nm = mk.shape[1]
    blk = lambda wd: pl.BlockSpec((1, tm, wd), lambda i, j: (i, j, 0))
    mem = pl.BlockSpec((1, nm, hw), lambda i, j: (i, 0, 0))
    kin = sum(o.shape[-1] for o in os_)
    return pl.pallas_call(
        functools.partial(_post_body, n_o=len(os_)),
        grid=(b, s // tm),
        in_specs=[blk(d)] + [blk(o.shape[-1]) for o in os_]
        + [_const_spec((kin, d)), _const_spec((1, d)), _const_spec((d, 2 * hw)), _const_spec((1, MEM_HD)),
           mem, mem, _const_spec((hw, d))],
        out_specs=blk(d),
        out_shape=jax.ShapeDtypeStruct((b, s, d), F32),
        compiler_params=_params(("arbitrary", "arbitrary")),
        name="post",
    )(x, *os_, wout, mg, wmq, mqg, mk, mv, wmo)


def _pad_cols(w, n):
    return jnp.pad(w, ((0, 0), (0, n - w.shape[1])))


def _prep_l0(w_in, conv_w, a_log, dt_bias, q_g, k_g):
    c0 = 3 * 1024
    ab = w_in[:, c0:c0 + 2 * GDN_HEADS]
    c1 = c0 + 2 * GDN_HEADS
    cols = lambda i: w_in[:, c1 + i * 1024:c1 + (i + 1) * 1024].astype(BF16)
    w = dict(
        wu=w_in[:, :c0].astype(BF16), wab=_pad_cols(ab, LANES).astype(BF16),
        wza=cols(0), wq=cols(1), wk=cols(2), wv=cols(3), wzb=cols(4),
        alog=jnp.pad(a_log, (0, LANES - GDN_HEADS)).reshape(1, LANES).astype(F32),
        dtb=jnp.pad(dt_bias, (0, LANES - GDN_HEADS)).reshape(1, LANES).astype(F32),
    )
    ns = 3 * GDN_HEADS
    cw = jnp.pad(conv_w.astype(F32), ((0, 8 - CONV_TAPS), (0, 0))).reshape(8, ns, LANES).transpose(1, 0, 2)
    dm = BAND_HEADS * BAND_HD
    qg = jnp.tile(q_g.astype(F32), BAND_HEADS).reshape(1, dm)
    kg = jnp.tile(k_g.astype(F32), BAND_HEADS).reshape(1, dm)
    return w, cw, qg, kg


def _band_bias_table(rel_bias, L, chunk):
    W = BAND_PAST + L
    n = np.arange(W + L - 1)
    idx = np.clip(BAND_PAST + (L - 1) - n, -BAND_MAX_REL, BAND_MAX_REL) + BAND_MAX_REL
    strip = rel_bias.astype(F32)[:, idx]
    bias = jnp.stack([strip[:, L - 1 - a:L - 1 - a + W] for a in range(L)], axis=1)
    back = np.arange(L)[:, None] // chunk - (np.arange(W)[None, :] - BAND_PAST) // chunk
    readable = (back >= 0) & (back <= BAND_PAST // chunk)
    return bias if readable.all() else jnp.where(jnp.asarray(readable)[None], bias, -jnp.inf)


def _prep_l1(w_in, w_gate_up, gate_bias):
    qk = GLA_HEADS * GLA_DK
    vw = GLA_HEADS * GLA_DV
    o = np.cumsum([0, qk, qk, vw, GLA_RANK, vw])
    return dict(
        wq=w_in[:, o[0]:o[1]].astype(BF16), wk=w_in[:, o[1]:o[2]].astype(BF16),
        wv=w_in[:, o[2]:o[3]].astype(BF16), wlr=_pad_cols(w_in[:, o[3]:o[4]], LANES).astype(BF16),
        wz=w_in[:, o[4]:o[5]].astype(BF16),
        wg=jnp.pad(w_gate_up, ((0, LANES - GLA_RANK), (0, 0))).astype(BF16),
        gbias=gate_bias.reshape(1, qk).astype(F32),
    )


def _layer0(x, gdn_states, band_caches, mk, mv, norm_g, pw, cw, qg, kg, bias,
            a_onorm_g, w_out, mnorm_g, w_mq, mq_g, w_mo, L):
    b, s, d = x.shape
    outs = _proj0(x.reshape(b * s, d), norm_g.reshape(1, d), pw, qg, kg, s)
    u, gb, za, qlo, qhi, kn, klast, vlo, vhi, vlast, zb = (t.reshape((b, -1) + t.shape[1:]) for t in outs)
    if gdn_states is not None:
        conv_state, gdn_state = gdn_states
        gdn_states = (jnp.pad(conv_state.astype(F32), ((0, 0), (8 - (CONV_TAPS - 1), 0), (0, 0))),
                      gdn_state.astype(F32))
    o_a, s_new = _gdn(u, gb, za, gdn_states, cw, a_onorm_g.reshape(1, LANES), L)
    o_b = _band(qlo, qhi, kn, vlo, vhi, zb, bias, band_caches, bias.shape[1])
    y = _post(x, (o_a, o_b), w_out, mnorm_g.reshape(1, d), w_mq, mq_g.reshape(1, MEM_HD), mk, mv, w_mo)
    return y, u[:, s - (CONV_TAPS - 1):, :], s_new, klast, vlast


def _layer1(x, gla_state, mk, mv, norm_g, pw, c_onorm_g, w_out, mnorm_g, w_mq, mq_g, w_mo, L):
    b, s, d = x.shape
    q, k, v, lg, z = _proj1(x.reshape(b * s, d), norm_g.reshape(1, d), pw)
    r3 = lambda t: t.reshape(b, s, t.shape[-1])
    q, k, v, lg, z = map(r3, (q, k, v, lg, z))
    o, st = _gla(q, k, v, lg, z, gla_state, c_onorm_g.reshape(1, GLA_DV), L)
    y = _post(x, (o,), w_out, mnorm_g.reshape(1, d), w_mq, mq_g.reshape(1, MEM_HD), mk, mv, w_mo)
    return y, st


def kernel(x_prompt, x_sample, mem_prompt, state_l0_gdn_conv, state_l0_gdn, cache_l0_band_k, cache_l0_band_v, cache_l0_mem_k, cache_l0_mem_v, state_l1_gla, cache_l1_mem_k, cache_l1_mem_v, l0_norm_g, l0_w_in, l0_conv_w, l0_a_log, l0_dt_bias, l0_a_onorm_g, l0_b_q_g, l0_b_k_g, l0_b_rel_bias, l0_w_out, l0_mnorm_g, l0_mem_norm_g, l0_w_mkv, l0_mk_g, l0_w_mq, l0_mq_g, l0_w_mo, l1_norm_g, l1_w_in, l1_w_gate_up, l1_gate_bias, l1_c_onorm_g, l1_w_out, l1_mnorm_g, l1_mem_norm_g, l1_w_mkv, l1_mk_g, l1_w_mq, l1_mq_g, l1_w_mo):
    bp, sp, d = x_prompt.shape
    bs, ss, _ = x_sample.shape
    nm = mem_prompt.shape[1]
    hw = MEM_HEADS * MEM_HD
    dm = BAND_HEADS * BAND_HD
    assert sp % CHUNK_ == 0 and ss % INV_SUB == 0 and ss <= CHUNK_
    assert cache_l0_band_k.shape[1] == BAND_PAST

    pw0, cw, qg, kg = _prep_l0(l0_w_in, l0_conv_w, l0_a_log, l0_dt_bias, l0_b_q_g, l0_b_k_g)
    pw1 = _prep_l1(l1_w_in, l1_w_gate_up, l1_gate_bias)
    band_rows = BAND_STEP_CHUNKS * CHUNK_ if sp % (BAND_STEP_CHUNKS * CHUNK_) == 0 else CHUNK_
    bias_p = _band_bias_table(l0_b_rel_bias, band_rows, CHUNK_)
    bias_s = _band_bias_table(l0_b_rel_bias, ss, ss)
    bf = lambda w: w.astype(BF16)
    mem2 = mem_prompt.reshape(bp * nm, d)

    p_mk0, p_mv0, mk0, mv0 = _memkv(mem2, l0_mem_norm_g.reshape(1, d), bf(l0_w_mkv), l0_mk_g.reshape(1, MEM_HD))
    mk0 = mk0.reshape(bp, nm, hw)
    mv0 = mv0.reshape(bp, nm, hw)
    l0_shared = (l0_norm_g, pw0, cw, qg, kg)
    l0_tail = (l0_a_onorm_g, bf(l0_w_out), l0_mnorm_g, bf(l0_w_mq), l0_mq_g, bf(l0_w_mo))
    yp, p_conv, p_gdn, p_kn, p_v = _layer0(
        x_prompt, None, None, mk0, mv0,
        *l0_shared, bias_p, *l0_tail, CHUNK_)
    ys, s_conv, s_gdn, s_kn, s_v = _layer0(
        x_sample, (state_l0_gdn_conv, state_l0_gdn),
        (cache_l0_band_k, cache_l0_band_v),
        cache_l0_mem_k.reshape(bs, nm, hw), cache_l0_mem_v.reshape(bs, nm, hw),
        *l0_shared, bias_s, *l0_tail, ss)

    p_mk1, p_mv1, mk1, mv1 = _memkv(mem2, l1_mem_norm_g.reshape(1, d), bf(l1_w_mkv), l1_mk_g.reshape(1, MEM_HD))
    mk1 = mk1.reshape(bp, nm, hw)
    mv1 = mv1.reshape(bp, nm, hw)
    l1_tail = (l1_c_onorm_g, bf(l1_w_out), l1_mnorm_g, bf(l1_w_mq), l1_mq_g, bf(l1_w_mo))
    yp, p_gla = _layer1(yp, None, mk1, mv1, l1_norm_g, pw1, *l1_tail, CHUNK_)
    ys, s_gla = _layer1(ys, state_l1_gla.astype(F32), cache_l1_mem_k.reshape(bs, nm, hw),
                        cache_l1_mem_v.reshape(bs, nm, hw), l1_norm_g, pw1, *l1_tail, ss)

    m4 = lambda t: t.reshape(bp, nm, MEM_HEADS, MEM_HD)
    return (yp, ys, p_conv, p_gdn, p_kn, p_v,
            m4(p_mk0), m4(p_mv0), p_gla, m4(p_mk1), m4(p_mv1),
            s_conv, s_gdn, s_kn, s_v, s_gla)
```
